```python
import numpy as np
import jax
import jax.numpy as jnp
from jax import lax

D_MODEL = 1024
BATCH = 2
SEQ = 16384
DEPTH = 2

GRID_W = 64
CTX_LEN = 256
HEAD_DIM = 64
N_Q_HEADS = 8
N_KV_HEADS = 2
WINDOW = 128
ATTN_BLOCK = 128
ROPE_THETA = 10000.0
POOL_WIDTH = 256
POOL_WINDOWS = (2, 4, 8, 16)
RET_HEADS = 4
RET_DIM = 64
RET_CHUNK = 128
N_EXPERTS = 32
TOP_K = 4
D_FF = 1024
SWIGLU_LIMIT = 7.0
SWIGLU_ALPHA = 1.702
MOE_BLOCK = 128
NORM_EPS = 1e-6
GN_EPS = 1e-5

ATTN_Q_WIDTH = N_Q_HEADS * HEAD_DIM
ATTN_KV_WIDTH = N_KV_HEADS * HEAD_DIM
RET_WIDTH = RET_HEADS * RET_DIM
IN_SPLIT_SIZES = (ATTN_Q_WIDTH, ATTN_KV_WIDTH, ATTN_KV_WIDTH, POOL_WIDTH, RET_WIDTH, RET_WIDTH, RET_WIDTH, RET_WIDTH, RET_WIDTH, D_MODEL, D_MODEL, D_MODEL)
IN_WIDTH = sum(IN_SPLIT_SIZES)

kernel_name = 'hybrid_gated_parallel_mixer_moe_dit'

F32 = jnp.float32


def rmsnorm(x, g):
    x32 = x.astype(F32)
    y = x32 * lax.rsqrt(jnp.mean(x32 * x32, axis=-1, keepdims=True) + NORM_EPS)
    return (y * g.astype(F32)).astype(x.dtype)


def heads(t, n):
    return t.reshape(t.shape[0], t.shape[1], n, -1)


def rotate(x, pos, inv_freq):
    ang = pos[:, None] * inv_freq[None, :]
    cos = jnp.cos(ang)[None, :, None, :]
    sin = jnp.sin(ang)[None, :, None, :]
    half = x.shape[-1] // 2
    x1 = x[..., :half].astype(F32)
    x2 = x[..., half:].astype(F32)
    return jnp.concatenate([x1 * cos - x2 * sin, x1 * sin + x2 * cos], axis=-1).astype(x.dtype)


def axial_rope(x, row, col):
    axis_dim = x.shape[-1] // 2
    inv = ROPE_THETA ** (-jnp.arange(0, axis_dim, 2, dtype=F32) / axis_dim)
    return jnp.concatenate([rotate(x[..., :axis_dim], row, inv), rotate(x[..., axis_dim:], col, inv)], axis=-1)


def retention_rotation(x, pos):
    dr = x.shape[-1]
    inv = 1.0 / (ROPE_THETA ** jnp.linspace(0.0, 1.0, dr // 2, dtype=F32))
    return rotate(x, pos, inv)


def sink_softmax(s, sink):
    sink_col = jnp.broadcast_to(sink.astype(F32), s.shape[:-1] + (1,))
    return jax.nn.softmax(jnp.concatenate([s, sink_col], axis=-1), axis=-1)[..., :-1]


def window_attention(q, k, v, ck, cv, sinks):
    B, L, Hq, d = q.shape
    Hkv = k.shape[2]
    G = Hq // Hkv
    C = ATTN_BLOCK
    N = L // C
    qb = (q * d ** -0.5).reshape(B, N, C, Hkv, G, d)
    pad = jnp.zeros((B, C, Hkv, d), k.dtype)

    def band(t):
        tp = jnp.concatenate([pad, t, pad], axis=1).reshape(B, N + 2, C, Hkv, d)
        return jnp.concatenate([tp[:, :-2], tp[:, 1:-1], tp[:, 2:]], axis=2)

    kw, vw = band(k), band(v)
    rel = jnp.arange(3 * C)[None, :] - C - jnp.arange(C)[:, None]
    kpos = (jnp.arange(N)[:, None] - 1) * C + jnp.arange(3 * C)[None, :]
    valid = (jnp.abs(rel)[None] <= WINDOW) & ((kpos >= 0) & (kpos < L))[:, None, :]
    s_loc = jnp.einsum('bnqhgd,bnkhd->bnhgqk', qb, kw).astype(F32)
    s_loc = jnp.where(valid[None, :, None, None], s_loc, -jnp.inf)
    s_ctx = jnp.einsum('bnqhgd,bchd->bnhgqc', qb, ck).astype(F32)
    p = sink_softmax(jnp.concatenate([s_loc, s_ctx], axis=-1), sinks.reshape(1, 1, Hkv, G, 1, 1)).astype(v.dtype)
    o = (jnp.einsum('bnhgqk,bnkhd->bnqhgd', p[..., :3 * C], vw)
         + jnp.einsum('bnhgqc,bchd->bnqhgd', p[..., 3 * C:], cv))
    return o.reshape(B, L, Hq * d)


def context_attention(cq, ck, cv, sinks):
    B, Lc, Hq, d = cq.shape
    Hkv = ck.shape[2]
    G = Hq // Hkv
    qg = (cq * d ** -0.5).reshape(B, Lc, Hkv, G, d)
    s = jnp.einsum('bqhgd,bkhd->bhgqk', qg, ck).astype(F32)
    p = sink_softmax(s, sinks.reshape(1, Hkv, G, 1, 1)).astype(cv.dtype)
    return jnp.einsum('bhgqk,bkhd->bqhgd', p, cv).reshape(B, Lc, Hq * d)


def multiscale_pool(u, pool_w, pool_scale):
    B, L, P = u.shape
    gw = P // len(POOL_WINDOWS)
    uf = u.astype(F32)
    cs = jnp.concatenate([jnp.zeros((B, 1, P), F32), lax.cumsum(uf, axis=1)], axis=1)
    t = jnp.arange(L)
    outs = []
    for g, w in enumerate(POOL_WINDOWS):
        lo = jnp.clip(t - w // 2, 0, L)
        hi = jnp.clip(t - w // 2 + w, 0, L)
        sl = slice(g * gw, (g + 1) * gw)
        mean = (cs[:, hi, sl] - cs[:, lo, sl]) / (hi - lo).astype(F32)[None, :, None]
        diff = (mean - uf[:, :, sl]).astype(u.dtype)
        outs.append(jnp.einsum('blc,ce->ble', diff, pool_w[g]))
    return jnp.concatenate(outs, axis=-1) * pool_scale


def retention_scan(q, k, v, log_g, s0):
    B, L, H, dk = q.shape
    dv = v.shape[-1]
    C = RET_CHUNK
    N = L // C
    qc = q.astype(F32).reshape(B, N, C, H, dk)
    kc = k.astype(F32).reshape(B, N, C, H, dk)
    vc = v.astype(F32).reshape(B, N, C, H, dv)
    i = jnp.arange(C, dtype=F32)
    rel = i[:, None] - i[None, :]
    dmat = jnp.where(rel[None] >= 0, jnp.exp(jnp.maximum(rel, 0.0)[None] * log_g[:, None, None]), 0.0)
    intra = jnp.einsum('bnihd,bnjhd->bnhij', qc, kc) * dmat
    y_intra = jnp.einsum('bnhij,bnjhe->bnihe', intra, vc)
    k_decay = jnp.exp((C - 1 - i)[None, :] * log_g[:, None])
    kv = jnp.einsum('bnjhd,hj,bnjhe->nbhde', kc, k_decay, vc)
    chunk_decay = jnp.exp(C * log_g)[:, None, None]

    def step(s, kv_n):
        return s * chunk_decay + kv_n, s

    s_final, s_prev = lax.scan(step, s0, kv)
    q_decay = jnp.exp((i + 1)[None, :] * log_g[:, None])
    y_cross = jnp.einsum('bnihd,hi,nbhde->bnihe', qc, q_decay, s_prev)
    return (y_intra + y_cross).reshape(B, L, H, dv), s_final


def context_final_state(k, v, log_g):
    Lc = k.shape[1]
    w = jnp.exp((Lc - 1 - jnp.arange(Lc, dtype=F32))[:, None] * log_g[None, :])
    return jnp.einsum('bmhd,mh,bmhe->bhde', k.astype(F32), w, v.astype(F32))


def head_norm(y):
    mu = jnp.mean(y, axis=-1, keepdims=True)
    var = jnp.mean(jnp.square(y - mu), axis=-1, keepdims=True)
    return (y - mu) * lax.rsqrt(var + GN_EPS)


def retention_bidir(q, k, v, gf, gb, log_g, s_f0, s_b0):
    B, L = q.shape[0], q.shape[1]
    y_f, s_f = retention_scan(q, k, v, log_g[0], s_f0)
    y_b, s_b = retention_scan(jnp.flip(q, 1), jnp.flip(k, 1), jnp.flip(v, 1), log_g[1], s_b0)
    y = (head_norm(y_f) * jax.nn.silu(gf.astype(F32))
         + head_norm(jnp.flip(y_b, 1)) * jax.nn.silu(gb.astype(F32)))
    return y.reshape(B, L, -1).astype(q.dtype), s_f, s_b


def merge_branches(attn, pool, ret, ga, gp, gr, w_o_attn, w_o_pool, w_o_ret, w_out):
    m = (jax.nn.sigmoid(ga) * (attn @ w_o_attn)
         + jax.nn.sigmoid(gp) * (pool @ w_o_pool)
         + jax.nn.sigmoid(gr) * (ret @ w_o_ret))
    return m @ w_out


def token_mixer(h, hc, w_in, attn_sinks, pool_w, pool_scale, ret_decay, w_o_attn, w_o_pool, w_o_ret, w_out, with_ctx_out):
    B, L, _ = h.shape
    split_at = np.cumsum(IN_SPLIT_SIZES)[:-1].tolist()
    q, k, v, u, rq, rk, rv, rgf, rgb, ga, gp, gr = jnp.split(h @ w_in, split_at, axis=-1)
    cq, ck, cv, cu, crq, crk, crv, crgf, crgb, cga, cgp, cgr = jnp.split(hc @ w_in, split_at, axis=-1)
    rows = L // GRID_W
    row = jnp.broadcast_to(jnp.arange(rows, dtype=F32)[:, None], (rows, GRID_W)).reshape(-1)
    col = jnp.broadcast_to(jnp.arange(GRID_W, dtype=F32)[None, :], (rows, GRID_W)).reshape(-1)
    tpos = jnp.arange(L, dtype=F32)

    ck = heads(ck, N_KV_HEADS)
    cv = heads(cv, N_KV_HEADS)
    attn = window_attention(axial_rope(heads(q, N_Q_HEADS), row, col), axial_rope(heads(k, N_KV_HEADS), row, col),
                            heads(v, N_KV_HEADS), ck, cv, attn_sinks)
    pool = multiscale_pool(u, pool_w, pool_scale)
    log_g = jax.nn.log_sigmoid(ret_decay.astype(F32))
    k_scale = RET_DIM ** -0.5
    crk_h = heads(crk, RET_HEADS) * k_scale
    crv_h = heads(crv, RET_HEADS)
    if with_ctx_out:
        s0 = jnp.zeros((B, RET_HEADS, RET_DIM, RET_DIM), F32)
        ret_c, s_f, s_b = retention_bidir(heads(crq, RET_HEADS), crk_h, crv_h, heads(crgf, RET_HEADS),
                                          heads(crgb, RET_HEADS), log_g, s0, s0)
    else:
        s_f = context_final_state(crk_h, crv_h, log_g[0])
        s_b = context_final_state(jnp.flip(crk_h, 1), jnp.flip(crv_h, 1), log_g[1])
    rq_h = retention_rotation(heads(rq, RET_HEADS), tpos)
    rk_h = retention_rotation(heads(rk, RET_HEADS), tpos) * k_scale
    ret, _, _ = retention_bidir(rq_h, rk_h, heads(rv, RET_HEADS), heads(rgf, RET_HEADS), heads(rgb, RET_HEADS),
                                log_g, s_f, s_b)
    out = merge_branches(attn, pool, ret, ga, gp, gr, w_o_attn, w_o_pool, w_o_ret, w_out)
    if not with_ctx_out:
        return out, None
    attn_c = context_attention(heads(cq, N_Q_HEADS), ck, cv, attn_sinks)
    pool_c = multiscale_pool(cu, pool_w, pool_scale)
    out_c = merge_branches(attn_c, pool_c, ret_c, cga, cgp, cgr, w_o_attn, w_o_pool, w_o_ret, w_out)
    return out, out_c


def moe(h, w_router, b_router, w1, b1, w2, b2):
    T, D = h.shape
    logits = (h @ w_router + b_router).astype(F32)
    top_logits, top_idx = lax.top_k(logits, TOP_K)
    gates = jax.nn.softmax(top_logits, axis=-1)
    A = T * TOP_K
    flat_e = top_idx.reshape(-1)
    flat_tok = jnp.arange(A, dtype=jnp.int32) // TOP_K
    flat_w = gates.reshape(-1)
    order = jnp.argsort(flat_e)
    sorted_e = flat_e[order]
    counts = jnp.bincount(flat_e, length=N_EXPERTS)
    start = jnp.cumsum(counts) - counts
    padded = (counts + MOE_BLOCK - 1) // MOE_BLOCK * MOE_BLOCK
    pad_end = jnp.cumsum(padded)
    pad_start = pad_end - padded
    dest = pad_start[sorted_e] + jnp.arange(A) - start[sorted_e]
    n_blocks = -(-A // MOE_BLOCK) + N_EXPERTS
    P = n_blocks * MOE_BLOCK
    row_tok = jnp.zeros((P,), jnp.int32).at[dest].set(flat_tok[order])
    row_w = jnp.zeros((P,), F32).at[dest].set(flat_w[order])
    block_e = jnp.minimum(jnp.searchsorted(pad_end, jnp.arange(n_blocks) * MOE_BLOCK, side='right'), N_EXPERTS - 1)
    xb = h[row_tok].reshape(n_blocks, MOE_BLOCK, D)

    def expert_block(args):
        xblk, e = args
        hid = xblk @ w1[e] + b1[e]
        glu = jnp.minimum(hid[:, :D_FF], SWIGLU_LIMIT)
        lin = jnp.clip(hid[:, D_FF:], -SWIGLU_LIMIT, SWIGLU_LIMIT)
        act = glu * jax.nn.sigmoid(SWIGLU_ALPHA * glu) * (lin + 1)
        return act @ w2[e] + b2[e]

    yb = lax.map(expert_block, (xb, block_e)).reshape(P, D)
    y = jax.ops.segment_sum(yb.astype(F32) * row_w[:, None], row_tok, num_segments=T)
    return y.astype(h.dtype)


def setup_inputs(seed: int = 0) -> dict:
    key = jax.random.key(seed)
    ks = jax.random.split(key, 24)
    D = D_MODEL
    gw = POOL_WIDTH // len(POOL_WINDOWS)

    def nrm(k, shape, s):
        return jax.random.normal(k, shape, F32) * s

    base_decay = jnp.log(2.0 ** (5.0 + jnp.arange(RET_HEADS, dtype=F32)) - 1.0)
    return {
        'x': nrm(ks[0], (BATCH, SEQ, D), 1.0),
        'c': nrm(ks[1], (BATCH, D), 1.0),
        'ctx': nrm(ks[2], (BATCH, CTX_LEN, D), 1.0),
        'c_ctx': nrm(ks[3], (D,), 1.0),
        'w_mod': nrm(ks[4], (DEPTH, D, 6 * D), 0.5 * D ** -0.5),
        'b_mod': nrm(ks[5], (DEPTH, 6 * D), 0.02),
        'norm1': 1.0 + nrm(ks[6], (DEPTH, D), 0.02),
        'norm2': 1.0 + nrm(ks[7], (DEPTH, D), 0.02),
        'w_in': nrm(ks[8], (DEPTH, D, IN_WIDTH), D ** -0.5),
        'attn_sinks': nrm(ks[9], (DEPTH, N_Q_HEADS), 1.0),
        'pool_w': nrm(ks[10], (DEPTH, len(POOL_WINDOWS), gw, gw), gw ** -0.5),
        'pool_scale': 1.0 + nrm(ks[11], (DEPTH, POOL_WIDTH), 0.02),
        'ret_decay': base_decay + nrm(ks[12], (DEPTH, 2, RET_HEADS), 0.1),
        'w_o_attn': nrm(ks[13], (DEPTH, ATTN_Q_WIDTH, D), ATTN_Q_WIDTH ** -0.5),
        'w_o_pool': nrm(ks[14], (DEPTH, POOL_WIDTH, D), POOL_WIDTH ** -0.5),
        'w_o_ret': nrm(ks[15], (DEPTH, RET_WIDTH, D), RET_WIDTH ** -0.5),
        'w_out': nrm(ks[16], (DEPTH, D, D), D ** -0.5),
        'w_router': nrm(ks[17], (DEPTH, D, N_EXPERTS), D ** -0.5),
        'b_router': nrm(ks[18], (DEPTH, N_EXPERTS), 0.01),
        'w_expert_in': nrm(ks[19], (DEPTH, N_EXPERTS, D, 2 * D_FF), D ** -0.5),
        'b_expert_in': nrm(ks[20], (DEPTH, N_EXPERTS, 2 * D_FF), 0.01),
        'w_expert_out': nrm(ks[21], (DEPTH, N_EXPERTS, D_FF, D), D_FF ** -0.5),
        'b_expert_out': nrm(ks[22], (DEPTH, N_EXPERTS, D), 0.01),
        'final_norm': 1.0 + nrm(ks[23], (D,), 0.02),
    }


def reference(x, c, ctx, c_ctx, w_mod, b_mod, norm1, norm2, w_in, attn_sinks, pool_w, pool_scale, ret_decay,
              w_o_attn, w_o_pool, w_o_ret, w_out, w_router, b_router, w_expert_in, b_expert_in, w_expert_out,
              b_expert_out, final_norm):
    B, L, D = x.shape
    silu_c = jax.nn.silu(c)
    silu_cc = jax.nn.silu(c_ctx)
    xc = ctx
    for l in range(DEPTH):
        last = l == DEPTH - 1
        sh1, sc1, g1, sh2, sc2, g2 = jnp.split((silu_c @ w_mod[l] + b_mod[l])[:, None, :], 6, axis=-1)
        csh1, csc1, cg1, csh2, csc2, cg2 = jnp.split(silu_cc @ w_mod[l] + b_mod[l], 6, axis=-1)
        h = rmsnorm(x, norm1[l]) * (1 + sc1) + sh1
        hc = rmsnorm(xc, norm1[l]) * (1 + csc1) + csh1
        y, yc = token_mixer(h, hc, w_in[l], attn_sinks[l], pool_w[l], pool_scale[l], ret_decay[l],
                            w_o_attn[l], w_o_pool[l], w_o_ret[l], w_out[l], not last)
        x = x + g1 * y
        h2 = rmsnorm(x, norm2[l]) * (1 + sc2) + sh2
        x = x + g2 * moe(h2.reshape(B * L, D), w_router[l], b_router[l], w_expert_in[l], b_expert_in[l],
                         w_expert_out[l], b_expert_out[l]).reshape(B, L, D)
        if not last:
            xc = xc + cg1 * yc
            hc2 = rmsnorm(xc, norm2[l]) * (1 + csc2) + csh2
            xc = xc + cg2 * moe(hc2.reshape(-1, D), w_router[l], b_router[l], w_expert_in[l], b_expert_in[l],
                                w_expert_out[l], b_expert_out[l]).reshape(xc.shape)
    return rmsnorm(x, final_norm)
```

```python
import functools

import jax
import jax.numpy as jnp
from jax import lax
from jax.experimental import pallas as pl
from jax.experimental.pallas import tpu as pltpu

F32 = jnp.float32
BF16 = jnp.bfloat16
I32 = jnp.int32

GRID_W = 64
HEAD_DIM = 64
N_Q_HEADS = 8
N_KV_HEADS = 2
WINDOW = 128
ROPE_THETA = 10000.0
POOL_WIDTH = 256
POOL_WINDOWS = (2, 4, 8, 16)
RET_HEADS = 4
RET_DIM = 64
N_EXPERTS = 32
TOP_K = 4
D_FF = 1024
SWIGLU_LIMIT = 7.0
SWIGLU_ALPHA = 1.702
NORM_EPS = 1e-6
GN_EPS = 1e-5

LANE = 128
SUBLANE = 8
TM = 256
AB = 128
CTX_BLOCKS = TM // AB
EB = 256
POOL_HALO = SUBLANE
NEG = -1e30
VMEM_LIMIT = 56 * 1024 * 1024

ATT_Q = N_Q_HEADS * HEAD_DIM
KV2 = 2 * N_KV_HEADS * HEAD_DIM
RET_W = RET_HEADS * RET_DIM


def _dot(a, b):
    return jnp.dot(a, b, preferred_element_type=F32)


def _dot_nt(a, b):
    return lax.dot_general(a, b, (((1,), (1,)), ((), ())), preferred_element_type=F32)


def _dot_tn(a, b):
    return lax.dot_general(a, b, (((0,), (0,)), ((), ())), preferred_element_type=F32)


def _split(a):
    hi = a.astype(BF16)
    lo = (a - hi.astype(F32)).astype(BF16)
    return hi, lo


def _dot_hilo(a, m):
    hi, lo = _split(a)
    return _dot(hi, m) + _dot(lo, m)


def _sigmoid(x):
    return 1.0 / (1.0 + jnp.exp(-x))


def _silu(x):
    return x * _sigmoid(x)


def _params(*sem):
    return pltpu.CompilerParams(dimension_semantics=sem, vmem_limit_bytes=VMEM_LIMIT)


def _mod_kernel(c_ref, w_ref, b_ref, o_ref):
    s = _silu(c_ref[...])
    sh, sl = _split(s)
    wh, wl = _split(w_ref[0])
    o_ref[0] = _dot(sh, wh) + _dot(sh, wl) + _dot(sl, wh) + b_ref[0]


def _modulation(c, c_ctx, w_mod, b_mod):
    depth, d, six_d = w_mod.shape
    b = c.shape[0]
    cc = jnp.zeros((SUBLANE, d), F32).at[:b].set(c).at[b].set(c_ctx)
    out = pl.pallas_call(
        _mod_kernel,
        grid=(depth, six_d // d),
        in_specs=[
            pl.BlockSpec((SUBLANE, d), lambda l, n: (0, 0)),
            pl.BlockSpec((1, d, d), lambda l, n: (l, 0, n)),
            pl.BlockSpec((1, 1, d), lambda l, n: (l, 0, n)),
        ],
        out_specs=pl.BlockSpec((1, SUBLANE, d), lambda l, n: (l, 0, n)),
        out_shape=jax.ShapeDtypeStruct((depth, SUBLANE, six_d), F32),
        compiler_params=_params("arbitrary", "arbitrary"),
    )(cc, w_mod, b_mod.reshape(depth, 1, six_d))
    return out[:, : b + 1].reshape(depth, b + 1, 6, d)


def _rope(x, cos, sin, half, first):
    partner = jnp.where(first, pltpu.roll(x, LANE - half, 1), pltpu.roll(x, half, 1))
    return x * cos + partner * sin


def _inproj_kernel(x_ref, mod_ref, n1_ref, w_ref, ac_ref, as_ref, rc_ref, rs_ref,
                   q_ref, k_ref, v_ref, rq_ref, rk_ref, rv_ref, rg_ref, u_ref, g_ref):
    x = x_ref[0]
    ms = jnp.mean(x * x, axis=-1, keepdims=True)
    y = x * lax.rsqrt(ms + NORM_EPS) * n1_ref[...]
    h = y * (1.0 + mod_ref[0, 1:2, :]) + mod_ref[0, 0:1, :]
    hb = h.astype(BF16)
    lane = lax.broadcasted_iota(I32, (TM, LANE), 1)
    a_first = (lane % (HEAD_DIM // 2)) < (HEAD_DIM // 4)
    r_first = (lane % RET_DIM) < (RET_DIM // 2)
    ac, asn, rc, rsn = ac_ref[...], as_ref[...], rc_ref[...], rs_ref[...]

    def proj(off, width):
        return _dot(hb, w_ref[:, off:off + width])

    off = 0
    for t in range(ATT_Q // LANE):
        q_ref[0, :, t * LANE:(t + 1) * LANE] = _rope(
            proj(off, LANE), ac, asn, HEAD_DIM // 4, a_first).astype(BF16)
        off += LANE
    for t in range(KV2 // LANE):
        k_ref[0, :, t * LANE:(t + 1) * LANE] = _rope(
            proj(off, LANE), ac, asn, HEAD_DIM // 4, a_first).astype(BF16)
        off += LANE
    v_ref[0] = proj(off, KV2).astype(BF16)
    off += KV2
    for ref in (rq_ref, rk_ref):
        for t in range(RET_W // LANE):
            ref[0, :, t * LANE:(t + 1) * LANE] = _rope(
                proj(off, LANE), rc, rsn, RET_DIM // 2, r_first).astype(BF16)
            off += LANE
    rv_ref[0] = proj(off, RET_W).astype(BF16)
    off += RET_W
    rg_ref[0] = proj(off, 2 * RET_W).astype(BF16)
    off += 2 * RET_W
    u_ref[0] = proj(off, POOL_WIDTH)
    off += POOL_WIDTH
    d = x.shape[-1]
    for t in range(3):
        g_ref[0, :, t * d:(t + 1) * d] = proj(off, d).astype(BF16)
        off += d


def _inproj(xm, mod_l, norm1_l, w1, tables):
    b, s, d = xm.shape
    nt = s // TM
    wcols = w1.shape[1]
    tok = lambda width: pl.BlockSpec((1, TM, width), lambda bi, j: (bi, j, 0))
    tab = pl.BlockSpec((TM, LANE), lambda bi, j: (j, 0))
    widths = (ATT_Q, KV2, KV2, RET_W, RET_W, RET_W, 2 * RET_W, POOL_WIDTH, 3 * d)
    dtypes = (BF16,) * 7 + (F32, BF16)
    return pl.pallas_call(
        _inproj_kernel,
        grid=(b, nt),
        in_specs=[
            tok(d),
            pl.BlockSpec((1, 6, d), lambda bi, j: (jnp.where(j == 0, b, bi), 0, 0)),
            pl.BlockSpec((1, d), lambda bi, j: (0, 0)),
            pl.BlockSpec((d, wcols), lambda bi, j: (0, 0)),
            tab, tab, tab, tab,
        ],
        out_specs=[tok(w) for w in widths],
        out_shape=[jax.ShapeDtypeStruct((b, s, w), dt) for w, dt in zip(widths, dtypes)],
        compiler_params=_params("arbitrary", "arbitrary"),
    )(xm, mod_l, norm1_l.reshape(1, d), w1, *tables)


def _rope_tables(seq):
    pos = jnp.arange(seq, dtype=F32)
    lane = jnp.arange(LANE)
    axis_dim = HEAD_DIM // 2
    inv_a = ROPE_THETA ** (-jnp.arange(0, axis_dim, 2, dtype=F32) / axis_dim)
    hl = lane % HEAD_DIM
    a_pos = jnp.where((hl < axis_dim)[None, :], jnp.floor(pos / GRID_W)[:, None], (pos % GRID_W)[:, None])
    a_ang = a_pos * inv_a[(hl % axis_dim) % (axis_dim // 2)][None, :]
    a_sign = jnp.where((hl % axis_dim) < axis_dim // 2, -1.0, 1.0)[None, :]
    inv_r = 1.0 / (ROPE_THETA ** jnp.linspace(0.0, 1.0, RET_DIM // 2, dtype=F32))
    rl = lane % RET_DIM
    r_ang = pos[:, None] * inv_r[rl % (RET_DIM // 2)][None, :]
    r_sign = jnp.where(rl < RET_DIM // 2, -1.0, 1.0)[None, :]
    ones = jnp.ones((TM, LANE), F32)
    zeros = jnp.zeros((TM, LANE), F32)
    cat = lambda head, body: jnp.concatenate([head, body], axis=0)
    return (cat(ones, jnp.cos(a_ang)), cat(zeros, jnp.sin(a_ang) * a_sign),
            cat(ones, jnp.cos(r_ang)), cat(zeros, jnp.sin(r_ang) * r_sign))


def _attn_kernel(sink_ref, q_ref, kp_ref, kc_ref, kn_ref, kx_ref, vp_ref, vc_ref, vn_ref, vx_ref,
                 o_ref, *, seq):
    n = pl.program_id(1) - CTX_BLOCKS
    g_heads = N_Q_HEADS // N_KV_HEADS
    lane = lax.broadcasted_iota(I32, (AB, LANE), 1)
    lo = lane < HEAD_DIM
    rows = lax.broadcasted_iota(I32, (g_heads * AB, 1), 0)
    col = lax.broadcasted_iota(I32, (1, 3 * AB + TM), 1)
    rel = col - AB - rows % AB
    kpos = (n - 1) * AB + col
    local_ok = (jnp.abs(rel) <= WINDOW) & (kpos >= 0) & (kpos < seq) & (n >= 0)
    valid = local_ok | (col >= 3 * AB)
    outs = []
    for g in range(N_KV_HEADS):
        ks = slice(g * LANE, (g + 1) * LANE)
        kd = jnp.concatenate([kp_ref[0, :, ks], kc_ref[0, :, ks], kn_ref[0, :, ks], kx_ref[0, :, ks]], axis=0)
        vd = jnp.concatenate([vp_ref[0, :, ks], vc_ref[0, :, ks], vn_ref[0, :, ks], vx_ref[0, :, ks]], axis=0)
        qs = []
        for c in range(2 * g, 2 * g + 2):
            qc = q_ref[0, :, c * LANE:(c + 1) * LANE]
            zero = jnp.zeros_like(qc)
            qs += [jnp.where(lo, qc, zero), jnp.where(lo, zero, qc)]
        s = _dot_nt(jnp.concatenate(qs, axis=0), kd)
        s = jnp.where(valid, s, NEG)
        sink = jnp.full((g_heads * AB, 1), sink_ref[g_heads * g + g_heads - 1], F32)
        for t in range(g_heads - 2, -1, -1):
            sink = jnp.where(rows < (t + 1) * AB, sink_ref[g_heads * g + t], sink)
        m = jnp.maximum(jnp.max(s, axis=-1, keepdims=True), sink)
        p = jnp.exp(s - m)
        den = jnp.sum(p, axis=-1, keepdims=True) + jnp.exp(sink - m)
        o = _dot(p.astype(BF16), vd) / den
        for t in range(2):
            outs.append(jnp.where(lo, o[2 * t * AB:(2 * t + 1) * AB], o[(2 * t + 1) * AB:(2 * t + 2) * AB]))
    o_ref[0] = jnp.concatenate(outs, axis=-1).astype(BF16)


def _attention(q, k2, v2, sinks, seq):
    b, s, _ = q.shape
    nb = s // AB
    prev = pl.BlockSpec((1, AB, KV2), lambda bi, i: (bi, jnp.maximum(i - 1, 0), 0))
    cur = pl.BlockSpec((1, AB, KV2), lambda bi, i: (bi, i, 0))
    nxt = pl.BlockSpec((1, AB, KV2), lambda bi, i: (bi, jnp.minimum(i + 1, nb - 1), 0))
    cx = pl.BlockSpec((1, TM, KV2), lambda bi, i: (bi, 0, 0))
    return pl.pallas_call(
        functools.partial(_attn_kernel, seq=seq),
        grid=(b, nb),
        in_specs=[pl.BlockSpec(memory_space=pltpu.SMEM),
                  pl.BlockSpec((1, AB, ATT_Q), lambda bi, i: (bi, i, 0)),
                  prev, cur, nxt, cx, prev, cur, nxt, cx],
        out_specs=pl.BlockSpec((1, AB, ATT_Q), lambda bi, i: (bi, i, 0)),
        out_shape=jax.ShapeDtypeStruct((b, s, ATT_Q), BF16),
        compiler_params=_params("arbitrary", "arbitrary"),
    )(sinks, q, k2, k2, k2, k2, v2, v2, v2, v2)


def _ret_kernel(lg_ref, qf_ref, kf_ref, vf_ref, qb_ref, kb_ref, vb_ref, yf_ref, yb_ref,
                st_ref, dm_ref, qd_ref, kd_ref, cd_ref, *, batch):
    step = pl.program_id(0)
    lane = lax.broadcasted_iota(I32, (AB, LANE), 1)
    row = lax.broadcasted_iota(I32, (AB, LANE), 0)
    lo = lane < RET_DIM
    tiles = RET_W // LANE

    @pl.when(step == 0)
    def _():
        st_ref[...] = jnp.zeros_like(st_ref)
        ii = row.astype(F32)
        jj = lane.astype(F32)
        for d in range(2):
            for c in range(tiles):
                lg0 = lg_ref[d * RET_HEADS + 2 * c]
                lg1 = lg_ref[d * RET_HEADS + 2 * c + 1]
                lgl = jnp.where(lo, lg0, lg1)
                q_exp = ii + 1.0 if d == 0 else AB - ii
                k_exp = (AB - 1.0) - ii if d == 0 else ii
                qd_ref[d * tiles + c] = jnp.exp(q_exp * lgl)
                kd_ref[d * tiles + c] = jnp.exp(k_exp * lgl)
                cd_ref[d * tiles + c] = jnp.exp(AB * lgl)
                rel = ii - jj if d == 0 else jj - ii
                for hh, lgh in enumerate((lg0, lg1)):
                    dm_ref[d * tiles + c, hh * AB:(hh + 1) * AB, :] = jnp.where(
                        rel >= 0, jnp.exp(jnp.maximum(rel, 0.0) * lgh), 0.0)

    same_head = (row < RET_DIM) == lo
    dirs = ((qf_ref, kf_ref, vf_ref, yf_ref), (qb_ref, kb_ref, vb_ref, yb_ref))
    for d, (q_ref, k_ref, v_ref, y_ref) in enumerate(dirs):
        for b in range(batch):
            for c in range(tiles):
                sl = slice(c * LANE, (c + 1) * LANE)
                t = d * tiles + c
                q = q_ref[b, :, sl]
                k = k_ref[b, :, sl]
                v = v_ref[b, :, sl]
                zero = jnp.zeros_like(q)
                q2 = jnp.concatenate([jnp.where(lo, q, zero), jnp.where(lo, zero, q)], axis=0)
                p = (_dot_nt(q2, k) * dm_ref[t]).astype(BF16)
                y_intra = jnp.where(lo, _dot(p[:AB], v), _dot(p[AB:], v))
                si = (d * batch + b) * tiles + c
                state = st_ref[si]
                q_dec = (q.astype(F32) * qd_ref[t]).astype(BF16)
                y_ref[b, :, sl] = y_intra + _dot(q_dec, state.astype(BF16))
                k_dec = (k.astype(F32) * kd_ref[t]).astype(BF16)
                st_ref[si] = state * cd_ref[t] + jnp.where(same_head, _dot_tn(k_dec, v), 0.0)


def _retention(rq, rk, rv, log_g):
    b, s, w = rq.shape
    nb = s // AB
    tiles = w // LANE

    def back(i):
        return jnp.where(i < CTX_BLOCKS, CTX_BLOCKS - 1 - i, nb - 1 + CTX_BLOCKS - i)

    fwd = pl.BlockSpec((b, AB, w), lambda i: (0, i, 0))
    bwd = pl.BlockSpec((b, AB, w), lambda i: (0, back(i), 0))
    return pl.pallas_call(
        functools.partial(_ret_kernel, batch=b),
        grid=(nb,),
        in_specs=[pl.BlockSpec(memory_space=pltpu.SMEM), fwd, fwd, fwd, bwd, bwd, bwd],
        out_specs=[fwd, bwd],
        out_shape=[jax.ShapeDtypeStruct((b, s, w), F32)] * 2,
        scratch_shapes=[
            pltpu.VMEM((2 * b * tiles, LANE, LANE), F32),
            pltpu.VMEM((2 * tiles, 2 * AB, LANE), F32),
            pltpu.VMEM((2 * tiles, AB, LANE), F32),
            pltpu.VMEM((2 * tiles, AB, LANE), F32),
            pltpu.VMEM((2 * tiles, AB, LANE), F32),
        ],
        compiler_params=_params("arbitrary"),
    )(log_g.reshape(-1), rq, rk, rv, rq, rk, rv)


def _merge_kernel(x_ref, attn_ref, up_ref, uc_ref, un_ref, yf_ref, yb_ref, rg_ref, g_ref, mod_ref,
                  wbd_ref, ps_ref, woa_ref, wop_ref, wor_ref, wout_ref, n2_ref, wrh_ref, wrl_ref,
                  br_ref, avg_ref, tri_ref,
                  xo_ref, h2_ref, re_ref, rr_ref, rw_ref, cnt_ref, carry_ref, *, seq, nt):
    i = pl.program_id(0)
    j = i % nt

    @pl.when(i == 0)
    def _():
        carry_ref[...] = jnp.zeros_like(carry_ref)

    ext = TM + 2 * POOL_HALO
    seq_len = jnp.where(j == 0, TM, seq)
    start = jnp.where(j == 0, 0, (j - 1) * TM)
    u = uc_ref[0]
    ue = jnp.concatenate([up_ref[0], u, un_ref[0]], axis=0)
    erow = lax.broadcasted_iota(I32, (ext, 1), 0) + (start - POOL_HALO)
    ue = jnp.where((erow >= 0) & (erow < seq_len), ue, 0.0)
    t_pos = lax.broadcasted_iota(I32, (TM, 1), 0) + start
    glane = lax.broadcasted_iota(I32, (TM, POOL_WIDTH), 1) // (POOL_WIDTH // len(POOL_WINDOWS))
    run = ue
    width = 1
    diff = jnp.zeros((TM, POOL_WIDTH), F32)
    for gi, w in enumerate(POOL_WINDOWS):
        while width < w:
            run = run + pltpu.roll(run, ext - width, 0)
            width *= 2
        win = pltpu.roll(run, w // 2, 0)[POOL_HALO:POOL_HALO + TM]
        cnt = jnp.minimum(t_pos - w // 2 + w, seq_len) - jnp.maximum(t_pos - w // 2, 0)
        diff = jnp.where(glane == gi, win / cnt.astype(F32) - u, diff)
    pool = _dot(diff.astype(BF16), wbd_ref[...]) * ps_ref[...]

    def head_norm(y):
        mu = _dot_hilo(y, avg_ref[...])
        dlt = y - mu
        var = _dot_hilo(dlt * dlt, avg_ref[...])
        return dlt * lax.rsqrt(var + GN_EPS)

    rg = rg_ref[0].astype(F32)
    ret = head_norm(yf_ref[0]) * _silu(rg[:, :RET_W]) + head_norm(yb_ref[0]) * _silu(rg[:, RET_W:])

    d = x_ref.shape[-1]
    gate = lambda t: _sigmoid(g_ref[0, :, t * d:(t + 1) * d].astype(F32))
    m = (gate(0) * _dot(attn_ref[0], woa_ref[...])
         + gate(1) * _dot(pool.astype(BF16), wop_ref[...])
         + gate(2) * _dot(ret.astype(BF16), wor_ref[...]))
    xn = x_ref[0] + mod_ref[0, 2:3, :] * _dot(m.astype(BF16), wout_ref[...])
    xo_ref[0] = xn

    ms = jnp.mean(xn * xn, axis=-1, keepdims=True)
    h2 = xn * lax.rsqrt(ms + NORM_EPS) * n2_ref[...] * (1.0 + mod_ref[0, 4:5, :]) + mod_ref[0, 3:4, :]
    h2_ref[0] = h2
    hh, hl = _split(h2)
    logits = _dot(hh, wrh_ref[...]) + _dot(hl, wrh_ref[...]) + _dot(hh, wrl_ref[...]) + br_ref[...]
    lanef = lax.broadcasted_iota(I32, (TM, LANE), 1).astype(F32)
    tops, hots = [], []
    for _ in range(TOP_K):
        mx = jnp.max(logits, axis=-1, keepdims=True)
        idx = jnp.min(jnp.where(logits == mx, lanef, float(LANE)), axis=-1, keepdims=True)
        hot = lanef == idx
        logits = jnp.where(hot, NEG * 2.0, logits)
        tops.append((mx, idx))
        hots.append(hot)
    ex = [jnp.exp(mx - tops[0][0]) for mx, _ in tops]
    tot = ex[0]
    for e in ex[1:]:
        tot = tot + e
    sel = jnp.zeros((TM, LANE), F32)
    for hot in hots:
        sel = jnp.where(hot, 1.0, sel)
    incl = _dot(tri_ref[...], sel.astype(BF16))
    before = carry_ref[...] + incl - 1.0
    re = jnp.zeros((TM, LANE), F32)
    rr = jnp.zeros((TM, LANE), F32)
    rw = jnp.zeros((TM, LANE), F32)
    for k in range(TOP_K):
        rank = jnp.sum(jnp.where(hots[k], before, 0.0), axis=-1, keepdims=True)
        re = jnp.where(lanef == k, tops[k][1], re)
        rr = jnp.where(lanef == k, rank, rr)
        rw = jnp.where(lanef == k, ex[k] / tot, rw)
    re_ref[...] = re.astype(I32)
    rr_ref[...] = rr.astype(I32)
    rw_ref[...] = rw
    carry = carry_ref[...] + incl[TM - 1:TM, :]
    carry_ref[...] = carry
    cnt_ref[...] = jnp.broadcast_to(carry, cnt_ref.shape).astype(I32)


def _merge(xm, attn, u, yf, yb, rg, gates, mod_l, seq, wts):
    b, s, d = xm.shape
    nt = s // TM
    hb = TM // POOL_HALO
    nh = s // POOL_HALO
    tok = lambda width: pl.BlockSpec((1, TM, width), lambda i: (i // nt, i % nt, 0))
    full = lambda a: pl.BlockSpec(a.shape, lambda i: (0,) * a.ndim)
    lanes = pl.BlockSpec((TM, LANE), lambda i: (i, 0))
    in_specs = [
        tok(d), tok(ATT_Q),
        pl.BlockSpec((1, POOL_HALO, POOL_WIDTH), lambda i: (i // nt, jnp.maximum((i % nt) * hb - 1, 0), 0)),
        tok(POOL_WIDTH),
        pl.BlockSpec((1, POOL_HALO, POOL_WIDTH), lambda i: (i // nt, jnp.minimum((i % nt + 1) * hb, nh - 1), 0)),
        tok(RET_W), tok(RET_W), tok(2 * RET_W), tok(3 * d),
        pl.BlockSpec((1, 6, d), lambda i: (jnp.where(i % nt == 0, b, i // nt), 0, 0)),
    ] + [full(a) for a in wts]
    return pl.pallas_call(
        functools.partial(_merge_kernel, seq=seq, nt=nt),
        grid=(b * nt,),
        in_specs=in_specs,
        out_specs=[tok(d), tok(d), lanes, lanes, lanes, pl.BlockSpec((SUBLANE, LANE), lambda i: (0, 0))],
        out_shape=[jax.ShapeDtypeStruct((b, s, d), F32), jax.ShapeDtypeStruct((b, s, d), F32),
                   jax.ShapeDtypeStruct((b * s, LANE), I32), jax.ShapeDtypeStruct((b * s, LANE), I32),
                   jax.ShapeDtypeStruct((b * s, LANE), F32), jax.ShapeDtypeStruct((SUBLANE, LANE), I32)],
        scratch_shapes=[pltpu.VMEM((1, LANE), F32)],
        compiler_params=_params("arbitrary"),
    )(xm, attn, u, u, u, yf, yb, rg, gates, mod_l, *wts)


def _row_copy_out(src_ref, dst_hbm, dsm, sem, r, k):
    return pltpu.make_async_copy(src_ref.at[pl.ds(r, 1)], dst_hbm.at[pl.ds(dsm[TOP_K * r + k], 1)], sem)


def _dispatch_kernel(dest_hbm, h_ref, zero_hbm, xs_hbm, dsm, sem_idx, sem):
    del zero_hbm
    i = pl.program_id(0)
    idx = pltpu.make_async_copy(dest_hbm.at[i], dsm, sem_idx)
    idx.start()
    idx.wait()

    def issue(r, carry):
        for k in range(TOP_K):
            _row_copy_out(h_ref, xs_hbm, dsm, sem, r, k).start()
        return carry

    def drain(r, carry):
        for k in range(TOP_K):
            _row_copy_out(h_ref, xs_hbm, dsm, sem, r, k).wait()
        return carry

    lax.fori_loop(0, TM, issue, 0)
    lax.fori_loop(0, TM, drain, 0)


def _dispatch(h2, dest, rows):
    t, d = h2.shape
    ntt = t // TM
    return pl.pallas_call(
        _dispatch_kernel,
        grid=(ntt,),
        in_specs=[pl.BlockSpec(memory_space=pl.ANY),
                  pl.BlockSpec((TM, d), lambda i: (i, 0)),
                  pl.BlockSpec(memory_space=pl.ANY)],
        out_specs=pl.BlockSpec(memory_space=pl.ANY),
        out_shape=jax.ShapeDtypeStruct((rows, d), F32),
        scratch_shapes=[pltpu.SMEM((TM * TOP_K,), I32), pltpu.SemaphoreType.DMA, pltpu.SemaphoreType.DMA],
        input_output_aliases={2: 0},
        compiler_params=_params("arbitrary"),
    )(dest, h2, jnp.zeros((rows, d), F32))


def _expert_kernel(be_ref, nu_ref, x_ref, w1_ref, b1_ref, w2_ref, b2_ref, y_ref):
    del be_ref
    used = pl.program_id(0) < nu_ref[0]

    @pl.when(jnp.logical_not(used))
    def _():
        y_ref[...] = jnp.zeros_like(y_ref)

    @pl.when(used)
    def _():
        hid = _dot(x_ref[...].astype(BF16), w1_ref[0]) + b1_ref[0]
        glu = jnp.minimum(hid[:, :D_FF], SWIGLU_LIMIT)
        lin = jnp.clip(hid[:, D_FF:], -SWIGLU_LIMIT, SWIGLU_LIMIT)
        act = glu * _sigmoid(SWIGLU_ALPHA * glu) * (lin + 1.0)
        y_ref[...] = _dot(act.astype(BF16), w2_ref[0]) + b2_ref[0]


def _experts(xs, block_e, n_used, w1, b1, w2, b2):
    rows, d = xs.shape
    ne, _, f2 = w1.shape
    row_blk = lambda bi, be, nu: (bi, 0)
    grid_spec = pltpu.PrefetchScalarGridSpec(
        num_scalar_prefetch=2,
        grid=(rows // EB,),
        in_specs=[
            pl.BlockSpec((EB, d), row_blk),
            pl.BlockSpec((1, d, f2), lambda bi, be, nu: (be[bi], 0, 0)),
            pl.BlockSpec((1, 1, f2), lambda bi, be, nu: (be[bi], 0, 0)),
            pl.BlockSpec((1, f2 // 2, d), lambda bi, be, nu: (be[bi], 0, 0)),
            pl.BlockSpec((1, 1, d), lambda bi, be, nu: (be[bi], 0, 0)),
        ],
        out_specs=pl.BlockSpec((EB, d), row_blk),
    )
    return pl.pallas_call(
        _expert_kernel,
        grid_spec=grid_spec,
        out_shape=jax.ShapeDtypeStruct((rows, d), F32),
        compiler_params=_params("arbitrary"),
    )(block_e, n_used, xs, w1, b1.reshape(ne, 1, f2), w2, b2.reshape(ne, 1, d))


def _row_copy_in(src_hbm, buf, dsm, sem, r, k):
    return pltpu.make_async_copy(src_hbm.at[pl.ds(dsm[TOP_K * r + k], 1)], buf.at[k, pl.ds(r, 1)], sem)


def _combine_kernel(dest_hbm, x_ref, rw_ref, mod_ref, ys_hbm, xo_ref, dsm, buf, sem_idx, sem):
    i = pl.program_id(0)
    idx = pltpu.make_async_copy(dest_hbm.at[i], dsm, sem_idx)
    idx.start()
    idx.wait()

    def issue(r, carry):
        for k in range(TOP_K):
            _row_copy_in(ys_hbm, buf, dsm, sem, r, k).start()
        return carry

    def drain(r, carry):
        for k in range(TOP_K):
            _row_copy_in(ys_hbm, buf, dsm, sem, r, k).wait()
        return carry

    lax.fori_loop(0, TM, issue, 0)
    lax.fori_loop(0, TM, drain, 0)
    rw = rw_ref[...]
    y = rw[:, 0:1] * buf[0]
    for k in range(1, TOP_K):
        y = y + rw[:, k:k + 1] * buf[k]
    xo_ref[0] = x_ref[0] + mod_ref[0, 5:6, :] * y


def _combine(xn, rw, mod_l, ys, dest):
    b, s, d = xn.shape
    nt = s // TM
    tok = pl.BlockSpec((1, TM, d), lambda i: (i // nt, i % nt, 0))
    return pl.pallas_call(
        _combine_kernel,
        grid=(b * nt,),
        in_specs=[pl.BlockSpec(memory_space=pl.ANY), tok,
                  pl.BlockSpec((TM, LANE), lambda i: (i, 0)),
                  pl.BlockSpec((1, 6, d), lambda i: (jnp.where(i % nt == 0, b, i // nt), 0, 0)),
                  pl.BlockSpec(memory_space=pl.ANY)],
        out_specs=tok,
        out_shape=jax.ShapeDtypeStruct((b, s, d), F32),
        scratch_shapes=[pltpu.SMEM((TM * TOP_K,), I32), pltpu.VMEM((TOP_K, TM, d), F32),
                        pltpu.SemaphoreType.DMA, pltpu.SemaphoreType.DMA],
        compiler_params=_params("arbitrary"),
    )(dest, xn, rw, mod_l, ys)


def _final_kernel(x_ref, g_ref, o_ref):
    x = x_ref[0]
    ms = jnp.mean(x * x, axis=-1, keepdims=True)
    o_ref[0] = x * lax.rsqrt(ms + NORM_EPS) * g_ref[...]


def _final_norm(xm, gain, seq):
    b, _, d = xm.shape
    return pl.pallas_call(
        _final_kernel,
        grid=(b, seq // TM),
        in_specs=[pl.BlockSpec((1, TM, d), lambda bi, j: (bi, j + 1, 0)),
                  pl.BlockSpec((1, d), lambda bi, j: (0, 0))],
        out_specs=pl.BlockSpec((1, TM, d), lambda bi, j: (bi, j, 0)),
        out_shape=jax.ShapeDtypeStruct((b, seq, d), F32),
        compiler_params=_params("arbitrary", "arbitrary"),
    )(xm, gain.reshape(1, d))


def _inproj_weight(w_in_l):
    d = w_in_l.shape[0]
    sizes = (ATT_Q, N_KV_HEADS * HEAD_DIM, N_KV_HEADS * HEAD_DIM, POOL_WIDTH,
             RET_W, RET_W, RET_W, RET_W, RET_W, d, d, d)
    parts, off = [], 0
    for sz in sizes:
        parts.append(w_in_l[:, off:off + sz])
        off += sz
    q, k, v, u, rq, rk, rv, rgf, rgb, ga, gp, gr = parts
    twice = lambda w: jnp.concatenate(
        [w[:, h * HEAD_DIM:(h + 1) * HEAD_DIM] for h in range(N_KV_HEADS) for _ in range(2)], axis=1)
    cols = [q * HEAD_DIM ** -0.5, twice(k), twice(v), rq, rk * RET_DIM ** -0.5, rv, rgf, rgb, u, ga, gp, gr]
    return jnp.concatenate(cols, axis=1).astype(BF16)


def _block_diag(blocks):
    n, r, c = blocks.shape
    out = jnp.zeros((n * r, n * c), blocks.dtype)
    for g in range(n):
        out = out.at[g * r:(g + 1) * r, g * c:(g + 1) * c].set(blocks[g])
    return out


def _routing_tables(cnt, re, rr, n_blocks):
    counts = cnt[0, :N_EXPERTS]
    padded = (counts + EB - 1) // EB * EB
    pad_end = jnp.cumsum(padded)
    pad_start = pad_end - padded
    dest = pad_start[re[:, :TOP_K]] + rr[:, :TOP_K]
    n_used = pad_end[-1] // EB
    blk = jnp.minimum(jnp.arange(n_blocks), n_used - 1) * EB
    block_e = jnp.minimum(jnp.searchsorted(pad_end, blk, side="right"), N_EXPERTS - 1)
    return dest.reshape(-1, TM * TOP_K).astype(I32), block_e.astype(I32), n_used.reshape(1).astype(I32)


def kernel(x, c, ctx, c_ctx, w_mod, b_mod, norm1, norm2, w_in, attn_sinks, pool_w, pool_scale, ret_decay,
           w_o_attn, w_o_pool, w_o_ret, w_out, w_router, b_router, w_expert_in, b_expert_in, w_expert_out,
           b_expert_out, final_norm):
    b, seq, d = x.shape
    depth = w_mod.shape[0]
    assert ctx.shape[1] == TM and seq % TM == 0 and seq % GRID_W == 0
    s = seq + TM
    xm = jnp.concatenate([ctx, x], axis=1)
    mod = _modulation(c, c_ctx, w_mod, b_mod)
    tables = _rope_tables(seq)
    n_blocks = (b * s * TOP_K) // EB + N_EXPERTS
    avg = _block_diag(jnp.full((RET_HEADS, RET_DIM, RET_DIM), 1.0 / RET_DIM, F32)).astype(BF16)
    tri = (jnp.arange(TM)[:, None] >= jnp.arange(TM)[None, :]).astype(BF16)
    for l in range(depth):
        q, k2, v2, rq, rk, rv, rg, u, gates = _inproj(xm, mod[l], norm1[l], _inproj_weight(w_in[l]), tables)
        attn = _attention(q, k2, v2, attn_sinks[l], seq)
        yf, yb = _retention(rq, rk, rv, jax.nn.log_sigmoid(ret_decay[l].astype(F32)))
        wr = jnp.zeros((d, LANE), F32).at[:, :N_EXPERTS].set(w_router[l])
        wrh, wrl = _split(wr)
        br = jnp.full((1, LANE), NEG, F32).at[0, :N_EXPERTS].set(b_router[l])
        wts = (_block_diag(pool_w[l]).astype(BF16), pool_scale[l].reshape(1, -1),
               w_o_attn[l].astype(BF16), w_o_pool[l].astype(BF16), w_o_ret[l].astype(BF16),
               w_out[l].astype(BF16), norm2[l].reshape(1, d), wrh, wrl, br, avg, tri)
        xn, h2, re, rr, rw, cnt = _merge(xm, attn, u, yf, yb, rg, gates, mod[l], seq, wts)
        dest, block_e, n_used = _routing_tables(cnt, re, rr, n_blocks)
        xs = _dispatch(h2.reshape(b * s, d), dest, n_blocks * EB)
        ys = _experts(xs, block_e, n_used, w_expert_in[l].astype(BF16), b_expert_in[l],
                      w_expert_out[l].astype(BF16), b_expert_out[l])
        xm = _combine(xn, rw, mod[l], ys, dest)
    return _final_norm(xm, final_norm, seq)
```

```python
import functools

import jax
import jax.numpy as jnp
from jax import lax
from jax.experimental import pallas as pl
from jax.experimental.pallas import tpu as pltpu

F32 = jnp.float32
BF16 = jnp.bfloat16
I32 = jnp.int32
U32 = jnp.uint32

GRID_W = 64
HEAD_DIM = 64
N_Q_HEADS = 8
N_KV_HEADS = 2
WINDOW = 128
ROPE_THETA = 10000.0
POOL_WIDTH = 256
POOL_WINDOWS = (2, 4, 8, 16)
RET_HEADS = 4
RET_DIM = 64
N_EXPERTS = 32
TOP_K = 4
D_FF = 1024
SWIGLU_LIMIT = 7.0
SWIGLU_ALPHA = 1.702
NORM_EPS = 1e-6
GN_EPS = 1e-5

LANE = 128
SUBLANE = 8
TM = 256
AB = 128
CTX_BLOCKS = TM // AB
EB = 256
POOL_HALO = SUBLANE
SORT_ROWS = TM * TOP_K + N_EXPERTS * SUBLANE
NEG = -1e30
VMEM_LIMIT = 56 * 1024 * 1024

ATT_Q = N_Q_HEADS * HEAD_DIM
KV2 = 2 * N_KV_HEADS * HEAD_DIM
RET_W = RET_HEADS * RET_DIM


def _dot(a, b):
    return jnp.dot(a, b, preferred_element_type=F32)


def _dot_nt(a, b):
    return lax.dot_general(a, b, (((1,), (1,)), ((), ())), preferred_element_type=F32)


def _dot_tn(a, b):
    return lax.dot_general(a, b, (((0,), (0,)), ((), ())), preferred_element_type=F32)


def _split(a):
    hi = a.astype(BF16)
    lo = (a - hi.astype(F32)).astype(BF16)
    return hi, lo


def _dot_hilo(a, m):
    hi, lo = _split(a)
    return _dot(hi, m) + _dot(lo, m)


def _sigmoid(x):
    return 0.5 * jnp.tanh(0.5 * x) + 0.5


def _silu(x):
    return x * _sigmoid(x)


def _params(*sem):
    return pltpu.CompilerParams(dimension_semantics=sem, vmem_limit_bytes=VMEM_LIMIT)


def _mod_kernel(c_ref, w_ref, b_ref, o_ref):
    s = _silu(c_ref[...])
    sh, sl = _split(s)
    wh, wl = _split(w_ref[0])
    o_ref[0] = _dot(sh, wh) + _dot(sh, wl) + _dot(sl, wh) + b_ref[0]


def _modulation(c, c_ctx, w_mod, b_mod):
    depth, d, six_d = w_mod.shape
    b = c.shape[0]
    cc = jnp.zeros((SUBLANE, d), F32).at[:b].set(c).at[b].set(c_ctx)
    out = pl.pallas_call(
        _mod_kernel,
        grid=(depth, six_d // d),
        in_specs=[
            pl.BlockSpec((SUBLANE, d), lambda l, n: (0, 0)),
            pl.BlockSpec((1, d, d), lambda l, n: (l, 0, n)),
            pl.BlockSpec((1, 1, d), lambda l, n: (l, 0, n)),
        ],
        out_specs=pl.BlockSpec((1, SUBLANE, d), lambda l, n: (l, 0, n)),
        out_shape=jax.ShapeDtypeStruct((depth, SUBLANE, six_d), F32),
        compiler_params=_params("arbitrary", "arbitrary"),
    )(cc, w_mod, b_mod.reshape(depth, 1, six_d))
    return out[:, : b + 1].reshape(depth, b + 1, 6, d)


def _rope(x, cos, sin, half, first):
    partner = jnp.where(first, pltpu.roll(x, LANE - half, 1), pltpu.roll(x, half, 1))
    return x * cos + partner * sin


def _inproj_kernel(x_ref, mod_ref, n1_ref, w_ref, ac_ref, as_ref, rc_ref, rs_ref,
                   q_ref, k_ref, v_ref, rq_ref, rk_ref, rv_ref, rg_ref, u_ref, g_ref):
    x = x_ref[0]
    ms = jnp.mean(x * x, axis=-1, keepdims=True)
    y = x * lax.rsqrt(ms + NORM_EPS) * n1_ref[...]
    h = y * (1.0 + mod_ref[0, 1:2, :]) + mod_ref[0, 0:1, :]
    hb = h.astype(BF16)
    lane = lax.broadcasted_iota(I32, (TM, LANE), 1)
    a_first = (lane % (HEAD_DIM // 2)) < (HEAD_DIM // 4)
    r_first = (lane % RET_DIM) < (RET_DIM // 2)
    ac, asn, rc, rsn = ac_ref[...], as_ref[...], rc_ref[...], rs_ref[...]

    def proj(off, width):
        return _dot(hb, w_ref[:, off:off + width])

    off = 0
    for t in range(ATT_Q // LANE):
        q_ref[0, :, t * LANE:(t + 1) * LANE] = _rope(
            proj(off, LANE), ac, asn, HEAD_DIM // 4, a_first).astype(BF16)
        off += LANE
    for t in range(KV2 // LANE):
        k_ref[0, :, t * LANE:(t + 1) * LANE] = _rope(
            proj(off, LANE), ac, asn, HEAD_DIM // 4, a_first).astype(BF16)
        off += LANE
    v_ref[0] = proj(off, KV2).astype(BF16)
    off += KV2
    for ref in (rq_ref, rk_ref):
        for t in range(RET_W // LANE):
            ref[0, :, t * LANE:(t + 1) * LANE] = _rope(
                proj(off, LANE), rc, rsn, RET_DIM // 2, r_first).astype(BF16)
            off += LANE
    rv_ref[0] = proj(off, RET_W).astype(BF16)
    off += RET_W
    rg_ref[0] = proj(off, 2 * RET_W).astype(BF16)
    off += 2 * RET_W
    u_ref[0] = proj(off, POOL_WIDTH)
    off += POOL_WIDTH
    d = x.shape[-1]
    for t in range(3):
        g_ref[0, :, t * d:(t + 1) * d] = proj(off, d).astype(BF16)
        off += d


def _inproj(xm, mod_l, norm1_l, w1, tables):
    b, s, d = xm.shape
    nt = s // TM
    wcols = w1.shape[1]
    tok = lambda width: pl.BlockSpec((1, TM, width), lambda bi, j: (bi, j, 0))
    tab = pl.BlockSpec((TM, LANE), lambda bi, j: (j, 0))
    widths = (ATT_Q, KV2, KV2, RET_W, RET_W, RET_W, 2 * RET_W, POOL_WIDTH, 3 * d)
    dtypes = (BF16,) * 7 + (F32, BF16)
    return pl.pallas_call(
        _inproj_kernel,
        grid=(b, nt),
        in_specs=[
            tok(d),
            pl.BlockSpec((1, 6, d), lambda bi, j: (jnp.where(j == 0, b, bi), 0, 0)),
            pl.BlockSpec((1, d), lambda bi, j: (0, 0)),
            pl.BlockSpec((d, wcols), lambda bi, j: (0, 0)),
            tab, tab, tab, tab,
        ],
        out_specs=[tok(w) for w in widths],
        out_shape=[jax.ShapeDtypeStruct((b, s, w), dt) for w, dt in zip(widths, dtypes)],
        compiler_params=_params("arbitrary", "arbitrary"),
    )(xm, mod_l, norm1_l.reshape(1, d), w1, *tables)


def _rope_tables(seq):
    rows = seq // GRID_W
    rpos = jnp.arange(rows, dtype=F32)[:, None]
    cpos = jnp.arange(GRID_W, dtype=F32)[:, None]
    lane = jnp.arange(LANE)
    grid = lambda per_row, per_col: (per_row[:, None, :] + per_col[None, :, :]).reshape(seq, LANE)
    axis_dim = HEAD_DIM // 2
    inv_a = ROPE_THETA ** (-jnp.arange(0, axis_dim, 2, dtype=F32) / axis_dim)
    hl = lane % HEAD_DIM
    inv_al = inv_a[(hl % axis_dim) % (axis_dim // 2)][None, :]
    by_row = (hl < axis_dim)[None, :]
    a_sign = jnp.where((hl % axis_dim) < axis_dim // 2, -1.0, 1.0)[None, :]
    a_cos = grid(jnp.where(by_row, jnp.cos(rpos * inv_al), 0.0), jnp.where(by_row, 0.0, jnp.cos(cpos * inv_al)))
    a_sin = grid(jnp.where(by_row, jnp.sin(rpos * inv_al), 0.0), jnp.where(by_row, 0.0, jnp.sin(cpos * inv_al)))
    inv_r = 1.0 / (ROPE_THETA ** jnp.linspace(0.0, 1.0, RET_DIM // 2, dtype=F32))
    rl = lane % RET_DIM
    inv_rl = inv_r[rl % (RET_DIM // 2)][None, :]
    r_sign = jnp.where(rl < RET_DIM // 2, -1.0, 1.0)[None, :]
    hi = (rpos * GRID_W) * inv_rl
    lo = cpos * inv_rl
    outer = lambda a, b: (a[:, None, :] * b[None, :, :]).reshape(seq, LANE)
    r_cos = outer(jnp.cos(hi), jnp.cos(lo)) - outer(jnp.sin(hi), jnp.sin(lo))
    r_sin = outer(jnp.sin(hi), jnp.cos(lo)) + outer(jnp.cos(hi), jnp.sin(lo))
    ones = jnp.ones((TM, LANE), F32)
    zeros = jnp.zeros((TM, LANE), F32)
    cat = lambda head, body: jnp.concatenate([head, body], axis=0)
    return (cat(ones, a_cos), cat(zeros, a_sin * a_sign), cat(ones, r_cos), cat(zeros, r_sin * r_sign))


def _attn_kernel(sink_ref, bias_ref, q_ref, kp_ref, kc_ref, kn_ref, kx_ref, vp_ref, vc_ref, vn_ref, vx_ref, o_ref):
    g_heads = N_Q_HEADS // N_KV_HEADS
    lane = lax.broadcasted_iota(I32, (AB, LANE), 1)
    lo = lane < HEAD_DIM
    rows = lax.broadcasted_iota(I32, (g_heads * AB, 1), 0)
    bias = bias_ref[0]
    outs = []
    for g in range(N_KV_HEADS):
        ks = slice(g * LANE, (g + 1) * LANE)
        kd = jnp.concatenate([kp_ref[0, :, ks], kc_ref[0, :, ks], kn_ref[0, :, ks], kx_ref[0, :, ks]], axis=0)
        vd = jnp.concatenate([vp_ref[0, :, ks], vc_ref[0, :, ks], vn_ref[0, :, ks], vx_ref[0, :, ks]], axis=0)
        qs = []
        for c in range(2 * g, 2 * g + 2):
            qc = q_ref[0, :, c * LANE:(c + 1) * LANE]
            zero = jnp.zeros_like(qc)
            qs += [jnp.where(lo, qc, zero), jnp.where(lo, zero, qc)]
        s = _dot_nt(jnp.concatenate(qs, axis=0), kd) + bias
        sink = jnp.full((g_heads * AB, 1), sink_ref[g_heads * g + g_heads - 1], F32)
        for t in range(g_heads - 2, -1, -1):
            sink = jnp.where(rows < (t + 1) * AB, sink_ref[g_heads * g + t], sink)
        m = jnp.maximum(jnp.max(s, axis=-1, keepdims=True), sink)
        p = jnp.exp(s - m)
        den = jnp.sum(p, axis=-1, keepdims=True) + jnp.exp(sink - m)
        o = _dot(p.astype(BF16), vd) / den
        for t in range(2):
            outs.append(jnp.where(lo, o[2 * t * AB:(2 * t + 1) * AB], o[(2 * t + 1) * AB:(2 * t + 2) * AB]))
    o_ref[0] = jnp.concatenate(outs, axis=-1).astype(BF16)


def _attn_bias():
    g_heads = N_Q_HEADS // N_KV_HEADS
    r = jnp.arange(g_heads * AB)[:, None] % AB
    j = jnp.arange(3 * AB + TM)[None, :]
    band = jnp.abs(j - AB - r) <= WINDOW
    is_ctx = j >= 3 * AB
    variants = (band, band & (j >= AB), band & (j < 2 * AB), jnp.zeros_like(band))
    return jnp.stack([jnp.where(v | is_ctx, 0.0, NEG) for v in variants]).astype(F32)


def _attention(q, k2, v2, sinks):
    b, s, _ = q.shape
    nb = s // AB
    assert nb - CTX_BLOCKS >= 2
    bias = _attn_bias()
    variant = lambda i: jnp.where(i < CTX_BLOCKS, 3, jnp.where(i == CTX_BLOCKS, 1, jnp.where(i == nb - 1, 2, 0)))
    prev = pl.BlockSpec((1, AB, KV2), lambda bi, i: (bi, jnp.maximum(i - 1, 0), 0))
    cur = pl.BlockSpec((1, AB, KV2), lambda bi, i: (bi, i, 0))
    nxt = pl.BlockSpec((1, AB, KV2), lambda bi, i: (bi, jnp.minimum(i + 1, nb - 1), 0))
    cx = pl.BlockSpec((1, TM, KV2), lambda bi, i: (bi, 0, 0))
    return pl.pallas_call(
        _attn_kernel,
        grid=(b, nb),
        in_specs=[pl.BlockSpec(memory_space=pltpu.SMEM),
                  pl.BlockSpec((1,) + bias.shape[1:], lambda bi, i: (variant(i), 0, 0)),
                  pl.BlockSpec((1, AB, ATT_Q), lambda bi, i: (bi, i, 0)),
                  prev, cur, nxt, cx, prev, cur, nxt, cx],
        out_specs=pl.BlockSpec((1, AB, ATT_Q), lambda bi, i: (bi, i, 0)),
        out_shape=jax.ShapeDtypeStruct((b, s, ATT_Q), BF16),
        compiler_params=_params("arbitrary", "arbitrary"),
    )(sinks, bias, q, k2, k2, k2, k2, v2, v2, v2, v2)


def _ret_kernel(lg_ref, qf_ref, kf_ref, vf_ref, qb_ref, kb_ref, vb_ref, yf_ref, yb_ref,
                st_ref, dm_ref, qd_ref, kd_ref, cd_ref, *, batch):
    step = pl.program_id(0)
    lane = lax.broadcasted_iota(I32, (AB, LANE), 1)
    row = lax.broadcasted_iota(I32, (AB, LANE), 0)
    lo = lane < RET_DIM
    tiles = RET_W // LANE

    @pl.when(step == 0)
    def _():
        st_ref[...] = jnp.zeros_like(st_ref)
        ii = row.astype(F32)
        jj = lane.astype(F32)
        for d in range(2):
            for c in range(tiles):
                lg0 = lg_ref[d * RET_HEADS + 2 * c]
                lg1 = lg_ref[d * RET_HEADS + 2 * c + 1]
                lgl = jnp.where(lo, lg0, lg1)
                q_exp = ii + 1.0 if d == 0 else AB - ii
                k_exp = (AB - 1.0) - ii if d == 0 else ii
                qd_ref[d * tiles + c] = jnp.exp(q_exp * lgl)
                kd_ref[d * tiles + c] = jnp.exp(k_exp * lgl)
                cd_ref[d * tiles + c] = jnp.exp(AB * lgl)
                rel = ii - jj if d == 0 else jj - ii
                for hh, lgh in enumerate((lg0, lg1)):
                    dm_ref[d * tiles + c, hh * AB:(hh + 1) * AB, :] = jnp.where(
                        rel >= 0, jnp.exp(jnp.maximum(rel, 0.0) * lgh), 0.0)

    same_head = (row < RET_DIM) == lo
    dirs = ((qf_ref, kf_ref, vf_ref, yf_ref), (qb_ref, kb_ref, vb_ref, yb_ref))
    for d, (q_ref, k_ref, v_ref, y_ref) in enumerate(dirs):
        for b in range(batch):
            for c in range(tiles):
                sl = slice(c * LANE, (c + 1) * LANE)
                t = d * tiles + c
                q = q_ref[b, :, sl]
                k = k_ref[b, :, sl]
                v = v_ref[b, :, sl]
                zero = jnp.zeros_like(q)
                q2 = jnp.concatenate([jnp.where(lo, q, zero), jnp.where(lo, zero, q)], axis=0)
                p = (_dot_nt(q2, k) * dm_ref[t]).astype(BF16)
                y_intra = jnp.where(lo, _dot(p[:AB], v), _dot(p[AB:], v))
                si = (d * batch + b) * tiles + c
                state = st_ref[si]
                q_dec = (q.astype(F32) * qd_ref[t]).astype(BF16)
                y_ref[b, :, sl] = y_intra + _dot(q_dec, state.astype(BF16))
                k_dec = (k.astype(F32) * kd_ref[t]).astype(BF16)
                st_ref[si] = state * cd_ref[t] + jnp.where(same_head, _dot_tn(k_dec, v), 0.0)


def _retention(rq, rk, rv, log_g):
    b, s, w = rq.shape
    nb = s // AB
    tiles = w // LANE

    def back(i):
        return jnp.where(i < CTX_BLOCKS, CTX_BLOCKS - 1 - i, nb - 1 + CTX_BLOCKS - i)

    fwd = pl.BlockSpec((b, AB, w), lambda i: (0, i, 0))
    bwd = pl.BlockSpec((b, AB, w), lambda i: (0, back(i), 0))
    return pl.pallas_call(
        functools.partial(_ret_kernel, batch=b),
        grid=(nb,),
        in_specs=[pl.BlockSpec(memory_space=pltpu.SMEM), fwd, fwd, fwd, bwd, bwd, bwd],
        out_specs=[fwd, bwd],
        out_shape=[jax.ShapeDtypeStruct((b, s, w), F32)] * 2,
        scratch_shapes=[
            pltpu.VMEM((2 * b * tiles, LANE, LANE), F32),
            pltpu.VMEM((2 * tiles, 2 * AB, LANE), F32),
            pltpu.VMEM((2 * tiles, AB, LANE), F32),
            pltpu.VMEM((2 * tiles, AB, LANE), F32),
            pltpu.VMEM((2 * tiles, AB, LANE), F32),
        ],
        compiler_params=_params("arbitrary"),
    )(log_g.reshape(-1), rq, rk, rv, rq, rk, rv)


def _merge_kernel(x_ref, attn_ref, up_ref, uc_ref, un_ref, yf_ref, yb_ref, rg_ref, g_ref, mod_ref,
                  wbd_ref, ps_ref, woa_ref, wop_ref, wor_ref, wout_ref, n2_ref, wrh_ref, wrl_ref,
                  br_ref, avg_ref, tri_ref,
                  xo_ref, h2_ref, ls_ref, lst_ref, rw_ref, cnt_ref, *, seq, nt):
    i = pl.program_id(0)
    j = i % nt

    ext = TM + 2 * POOL_HALO
    seq_len = jnp.where(j == 0, TM, seq)
    start = jnp.where(j == 0, 0, (j - 1) * TM)
    u = uc_ref[0]
    ue = jnp.concatenate([up_ref[0], u, un_ref[0]], axis=0)
    erow = lax.broadcasted_iota(I32, (ext, 1), 0) + (start - POOL_HALO)
    ue = jnp.where((erow >= 0) & (erow < seq_len), ue, 0.0)
    t_pos = lax.broadcasted_iota(I32, (TM, 1), 0) + start
    glane = lax.broadcasted_iota(I32, (TM, POOL_WIDTH), 1) // (POOL_WIDTH // len(POOL_WINDOWS))
    run = ue
    width = 1
    diff = jnp.zeros((TM, POOL_WIDTH), F32)
    for gi, w in enumerate(POOL_WINDOWS):
        while width < w:
            run = run + pltpu.roll(run, ext - width, 0)
            width *= 2
        win = pltpu.roll(run, w // 2, 0)[POOL_HALO:POOL_HALO + TM]
        cnt = jnp.minimum(t_pos - w // 2 + w, seq_len) - jnp.maximum(t_pos - w // 2, 0)
        diff = jnp.where(glane == gi, win / cnt.astype(F32) - u, diff)
    pool = _dot(diff.astype(BF16), wbd_ref[...]) * ps_ref[...]

    def head_norm(y):
        mu = _dot_hilo(y, avg_ref[...])
        dlt = y - mu
        var = _dot_hilo(dlt * dlt, avg_ref[...])
        return dlt * lax.rsqrt(var + GN_EPS)

    rg = rg_ref[0].astype(F32)
    ret = head_norm(yf_ref[0]) * _silu(rg[:, :RET_W]) + head_norm(yb_ref[0]) * _silu(rg[:, RET_W:])

    d = x_ref.shape[-1]
    gate = lambda t: _sigmoid(g_ref[0, :, t * d:(t + 1) * d].astype(F32))
    m = (gate(0) * _dot(attn_ref[0], woa_ref[...])
         + gate(1) * _dot(pool.astype(BF16), wop_ref[...])
         + gate(2) * _dot(ret.astype(BF16), wor_ref[...]))
    xn = x_ref[0] + mod_ref[0, 2:3, :] * _dot(m.astype(BF16), wout_ref[...])
    xo_ref[0] = xn

    ms = jnp.mean(xn * xn, axis=-1, keepdims=True)
    h2 = xn * lax.rsqrt(ms + NORM_EPS) * n2_ref[...] * (1.0 + mod_ref[0, 4:5, :]) + mod_ref[0, 3:4, :]
    h2_ref[0] = h2.astype(BF16)
    hh, hl = _split(h2)
    logits = _dot(hh, wrh_ref[...]) + _dot(hl, wrh_ref[...]) + _dot(hh, wrl_ref[...]) + br_ref[...]
    lanef = lax.broadcasted_iota(I32, (TM, LANE), 1).astype(F32)
    tops, hots = [], []
    for _ in range(TOP_K):
        mx = jnp.max(logits, axis=-1, keepdims=True)
        idx = jnp.min(jnp.where(logits == mx, lanef, float(LANE)), axis=-1, keepdims=True)
        hot = lanef == idx
        logits = jnp.where(hot, NEG * 2.0, logits)
        tops.append((mx, idx))
        hots.append(hot)
    ex = [jnp.exp(mx - tops[0][0]) for mx, _ in tops]
    tot = ex[0]
    for e in ex[1:]:
        tot = tot + e
    sel = jnp.zeros((TM, LANE), F32)
    for hot in hots:
        sel = jnp.where(hot, 1.0, sel)
    incl = _dot(tri_ref[...], sel.astype(BF16))
    cnt = incl[TM - 1:TM, :]
    run = jnp.floor((cnt + (SUBLANE - 1.0)) * (1.0 / SUBLANE)) * SUBLANE
    er = lax.broadcasted_iota(I32, (LANE, LANE), 0)
    ec = lax.broadcasted_iota(I32, (LANE, LANE), 1)
    earlier = jnp.where(er < ec, 1.0, 0.0).astype(BF16)
    run_start = _dot(jnp.broadcast_to(run, (SUBLANE, LANE)).astype(BF16), earlier)[0:1]
    slot = run_start + incl - 1.0
    ls = jnp.zeros((TM, LANE), F32)
    rw = jnp.zeros((TM, LANE), F32)
    for k in range(TOP_K):
        ls = jnp.where(lanef == k, jnp.sum(jnp.where(hots[k], slot, 0.0), axis=-1, keepdims=True), ls)
        rw = jnp.where(lanef == k, ex[k] / tot, rw)
    ls = jnp.where(lanef < TOP_K, ls, -1.0)
    ls_ref[...] = ls.astype(I32)
    lst_ref[...] = jnp.transpose(ls)[:SUBLANE, :].astype(I32)
    rw_ref[...] = rw
    cnt_ref[...] = jnp.broadcast_to(cnt, cnt_ref.shape).astype(I32)


def _merge(xm, attn, u, yf, yb, rg, gates, mod_l, seq, wts):
    b, s, d = xm.shape
    nt = s // TM
    ntt = b * nt
    hb = TM // POOL_HALO
    nh = s // POOL_HALO
    tok = lambda width: pl.BlockSpec((1, TM, width), lambda i: (i // nt, i % nt, 0))
    full = lambda a: pl.BlockSpec(a.shape, lambda i: (0,) * a.ndim)
    lanes = pl.BlockSpec((TM, LANE), lambda i: (i, 0))
    in_specs = [
        tok(d), tok(ATT_Q),
        pl.BlockSpec((1, POOL_HALO, POOL_WIDTH), lambda i: (i // nt, jnp.maximum((i % nt) * hb - 1, 0), 0)),
        tok(POOL_WIDTH),
        pl.BlockSpec((1, POOL_HALO, POOL_WIDTH), lambda i: (i // nt, jnp.minimum((i % nt + 1) * hb, nh - 1), 0)),
        tok(RET_W), tok(RET_W), tok(2 * RET_W), tok(3 * d),
        pl.BlockSpec((1, 6, d), lambda i: (jnp.where(i % nt == 0, b, i // nt), 0, 0)),
    ] + [full(a) for a in wts]
    return pl.pallas_call(
        functools.partial(_merge_kernel, seq=seq, nt=nt),
        grid=(ntt,),
        in_specs=in_specs,
        out_specs=[tok(d), tok(d), lanes, pl.BlockSpec((SUBLANE, TM), lambda i: (i, 0)), lanes,
                   pl.BlockSpec((SUBLANE, LANE), lambda i: (i, 0))],
        out_shape=[jax.ShapeDtypeStruct((b, s, d), F32), jax.ShapeDtypeStruct((b, s, d), BF16),
                   jax.ShapeDtypeStruct((b * s, LANE), I32), jax.ShapeDtypeStruct((ntt * SUBLANE, TM), I32),
                   jax.ShapeDtypeStruct((b * s, LANE), F32), jax.ShapeDtypeStruct((ntt * SUBLANE, LANE), I32)],
        compiler_params=_params("arbitrary"),
    )(xm, attn, u, u, u, yf, yb, rg, gates, mod_l, *wts)


def _pack_pairs(v):
    half = v.shape[-1] // 2
    return pltpu.bitcast(v[:, :half], U32) | (pltpu.bitcast(v[:, half:], U32) >> 16)


def _unpack_pairs(u):
    return (pltpu.bitcast(u & jnp.uint32(0xFFFF0000), F32).astype(BF16),
            pltpu.bitcast(u << 16, F32).astype(BF16))


def _each_chunk(count_ref, base, fn):
    def per_expert(e, carry):
        def per_chunk(c, inner):
            fn(e, c)
            return inner

        return lax.fori_loop(0, count_ref[base + e], per_chunk, carry)

    lax.fori_loop(0, N_EXPERTS, per_expert, 0)


def _rows(tile_index):
    return pl.ds(pl.multiple_of(tile_index * SUBLANE, SUBLANE), SUBLANE)


def _dispatch_kernel(off_ref, ls0_ref, n8_ref, fs_ref, fn_ref, h_ref, lst_ref, xs_hbm, buf, zbuf, sem):
    i = pl.program_id(0)
    base = i * N_EXPERTS

    def fill_copy(e, c):
        return pltpu.make_async_copy(zbuf, xs_hbm.at[_rows(fs_ref[e] + c)], sem)

    def run_copy(e, c):
        return pltpu.make_async_copy(buf.at[_rows(ls0_ref[base + e] + c)],
                                     xs_hbm.at[_rows(off_ref[base + e] + c)], sem)

    @pl.when(i == 0)
    def _():
        zbuf[...] = jnp.zeros_like(zbuf)
        _each_chunk(fn_ref, 0, lambda e, c: fill_copy(e, c).start())
        _each_chunk(fn_ref, 0, lambda e, c: fill_copy(e, c).wait())

    slot = lax.broadcasted_iota(I32, (SORT_ROWS, TM), 0)
    lst = lst_ref[...]
    p = jnp.zeros((SORT_ROWS, TM), F32)
    for k in range(TOP_K):
        p = jnp.where(slot == lst[k:k + 1, :], 1.0, p)
    buf[...] = _pack_pairs(_dot(p.astype(BF16), h_ref[...]))
    _each_chunk(n8_ref, base, lambda e, c: run_copy(e, c).start())
    _each_chunk(n8_ref, base, lambda e, c: run_copy(e, c).wait())


def _dispatch(h2, lst, tabs, rows):
    t, d = h2.shape
    ntt = t // TM
    grid_spec = pltpu.PrefetchScalarGridSpec(
        num_scalar_prefetch=5,
        grid=(ntt,),
        in_specs=[pl.BlockSpec((TM, d), lambda i, *_: (i, 0)),
                  pl.BlockSpec((SUBLANE, TM), lambda i, *_: (i, 0))],
        out_specs=pl.BlockSpec(memory_space=pl.ANY),
        scratch_shapes=[pltpu.VMEM((SORT_ROWS, d // 2), U32), pltpu.VMEM((SUBLANE, d // 2), U32),
                        pltpu.SemaphoreType.DMA],
    )
    return pl.pallas_call(
        _dispatch_kernel,
        grid_spec=grid_spec,
        out_shape=jax.ShapeDtypeStruct((rows, d // 2), U32),
        compiler_params=_params("arbitrary"),
    )(tabs["off"], tabs["ls0"], tabs["n8"], tabs["fill_start"], tabs["fill_n"], h2, lst)


def _expert_kernel(be_ref, nu_ref, x_ref, w1_ref, b1_ref, w2_ref, b2_ref, y_ref, w1b, w2b):
    bi = pl.program_id(0)
    used = bi < nu_ref[0]
    fresh = (bi == 0) | (be_ref[bi] != be_ref[jnp.maximum(bi - 1, 0)])

    @pl.when(used & fresh)
    def _():
        w1b[...] = w1_ref[0, 0].astype(BF16)
        w2b[...] = w2_ref[0, 0].astype(BF16)

    @pl.when(jnp.logical_not(used))
    def _():
        y_ref[...] = jnp.zeros_like(y_ref)

    @pl.when(used)
    def _():
        xa, xb = _unpack_pairs(x_ref[...])
        half = xa.shape[-1]
        hid = _dot(xa, w1b[:half]) + _dot(xb, w1b[half:]) + b1_ref[0, 0]
        glu = jnp.minimum(hid[:, :D_FF], SWIGLU_LIMIT)
        lin = jnp.clip(hid[:, D_FF:], -SWIGLU_LIMIT, SWIGLU_LIMIT)
        act = glu * _sigmoid(SWIGLU_ALPHA * glu) * (lin + 1.0)
        y = _dot(act.astype(BF16), w2b[...]) + b2_ref[0, 0]
        y_ref[...] = _pack_pairs(y.astype(BF16).astype(F32))


def _experts(xs, block_e, n_used, w1, b1, w2, b2, layer):
    rows, half = xs.shape
    depth, ne, d, f2 = w1.shape
    row_blk = lambda bi, be, nu: (bi, 0)
    grid_spec = pltpu.PrefetchScalarGridSpec(
        num_scalar_prefetch=2,
        grid=(rows // EB,),
        in_specs=[
            pl.BlockSpec((EB, half), row_blk),
            pl.BlockSpec((1, 1, d, f2), lambda bi, be, nu: (layer, be[bi], 0, 0)),
            pl.BlockSpec((1, 1, 1, f2), lambda bi, be, nu: (layer, be[bi], 0, 0)),
            pl.BlockSpec((1, 1, f2 // 2, d), lambda bi, be, nu: (layer, be[bi], 0, 0)),
            pl.BlockSpec((1, 1, 1, d), lambda bi, be, nu: (layer, be[bi], 0, 0)),
        ],
        out_specs=pl.BlockSpec((EB, half), row_blk),
        scratch_shapes=[pltpu.VMEM((d, f2), BF16), pltpu.VMEM((f2 // 2, d), BF16)],
    )
    return pl.pallas_call(
        _expert_kernel,
        grid_spec=grid_spec,
        out_shape=jax.ShapeDtypeStruct((rows, half), U32),
        compiler_params=_params("arbitrary"),
    )(block_e, n_used, xs, w1, b1.reshape(depth, ne, 1, f2), w2, b2.reshape(depth, ne, 1, d))


def _combine_kernel(off_ref, ls0_ref, n8_ref, x_ref, ls_ref, rw_ref, mod_ref, ys_hbm, xo_ref, buf, sem):
    i = pl.program_id(0)
    base = i * N_EXPERTS

    def run_copy(e, c):
        return pltpu.make_async_copy(ys_hbm.at[_rows(off_ref[base + e] + c)],
                                     buf.at[_rows(ls0_ref[base + e] + c)], sem)

    @pl.when(i == 0)
    def _():
        buf[...] = jnp.zeros_like(buf)

    _each_chunk(n8_ref, base, lambda e, c: run_copy(e, c).start())
    _each_chunk(n8_ref, base, lambda e, c: run_copy(e, c).wait())
    ya, yb = _unpack_pairs(buf[...])
    slot = lax.broadcasted_iota(I32, (TM, SORT_ROWS), 1)
    ls = ls_ref[...]
    rw = rw_ref[...]
    g = jnp.zeros((TM, SORT_ROWS), F32)
    for k in range(TOP_K):
        g = jnp.where(slot == ls[:, k:k + 1], rw[:, k:k + 1], g)
    gh, gl = _split(g)
    y = jnp.concatenate([_dot(gh, ya) + _dot(gl, ya), _dot(gh, yb) + _dot(gl, yb)], axis=-1)
    xo_ref[0] = x_ref[0] + mod_ref[0, 5:6, :] * y


def _combine(xn, ls, rw, mod_l, ys, tabs):
    b, s, d = xn.shape
    nt = s // TM
    tok = pl.BlockSpec((1, TM, d), lambda i, *_: (i // nt, i % nt, 0))
    lanes = pl.BlockSpec((TM, LANE), lambda i, *_: (i, 0))
    grid_spec = pltpu.PrefetchScalarGridSpec(
        num_scalar_prefetch=3,
        grid=(b * nt,),
        in_specs=[tok, lanes, lanes,
                  pl.BlockSpec((1, 6, d), lambda i, *_: (jnp.where(i % nt == 0, b, i // nt), 0, 0)),
                  pl.BlockSpec(memory_space=pl.ANY)],
        out_specs=tok,
        scratch_shapes=[pltpu.VMEM((SORT_ROWS, d // 2), U32), pltpu.SemaphoreType.DMA],
    )
    return pl.pallas_call(
        _combine_kernel,
        grid_spec=grid_spec,
        out_shape=jax.ShapeDtypeStruct((b, s, d), F32),
        compiler_params=_params("arbitrary"),
    )(tabs["off"], tabs["ls0"], tabs["n8"], xn, ls, rw, mod_l, ys)


def _final_kernel(x_ref, g_ref, o_ref):
    x = x_ref[0]
    ms = jnp.mean(x * x, axis=-1, keepdims=True)
    o_ref[0] = x * lax.rsqrt(ms + NORM_EPS) * g_ref[...]


def _final_norm(xm, gain, seq):
    b, _, d = xm.shape
    return pl.pallas_call(
        _final_kernel,
        grid=(b, seq // TM),
        in_specs=[pl.BlockSpec((1, TM, d), lambda bi, j: (bi, j + 1, 0)),
                  pl.BlockSpec((1, d), lambda bi, j: (0, 0))],
        out_specs=pl.BlockSpec((1, TM, d), lambda bi, j: (bi, j, 0)),
        out_shape=jax.ShapeDtypeStruct((b, seq, d), F32),
        compiler_params=_params("arbitrary", "arbitrary"),
    )(xm, gain.reshape(1, d))


def _inproj_weight(w_in_l):
    d = w_in_l.shape[0]
    sizes = (ATT_Q, N_KV_HEADS * HEAD_DIM, N_KV_HEADS * HEAD_DIM, POOL_WIDTH,
             RET_W, RET_W, RET_W, RET_W, RET_W, d, d, d)
    parts, off = [], 0
    for sz in sizes:
        parts.append(w_in_l[:, off:off + sz])
        off += sz
    q, k, v, u, rq, rk, rv, rgf, rgb, ga, gp, gr = parts
    twice = lambda w: jnp.concatenate(
        [w[:, h * HEAD_DIM:(h + 1) * HEAD_DIM] for h in range(N_KV_HEADS) for _ in range(2)], axis=1)
    cols = [q * HEAD_DIM ** -0.5, twice(k), twice(v), rq, rk * RET_DIM ** -0.5, rv, rgf, rgb, u, ga, gp, gr]
    return jnp.concatenate(cols, axis=1).astype(BF16)


def _block_diag(blocks):
    n, r, c = blocks.shape
    out = jnp.zeros((n * r, n * c), blocks.dtype)
    for g in range(n):
        out = out.at[g * r:(g + 1) * r, g * c:(g + 1) * c].set(blocks[g])
    return out


def _routing_tables(cnt, n_blocks):
    ntt = cnt.shape[0] // SUBLANE
    counts = cnt.reshape(ntt, SUBLANE, LANE)[:, 0, :N_EXPERTS]
    run = (counts + SUBLANE - 1) // SUBLANE * SUBLANE
    total = jnp.sum(run, axis=0)
    padded = (total + EB - 1) // EB * EB
    pad_end = jnp.cumsum(padded)
    pad_start = pad_end - padded
    off = pad_start[None, :] + jnp.cumsum(run, axis=0) - run
    ls0 = jnp.cumsum(run, axis=1) - run
    n_used = pad_end[-1] // EB
    blk = jnp.minimum(jnp.arange(n_blocks), n_used - 1) * EB
    block_e = jnp.minimum(jnp.sum(pad_end[None, :] <= blk[:, None], axis=1), N_EXPERTS - 1)
    tiles = lambda a: (a // SUBLANE).reshape(-1).astype(I32)
    tabs = dict(off=tiles(off), ls0=tiles(ls0), n8=tiles(run),
                fill_start=tiles(pad_start + total), fill_n=tiles(padded - total))
    return tabs, block_e.astype(I32), n_used.reshape(1).astype(I32)


def kernel(x, c, ctx, c_ctx, w_mod, b_mod, norm1, norm2, w_in, attn_sinks, pool_w, pool_scale, ret_decay,
           w_o_attn, w_o_pool, w_o_ret, w_out, w_router, b_router, w_expert_in, b_expert_in, w_expert_out,
           b_expert_out, final_norm):
    b, seq, d = x.shape
    depth = w_mod.shape[0]
    assert ctx.shape[1] == TM and seq % TM == 0 and seq % GRID_W == 0
    s = seq + TM
    xm = jnp.concatenate([ctx, x], axis=1)
    mod = _modulation(c, c_ctx, w_mod, b_mod)
    tables = _rope_tables(seq)
    ntt = b * s // TM
    n_blocks = -(-(b * s * TOP_K + ntt * N_EXPERTS * (SUBLANE - 1)) // EB) + N_EXPERTS
    avg = _block_diag(jnp.full((RET_HEADS, RET_DIM, RET_DIM), 1.0 / RET_DIM, F32)).astype(BF16)
    tri = (jnp.arange(TM)[:, None] >= jnp.arange(TM)[None, :]).astype(BF16)
    for l in range(depth):
        q, k2, v2, rq, rk, rv, rg, u, gates = _inproj(xm, mod[l], norm1[l], _inproj_weight(w_in[l]), tables)
        attn = _attention(q, k2, v2, attn_sinks[l])
        yf, yb = _retention(rq, rk, rv, jax.nn.log_sigmoid(ret_decay[l].astype(F32)))
        wr = jnp.zeros((d, LANE), F32).at[:, :N_EXPERTS].set(w_router[l])
        wrh, wrl = _split(wr)
        br = jnp.full((1, LANE), NEG, F32).at[0, :N_EXPERTS].set(b_router[l])
        wts = (_block_diag(pool_w[l]).astype(BF16), pool_scale[l].reshape(1, -1),
               w_o_attn[l].astype(BF16), w_o_pool[l].astype(BF16), w_o_ret[l].astype(BF16),
               w_out[l].astype(BF16), norm2[l].reshape(1, d), wrh, wrl, br, avg, tri)
        xn, h2, ls, lst, rw, cnt = _merge(xm, attn, u, yf, yb, rg, gates, mod[l], seq, wts)
        tabs, block_e, n_used = _routing_tables(cnt, n_blocks)
        xs = _dispatch(h2.reshape(b * s, d), lst, tabs, n_blocks * EB)
        ys = _experts(xs, block_e, n_used, w_expert_in, b_expert_in, w_expert_out, b_expert_out, l)
        xm = _combine(xn, ls, rw, mod[l], ys, tabs)
    return _final_norm(xm, final_norm, seq)
```

```python
import functools

import jax
import jax.numpy as jnp
from jax import lax
from jax.experimental import pallas as pl
from jax.experimental.pallas import tpu as pltpu

F32 = jnp.float32
BF16 = jnp.bfloat16
I32 = jnp.int32

GRID_W = 64
HEAD_DIM = 64
N_Q_HEADS = 8
N_KV_HEADS = 2
WINDOW = 128
ROPE_THETA = 10000.0
POOL_WIDTH = 256
POOL_WINDOWS = (2, 4, 8, 16)
RET_HEADS = 4
RET_DIM = 64
N_EXPERTS = 32
TOP_K = 4
D_FF = 1024
SWIGLU_LIMIT = 7.0
SWIGLU_ALPHA = 1.702
NORM_EPS = 1e-6
GN_EPS = 1e-5

LANE = 128
SUBLANE = 8
TM = 256
AB = 128
CTX_BLOCKS = TM // AB
EB = 256
POOL_HALO = SUBLANE
SORT_ROWS = TM * TOP_K + N_EXPERTS * SUBLANE
SORT_TILES = SORT_ROWS // SUBLANE
SPARE_BLOCKS = 2 * SORT_ROWS // EB
NEG = -1e30
VMEM_LIMIT = 56 * 1024 * 1024

ATT_Q = N_Q_HEADS * HEAD_DIM
KV2 = 2 * N_KV_HEADS * HEAD_DIM
RET_W = RET_HEADS * RET_DIM


def _dot(a, b):
    return jnp.dot(a, b, preferred_element_type=F32)


def _dot_nt(a, b):
    return lax.dot_general(a, b, (((1,), (1,)), ((), ())), preferred_element_type=F32)


def _dot_tn(a, b):
    return lax.dot_general(a, b, (((0,), (0,)), ((), ())), preferred_element_type=F32)


def _split(a):
    hi = a.astype(BF16)
    lo = (a - hi.astype(F32)).astype(BF16)
    return hi, lo


def _dot_hilo(a, m):
    hi, lo = _split(a)
    return _dot(hi, m) + _dot(lo, m)


def _sigmoid(x):
    return 0.5 * jnp.tanh(0.5 * x) + 0.5


def _silu(x):
    return x * _sigmoid(x)


def _params(*sem):
    return pltpu.CompilerParams(dimension_semantics=sem, vmem_limit_bytes=VMEM_LIMIT)


def _mod_kernel(c_ref, w_ref, b_ref, o_ref):
    s = _silu(c_ref[...])
    sh, sl = _split(s)
    wh, wl = _split(w_ref[0])
    o_ref[0] = _dot(sh, wh) + _dot(sh, wl) + _dot(sl, wh) + b_ref[0]


def _modulation(c, c_ctx, w_mod, b_mod):
    depth, d, six_d = w_mod.shape
    b = c.shape[0]
    cc = jnp.zeros((SUBLANE, d), F32).at[:b].set(c).at[b].set(c_ctx)
    out = pl.pallas_call(
        _mod_kernel,
        grid=(depth, six_d // d),
        in_specs=[
            pl.BlockSpec((SUBLANE, d), lambda l, n: (0, 0)),
            pl.BlockSpec((1, d, d), lambda l, n: (l, 0, n)),
            pl.BlockSpec((1, 1, d), lambda l, n: (l, 0, n)),
        ],
        out_specs=pl.BlockSpec((1, SUBLANE, d), lambda l, n: (l, 0, n)),
        out_shape=jax.ShapeDtypeStruct((depth, SUBLANE, six_d), F32),
        compiler_params=_params("arbitrary", "arbitrary"),
    )(cc, w_mod, b_mod.reshape(depth, 1, six_d))
    return out[:, : b + 1].reshape(depth, b + 1, 6, d)


def _rope(x, cos, sin, half, first):
    partner = jnp.where(first, pltpu.roll(x, LANE - half, 1), pltpu.roll(x, half, 1))
    return x * cos + partner * sin


def _inproj_kernel(x_ref, mod_ref, n1_ref, w_ref, ac_ref, as_ref, rc_ref, rs_ref,
                   q_ref, k_ref, v_ref, rq_ref, rk_ref, rv_ref, rg_ref, u_ref, g_ref):
    x = x_ref[0]
    ms = jnp.mean(x * x, axis=-1, keepdims=True)
    y = x * lax.rsqrt(ms + NORM_EPS) * n1_ref[...]
    h = y * (1.0 + mod_ref[0, 1:2, :]) + mod_ref[0, 0:1, :]
    hb = h.astype(BF16)
    lane = lax.broadcasted_iota(I32, (TM, LANE), 1)
    a_first = (lane % (HEAD_DIM // 2)) < (HEAD_DIM // 4)
    r_first = (lane % RET_DIM) < (RET_DIM // 2)
    ac, asn, rc, rsn = ac_ref[...], as_ref[...], rc_ref[...], rs_ref[...]

    def proj(off, width):
        return _dot(hb, w_ref[:, off:off + width])

    off = 0
    for t in range(ATT_Q // LANE):
        q_ref[0, :, t * LANE:(t + 1) * LANE] = _rope(
            proj(off, LANE), ac, asn, HEAD_DIM // 4, a_first).astype(BF16)
        off += LANE
    for t in range(KV2 // LANE):
        k_ref[0, :, t * LANE:(t + 1) * LANE] = _rope(
            proj(off, LANE), ac, asn, HEAD_DIM // 4, a_first).astype(BF16)
        off += LANE
    v_ref[0] = proj(off, KV2).astype(BF16)
    off += KV2
    for ref in (rq_ref, rk_ref):
        for t in range(RET_W // LANE):
            ref[0, :, t * LANE:(t + 1) * LANE] = _rope(
                proj(off, LANE), rc, rsn, RET_DIM // 2, r_first).astype(BF16)
            off += LANE
    rv_ref[0] = proj(off, RET_W).astype(BF16)
    off += RET_W
    rg_ref[0] = proj(off, 2 * RET_W).astype(BF16)
    off += 2 * RET_W
    u_ref[0] = proj(off, POOL_WIDTH)
    off += POOL_WIDTH
    d = x.shape[-1]
    for t in range(3):
        g_ref[0, :, t * d:(t + 1) * d] = proj(off, d).astype(BF16)
        off += d


def _inproj(xm, mod_l, norm1_l, w1, tables):
    b, s, d = xm.shape
    nt = s // TM
    wcols = w1.shape[1]
    tok = lambda width: pl.BlockSpec((1, TM, width), lambda bi, j: (bi, j, 0))
    tab = pl.BlockSpec((TM, LANE), lambda bi, j: (j, 0))
    widths = (ATT_Q, KV2, KV2, RET_W, RET_W, RET_W, 2 * RET_W, POOL_WIDTH, 3 * d)
    dtypes = (BF16,) * 7 + (F32, BF16)
    return pl.pallas_call(
        _inproj_kernel,
        grid=(b, nt),
        in_specs=[
            tok(d),
            pl.BlockSpec((1, 6, d), lambda bi, j: (jnp.where(j == 0, b, bi), 0, 0)),
            pl.BlockSpec((1, d), lambda bi, j: (0, 0)),
            pl.BlockSpec((d, wcols), lambda bi, j: (0, 0)),
            tab, tab, tab, tab,
        ],
        out_specs=[tok(w) for w in widths],
        out_shape=[jax.ShapeDtypeStruct((b, s, w), dt) for w, dt in zip(widths, dtypes)],
        compiler_params=_params("arbitrary", "arbitrary"),
    )(xm, mod_l, norm1_l.reshape(1, d), w1, *tables)


def _rope_tables(seq):
    rows = seq // GRID_W
    rpos = jnp.arange(rows, dtype=F32)[:, None]
    cpos = jnp.arange(GRID_W, dtype=F32)[:, None]
    lane = jnp.arange(LANE)
    grid = lambda per_row, per_col: (per_row[:, None, :] + per_col[None, :, :]).reshape(seq, LANE)
    axis_dim = HEAD_DIM // 2
    inv_a = ROPE_THETA ** (-jnp.arange(0, axis_dim, 2, dtype=F32) / axis_dim)
    hl = lane % HEAD_DIM
    inv_al = inv_a[(hl % axis_dim) % (axis_dim // 2)][None, :]
    by_row = (hl < axis_dim)[None, :]
    a_sign = jnp.where((hl % axis_dim) < axis_dim // 2, -1.0, 1.0)[None, :]
    a_cos = grid(jnp.where(by_row, jnp.cos(rpos * inv_al), 0.0), jnp.where(by_row, 0.0, jnp.cos(cpos * inv_al)))
    a_sin = grid(jnp.where(by_row, jnp.sin(rpos * inv_al), 0.0), jnp.where(by_row, 0.0, jnp.sin(cpos * inv_al)))
    inv_r = 1.0 / (ROPE_THETA ** jnp.linspace(0.0, 1.0, RET_DIM // 2, dtype=F32))
    rl = lane % RET_DIM
    inv_rl = inv_r[rl % (RET_DIM // 2)][None, :]
    r_sign = jnp.where(rl < RET_DIM // 2, -1.0, 1.0)[None, :]
    hi = (rpos * GRID_W) * inv_rl
    lo = cpos * inv_rl
    outer = lambda a, b: (a[:, None, :] * b[None, :, :]).reshape(seq, LANE)
    r_cos = outer(jnp.cos(hi), jnp.cos(lo)) - outer(jnp.sin(hi), jnp.sin(lo))
    r_sin = outer(jnp.sin(hi), jnp.cos(lo)) + outer(jnp.cos(hi), jnp.sin(lo))
    ones = jnp.ones((TM, LANE), F32)
    zeros = jnp.zeros((TM, LANE), F32)
    cat = lambda head, body: jnp.concatenate([head, body], axis=0)
    return (cat(ones, a_cos), cat(zeros, a_sin * a_sign), cat(ones, r_cos), cat(zeros, r_sin * r_sign))


def _attn_kernel(sink_ref, bias_ref, q_ref, kp_ref, kc_ref, kn_ref, kx_ref, vp_ref, vc_ref, vn_ref, vx_ref, o_ref):
    g_heads = N_Q_HEADS // N_KV_HEADS
    lane = lax.broadcasted_iota(I32, (AB, LANE), 1)
    lo = lane < HEAD_DIM
    rows = lax.broadcasted_iota(I32, (g_heads * AB, 1), 0)
    bias = bias_ref[0]
    outs = []
    for g in range(N_KV_HEADS):
        ks = slice(g * LANE, (g + 1) * LANE)
        kd = jnp.concatenate([kp_ref[0, :, ks], kc_ref[0, :, ks], kn_ref[0, :, ks], kx_ref[0, :, ks]], axis=0)
        vd = jnp.concatenate([vp_ref[0, :, ks], vc_ref[0, :, ks], vn_ref[0, :, ks], vx_ref[0, :, ks]], axis=0)
        qs = []
        for c in range(2 * g, 2 * g + 2):
            qc = q_ref[0, :, c * LANE:(c + 1) * LANE]
            zero = jnp.zeros_like(qc)
            qs += [jnp.where(lo, qc, zero), jnp.where(lo, zero, qc)]
        s = _dot_nt(jnp.concatenate(qs, axis=0), kd) + bias
        sink = jnp.full((g_heads * AB, 1), sink_ref[g_heads * g + g_heads - 1], F32)
        for t in range(g_heads - 2, -1, -1):
            sink = jnp.where(rows < (t + 1) * AB, sink_ref[g_heads * g + t], sink)
        m = jnp.maximum(jnp.max(s, axis=-1, keepdims=True), sink)
        p = jnp.exp(s - m)
        den = jnp.sum(p, axis=-1, keepdims=True) + jnp.exp(sink - m)
        o = _dot(p.astype(BF16), vd) / den
        for t in range(2):
            outs.append(jnp.where(lo, o[2 * t * AB:(2 * t + 1) * AB], o[(2 * t + 1) * AB:(2 * t + 2) * AB]))
    o_ref[0] = jnp.concatenate(outs, axis=-1).astype(BF16)


def _attn_bias():
    g_heads = N_Q_HEADS // N_KV_HEADS
    r = jnp.arange(g_heads * AB)[:, None] % AB
    j = jnp.arange(3 * AB + TM)[None, :]
    band = jnp.abs(j - AB - r) <= WINDOW
    is_ctx = j >= 3 * AB
    variants = (band, band & (j >= AB), band & (j < 2 * AB), jnp.zeros_like(band))
    return jnp.stack([jnp.where(v | is_ctx, 0.0, NEG) for v in variants]).astype(F32)


def _attention(q, k2, v2, sinks):
    b, s, _ = q.shape
    nb = s // AB
    assert nb - CTX_BLOCKS >= 2
    bias = _attn_bias()
    variant = lambda i: jnp.where(i < CTX_BLOCKS, 3, jnp.where(i == CTX_BLOCKS, 1, jnp.where(i == nb - 1, 2, 0)))
    prev = pl.BlockSpec((1, AB, KV2), lambda bi, i: (bi, jnp.maximum(i - 1, 0), 0))
    cur = pl.BlockSpec((1, AB, KV2), lambda bi, i: (bi, i, 0))
    nxt = pl.BlockSpec((1, AB, KV2), lambda bi, i: (bi, jnp.minimum(i + 1, nb - 1), 0))
    cx = pl.BlockSpec((1, TM, KV2), lambda bi, i: (bi, 0, 0))
    return pl.pallas_call(
        _attn_kernel,
        grid=(b, nb),
        in_specs=[pl.BlockSpec(memory_space=pltpu.SMEM),
                  pl.BlockSpec((1,) + bias.shape[1:], lambda bi, i: (variant(i), 0, 0)),
                  pl.BlockSpec((1, AB, ATT_Q), lambda bi, i: (bi, i, 0)),
                  prev, cur, nxt, cx, prev, cur, nxt, cx],
        out_specs=pl.BlockSpec((1, AB, ATT_Q), lambda bi, i: (bi, i, 0)),
        out_shape=jax.ShapeDtypeStruct((b, s, ATT_Q), BF16),
        compiler_params=_params("arbitrary", "arbitrary"),
    )(sinks, bias, q, k2, k2, k2, k2, v2, v2, v2, v2)


def _ret_kernel(lg_ref, qf_ref, kf_ref, vf_ref, qb_ref, kb_ref, vb_ref, yf_ref, yb_ref,
                st_ref, dm_ref, qd_ref, kd_ref, cd_ref, *, batch):
    step = pl.program_id(0)
    lane = lax.broadcasted_iota(I32, (AB, LANE), 1)
    row = lax.broadcasted_iota(I32, (AB, LANE), 0)
    lo = lane < RET_DIM
    tiles = RET_W // LANE

    @pl.when(step == 0)
    def _():
        st_ref[...] = jnp.zeros_like(st_ref)
        ii = row.astype(F32)
        jj = lane.astype(F32)
        for d in range(2):
            for c in range(tiles):
                lg0 = lg_ref[d * RET_HEADS + 2 * c]
                lg1 = lg_ref[d * RET_HEADS + 2 * c + 1]
                lgl = jnp.where(lo, lg0, lg1)
                q_exp = ii + 1.0 if d == 0 else AB - ii
                k_exp = (AB - 1.0) - ii if d == 0 else ii
                qd_ref[d * tiles + c] = jnp.exp(q_exp * lgl)
                kd_ref[d * tiles + c] = jnp.exp(k_exp * lgl)
                cd_ref[d * tiles + c] = jnp.exp(AB * lgl)
                rel = ii - jj if d == 0 else jj - ii
                for hh, lgh in enumerate((lg0, lg1)):
                    dm_ref[d * tiles + c, hh * AB:(hh + 1) * AB, :] = jnp.where(
                        rel >= 0, jnp.exp(jnp.maximum(rel, 0.0) * lgh), 0.0)

    same_head = (row < RET_DIM) == lo
    dirs = ((qf_ref, kf_ref, vf_ref, yf_ref), (qb_ref, kb_ref, vb_ref, yb_ref))
    for d, (q_ref, k_ref, v_ref, y_ref) in enumerate(dirs):
        for b in range(batch):
            for c in range(tiles):
                sl = slice(c * LANE, (c + 1) * LANE)
                t = d * tiles + c
                q = q_ref[b, :, sl]
                k = k_ref[b, :, sl]
                v = v_ref[b, :, sl]
                zero = jnp.zeros_like(q)
                q2 = jnp.concatenate([jnp.where(lo, q, zero), jnp.where(lo, zero, q)], axis=0)
                p = (_dot_nt(q2, k) * dm_ref[t]).astype(BF16)
                y_intra = jnp.where(lo, _dot(p[:AB], v), _dot(p[AB:], v))
                si = (d * batch + b) * tiles + c
                state = st_ref[si]
                q_dec = (q.astype(F32) * qd_ref[t]).astype(BF16)
                y_ref[b, :, sl] = y_intra + _dot(q_dec, state.astype(BF16))
                k_dec = (k.astype(F32) * kd_ref[t]).astype(BF16)
                st_ref[si] = state * cd_ref[t] + jnp.where(same_head, _dot_tn(k_dec, v), 0.0)


def _retention(rq, rk, rv, log_g):
    b, s, w = rq.shape
    nb = s // AB
    tiles = w // LANE

    def back(i):
        return jnp.where(i < CTX_BLOCKS, CTX_BLOCKS - 1 - i, nb - 1 + CTX_BLOCKS - i)

    fwd = pl.BlockSpec((b, AB, w), lambda i: (0, i, 0))
    bwd = pl.BlockSpec((b, AB, w), lambda i: (0, back(i), 0))
    return pl.pallas_call(
        functools.partial(_ret_kernel, batch=b),
        grid=(nb,),
        in_specs=[pl.BlockSpec(memory_space=pltpu.SMEM), fwd, fwd, fwd, bwd, bwd, bwd],
        out_specs=[fwd, bwd],
        out_shape=[jax.ShapeDtypeStruct((b, s, w), F32)] * 2,
        scratch_shapes=[
            pltpu.VMEM((2 * b * tiles, LANE, LANE), F32),
            pltpu.VMEM((2 * tiles, 2 * AB, LANE), F32),
            pltpu.VMEM((2 * tiles, AB, LANE), F32),
            pltpu.VMEM((2 * tiles, AB, LANE), F32),
            pltpu.VMEM((2 * tiles, AB, LANE), F32),
        ],
        compiler_params=_params("arbitrary"),
    )(log_g.reshape(-1), rq, rk, rv, rq, rk, rv)


def _merge_kernel(x_ref, attn_ref, up_ref, uc_ref, un_ref, yf_ref, yb_ref, rg_ref, g_ref, mod_ref,
                  wbd_ref, ps_ref, woa_ref, wop_ref, wor_ref, wout_ref, n2_ref, wrh_ref, wrl_ref,
                  br_ref, avg_ref, tri_ref,
                  xo_ref, h2_ref, ls_ref, lst_ref, rw_ref, cnt_ref, *, seq, nt):
    i = pl.program_id(0)
    j = i % nt

    ext = TM + 2 * POOL_HALO
    seq_len = jnp.where(j == 0, TM, seq)
    start = jnp.where(j == 0, 0, (j - 1) * TM)
    u = uc_ref[0]
    ue = jnp.concatenate([up_ref[0], u, un_ref[0]], axis=0)
    erow = lax.broadcasted_iota(I32, (ext, 1), 0) + (start - POOL_HALO)
    ue = jnp.where((erow >= 0) & (erow < seq_len), ue, 0.0)
    t_pos = lax.broadcasted_iota(I32, (TM, 1), 0) + start
    glane = lax.broadcasted_iota(I32, (TM, POOL_WIDTH), 1) // (POOL_WIDTH // len(POOL_WINDOWS))
    run = ue
    width = 1
    diff = jnp.zeros((TM, POOL_WIDTH), F32)
    for gi, w in enumerate(POOL_WINDOWS):
        while width < w:
            run = run + pltpu.roll(run, ext - width, 0)
            width *= 2
        win = pltpu.roll(run, w // 2, 0)[POOL_HALO:POOL_HALO + TM]
        cnt = jnp.minimum(t_pos - w // 2 + w, seq_len) - jnp.maximum(t_pos - w // 2, 0)
        diff = jnp.where(glane == gi, win / cnt.astype(F32) - u, diff)
    pool = _dot(diff.astype(BF16), wbd_ref[...]) * ps_ref[...]

    def head_norm(y):
        mu = _dot_hilo(y, avg_ref[...])
        dlt = y - mu
        var = _dot_hilo(dlt * dlt, avg_ref[...])
        return dlt * lax.rsqrt(var + GN_EPS)

    rg = rg_ref[0].astype(F32)
    ret = head_norm(yf_ref[0]) * _silu(rg[:, :RET_W]) + head_norm(yb_ref[0]) * _silu(rg[:, RET_W:])

    d = x_ref.shape[-1]
    gate = lambda t: _sigmoid(g_ref[0, :, t * d:(t + 1) * d].astype(F32))
    m = (gate(0) * _dot(attn_ref[0], woa_ref[...])
         + gate(1) * _dot(pool.astype(BF16), wop_ref[...])
         + gate(2) * _dot(ret.astype(BF16), wor_ref[...]))
    xn = x_ref[0] + mod_ref[0, 2:3, :] * _dot(m.astype(BF16), wout_ref[...])
    xo_ref[0] = xn

    ms = jnp.mean(xn * xn, axis=-1, keepdims=True)
    h2 = xn * lax.rsqrt(ms + NORM_EPS) * n2_ref[...] * (1.0 + mod_ref[0, 4:5, :]) + mod_ref[0, 3:4, :]
    h2_ref[0] = h2.astype(BF16)
    hh, hl = _split(h2)
    logits = _dot(hh, wrh_ref[...]) + _dot(hl, wrh_ref[...]) + _dot(hh, wrl_ref[...]) + br_ref[...]
    lanef = lax.broadcasted_iota(I32, (TM, LANE), 1).astype(F32)
    tops, hots = [], []
    for _ in range(TOP_K):
        mx = jnp.max(logits, axis=-1, keepdims=True)
        idx = jnp.min(jnp.where(logits == mx, lanef, float(LANE)), axis=-1, keepdims=True)
        hot = lanef == idx
        logits = jnp.where(hot, NEG * 2.0, logits)
        tops.append((mx, idx))
        hots.append(hot)
    ex = [jnp.exp(mx - tops[0][0]) for mx, _ in tops]
    tot = ex[0]
    for e in ex[1:]:
        tot = tot + e
    sel = jnp.zeros((TM, LANE), F32)
    for hot in hots:
        sel = jnp.where(hot, 1.0, sel)
    incl = _dot(tri_ref[...], sel.astype(BF16))
    cnt = incl[TM - 1:TM, :]
    run = jnp.floor((cnt + (SUBLANE - 1.0)) * (1.0 / SUBLANE)) * SUBLANE
    er = lax.broadcasted_iota(I32, (LANE, LANE), 0)
    ec = lax.broadcasted_iota(I32, (LANE, LANE), 1)
    earlier = jnp.where(er < ec, 1.0, 0.0).astype(BF16)
    run_start = _dot(jnp.broadcast_to(run, (SUBLANE, LANE)).astype(BF16), earlier)[0:1]
    slot = run_start + incl - 1.0
    ls = jnp.zeros((TM, LANE), F32)
    rw = jnp.zeros((TM, LANE), F32)
    for k in range(TOP_K):
        ls = jnp.where(lanef == k, jnp.sum(jnp.where(hots[k], slot, 0.0), axis=-1, keepdims=True), ls)
        rw = jnp.where(lanef == k, ex[k] / tot, rw)
    ls = jnp.where(lanef < TOP_K, ls, -1.0)
    ls_ref[...] = ls.astype(I32)
    lst_ref[...] = jnp.transpose(ls)[:SUBLANE, :].astype(I32)
    rw_ref[...] = rw
    cnt_ref[...] = jnp.broadcast_to(cnt, cnt_ref.shape).astype(I32)


def _merge(xm, attn, u, yf, yb, rg, gates, mod_l, seq, wts):
    b, s, d = xm.shape
    nt = s // TM
    ntt = b * nt
    hb = TM // POOL_HALO
    nh = s // POOL_HALO
    tok = lambda width: pl.BlockSpec((1, TM, width), lambda i: (i // nt, i % nt, 0))
    full = lambda a: pl.BlockSpec(a.shape, lambda i: (0,) * a.ndim)
    lanes = pl.BlockSpec((TM, LANE), lambda i: (i, 0))
    in_specs = [
        tok(d), tok(ATT_Q),
        pl.BlockSpec((1, POOL_HALO, POOL_WIDTH), lambda i: (i // nt, jnp.maximum((i % nt) * hb - 1, 0), 0)),
        tok(POOL_WIDTH),
        pl.BlockSpec((1, POOL_HALO, POOL_WIDTH), lambda i: (i // nt, jnp.minimum((i % nt + 1) * hb, nh - 1), 0)),
        tok(RET_W), tok(RET_W), tok(2 * RET_W), tok(3 * d),
        pl.BlockSpec((1, 6, d), lambda i: (jnp.where(i % nt == 0, b, i // nt), 0, 0)),
    ] + [full(a) for a in wts]
    return pl.pallas_call(
        functools.partial(_merge_kernel, seq=seq, nt=nt),
        grid=(ntt,),
        in_specs=in_specs,
        out_specs=[tok(d), tok(d), lanes, pl.BlockSpec((SUBLANE, TM), lambda i: (i, 0)), lanes,
                   pl.BlockSpec((SUBLANE, LANE), lambda i: (i, 0))],
        out_shape=[jax.ShapeDtypeStruct((b, s, d), F32), jax.ShapeDtypeStruct((b, s, d), BF16),
                   jax.ShapeDtypeStruct((b * s, LANE), I32), jax.ShapeDtypeStruct((ntt * SUBLANE, TM), I32),
                   jax.ShapeDtypeStruct((b * s, LANE), F32), jax.ShapeDtypeStruct((ntt * SUBLANE, LANE), I32)],
        compiler_params=_params("arbitrary"),
    )(xm, attn, u, u, u, yf, yb, rg, gates, mod_l, *wts)


def _rows(tile_index):
    return pl.ds(pl.multiple_of(tile_index * SUBLANE, SUBLANE), SUBLANE)


def _tile_copies(table_ref, tile, copy):
    def body(j, carry):
        copy(j, table_ref[tile * SORT_TILES + j])
        return carry

    lax.fori_loop(0, SORT_TILES, body, 0, unroll=8)


def _dispatch_kernel(dst_ref, fs_ref, fn_ref, nu_ref, h_ref, lst_ref, xs_hbm, buf, zbuf, sem, fill_sem,
                     *, n_blocks):
    i = pl.program_id(0)
    last = pl.num_programs(0) - 1
    cur = i % 2

    def send(tile, half):
        _tile_copies(dst_ref, tile, lambda j, t: pltpu.make_async_copy(
            buf.at[half, _rows(j)], xs_hbm.at[_rows(t)], sem.at[half]).start())

    def drain(half):
        pltpu.make_async_copy(buf.at[half], xs_hbm.at[pl.ds(0, SORT_ROWS)], sem.at[half]).wait()

    @pl.when(i == 0)
    def _():
        zbuf[...] = jnp.zeros_like(zbuf)

        def pad_copy(e, c):
            return pltpu.make_async_copy(zbuf.at[pl.ds(0, SUBLANE)], xs_hbm.at[_rows(fs_ref[e] + c)], fill_sem)

        def blk_copy(blk):
            return pltpu.make_async_copy(zbuf, xs_hbm.at[pl.ds(pl.multiple_of(blk * EB, EB), EB)], fill_sem)

        def per_expert(fn):
            def outer(e, carry):
                def inner(c, cc):
                    fn(e, c)
                    return cc
                return lax.fori_loop(0, fn_ref[e], inner, carry)
            lax.fori_loop(0, N_EXPERTS, outer, 0)

        def per_block(fn):
            def body(blk, carry):
                fn(blk)
                return carry
            lax.fori_loop(nu_ref[0], n_blocks, body, 0)

        per_expert(lambda e, c: pad_copy(e, c).start())
        per_block(lambda blk: blk_copy(blk).start())
        per_expert(lambda e, c: pad_copy(e, c).wait())
        per_block(lambda blk: blk_copy(blk).wait())

    slot = lax.broadcasted_iota(I32, (SORT_ROWS, TM), 0)
    lst = lst_ref[...]
    p = jnp.zeros((SORT_ROWS, TM), F32)
    for k in range(TOP_K):
        p = jnp.where(slot == lst[k:k + 1, :], 1.0, p)
    srt = _dot(p.astype(BF16), h_ref[...])

    @pl.when(i >= 2)
    def _():
        drain(cur)

    buf[cur] = srt
    send(i, cur)

    @pl.when(i == last)
    def _():
        @pl.when(i >= 1)
        def _():
            drain(1 - cur)
        drain(cur)


def _dispatch(h2, lst, tabs, n_used, n_blocks):
    t, d = h2.shape
    ntt = t // TM
    grid_spec = pltpu.PrefetchScalarGridSpec(
        num_scalar_prefetch=4,
        grid=(ntt,),
        in_specs=[pl.BlockSpec((TM, d), lambda i, *_: (i, 0)),
                  pl.BlockSpec((SUBLANE, TM), lambda i, *_: (i, 0))],
        out_specs=pl.BlockSpec(memory_space=pl.ANY),
        scratch_shapes=[pltpu.VMEM((2, SORT_ROWS, d), F32), pltpu.VMEM((EB, d), F32),
                        pltpu.SemaphoreType.DMA((2,)), pltpu.SemaphoreType.DMA],
    )
    return pl.pallas_call(
        functools.partial(_dispatch_kernel, n_blocks=n_blocks),
        grid_spec=grid_spec,
        out_shape=jax.ShapeDtypeStruct((n_blocks * EB, d), F32),
        compiler_params=_params("arbitrary"),
    )(tabs["dst"], tabs["fill_start"], tabs["fill_n"], n_used, h2, lst)


def _expert_kernel(be_ref, nu_ref, x_ref, w1_ref, b1_ref, w2_ref, b2_ref, y_ref, w1b, w2b):
    bi = pl.program_id(0)
    used = bi < nu_ref[0]
    fresh = (bi == 0) | (be_ref[bi] != be_ref[jnp.maximum(bi - 1, 0)])

    @pl.when(used & fresh)
    def _():
        w1b[...] = w1_ref[0, 0].astype(BF16)
        w2b[...] = w2_ref[0, 0].astype(BF16)

    @pl.when(jnp.logical_not(used))
    def _():
        y_ref[...] = jnp.zeros_like(y_ref)

    @pl.when(used)
    def _():
        hid = _dot(x_ref[...].astype(BF16), w1b[...]) + b1_ref[0, 0]
        glu = jnp.minimum(hid[:, :D_FF], SWIGLU_LIMIT)
        lin = jnp.clip(hid[:, D_FF:], -SWIGLU_LIMIT, SWIGLU_LIMIT)
        act = glu * _sigmoid(SWIGLU_ALPHA * glu) * (lin + 1.0)
        y_ref[...] = _dot(act.astype(BF16), w2b[...]) + b2_ref[0, 0]


def _experts(xs, block_e, n_used, w1, b1, w2, b2, layer):
    rows, d = xs.shape
    depth, ne, _, f2 = w1.shape
    row_blk = lambda bi, be, nu: (bi, 0)
    grid_spec = pltpu.PrefetchScalarGridSpec(
        num_scalar_prefetch=2,
        grid=(rows // EB,),
        in_specs=[
            pl.BlockSpec((EB, d), row_blk),
            pl.BlockSpec((1, 1, d, f2), lambda bi, be, nu: (layer, be[bi], 0, 0)),
            pl.BlockSpec((1, 1, 1, f2), lambda bi, be, nu: (layer, be[bi], 0, 0)),
            pl.BlockSpec((1, 1, f2 // 2, d), lambda bi, be, nu: (layer, be[bi], 0, 0)),
            pl.BlockSpec((1, 1, 1, d), lambda bi, be, nu: (layer, be[bi], 0, 0)),
        ],
        out_specs=pl.BlockSpec((EB, d), row_blk),
        scratch_shapes=[pltpu.VMEM((d, f2), BF16), pltpu.VMEM((f2 // 2, d), BF16)],
    )
    return pl.pallas_call(
        _expert_kernel,
        grid_spec=grid_spec,
        out_shape=jax.ShapeDtypeStruct((rows, d), F32),
        compiler_params=_params("arbitrary"),
    )(block_e, n_used, xs, w1, b1.reshape(depth, ne, 1, f2), w2, b2.reshape(depth, ne, 1, d))


def _combine_kernel(dst_ref, x_ref, ls_ref, rw_ref, mod_ref, fg_ref, ys_hbm, xo_ref, buf, sem, *, final):
    i = pl.program_id(0)
    last = pl.num_programs(0) - 1
    cur = i % 2

    def fetch(tile, half):
        _tile_copies(dst_ref, tile, lambda j, t: pltpu.make_async_copy(
            ys_hbm.at[_rows(t)], buf.at[half, _rows(j)], sem.at[half]).start())

    @pl.when(i == 0)
    def _():
        fetch(0, 0)

    @pl.when(i < last)
    def _():
        fetch(i + 1, 1 - cur)

    pltpu.make_async_copy(ys_hbm.at[pl.ds(0, SORT_ROWS)], buf.at[cur], sem.at[cur]).wait()
    yb = buf[cur].astype(BF16)
    slot = lax.broadcasted_iota(I32, (TM, SORT_ROWS), 1)
    ls = ls_ref[...]
    rw = rw_ref[...]
    g = jnp.zeros((TM, SORT_ROWS), F32)
    for k in range(TOP_K):
        g = jnp.where(slot == ls[:, k:k + 1], rw[:, k:k + 1], g)
    gh, gl = _split(g)
    xn = x_ref[0] + mod_ref[0, 5:6, :] * (_dot(gh, yb) + _dot(gl, yb))
    if final:
        ms = jnp.mean(xn * xn, axis=-1, keepdims=True)
        xn = xn * lax.rsqrt(ms + NORM_EPS) * fg_ref[...]
    xo_ref[0] = xn


def _combine(xn, ls, rw, mod_l, final_gain, ys, tabs, final):
    b, s, d = xn.shape
    nt = s // TM
    tok = pl.BlockSpec((1, TM, d), lambda i, *_: (i // nt, i % nt, 0))
    lanes = pl.BlockSpec((TM, LANE), lambda i, *_: (i, 0))
    if final:
        out_spec = pl.BlockSpec((1, TM, d), lambda i, *_: (i // nt, jnp.maximum(i % nt - 1, 0), 0))
        out_shape = jax.ShapeDtypeStruct((b, s - TM, d), F32)
    else:
        out_spec, out_shape = tok, jax.ShapeDtypeStruct((b, s, d), F32)
    grid_spec = pltpu.PrefetchScalarGridSpec(
        num_scalar_prefetch=1,
        grid=(b * nt,),
        in_specs=[tok, lanes, lanes,
                  pl.BlockSpec((1, 6, d), lambda i, *_: (jnp.where(i % nt == 0, b, i // nt), 0, 0)),
                  pl.BlockSpec((1, d), lambda i, *_: (0, 0)),
                  pl.BlockSpec(memory_space=pl.ANY)],
        out_specs=out_spec,
        scratch_shapes=[pltpu.VMEM((2, SORT_ROWS, d), F32), pltpu.SemaphoreType.DMA((2,))],
    )
    return pl.pallas_call(
        functools.partial(_combine_kernel, final=final),
        grid_spec=grid_spec,
        out_shape=out_shape,
        compiler_params=_params("arbitrary"),
    )(tabs["dst"], xn, ls, rw, mod_l, final_gain.reshape(1, d), ys)


def _inproj_weight(w_in_l):
    d = w_in_l.shape[0]
    sizes = (ATT_Q, N_KV_HEADS * HEAD_DIM, N_KV_HEADS * HEAD_DIM, POOL_WIDTH,
             RET_W, RET_W, RET_W, RET_W, RET_W, d, d, d)
    parts, off = [], 0
    for sz in sizes:
        parts.append(w_in_l[:, off:off + sz])
        off += sz
    q, k, v, u, rq, rk, rv, rgf, rgb, ga, gp, gr = parts
    twice = lambda w: jnp.concatenate(
        [w[:, h * HEAD_DIM:(h + 1) * HEAD_DIM] for h in range(N_KV_HEADS) for _ in range(2)], axis=1)
    cols = [q * HEAD_DIM ** -0.5, twice(k), twice(v), rq, rk * RET_DIM ** -0.5, rv, rgf, rgb, u, ga, gp, gr]
    return jnp.concatenate(cols, axis=1).astype(BF16)


def _block_diag(blocks):
    n, r, c = blocks.shape
    out = jnp.zeros((n * r, n * c), blocks.dtype)
    for g in range(n):
        out = out.at[g * r:(g + 1) * r, g * c:(g + 1) * c].set(blocks[g])
    return out


def _routing_tables(cnt, n_blocks):
    ntt = cnt.shape[0] // SUBLANE
    counts = cnt.reshape(ntt, SUBLANE, LANE)[:, 0, :N_EXPERTS]
    run = (counts + SUBLANE - 1) // SUBLANE
    total = jnp.sum(run, axis=0)
    eb = EB // SUBLANE
    padded = (total + eb - 1) // eb * eb
    pad_end = jnp.cumsum(padded)
    pad_start = pad_end - padded
    off = pad_start[None, :] + jnp.cumsum(run, axis=0) - run
    run_end = jnp.cumsum(run, axis=1)
    j = jnp.arange(SORT_TILES)
    owner = jnp.sum(run_end[:, None, :] <= j[None, :, None], axis=2)
    e = jnp.minimum(owner, N_EXPERTS - 1)
    in_run = j[None, :] - jnp.take_along_axis(run_end - run, e, axis=1)
    spare = (n_blocks - SPARE_BLOCKS) * eb + (jnp.arange(ntt) % 2)[:, None] * SORT_TILES + j[None, :]
    dst = jnp.where(owner < N_EXPERTS, jnp.take_along_axis(off, e, axis=1) + in_run, spare)
    n_used = pad_end[-1] // eb
    blk = jnp.minimum(jnp.arange(n_blocks), n_used - 1) * eb
    block_e = jnp.minimum(jnp.sum(pad_end[None, :] <= blk[:, None], axis=1), N_EXPERTS - 1)
    tabs = dict(dst=dst.reshape(-1).astype(I32), fill_start=(pad_start + total).astype(I32),
                fill_n=(padded - total).astype(I32))
    return tabs, block_e.astype(I32), n_used.reshape(1).astype(I32)


def kernel(x, c, ctx, c_ctx, w_mod, b_mod, norm1, norm2, w_in, attn_sinks, pool_w, pool_scale, ret_decay,
           w_o_attn, w_o_pool, w_o_ret, w_out, w_router, b_router, w_expert_in, b_expert_in, w_expert_out,
           b_expert_out, final_norm):
    b, seq, d = x.shape
    depth = w_mod.shape[0]
    assert ctx.shape[1] == TM and seq % TM == 0 and seq % GRID_W == 0
    s = seq + TM
    xm = jnp.concatenate([ctx, x], axis=1)
    mod = _modulation(c, c_ctx, w_mod, b_mod)
    tables = _rope_tables(seq)
    ntt = b * s // TM
    n_blocks = -(-(b * s * TOP_K + ntt * N_EXPERTS * (SUBLANE - 1)) // EB) + N_EXPERTS + SPARE_BLOCKS
    avg = _block_diag(jnp.full((RET_HEADS, RET_DIM, RET_DIM), 1.0 / RET_DIM, F32)).astype(BF16)
    tri = (jnp.arange(TM)[:, None] >= jnp.arange(TM)[None, :]).astype(BF16)
    for l in range(depth):
        q, k2, v2, rq, rk, rv, rg, u, gates = _inproj(xm, mod[l], norm1[l], _inproj_weight(w_in[l]), tables)
        attn = _attention(q, k2, v2, attn_sinks[l])
        yf, yb = _retention(rq, rk, rv, jax.nn.log_sigmoid(ret_decay[l].astype(F32)))
        wr = jnp.zeros((d, LANE), F32).at[:, :N_EXPERTS].set(w_router[l])
        wrh, wrl = _split(wr)
        br = jnp.full((1, LANE), NEG, F32).at[0, :N_EXPERTS].set(b_router[l])
        wts = (_block_diag(pool_w[l]).astype(BF16), pool_scale[l].reshape(1, -1),
               w_o_attn[l].astype(BF16), w_o_pool[l].astype(BF16), w_o_ret[l].astype(BF16),
               w_out[l].astype(BF16), norm2[l].reshape(1, d), wrh, wrl, br, avg, tri)
        xn, h2, ls, lst, rw, cnt = _merge(xm, attn, u, yf, yb, rg, gates, mod[l], seq, wts)
        tabs, block_e, n_used = _routing_tables(cnt, n_blocks)
        xs = _dispatch(h2.reshape(b * s, d), lst, tabs, n_used, n_blocks)
        ys = _experts(xs, block_e, n_used, w_expert_in, b_expert_in, w_expert_out, b_expert_out, l)
        xm = _combine(xn, ls, rw, mod[l], final_norm, ys, tabs, final=l == depth - 1)
    return xm
```

```python
import functools

import jax
import jax.numpy as jnp
from jax import lax
from jax.experimental import pallas as pl
from jax.experimental.pallas import tpu as pltpu

F32 = jnp.float32
BF16 = jnp.bfloat16
I32 = jnp.int32

GRID_W = 64
HEAD_DIM = 64
N_Q_HEADS = 8
N_KV_HEADS = 2
WINDOW = 128
ROPE_THETA = 10000.0
POOL_WIDTH = 256
POOL_WINDOWS = (2, 4, 8, 16)
RET_HEADS = 4
RET_DIM = 64
N_EXPERTS = 32
TOP_K = 4
D_FF = 1024
SWIGLU_LIMIT = 7.0
SWIGLU_ALPHA = 1.702
NORM_EPS = 1e-6
GN_EPS = 1e-5

LANE = 128
SUBLANE = 8
TM = 256
AB = 128
CTX_BLOCKS = TM // AB
EB = 512
FF_CHUNK = 256
POOL_HALO = SUBLANE
SORT_ROWS = TM * TOP_K + N_EXPERTS * SUBLANE
SORT_TILES = SORT_ROWS // SUBLANE
SPARE_BLOCKS = 2 * SORT_ROWS // EB
NEG = -1e30
VMEM_LIMIT = 56 * 1024 * 1024

ATT_Q = N_Q_HEADS * HEAD_DIM
KV2 = 2 * N_KV_HEADS * HEAD_DIM
RET_W = RET_HEADS * RET_DIM


def _dot(a, b):
    return jnp.dot(a, b, preferred_element_type=F32)


def _dot_nt(a, b):
    return lax.dot_general(a, b, (((1,), (1,)), ((), ())), preferred_element_type=F32)


def _dot_tn(a, b):
    return lax.dot_general(a, b, (((0,), (0,)), ((), ())), preferred_element_type=F32)


def _split(a):
    hi = a.astype(BF16)
    lo = (a - hi.astype(F32)).astype(BF16)
    return hi, lo


def _dot_hilo(a, m):
    hi, lo = _split(a)
    return _dot(hi, m) + _dot(lo, m)


def _sigmoid(x):
    return 0.5 * jnp.tanh(0.5 * x) + 0.5


def _silu(x):
    return x * _sigmoid(x)


def _params(*sem):
    return pltpu.CompilerParams(dimension_semantics=sem, vmem_limit_bytes=VMEM_LIMIT)


def _mod_kernel(c_ref, w_ref, b_ref, o_ref):
    s = _silu(c_ref[...])
    sh, sl = _split(s)
    wh, wl = _split(w_ref[0])
    o_ref[0] = _dot(sh, wh) + _dot(sh, wl) + _dot(sl, wh) + b_ref[0]


def _modulation(c, c_ctx, w_mod, b_mod):
    depth, d, six_d = w_mod.shape
    b = c.shape[0]
    cc = jnp.zeros((SUBLANE, d), F32).at[:b].set(c).at[b].set(c_ctx)
    out = pl.pallas_call(
        _mod_kernel,
        grid=(depth, six_d // d),
        in_specs=[
            pl.BlockSpec((SUBLANE, d), lambda l, n: (0, 0)),
            pl.BlockSpec((1, d, d), lambda l, n: (l, 0, n)),
            pl.BlockSpec((1, 1, d), lambda l, n: (l, 0, n)),
        ],
        out_specs=pl.BlockSpec((1, SUBLANE, d), lambda l, n: (l, 0, n)),
        out_shape=jax.ShapeDtypeStruct((depth, SUBLANE, six_d), F32),
        compiler_params=_params("arbitrary", "arbitrary"),
    )(cc, w_mod, b_mod.reshape(depth, 1, six_d))
    return out[:, : b + 1].reshape(depth, b + 1, 6, d)


def _rope(x, cos, sin, half, first):
    partner = jnp.where(first, pltpu.roll(x, LANE - half, 1), pltpu.roll(x, half, 1))
    return x * cos + partner * sin


def _inproj_kernel(x_ref, mod_ref, n1_ref, w_ref, ac_ref, as_ref, rc_ref, rs_ref,
                   q_ref, k_ref, v_ref, rq_ref, rk_ref, rv_ref, rg_ref, u_ref, g_ref):
    x = x_ref[0]
    ms = jnp.mean(x * x, axis=-1, keepdims=True)
    y = x * lax.rsqrt(ms + NORM_EPS) * n1_ref[...]
    h = y * (1.0 + mod_ref[0, 1:2, :]) + mod_ref[0, 0:1, :]
    hb = h.astype(BF16)
    lane = lax.broadcasted_iota(I32, (TM, LANE), 1)
    a_first = (lane % (HEAD_DIM // 2)) < (HEAD_DIM // 4)
    r_first = (lane % RET_DIM) < (RET_DIM // 2)
    ac, asn, rc, rsn = ac_ref[...], as_ref[...], rc_ref[...], rs_ref[...]

    def proj(off, width):
        return _dot(hb, w_ref[:, off:off + width])

    off = 0
    for t in range(ATT_Q // LANE):
        q_ref[0, :, t * LANE:(t + 1) * LANE] = _rope(
            proj(off, LANE), ac, asn, HEAD_DIM // 4, a_first).astype(BF16)
        off += LANE
    for t in range(KV2 // LANE):
        k_ref[0, :, t * LANE:(t + 1) * LANE] = _rope(
            proj(off, LANE), ac, asn, HEAD_DIM // 4, a_first).astype(BF16)
        off += LANE
    v_ref[0] = proj(off, KV2).astype(BF16)
    off += KV2
    for ref in (rq_ref, rk_ref):
        for t in range(RET_W // LANE):
            ref[0, :, t * LANE:(t + 1) * LANE] = _rope(
                proj(off, LANE), rc, rsn, RET_DIM // 2, r_first).astype(BF16)
            off += LANE
    rv_ref[0] = proj(off, RET_W).astype(BF16)
    off += RET_W
    rg_ref[0] = proj(off, 2 * RET_W).astype(BF16)
    off += 2 * RET_W
    u_ref[0] = proj(off, POOL_WIDTH)
    off += POOL_WIDTH
    d = x.shape[-1]
    for t in range(3):
        g_ref[0, :, t * d:(t + 1) * d] = proj(off, d).astype(BF16)
        off += d


def _inproj(xm, mod_l, norm1_l, w1, tables):
    b, s, d = xm.shape
    nt = s // TM
    wcols = w1.shape[1]
    tok = lambda width: pl.BlockSpec((1, TM, width), lambda bi, j: (bi, j, 0))
    tab = pl.BlockSpec((TM, LANE), lambda bi, j: (j, 0))
    widths = (ATT_Q, KV2, KV2, RET_W, RET_W, RET_W, 2 * RET_W, POOL_WIDTH, 3 * d)
    dtypes = (BF16,) * 7 + (F32, BF16)
    return pl.pallas_call(
        _inproj_kernel,
        grid=(b, nt),
        in_specs=[
            tok(d),
            pl.BlockSpec((1, 6, d), lambda bi, j: (jnp.where(j == 0, b, bi), 0, 0)),
            pl.BlockSpec((1, d), lambda bi, j: (0, 0)),
            pl.BlockSpec((d, wcols), lambda bi, j: (0, 0)),
            tab, tab, tab, tab,
        ],
        out_specs=[tok(w) for w in widths],
        out_shape=[jax.ShapeDtypeStruct((b, s, w), dt) for w, dt in zip(widths, dtypes)],
        compiler_params=_params("arbitrary", "arbitrary"),
    )(xm, mod_l, norm1_l.reshape(1, d), w1, *tables)


def _rope_tables(seq):
    rows = seq // GRID_W
    rpos = jnp.arange(rows, dtype=F32)[:, None]
    cpos = jnp.arange(GRID_W, dtype=F32)[:, None]
    lane = jnp.arange(LANE)
    grid = lambda per_row, per_col: (per_row[:, None, :] + per_col[None, :, :]).reshape(seq, LANE)
    axis_dim = HEAD_DIM // 2
    inv_a = ROPE_THETA ** (-jnp.arange(0, axis_dim, 2, dtype=F32) / axis_dim)
    hl = lane % HEAD_DIM
    inv_al = inv_a[(hl % axis_dim) % (axis_dim // 2)][None, :]
    by_row = (hl < axis_dim)[None, :]
    a_sign = jnp.where((hl % axis_dim) < axis_dim // 2, -1.0, 1.0)[None, :]
    a_cos = grid(jnp.where(by_row, jnp.cos(rpos * inv_al), 0.0), jnp.where(by_row, 0.0, jnp.cos(cpos * inv_al)))
    a_sin = grid(jnp.where(by_row, jnp.sin(rpos * inv_al), 0.0), jnp.where(by_row, 0.0, jnp.sin(cpos * inv_al)))
    inv_r = 1.0 / (ROPE_THETA ** jnp.linspace(0.0, 1.0, RET_DIM // 2, dtype=F32))
    rl = lane % RET_DIM
    inv_rl = inv_r[rl % (RET_DIM // 2)][None, :]
    r_sign = jnp.where(rl < RET_DIM // 2, -1.0, 1.0)[None, :]
    hi = (rpos * GRID_W) * inv_rl
    lo = cpos * inv_rl
    outer = lambda a, b: (a[:, None, :] * b[None, :, :]).reshape(seq, LANE)
    r_cos = outer(jnp.cos(hi), jnp.cos(lo)) - outer(jnp.sin(hi), jnp.sin(lo))
    r_sin = outer(jnp.sin(hi), jnp.cos(lo)) + outer(jnp.cos(hi), jnp.sin(lo))
    ones = jnp.ones((TM, LANE), F32)
    zeros = jnp.zeros((TM, LANE), F32)
    cat = lambda head, body: jnp.concatenate([head, body], axis=0)
    return (cat(ones, a_cos), cat(zeros, a_sin * a_sign), cat(ones, r_cos), cat(zeros, r_sin * r_sign))


def _attn_kernel(sink_ref, bias_ref, q_ref, kp_ref, kc_ref, kn_ref, kx_ref, vp_ref, vc_ref, vn_ref, vx_ref, o_ref):
    g_heads = N_Q_HEADS // N_KV_HEADS
    lane = lax.broadcasted_iota(I32, (AB, LANE), 1)
    lo = lane < HEAD_DIM
    rows = lax.broadcasted_iota(I32, (g_heads * AB, 1), 0)
    bias = bias_ref[0]
    outs = []
    for g in range(N_KV_HEADS):
        ks = slice(g * LANE, (g + 1) * LANE)
        kd = jnp.concatenate([kp_ref[0, :, ks], kc_ref[0, :, ks], kn_ref[0, :, ks], kx_ref[0, :, ks]], axis=0)
        vd = jnp.concatenate([vp_ref[0, :, ks], vc_ref[0, :, ks], vn_ref[0, :, ks], vx_ref[0, :, ks]], axis=0)
        qs = []
        for c in range(2 * g, 2 * g + 2):
            qc = q_ref[0, :, c * LANE:(c + 1) * LANE]
            zero = jnp.zeros_like(qc)
            qs += [jnp.where(lo, qc, zero), jnp.where(lo, zero, qc)]
        s = _dot_nt(jnp.concatenate(qs, axis=0), kd) + bias
        sink = jnp.full((g_heads * AB, 1), sink_ref[g_heads * g + g_heads - 1], F32)
        for t in range(g_heads - 2, -1, -1):
            sink = jnp.where(rows < (t + 1) * AB, sink_ref[g_heads * g + t], sink)
        m = jnp.maximum(jnp.max(s, axis=-1, keepdims=True), sink)
        p = jnp.exp(s - m)
        den = jnp.sum(p, axis=-1, keepdims=True) + jnp.exp(sink - m)
        o = _dot(p.astype(BF16), vd) / den
        for t in range(2):
            outs.append(jnp.where(lo, o[2 * t * AB:(2 * t + 1) * AB], o[(2 * t + 1) * AB:(2 * t + 2) * AB]))
    o_ref[0] = jnp.concatenate(outs, axis=-1).astype(BF16)


def _attn_bias():
    g_heads = N_Q_HEADS // N_KV_HEADS
    r = jnp.arange(g_heads * AB)[:, None] % AB
    j = jnp.arange(3 * AB + TM)[None, :]
    band = jnp.abs(j - AB - r) <= WINDOW
    is_ctx = j >= 3 * AB
    variants = (band, band & (j >= AB), band & (j < 2 * AB), jnp.zeros_like(band))
    return jnp.stack([jnp.where(v | is_ctx, 0.0, NEG) for v in variants]).astype(F32)


def _attention(q, k2, v2, sinks):
    b, s, _ = q.shape
    nb = s // AB
    assert nb - CTX_BLOCKS >= 2
    bias = _attn_bias()
    variant = lambda i: jnp.where(i < CTX_BLOCKS, 3, jnp.where(i == CTX_BLOCKS, 1, jnp.where(i == nb - 1, 2, 0)))
    prev = pl.BlockSpec((1, AB, KV2), lambda bi, i: (bi, jnp.maximum(i - 1, 0), 0))
    cur = pl.BlockSpec((1, AB, KV2), lambda bi, i: (bi, i, 0))
    nxt = pl.BlockSpec((1, AB, KV2), lambda bi, i: (bi, jnp.minimum(i + 1, nb - 1), 0))
    cx = pl.BlockSpec((1, TM, KV2), lambda bi, i: (bi, 0, 0))
    return pl.pallas_call(
        _attn_kernel,
        grid=(b, nb),
        in_specs=[pl.BlockSpec(memory_space=pltpu.SMEM),
                  pl.BlockSpec((1,) + bias.shape[1:], lambda bi, i: (variant(i), 0, 0)),
                  pl.BlockSpec((1, AB, ATT_Q), lambda bi, i: (bi, i, 0)),
                  prev, cur, nxt, cx, prev, cur, nxt, cx],
        out_specs=pl.BlockSpec((1, AB, ATT_Q), lambda bi, i: (bi, i, 0)),
        out_shape=jax.ShapeDtypeStruct((b, s, ATT_Q), BF16),
        compiler_params=_params("arbitrary", "arbitrary"),
    )(sinks, bias, q, k2, k2, k2, k2, v2, v2, v2, v2)


def _ret_kernel(lg_ref, qf_ref, kf_ref, vf_ref, qb_ref, kb_ref, vb_ref, yf_ref, yb_ref,
                st_ref, dm_ref, qd_ref, kd_ref, cd_ref, *, batch):
    step = pl.program_id(0)
    lane = lax.broadcasted_iota(I32, (AB, LANE), 1)
    row = lax.broadcasted_iota(I32, (AB, LANE), 0)
    lo = lane < RET_DIM
    tiles = RET_W // LANE

    @pl.when(step == 0)
    def _():
        st_ref[...] = jnp.zeros_like(st_ref)
        ii = row.astype(F32)
        jj = lane.astype(F32)
        for d in range(2):
            for c in range(tiles):
                lg0 = lg_ref[d * RET_HEADS + 2 * c]
                lg1 = lg_ref[d * RET_HEADS + 2 * c + 1]
                lgl = jnp.where(lo, lg0, lg1)
                q_exp = ii + 1.0 if d == 0 else AB - ii
                k_exp = (AB - 1.0) - ii if d == 0 else ii
                qd_ref[d * tiles + c] = jnp.exp(q_exp * lgl)
                kd_ref[d * tiles + c] = jnp.exp(k_exp * lgl)
                cd_ref[d * tiles + c] = jnp.exp(AB * lgl)
                rel = ii - jj if d == 0 else jj - ii
                for hh, lgh in enumerate((lg0, lg1)):
                    dm_ref[d * tiles + c, hh * AB:(hh + 1) * AB, :] = jnp.where(
                        rel >= 0, jnp.exp(jnp.maximum(rel, 0.0) * lgh), 0.0)

    same_head = (row < RET_DIM) == lo
    dirs = ((qf_ref, kf_ref, vf_ref, yf_ref), (qb_ref, kb_ref, vb_ref, yb_ref))
    for d, (q_ref, k_ref, v_ref, y_ref) in enumerate(dirs):
        for b in range(batch):
            for c in range(tiles):
                sl = slice(c * LANE, (c + 1) * LANE)
                t = d * tiles + c
                q = q_ref[b, :, sl]
                k = k_ref[b, :, sl]
                v = v_ref[b, :, sl]
                zero = jnp.zeros_like(q)
                q2 = jnp.concatenate([jnp.where(lo, q, zero), jnp.where(lo, zero, q)], axis=0)
                p = (_dot_nt(q2, k) * dm_ref[t]).astype(BF16)
                y_intra = jnp.where(lo, _dot(p[:AB], v), _dot(p[AB:], v))
                si = (d * batch + b) * tiles + c
                state = st_ref[si]
                q_dec = (q.astype(F32) * qd_ref[t]).astype(BF16)
                y_ref[b, :, sl] = y_intra + _dot(q_dec, state.astype(BF16))
                k_dec = (k.astype(F32) * kd_ref[t]).astype(BF16)
                st_ref[si] = state * cd_ref[t] + jnp.where(same_head, _dot_tn(k_dec, v), 0.0)


def _retention(rq, rk, rv, log_g):
    b, s, w = rq.shape
    nb = s // AB
    tiles = w // LANE

    def back(i):
        return jnp.where(i < CTX_BLOCKS, CTX_BLOCKS - 1 - i, nb - 1 + CTX_BLOCKS - i)

    fwd = pl.BlockSpec((b, AB, w), lambda i: (0, i, 0))
    bwd = pl.BlockSpec((b, AB, w), lambda i: (0, back(i), 0))
    return pl.pallas_call(
        functools.partial(_ret_kernel, batch=b),
        grid=(nb,),
        in_specs=[pl.BlockSpec(memory_space=pltpu.SMEM), fwd, fwd, fwd, bwd, bwd, bwd],
        out_specs=[fwd, bwd],
        out_shape=[jax.ShapeDtypeStruct((b, s, w), F32)] * 2,
        scratch_shapes=[
            pltpu.VMEM((2 * b * tiles, LANE, LANE), F32),
            pltpu.VMEM((2 * tiles, 2 * AB, LANE), F32),
            pltpu.VMEM((2 * tiles, AB, LANE), F32),
            pltpu.VMEM((2 * tiles, AB, LANE), F32),
            pltpu.VMEM((2 * tiles, AB, LANE), F32),
        ],
        compiler_params=_params("arbitrary"),
    )(log_g.reshape(-1), rq, rk, rv, rq, rk, rv)


def _merge_kernel(x_ref, attn_ref, up_ref, uc_ref, un_ref, yf_ref, yb_ref, rg_ref, g_ref, mod_ref,
                  wbd_ref, ps_ref, woa_ref, wop_ref, wor_ref, wout_ref, n2_ref, wrh_ref, wrl_ref,
                  br_ref, avg_ref, tri_ref,
                  xo_ref, h2_ref, ls_ref, lst_ref, rw_ref, cnt_ref, *, seq, nt):
    i = pl.program_id(0)
    j = i % nt

    ext = TM + 2 * POOL_HALO
    seq_len = jnp.where(j == 0, TM, seq)
    start = jnp.where(j == 0, 0, (j - 1) * TM)
    u = uc_ref[0]
    ue = jnp.concatenate([up_ref[0], u, un_ref[0]], axis=0)
    erow = lax.broadcasted_iota(I32, (ext, 1), 0) + (start - POOL_HALO)
    ue = jnp.where((erow >= 0) & (erow < seq_len), ue, 0.0)
    t_pos = lax.broadcasted_iota(I32, (TM, 1), 0) + start
    glane = lax.broadcasted_iota(I32, (TM, POOL_WIDTH), 1) // (POOL_WIDTH // len(POOL_WINDOWS))
    run = ue
    width = 1
    diff = jnp.zeros((TM, POOL_WIDTH), F32)
    for gi, w in enumerate(POOL_WINDOWS):
        while width < w:
            run = run + pltpu.roll(run, ext - width, 0)
            width *= 2
        win = pltpu.roll(run, w // 2, 0)[POOL_HALO:POOL_HALO + TM]
        cnt = jnp.minimum(t_pos - w // 2 + w, seq_len) - jnp.maximum(t_pos - w // 2, 0)
        diff = jnp.where(glane == gi, win / cnt.astype(F32) - u, diff)
    pool = _dot(diff.astype(BF16), wbd_ref[...]) * ps_ref[...]

    def head_norm(y):
        mu = _dot_hilo(y, avg_ref[...])
        dlt = y - mu
        var = _dot_hilo(dlt * dlt, avg_ref[...])
        return dlt * lax.rsqrt(var + GN_EPS)

    rg = rg_ref[0].astype(F32)
    ret = head_norm(yf_ref[0]) * _silu(rg[:, :RET_W]) + head_norm(yb_ref[0]) * _silu(rg[:, RET_W:])

    d = x_ref.shape[-1]
    gate = lambda t: _sigmoid(g_ref[0, :, t * d:(t + 1) * d].astype(F32))
    m = (gate(0) * _dot(attn_ref[0], woa_ref[...])
         + gate(1) * _dot(pool.astype(BF16), wop_ref[...])
         + gate(2) * _dot(ret.astype(BF16), wor_ref[...]))
    xn = x_ref[0] + mod_ref[0, 2:3, :] * _dot(m.astype(BF16), wout_ref[...])
    xo_ref[0] = xn

    ms = jnp.mean(xn * xn, axis=-1, keepdims=True)
    h2 = xn * lax.rsqrt(ms + NORM_EPS) * n2_ref[...] * (1.0 + mod_ref[0, 4:5, :]) + mod_ref[0, 3:4, :]
    h2_ref[0] = h2.astype(BF16)
    hh, hl = _split(h2)
    logits = _dot(hh, wrh_ref[...]) + _dot(hl, wrh_ref[...]) + _dot(hh, wrl_ref[...]) + br_ref[...]
    lanef = lax.broadcasted_iota(I32, (TM, LANE), 1).astype(F32)
    tops, hots = [], []
    for _ in range(TOP_K):
        mx = jnp.max(logits, axis=-1, keepdims=True)
        idx = jnp.min(jnp.where(logits == mx, lanef, float(LANE)), axis=-1, keepdims=True)
        hot = lanef == idx
        logits = jnp.where(hot, NEG * 2.0, logits)
        tops.append((mx, idx))
        hots.append(hot)
    ex = [jnp.exp(mx - tops[0][0]) for mx, _ in tops]
    tot = ex[0]
    for e in ex[1:]:
        tot = tot + e
    sel = jnp.zeros((TM, LANE), F32)
    for hot in hots:
        sel = jnp.where(hot, 1.0, sel)
    incl = _dot(tri_ref[...], sel.astype(BF16))
    cnt = incl[TM - 1:TM, :]
    run = jnp.floor((cnt + (SUBLANE - 1.0)) * (1.0 / SUBLANE)) * SUBLANE
    er = lax.broadcasted_iota(I32, (LANE, LANE), 0)
    ec = lax.broadcasted_iota(I32, (LANE, LANE), 1)
    earlier = jnp.where(er < ec, 1.0, 0.0).astype(BF16)
    run_start = _dot(jnp.broadcast_to(run, (SUBLANE, LANE)).astype(BF16), earlier)[0:1]
    slot = run_start + incl - 1.0
    ls = jnp.zeros((TM, LANE), F32)
    rw = jnp.zeros((TM, LANE), F32)
    for k in range(TOP_K):
        ls = jnp.where(lanef == k, jnp.sum(jnp.where(hots[k], slot, 0.0), axis=-1, keepdims=True), ls)
        rw = jnp.where(lanef == k, ex[k] / tot, rw)
    ls = jnp.where(lanef < TOP_K, ls, -1.0)
    ls_ref[...] = ls.astype(I32)
    lst_ref[...] = jnp.transpose(ls)[:SUBLANE, :].astype(I32)
    rw_ref[...] = rw
    cnt_ref[...] = jnp.broadcast_to(cnt, cnt_ref.shape).astype(I32)


def _merge(xm, attn, u, yf, yb, rg, gates, mod_l, seq, wts):
    b, s, d = xm.shape
    nt = s // TM
    ntt = b * nt
    hb = TM // POOL_HALO
    nh = s // POOL_HALO
    tok = lambda width: pl.BlockSpec((1, TM, width), lambda i: (i // nt, i % nt, 0))
    full = lambda a: pl.BlockSpec(a.shape, lambda i: (0,) * a.ndim)
    lanes = pl.BlockSpec((TM, LANE), lambda i: (i, 0))
    in_specs = [
        tok(d), tok(ATT_Q),
        pl.BlockSpec((1, POOL_HALO, POOL_WIDTH), lambda i: (i // nt, jnp.maximum((i % nt) * hb - 1, 0), 0)),
        tok(POOL_WIDTH),
        pl.BlockSpec((1, POOL_HALO, POOL_WIDTH), lambda i: (i // nt, jnp.minimum((i % nt + 1) * hb, nh - 1), 0)),
        tok(RET_W), tok(RET_W), tok(2 * RET_W), tok(3 * d),
        pl.BlockSpec((1, 6, d), lambda i: (jnp.where(i % nt == 0, b, i // nt), 0, 0)),
    ] + [full(a) for a in wts]
    return pl.pallas_call(
        functools.partial(_merge_kernel, seq=seq, nt=nt),
        grid=(ntt,),
        in_specs=in_specs,
        out_specs=[tok(d), tok(d), lanes, pl.BlockSpec((SUBLANE, TM), lambda i: (i, 0)), lanes,
                   pl.BlockSpec((SUBLANE, LANE), lambda i: (i, 0))],
        out_shape=[jax.ShapeDtypeStruct((b, s, d), F32), jax.ShapeDtypeStruct((b, s, d), BF16),
                   jax.ShapeDtypeStruct((b * s, LANE), I32), jax.ShapeDtypeStruct((ntt * SUBLANE, TM), I32),
                   jax.ShapeDtypeStruct((b * s, LANE), F32), jax.ShapeDtypeStruct((ntt * SUBLANE, LANE), I32)],
        compiler_params=_params("arbitrary"),
    )(xm, attn, u, u, u, yf, yb, rg, gates, mod_l, *wts)


def _rows(tile_index):
    return pl.ds(pl.multiple_of(tile_index * SUBLANE, SUBLANE), SUBLANE)


def _tile_copies(table_ref, tile, copy):
    def body(j, carry):
        copy(j, table_ref[tile * SORT_TILES + j])
        return carry

    lax.fori_loop(0, SORT_TILES, body, 0, unroll=8)


def _tile_waits(wait):
    def body(j, carry):
        wait(j)
        return carry

    lax.fori_loop(0, SORT_TILES, body, 0, unroll=8)


def _dispatch_kernel(dst_ref, fs_ref, fn_ref, nu_ref, h_ref, lst_ref, xs_hbm, buf, zbuf, sem, fill_sem,
                     *, n_blocks):
    i = pl.program_id(0)
    last = pl.num_programs(0) - 1
    cur = i % 2

    def send(tile, half):
        _tile_copies(dst_ref, tile, lambda j, t: pltpu.make_async_copy(
            buf.at[half, _rows(j)], xs_hbm.at[_rows(t)], sem.at[half]).start())

    def drain(half):
        _tile_waits(lambda j: pltpu.make_async_copy(
            buf.at[half, _rows(j)], xs_hbm.at[_rows(j)], sem.at[half]).wait())

    @pl.when(i == 0)
    def _():
        zbuf[...] = jnp.zeros_like(zbuf)

        def pad_copy(e, c):
            return pltpu.make_async_copy(zbuf.at[pl.ds(0, SUBLANE)], xs_hbm.at[_rows(fs_ref[e] + c)], fill_sem)

        def blk_copy(blk):
            return pltpu.make_async_copy(zbuf, xs_hbm.at[pl.ds(pl.multiple_of(blk * EB, EB), EB)], fill_sem)

        def per_expert(fn):
            def outer(e, carry):
                def inner(c, cc):
                    fn(e, c)
                    return cc
                return lax.fori_loop(0, fn_ref[e], inner, carry)
            lax.fori_loop(0, N_EXPERTS, outer, 0)

        def per_block(fn):
            def body(blk, carry):
                fn(blk)
                return carry
            lax.fori_loop(nu_ref[0], n_blocks, body, 0)

        per_expert(lambda e, c: pad_copy(e, c).start())
        per_block(lambda blk: blk_copy(blk).start())
        per_expert(lambda e, c: pad_copy(e, c).wait())
        per_block(lambda blk: blk_copy(blk).wait())

    slot = lax.broadcasted_iota(I32, (SORT_ROWS, TM), 0)
    lst = lst_ref[...]
    p = jnp.zeros((SORT_ROWS, TM), F32)
    for k in range(TOP_K):
        p = jnp.where(slot == lst[k:k + 1, :], 1.0, p)
    srt = _dot(p.astype(BF16), h_ref[...])

    @pl.when(i >= 2)
    def _():
        drain(cur)

    buf[cur] = srt
    send(i, cur)

    @pl.when(i == last)
    def _():
        @pl.when(i >= 1)
        def _():
            drain(1 - cur)
        drain(cur)


def _dispatch(h2, lst, tabs, n_used, n_blocks):
    t, d = h2.shape
    ntt = t // TM
    grid_spec = pltpu.PrefetchScalarGridSpec(
        num_scalar_prefetch=4,
        grid=(ntt,),
        in_specs=[pl.BlockSpec((TM, d), lambda i, *_: (i, 0)),
                  pl.BlockSpec((SUBLANE, TM), lambda i, *_: (i, 0))],
        out_specs=pl.BlockSpec(memory_space=pl.ANY),
        scratch_shapes=[pltpu.VMEM((2, SORT_ROWS, d), F32), pltpu.VMEM((EB, d), F32),
                        pltpu.SemaphoreType.DMA((2,)), pltpu.SemaphoreType.DMA],
    )
    return pl.pallas_call(
        functools.partial(_dispatch_kernel, n_blocks=n_blocks),
        grid_spec=grid_spec,
        out_shape=jax.ShapeDtypeStruct((n_blocks * EB, d), F32),
        compiler_params=_params("arbitrary"),
    )(tabs["dst"], tabs["fill_start"], tabs["fill_n"], n_used, h2, lst)


def _expert_kernel(be_ref, nu_ref, x_ref, w1_ref, b1_ref, w2_ref, b2_ref, y_ref, w1b, w2b):
    bi = pl.program_id(0)
    used = bi < nu_ref[0]
    fresh = (bi == 0) | (be_ref[bi] != be_ref[jnp.maximum(bi - 1, 0)])

    @pl.when(used & fresh)
    def _():
        w1b[...] = w1_ref[0, 0].astype(BF16)
        w2b[...] = w2_ref[0, 0].astype(BF16)

    @pl.when(jnp.logical_not(used))
    def _():
        y_ref[...] = jnp.zeros_like(y_ref)

    @pl.when(used)
    def _():
        xb = x_ref[...].astype(BF16)
        y = None
        for c in range(0, D_FF, FF_CHUNK):
            glu = _dot(xb, w1b[:, c:c + FF_CHUNK]) + b1_ref[0, 0, :, c:c + FF_CHUNK]
            lin = _dot(xb, w1b[:, D_FF + c:D_FF + c + FF_CHUNK]) + b1_ref[0, 0, :, D_FF + c:D_FF + c + FF_CHUNK]
            glu = jnp.minimum(glu, SWIGLU_LIMIT)
            lin = jnp.clip(lin, -SWIGLU_LIMIT, SWIGLU_LIMIT)
            act = glu * _sigmoid(SWIGLU_ALPHA * glu) * (lin + 1.0)
            part = _dot(act.astype(BF16), w2b[c:c + FF_CHUNK, :])
            y = part if y is None else y + part
        y_ref[...] = y + b2_ref[0, 0]


def _experts(xs, block_e, n_used, w1, b1, w2, b2, layer):
    rows, d = xs.shape
    depth, ne, _, f2 = w1.shape
    row_blk = lambda bi, be, nu: (bi, 0)
    grid_spec = pltpu.PrefetchScalarGridSpec(
        num_scalar_prefetch=2,
        grid=(rows // EB,),
        in_specs=[
            pl.BlockSpec((EB, d), row_blk),
            pl.BlockSpec((1, 1, d, f2), lambda bi, be, nu: (layer, be[bi], 0, 0)),
            pl.BlockSpec((1, 1, 1, f2), lambda bi, be, nu: (layer, be[bi], 0, 0)),
            pl.BlockSpec((1, 1, f2 // 2, d), lambda bi, be, nu: (layer, be[bi], 0, 0)),
            pl.BlockSpec((1, 1, 1, d), lambda bi, be, nu: (layer, be[bi], 0, 0)),
        ],
        out_specs=pl.BlockSpec((EB, d), row_blk),
        scratch_shapes=[pltpu.VMEM((d, f2), BF16), pltpu.VMEM((f2 // 2, d), BF16)],
    )
    return pl.pallas_call(
        _expert_kernel,
        grid_spec=grid_spec,
        out_shape=jax.ShapeDtypeStruct((rows, d), F32),
        compiler_params=_params("arbitrary"),
    )(block_e, n_used, xs, w1, b1.reshape(depth, ne, 1, f2), w2, b2.reshape(depth, ne, 1, d))


def _combine_kernel(dst_ref, x_ref, ls_ref, rw_ref, mod_ref, fg_ref, ys_hbm, xo_ref, buf, sem, *, final):
    i = pl.program_id(0)
    last = pl.num_programs(0) - 1
    cur = i % 2

    def fetch(tile, half):
        _tile_copies(dst_ref, tile, lambda j, t: pltpu.make_async_copy(
            ys_hbm.at[_rows(t)], buf.at[half, _rows(j)], sem.at[half]).start())

    @pl.when(i == 0)
    def _():
        fetch(0, 0)

    @pl.when(i < last)
    def _():
        fetch(i + 1, 1 - cur)

    _tile_waits(lambda j: pltpu.make_async_copy(
        ys_hbm.at[_rows(j)], buf.at[cur, _rows(j)], sem.at[cur]).wait())
    yb = buf[cur].astype(BF16)
    slot = lax.broadcasted_iota(I32, (TM, SORT_ROWS), 1)
    ls = ls_ref[...]
    rw = rw_ref[...]
    g = jnp.zeros((TM, SORT_ROWS), F32)
    for k in range(TOP_K):
        g = jnp.where(slot == ls[:, k:k + 1], rw[:, k:k + 1], g)
    gh, gl = _split(g)
    xn = x_ref[0] + mod_ref[0, 5:6, :] * (_dot(gh, yb) + _dot(gl, yb))
    if final:
        ms = jnp.mean(xn * xn, axis=-1, keepdims=True)
        xn = xn * lax.rsqrt(ms + NORM_EPS) * fg_ref[...]
    xo_ref[0] = xn


def _combine(xn, ls, rw, mod_l, final_gain, ys, tabs, final):
    b, s, d = xn.shape
    nt = s // TM
    tok = pl.BlockSpec((1, TM, d), lambda i, *_: (i // nt, i % nt, 0))
    lanes = pl.BlockSpec((TM, LANE), lambda i, *_: (i, 0))
    if final:
        out_spec = pl.BlockSpec((1, TM, d), lambda i, *_: (i // nt, jnp.maximum(i % nt - 1, 0), 0))
        out_shape = jax.ShapeDtypeStruct((b, s - TM, d), F32)
    else:
        out_spec, out_shape = tok, jax.ShapeDtypeStruct((b, s, d), F32)
    grid_spec = pltpu.PrefetchScalarGridSpec(
        num_scalar_prefetch=1,
        grid=(b * nt,),
        in_specs=[tok, lanes, lanes,
                  pl.BlockSpec((1, 6, d), lambda i, *_: (jnp.where(i % nt == 0, b, i // nt), 0, 0)),
                  pl.BlockSpec((1, d), lambda i, *_: (0, 0)),
                  pl.BlockSpec(memory_space=pl.ANY)],
        out_specs=out_spec,
        scratch_shapes=[pltpu.VMEM((2, SORT_ROWS, d), F32), pltpu.SemaphoreType.DMA((2,))],
    )
    return pl.pallas_call(
        functools.partial(_combine_kernel, final=final),
        grid_spec=grid_spec,
        out_shape=out_shape,
        compiler_params=_params("arbitrary"),
    )(tabs["dst"], xn, ls, rw, mod_l, final_gain.reshape(1, d), ys)


def _inproj_weight(w_in_l):
    d = w_in_l.shape[0]
    sizes = (ATT_Q, N_KV_HEADS * HEAD_DIM, N_KV_HEADS * HEAD_DIM, POOL_WIDTH,
             RET_W, RET_W, RET_W, RET_W, RET_W, d, d, d)
    parts, off = [], 0
    for sz in sizes:
        parts.append(w_in_l[:, off:off + sz])
        off += sz
    q, k, v, u, rq, rk, rv, rgf, rgb, ga, gp, gr = parts
    twice = lambda w: jnp.concatenate(
        [w[:, h * HEAD_DIM:(h + 1) * HEAD_DIM] for h in range(N_KV_HEADS) for _ in range(2)], axis=1)
    cols = [q * HEAD_DIM ** -0.5, twice(k), twice(v), rq, rk * RET_DIM ** -0.5, rv, rgf, rgb, u, ga, gp, gr]
    return jnp.concatenate(cols, axis=1).astype(BF16)


def _block_diag(blocks):
    n, r, c = blocks.shape
    out = jnp.zeros((n * r, n * c), blocks.dtype)
    for g in range(n):
        out = out.at[g * r:(g + 1) * r, g * c:(g + 1) * c].set(blocks[g])
    return out


def _routing_tables(cnt, n_blocks):
    ntt = cnt.shape[0] // SUBLANE
    counts = cnt.reshape(ntt, SUBLANE, LANE)[:, 0, :N_EXPERTS]
    run = (counts + SUBLANE - 1) // SUBLANE
    total = jnp.sum(run, axis=0)
    eb = EB // SUBLANE
    padded = (total + eb - 1) // eb * eb
    pad_end = jnp.cumsum(padded)
    pad_start = pad_end - padded
    off = pad_start[None, :] + jnp.cumsum(run, axis=0) - run
    run_end = jnp.cumsum(run, axis=1)
    j = jnp.arange(SORT_TILES)
    owner = jnp.sum(run_end[:, None, :] <= j[None, :, None], axis=2)
    mine = owner[:, :, None] == jnp.arange(N_EXPERTS)[None, None, :]
    in_region = j[None, :] + jnp.sum(jnp.where(mine, (off - (run_end - run))[:, None, :], 0), axis=2)
    spare = (n_blocks - SPARE_BLOCKS) * eb + (jnp.arange(ntt) % 2)[:, None] * SORT_TILES + j[None, :]
    dst = jnp.where(owner < N_EXPERTS, in_region, spare)
    n_used = pad_end[-1] // eb
    blk = jnp.minimum(jnp.arange(n_blocks), n_used - 1) * eb
    block_e = jnp.minimum(jnp.sum(pad_end[None, :] <= blk[:, None], axis=1), N_EXPERTS - 1)
    tabs = dict(dst=dst.reshape(-1).astype(I32), fill_start=(pad_start + total).astype(I32),
                fill_n=(padded - total).astype(I32))
    return tabs, block_e.astype(I32), n_used.reshape(1).astype(I32)


def kernel(x, c, ctx, c_ctx, w_mod, b_mod, norm1, norm2, w_in, attn_sinks, pool_w, pool_scale, ret_decay,
           w_o_attn, w_o_pool, w_o_ret, w_out, w_router, b_router, w_expert_in, b_expert_in, w_expert_out,
           b_expert_out, final_norm):
    b, seq, d = x.shape
    depth = w_mod.shape[0]
    assert ctx.shape[1] == TM and seq % TM == 0 and seq % GRID_W == 0
    s = seq + TM
    xm = jnp.concatenate([ctx, x], axis=1)
    mod = _modulation(c, c_ctx, w_mod, b_mod)
    tables = _rope_tables(seq)
    ntt = b * s // TM
    n_blocks = -(-(b * s * TOP_K + ntt * N_EXPERTS * (SUBLANE - 1)) // EB) + N_EXPERTS + SPARE_BLOCKS
    avg = _block_diag(jnp.full((RET_HEADS, RET_DIM, RET_DIM), 1.0 / RET_DIM, F32)).astype(BF16)
    tri = (jnp.arange(TM)[:, None] >= jnp.arange(TM)[None, :]).astype(BF16)
    for l in range(depth):
        q, k2, v2, rq, rk, rv, rg, u, gates = _inproj(xm, mod[l], norm1[l], _inproj_weight(w_in[l]), tables)
        attn = _attention(q, k2, v2, attn_sinks[l])
        yf, yb = _retention(rq, rk, rv, jax.nn.log_sigmoid(ret_decay[l].astype(F32)))
        wr = jnp.zeros((d, LANE), F32).at[:, :N_EXPERTS].set(w_router[l])
        wrh, wrl = _split(wr)
        br = jnp.full((1, LANE), NEG, F32).at[0, :N_EXPERTS].set(b_router[l])
        wts = (_block_diag(pool_w[l]).astype(BF16), pool_scale[l].reshape(1, -1),
               w_o_attn[l].astype(BF16), w_o_pool[l].astype(BF16), w_o_ret[l].astype(BF16),
               w_out[l].astype(BF16), norm2[l].reshape(1, d), wrh, wrl, br, avg, tri)
        xn, h2, ls, lst, rw, cnt = _merge(xm, attn, u, yf, yb, rg, gates, mod[l], seq, wts)
        tabs, block_e, n_used = _routing_tables(cnt, n_blocks)
        xs = _dispatch(h2.reshape(b * s, d), lst, tabs, n_used, n_blocks)
        ys = _experts(xs, block_e, n_used, w_expert_in, b_expert_in, w_expert_out, b_expert_out, l)
        xm = _combine(xn, ls, rw, mod[l], final_norm, ys, tabs, final=l == depth - 1)
    return xm
```

```python
import functools

import jax
import jax.numpy as jnp
from jax import lax
from jax.experimental import pallas as pl
from jax.experimental.pallas import tpu as pltpu

F32 = jnp.float32
BF16 = jnp.bfloat16
I32 = jnp.int32

GRID_W = 64
HEAD_DIM = 64
N_Q_HEADS = 8
N_KV_HEADS = 2
WINDOW = 128
ROPE_THETA = 10000.0
POOL_WIDTH = 256
POOL_WINDOWS = (2, 4, 8, 16)
RET_HEADS = 4
RET_DIM = 64
N_EXPERTS = 32
TOP_K = 4
D_FF = 1024
SWIGLU_LIMIT = 7.0
SWIGLU_ALPHA = 1.702
NORM_EPS = 1e-6
GN_EPS = 1e-5

LANE = 128
SUBLANE = 8
MXU_N = 256
TM = 256
AB = 128
IN_TILE = 640
CTX_BLOCKS = TM // AB
EB = 512
FF_CHUNK = 256
POOL_HALO = SUBLANE
SORT_ROWS = TM * TOP_K + N_EXPERTS * SUBLANE
SORT_TILES = SORT_ROWS // SUBLANE
SPARE_BLOCKS = 2 * SORT_ROWS // EB
NEG = -1e30
VMEM_LIMIT = 56 * 1024 * 1024

ATT_Q = N_Q_HEADS * HEAD_DIM
KV2 = 2 * N_KV_HEADS * HEAD_DIM
RET_W = RET_HEADS * RET_DIM


def _dot(a, b):
    return jnp.dot(a, b, preferred_element_type=F32)


def _dot_nt(a, b):
    return lax.dot_general(a, b, (((1,), (1,)), ((), ())), preferred_element_type=F32)


def _dot_tn(a, b):
    return lax.dot_general(a, b, (((0,), (0,)), ((), ())), preferred_element_type=F32)


def _split(a):
    hi = a.astype(BF16)
    lo = (a - hi.astype(F32)).astype(BF16)
    return hi, lo


def _dot_hilo(a, m):
    hi, lo = _split(a)
    return _dot(hi, m) + _dot(lo, m)


def _sigmoid(x):
    return 0.5 * jnp.tanh(0.5 * x) + 0.5


def _silu(x):
    return x * _sigmoid(x)


def _params(*sem):
    return pltpu.CompilerParams(dimension_semantics=sem, vmem_limit_bytes=VMEM_LIMIT)


def _mod_kernel(c_ref, w_ref, b_ref, o_ref):
    s = _silu(c_ref[...])
    sh, sl = _split(s)
    wh, wl = _split(w_ref[0])
    o_ref[0] = _dot(sh, wh) + _dot(sh, wl) + _dot(sl, wh) + b_ref[0]


def _modulation(c, c_ctx, w_mod, b_mod):
    depth, d, six_d = w_mod.shape
    b = c.shape[0]
    cc = jnp.zeros((SUBLANE, d), F32).at[:b].set(c).at[b].set(c_ctx)
    out = pl.pallas_call(
        _mod_kernel,
        grid=(depth, six_d // d),
        in_specs=[
            pl.BlockSpec((SUBLANE, d), lambda l, n: (0, 0)),
            pl.BlockSpec((1, d, d), lambda l, n: (l, 0, n)),
            pl.BlockSpec((1, 1, d), lambda l, n: (l, 0, n)),
        ],
        out_specs=pl.BlockSpec((1, SUBLANE, d), lambda l, n: (l, 0, n)),
        out_shape=jax.ShapeDtypeStruct((depth, SUBLANE, six_d), F32),
        compiler_params=_params("arbitrary", "arbitrary"),
    )(cc, w_mod, b_mod.reshape(depth, 1, six_d))
    return out[:, : b + 1].reshape(depth, b + 1, 6, d)


def _rope(x, cos, sin, half, first):
    partner = jnp.where(first, pltpu.roll(x, LANE - half, 1), pltpu.roll(x, half, 1))
    return x * cos + partner * sin


def _inproj_kernel(x_ref, mod_ref, modc_ref, n1_ref, w_ref, ac_ref, as_ref, rc_ref, rs_ref,
                   q_ref, k_ref, v_ref, rq_ref, rk_ref, rv_ref, rg_ref, u_ref, g_ref):
    x = x_ref[0]
    rows = x.shape[0]
    ms = jnp.mean(x * x, axis=-1, keepdims=True)
    y = x * lax.rsqrt(ms + NORM_EPS) * n1_ref[...]
    is_ctx = (lax.broadcasted_iota(I32, (rows, 1), 0) < TM) & (pl.program_id(1) == 0)
    scale = jnp.where(is_ctx, modc_ref[0, 1:2, :], mod_ref[0, 1:2, :])
    shift = jnp.where(is_ctx, modc_ref[0, 0:1, :], mod_ref[0, 0:1, :])
    hb = (y * (1.0 + scale) + shift).astype(BF16)
    lane = lax.broadcasted_iota(I32, (rows, LANE), 1)
    a_first = (lane % (HEAD_DIM // 2)) < (HEAD_DIM // 4)
    r_first = (lane % RET_DIM) < (RET_DIM // 2)
    ac, asn, rc, rsn = ac_ref[...], as_ref[...], rc_ref[...], rs_ref[...]

    def proj(off, width):
        return _dot(hb, w_ref[:, off:off + width])

    def rotated(ref, width, cos, sin, half, first, off):
        for t in range(0, width, MXU_N):
            pr = proj(off + t, MXU_N)
            for g in range(0, MXU_N, LANE):
                ref[0, :, t + g:t + g + LANE] = _rope(pr[:, g:g + LANE], cos, sin, half, first).astype(BF16)

    off = 0
    rotated(q_ref, ATT_Q, ac, asn, HEAD_DIM // 4, a_first, off)
    off += ATT_Q
    rotated(k_ref, KV2, ac, asn, HEAD_DIM // 4, a_first, off)
    off += KV2
    v_ref[0] = proj(off, KV2).astype(BF16)
    off += KV2
    for ref in (rq_ref, rk_ref):
        rotated(ref, RET_W, rc, rsn, RET_DIM // 2, r_first, off)
        off += RET_W
    rv_ref[0] = proj(off, RET_W).astype(BF16)
    off += RET_W
    rg_ref[0] = proj(off, 2 * RET_W).astype(BF16)
    off += 2 * RET_W
    u_ref[0] = proj(off, POOL_WIDTH)
    off += POOL_WIDTH
    d = x.shape[-1]
    for t in range(3):
        g_ref[0, :, t * d:(t + 1) * d] = proj(off, d).astype(BF16)
        off += d


def _inproj(xm, mod_l, norm1_l, w1, tables):
    b, s, d = xm.shape
    ti = IN_TILE if s % IN_TILE == 0 else TM
    wcols = w1.shape[1]
    tok = lambda width: pl.BlockSpec((1, ti, width), lambda bi, j: (bi, j, 0))
    tab = pl.BlockSpec((ti, LANE), lambda bi, j: (j, 0))
    widths = (ATT_Q, KV2, KV2, RET_W, RET_W, RET_W, 2 * RET_W, POOL_WIDTH, 3 * d)
    dtypes = (BF16,) * 7 + (F32, BF16)
    return pl.pallas_call(
        _inproj_kernel,
        grid=(b, s // ti),
        in_specs=[
            tok(d),
            pl.BlockSpec((1, 6, d), lambda bi, j: (bi, 0, 0)),
            pl.BlockSpec((1, 6, d), lambda bi, j: (b, 0, 0)),
            pl.BlockSpec((1, d), lambda bi, j: (0, 0)),
            pl.BlockSpec((d, wcols), lambda bi, j: (0, 0), pipeline_mode=pl.Buffered(1)),
            tab, tab, tab, tab,
        ],
        out_specs=[tok(w) for w in widths],
        out_shape=[jax.ShapeDtypeStruct((b, s, w), dt) for w, dt in zip(widths, dtypes)],
        compiler_params=_params("arbitrary", "arbitrary"),
    )(xm, mod_l, mod_l, norm1_l.reshape(1, d), w1, *tables)


def _rope_tables(seq):
    rows = seq // GRID_W
    rpos = jnp.arange(rows, dtype=F32)[:, None]
    cpos = jnp.arange(GRID_W, dtype=F32)[:, None]
    lane = jnp.arange(LANE)
    grid = lambda per_row, per_col: (per_row[:, None, :] + per_col[None, :, :]).reshape(seq, LANE)
    axis_dim = HEAD_DIM // 2
    inv_a = ROPE_THETA ** (-jnp.arange(0, axis_dim, 2, dtype=F32) / axis_dim)
    hl = lane % HEAD_DIM
    inv_al = inv_a[(hl % axis_dim) % (axis_dim // 2)][None, :]
    by_row = (hl < axis_dim)[None, :]
    a_sign = jnp.where((hl % axis_dim) < axis_dim // 2, -1.0, 1.0)[None, :]
    a_cos = grid(jnp.where(by_row, jnp.cos(rpos * inv_al), 0.0), jnp.where(by_row, 0.0, jnp.cos(cpos * inv_al)))
    a_sin = grid(jnp.where(by_row, jnp.sin(rpos * inv_al), 0.0), jnp.where(by_row, 0.0, jnp.sin(cpos * inv_al)))
    inv_r = 1.0 / (ROPE_THETA ** jnp.linspace(0.0, 1.0, RET_DIM // 2, dtype=F32))
    rl = lane % RET_DIM
    inv_rl = inv_r[rl % (RET_DIM // 2)][None, :]
    r_sign = jnp.where(rl < RET_DIM // 2, -1.0, 1.0)[None, :]
    hi = (rpos * GRID_W) * inv_rl
    lo = cpos * inv_rl
    outer = lambda a, b: (a[:, None, :] * b[None, :, :]).reshape(seq, LANE)
    r_cos = outer(jnp.cos(hi), jnp.cos(lo)) - outer(jnp.sin(hi), jnp.sin(lo))
    r_sin = outer(jnp.sin(hi), jnp.cos(lo)) + outer(jnp.cos(hi), jnp.sin(lo))
    ones = jnp.ones((TM, LANE), F32)
    zeros = jnp.zeros((TM, LANE), F32)
    cat = lambda head, body: jnp.concatenate([head, body], axis=0)
    return (cat(ones, a_cos), cat(zeros, a_sin * a_sign), cat(ones, r_cos), cat(zeros, r_sin * r_sign))


def _attn_kernel(sink_ref, bias_ref, q_ref, kp_ref, kc_ref, kn_ref, kx_ref, vp_ref, vc_ref, vn_ref, vx_ref, o_ref):
    g_heads = N_Q_HEADS // N_KV_HEADS
    lane = lax.broadcasted_iota(I32, (AB, LANE), 1)
    lo = lane < HEAD_DIM
    rows = lax.broadcasted_iota(I32, (g_heads * AB, 1), 0)
    bias = bias_ref[0]
    outs = []
    for g in range(N_KV_HEADS):
        ks = slice(g * LANE, (g + 1) * LANE)
        kd = jnp.concatenate([kp_ref[0, :, ks], kc_ref[0, :, ks], kn_ref[0, :, ks], kx_ref[0, :, ks]], axis=0)
        vd = jnp.concatenate([vp_ref[0, :, ks], vc_ref[0, :, ks], vn_ref[0, :, ks], vx_ref[0, :, ks]], axis=0)
        qs = []
        for c in range(2 * g, 2 * g + 2):
            qc = q_ref[0, :, c * LANE:(c + 1) * LANE]
            zero = jnp.zeros_like(qc)
            qs += [jnp.where(lo, qc, zero), jnp.where(lo, zero, qc)]
        s = _dot_nt(jnp.concatenate(qs, axis=0), kd) + bias
        sink = jnp.full((g_heads * AB, 1), sink_ref[g_heads * g + g_heads - 1], F32)
        for t in range(g_heads - 2, -1, -1):
            sink = jnp.where(rows < (t + 1) * AB, sink_ref[g_heads * g + t], sink)
        m = jnp.maximum(jnp.max(s, axis=-1, keepdims=True), sink)
        p = jnp.exp(s - m)
        den = jnp.sum(p, axis=-1, keepdims=True) + jnp.exp(sink - m)
        o = _dot(p.astype(BF16), vd) / den
        for t in range(2):
            outs.append(jnp.where(lo, o[2 * t * AB:(2 * t + 1) * AB], o[(2 * t + 1) * AB:(2 * t + 2) * AB]))
    o_ref[0] = jnp.concatenate(outs, axis=-1).astype(BF16)


def _attn_bias():
    g_heads = N_Q_HEADS // N_KV_HEADS
    r = jnp.arange(g_heads * AB)[:, None] % AB
    j = jnp.arange(3 * AB + TM)[None, :]
    band = jnp.abs(j - AB - r) <= WINDOW
    is_ctx = j >= 3 * AB
    variants = (band, band & (j >= AB), band & (j < 2 * AB), jnp.zeros_like(band))
    return jnp.stack([jnp.where(v | is_ctx, 0.0, NEG) for v in variants]).astype(F32)


def _attention(q, k2, v2, sinks):
    b, s, _ = q.shape
    nb = s // AB
    assert nb - CTX_BLOCKS >= 2
    bias = _attn_bias()
    variant = lambda i: jnp.where(i < CTX_BLOCKS, 3, jnp.where(i == CTX_BLOCKS, 1, jnp.where(i == nb - 1, 2, 0)))
    prev = pl.BlockSpec((1, AB, KV2), lambda bi, i: (bi, jnp.maximum(i - 1, 0), 0))
    cur = pl.BlockSpec((1, AB, KV2), lambda bi, i: (bi, i, 0))
    nxt = pl.BlockSpec((1, AB, KV2), lambda bi, i: (bi, jnp.minimum(i + 1, nb - 1), 0))
    cx = pl.BlockSpec((1, TM, KV2), lambda bi, i: (bi, 0, 0))
    return pl.pallas_call(
        _attn_kernel,
        grid=(b, nb),
        in_specs=[pl.BlockSpec(memory_space=pltpu.SMEM),
                  pl.BlockSpec((1,) + bias.shape[1:], lambda bi, i: (variant(i), 0, 0)),
                  pl.BlockSpec((1, AB, ATT_Q), lambda bi, i: (bi, i, 0)),
                  prev, cur, nxt, cx, prev, cur, nxt, cx],
        out_specs=pl.BlockSpec((1, AB, ATT_Q), lambda bi, i: (bi, i, 0)),
        out_shape=jax.ShapeDtypeStruct((b, s, ATT_Q), BF16),
        compiler_params=_params("arbitrary", "arbitrary"),
    )(sinks, bias, q, k2, k2, k2, k2, v2, v2, v2, v2)


def _ret_kernel(lg_ref, qf_ref, kf_ref, vf_ref, qb_ref, kb_ref, vb_ref, yf_ref, yb_ref,
                st_ref, dm_ref, qd_ref, kd_ref, cd_ref, *, batch):
    step = pl.program_id(0)
    lane = lax.broadcasted_iota(I32, (AB, LANE), 1)
    row = lax.broadcasted_iota(I32, (AB, LANE), 0)
    lo = lane < RET_DIM
    tiles = RET_W // LANE

    @pl.when(step == 0)
    def _():
        st_ref[...] = jnp.zeros_like(st_ref)
        ii = row.astype(F32)
        jj = lane.astype(F32)
        for d in range(2):
            for c in range(tiles):
                lg0 = lg_ref[d * RET_HEADS + 2 * c]
                lg1 = lg_ref[d * RET_HEADS + 2 * c + 1]
                lgl = jnp.where(lo, lg0, lg1)
                q_exp = ii + 1.0 if d == 0 else AB - ii
                k_exp = (AB - 1.0) - ii if d == 0 else ii
                qd_ref[d * tiles + c] = jnp.exp(q_exp * lgl)
                kd_ref[d * tiles + c] = jnp.exp(k_exp * lgl)
                cd_ref[d * tiles + c] = jnp.exp(AB * lgl)
                rel = ii - jj if d == 0 else jj - ii
                for hh, lgh in enumerate((lg0, lg1)):
                    dm_ref[d * tiles + c, hh * AB:(hh + 1) * AB, :] = jnp.where(
                        rel >= 0, jnp.exp(jnp.maximum(rel, 0.0) * lgh), 0.0)

    same_head = (row < RET_DIM) == lo
    dirs = ((qf_ref, kf_ref, vf_ref, yf_ref), (qb_ref, kb_ref, vb_ref, yb_ref))
    for d, (q_ref, k_ref, v_ref, y_ref) in enumerate(dirs):
        for b in range(batch):
            for c in range(tiles):
                sl = slice(c * LANE, (c + 1) * LANE)
                t = d * tiles + c
                q = q_ref[b, :, sl]
                k = k_ref[b, :, sl]
                v = v_ref[b, :, sl]
                zero = jnp.zeros_like(q)
                q2 = jnp.concatenate([jnp.where(lo, q, zero), jnp.where(lo, zero, q)], axis=0)
                p = (_dot_nt(q2, k) * dm_ref[t]).astype(BF16)
                y_intra = jnp.where(lo, _dot(p[:AB], v), _dot(p[AB:], v))
                si = (d * batch + b) * tiles + c
                state = st_ref[si]
                q_dec = (q.astype(F32) * qd_ref[t]).astype(BF16)
                y_ref[b, :, sl] = y_intra + _dot(q_dec, state.astype(BF16))
                k_dec = (k.astype(F32) * kd_ref[t]).astype(BF16)
                st_ref[si] = state * cd_ref[t] + jnp.where(same_head, _dot_tn(k_dec, v), 0.0)


def _retention(rq, rk, rv, log_g):
    b, s, w = rq.shape
    nb = s // AB
    tiles = w // LANE

    def back(i):
        return jnp.where(i < CTX_BLOCKS, CTX_BLOCKS - 1 - i, nb - 1 + CTX_BLOCKS - i)

    fwd = pl.BlockSpec((b, AB, w), lambda i: (0, i, 0))
    bwd = pl.BlockSpec((b, AB, w), lambda i: (0, back(i), 0))
    return pl.pallas_call(
        functools.partial(_ret_kernel, batch=b),
        grid=(nb,),
        in_specs=[pl.BlockSpec(memory_space=pltpu.SMEM), fwd, fwd, fwd, bwd, bwd, bwd],
        out_specs=[fwd, bwd],
        out_shape=[jax.ShapeDtypeStruct((b, s, w), F32)] * 2,
        scratch_shapes=[
            pltpu.VMEM((2 * b * tiles, LANE, LANE), F32),
            pltpu.VMEM((2 * tiles, 2 * AB, LANE), F32),
            pltpu.VMEM((2 * tiles, AB, LANE), F32),
            pltpu.VMEM((2 * tiles, AB, LANE), F32),
            pltpu.VMEM((2 * tiles, AB, LANE), F32),
        ],
        compiler_params=_params("arbitrary"),
    )(log_g.reshape(-1), rq, rk, rv, rq, rk, rv)


def _merge_kernel(x_ref, attn_ref, up_ref, uc_ref, un_ref, yf_ref, yb_ref, rg_ref, g_ref, mod_ref,
                  wbd_ref, ps_ref, woa_ref, wop_ref, wor_ref, wout_ref, n2_ref, wr_ref,
                  br_ref, avg_ref, tri_ref,
                  xo_ref, h2_ref, ls_ref, lst_ref, rw_ref, cnt_ref, *, seq, nt):
    i = pl.program_id(0)
    j = i % nt

    ext = TM + 2 * POOL_HALO
    seq_len = jnp.where(j == 0, TM, seq)
    start = jnp.where(j == 0, 0, (j - 1) * TM)
    u = uc_ref[0]
    ue = jnp.concatenate([up_ref[0], u, un_ref[0]], axis=0)
    erow = lax.broadcasted_iota(I32, (ext, 1), 0) + (start - POOL_HALO)
    ue = jnp.where((erow >= 0) & (erow < seq_len), ue, 0.0)
    t_pos = lax.broadcasted_iota(I32, (TM, 1), 0) + start
    glane = lax.broadcasted_iota(I32, (TM, POOL_WIDTH), 1) // (POOL_WIDTH // len(POOL_WINDOWS))
    run = ue
    width = 1
    diff = jnp.zeros((TM, POOL_WIDTH), F32)
    for gi, w in enumerate(POOL_WINDOWS):
        while width < w:
            run = run + pltpu.roll(run, ext - width, 0)
            width *= 2
        win = pltpu.roll(run, w // 2, 0)[POOL_HALO:POOL_HALO + TM]
        cnt = jnp.minimum(t_pos - w // 2 + w, seq_len) - jnp.maximum(t_pos - w // 2, 0)
        diff = jnp.where(glane == gi, win / cnt.astype(F32) - u, diff)
    pool = _dot(diff.astype(BF16), wbd_ref[...]) * ps_ref[...]

    def head_norm(y):
        mu = _dot_hilo(y, avg_ref[...])
        dlt = y - mu
        var = _dot_hilo(dlt * dlt, avg_ref[...])
        return dlt * lax.rsqrt(var + GN_EPS)

    rg = rg_ref[0].astype(F32)
    ret = head_norm(yf_ref[0]) * _silu(rg[:, :RET_W]) + head_norm(yb_ref[0]) * _silu(rg[:, RET_W:])

    d = x_ref.shape[-1]
    gate = lambda t: _sigmoid(g_ref[0, :, t * d:(t + 1) * d])
    m = (gate(0) * _dot(attn_ref[0], woa_ref[...]).astype(BF16)
         + gate(1) * _dot(pool.astype(BF16), wop_ref[...]).astype(BF16)
         + gate(2) * _dot(ret.astype(BF16), wor_ref[...]).astype(BF16))
    xn = x_ref[0] + mod_ref[0, 2:3, :] * _dot(m, wout_ref[...])
    xo_ref[0] = xn

    ms = jnp.mean(xn * xn, axis=-1, keepdims=True)
    h2 = xn * lax.rsqrt(ms + NORM_EPS) * n2_ref[...] * (1.0 + mod_ref[0, 4:5, :]) + mod_ref[0, 3:4, :]
    h2_ref[0] = h2.astype(BF16)
    hh, hl = _split(h2)
    full = _dot(hh, wr_ref[...])
    logits = full[:, :LANE] + full[:, LANE:] + _dot(hl, wr_ref[:, :LANE]) + br_ref[...]
    lanef = lax.broadcasted_iota(I32, (TM, LANE), 1).astype(F32)
    tops, hots = [], []
    for _ in range(TOP_K):
        mx = jnp.max(logits, axis=-1, keepdims=True)
        idx = jnp.min(jnp.where(logits == mx, lanef, float(LANE)), axis=-1, keepdims=True)
        hot = lanef == idx
        logits = jnp.where(hot, NEG * 2.0, logits)
        tops.append((mx, idx))
        hots.append(hot)
    ex = [jnp.exp(mx - tops[0][0]) for mx, _ in tops]
    tot = ex[0]
    for e in ex[1:]:
        tot = tot + e
    sel = jnp.zeros((TM, LANE), F32)
    for hot in hots:
        sel = jnp.where(hot, 1.0, sel)
    incl = _dot(tri_ref[...], sel.astype(BF16))
    cnt = incl[TM - 1:TM, :]
    run = jnp.floor((cnt + (SUBLANE - 1.0)) * (1.0 / SUBLANE)) * SUBLANE
    er = lax.broadcasted_iota(I32, (LANE, LANE), 0)
    ec = lax.broadcasted_iota(I32, (LANE, LANE), 1)
    earlier = jnp.where(er < ec, 1.0, 0.0).astype(BF16)
    run_start = _dot(jnp.broadcast_to(run, (SUBLANE, LANE)).astype(BF16), earlier)[0:1]
    slot = run_start + incl - 1.0
    ls = jnp.zeros((TM, LANE), F32)
    rw = jnp.zeros((TM, LANE), F32)
    for k in range(TOP_K):
        ls = jnp.where(lanef == k, jnp.sum(jnp.where(hots[k], slot, 0.0), axis=-1, keepdims=True), ls)
        rw = jnp.where(lanef == k, ex[k] / tot, rw)
    ls = jnp.where(lanef < TOP_K, ls, -1.0)
    ls_ref[...] = ls.astype(I32)
    lst_ref[...] = jnp.transpose(ls)[:SUBLANE, :].astype(I32)
    rw_ref[...] = rw
    cnt_ref[...] = jnp.broadcast_to(cnt, cnt_ref.shape).astype(I32)


def _merge(xm, attn, u, yf, yb, rg, gates, mod_l, seq, wts):
    b, s, d = xm.shape
    nt = s // TM
    ntt = b * nt
    hb = TM // POOL_HALO
    nh = s // POOL_HALO
    tok = lambda width: pl.BlockSpec((1, TM, width), lambda i: (i // nt, i % nt, 0))
    full = lambda a: pl.BlockSpec(a.shape, lambda i: (0,) * a.ndim)
    lanes = pl.BlockSpec((TM, LANE), lambda i: (i, 0))
    in_specs = [
        tok(d), tok(ATT_Q),
        pl.BlockSpec((1, POOL_HALO, POOL_WIDTH), lambda i: (i // nt, jnp.maximum((i % nt) * hb - 1, 0), 0)),
        tok(POOL_WIDTH),
        pl.BlockSpec((1, POOL_HALO, POOL_WIDTH), lambda i: (i // nt, jnp.minimum((i % nt + 1) * hb, nh - 1), 0)),
        tok(RET_W), tok(RET_W), tok(2 * RET_W), tok(3 * d),
        pl.BlockSpec((1, 6, d), lambda i: (jnp.where(i % nt == 0, b, i // nt), 0, 0)),
    ] + [full(a) for a in wts]
    return pl.pallas_call(
        functools.partial(_merge_kernel, seq=seq, nt=nt),
        grid=(ntt,),
        in_specs=in_specs,
        out_specs=[tok(d), tok(d), lanes, pl.BlockSpec((SUBLANE, TM), lambda i: (i, 0)), lanes,
                   pl.BlockSpec((SUBLANE, LANE), lambda i: (i, 0))],
        out_shape=[jax.ShapeDtypeStruct((b, s, d), F32), jax.ShapeDtypeStruct((b, s, d), BF16),
                   jax.ShapeDtypeStruct((b * s, LANE), I32), jax.ShapeDtypeStruct((ntt * SUBLANE, TM), I32),
                   jax.ShapeDtypeStruct((b * s, LANE), F32), jax.ShapeDtypeStruct((ntt * SUBLANE, LANE), I32)],
        compiler_params=_params("arbitrary"),
    )(xm, attn, u, u, u, yf, yb, rg, gates, mod_l, *wts)


def _rows(tile_index):
    return pl.ds(pl.multiple_of(tile_index * SUBLANE, SUBLANE), SUBLANE)


def _tile_copies(table_ref, tile, copy):
    def body(j, carry):
        copy(j, table_ref[tile * SORT_TILES + j])
        return carry

    lax.fori_loop(0, SORT_TILES, body, 0, unroll=8)


def _tile_waits(wait):
    def body(j, carry):
        wait(j)
        return carry

    lax.fori_loop(0, SORT_TILES, body, 0, unroll=8)


def _dispatch_kernel(dst_ref, fs_ref, fn_ref, nu_ref, h_ref, lst_ref, xs_hbm, buf, zbuf, sem, fill_sem,
                     *, n_blocks):
    i = pl.program_id(0)
    last = pl.num_programs(0) - 1
    cur = i % 2

    def send(tile, half):
        _tile_copies(dst_ref, tile, lambda j, t: pltpu.make_async_copy(
            buf.at[half, _rows(j)], xs_hbm.at[_rows(t)], sem.at[half]).start())

    def drain(half):
        _tile_waits(lambda j: pltpu.make_async_copy(
            buf.at[half, _rows(j)], xs_hbm.at[_rows(j)], sem.at[half]).wait())

    @pl.when(i == 0)
    def _():
        zbuf[...] = jnp.zeros_like(zbuf)

        def pad_copy(e, c):
            return pltpu.make_async_copy(zbuf.at[pl.ds(0, SUBLANE)], xs_hbm.at[_rows(fs_ref[e] + c)], fill_sem)

        def blk_copy(blk):
            return pltpu.make_async_copy(zbuf, xs_hbm.at[pl.ds(pl.multiple_of(blk * EB, EB), EB)], fill_sem)

        def per_expert(fn):
            def outer(e, carry):
                def inner(c, cc):
                    fn(e, c)
                    return cc
                return lax.fori_loop(0, fn_ref[e], inner, carry)
            lax.fori_loop(0, N_EXPERTS, outer, 0)

        def per_block(fn):
            def body(blk, carry):
                fn(blk)
                return carry
            lax.fori_loop(nu_ref[0], n_blocks, body, 0)

        per_expert(lambda e, c: pad_copy(e, c).start())
        per_block(lambda blk: blk_copy(blk).start())
        per_expert(lambda e, c: pad_copy(e, c).wait())
        per_block(lambda blk: blk_copy(blk).wait())

    slot = lax.broadcasted_iota(I32, (SORT_ROWS, TM), 0)
    lst = lst_ref[...]
    p = jnp.zeros((SORT_ROWS, TM), F32)
    for k in range(TOP_K):
        p = jnp.where(slot == lst[k:k + 1, :], 1.0, p)
    srt = _dot(p.astype(BF16), h_ref[...])

    @pl.when(i >= 2)
    def _():
        drain(cur)

    buf[cur] = srt
    send(i, cur)

    @pl.when(i == last)
    def _():
        @pl.when(i >= 1)
        def _():
            drain(1 - cur)
        drain(cur)


def _dispatch(h2, lst, tabs, n_used, n_blocks):
    t, d = h2.shape
    ntt = t // TM
    grid_spec = pltpu.PrefetchScalarGridSpec(
        num_scalar_prefetch=4,
        grid=(ntt,),
        in_specs=[pl.BlockSpec((TM, d), lambda i, *_: (i, 0)),
                  pl.BlockSpec((SUBLANE, TM), lambda i, *_: (i, 0))],
        out_specs=pl.BlockSpec(memory_space=pl.ANY),
        scratch_shapes=[pltpu.VMEM((2, SORT_ROWS, d), F32), pltpu.VMEM((EB, d), F32),
                        pltpu.SemaphoreType.DMA((2,)), pltpu.SemaphoreType.DMA],
    )
    return pl.pallas_call(
        functools.partial(_dispatch_kernel, n_blocks=n_blocks),
        grid_spec=grid_spec,
        out_shape=jax.ShapeDtypeStruct((n_blocks * EB, d), F32),
        compiler_params=_params("arbitrary"),
    )(tabs["dst"], tabs["fill_start"], tabs["fill_n"], n_used, h2, lst)


def _expert_kernel(be_ref, nu_ref, x_ref, w1_ref, b1_ref, w2_ref, b2_ref, y_ref, w1b, w2b):
    bi = pl.program_id(0)
    used = bi < nu_ref[0]
    fresh = (bi == 0) | (be_ref[bi] != be_ref[jnp.maximum(bi - 1, 0)])

    @pl.when(used & fresh)
    def _():
        w1b[...] = w1_ref[0, 0].astype(BF16)
        w2b[...] = w2_ref[0, 0].astype(BF16)

    @pl.when(jnp.logical_not(used))
    def _():
        y_ref[...] = jnp.zeros_like(y_ref)

    @pl.when(used)
    def _():
        xb = x_ref[...].astype(BF16)
        y = None
        for c in range(0, D_FF, FF_CHUNK):
            glu = _dot(xb, w1b[:, c:c + FF_CHUNK]) + b1_ref[0, 0, :, c:c + FF_CHUNK]
            lin = _dot(xb, w1b[:, D_FF + c:D_FF + c + FF_CHUNK]) + b1_ref[0, 0, :, D_FF + c:D_FF + c + FF_CHUNK]
            glu = jnp.minimum(glu, SWIGLU_LIMIT)
            lin = jnp.clip(lin, -SWIGLU_LIMIT, SWIGLU_LIMIT)
            act = glu * _sigmoid(SWIGLU_ALPHA * glu) * (lin + 1.0)
            part = _dot(act.astype(BF16), w2b[c:c + FF_CHUNK, :])
            y = part if y is None else y + part
        y_ref[...] = y + b2_ref[0, 0]


def _experts(xs, block_e, n_used, w1, b1, w2, b2, layer):
    rows, d = xs.shape
    depth, ne, _, f2 = w1.shape
    row_blk = lambda bi, be, nu: (bi, 0)
    grid_spec = pltpu.PrefetchScalarGridSpec(
        num_scalar_prefetch=2,
        grid=(rows // EB,),
        in_specs=[
            pl.BlockSpec((EB, d), row_blk),
            pl.BlockSpec((1, 1, d, f2), lambda bi, be, nu: (layer, be[bi], 0, 0)),
            pl.BlockSpec((1, 1, 1, f2), lambda bi, be, nu: (layer, be[bi], 0, 0)),
            pl.BlockSpec((1, 1, f2 // 2, d), lambda bi, be, nu: (layer, be[bi], 0, 0)),
            pl.BlockSpec((1, 1, 1, d), lambda bi, be, nu: (layer, be[bi], 0, 0)),
        ],
        out_specs=pl.BlockSpec((EB, d), row_blk),
        scratch_shapes=[pltpu.VMEM((d, f2), BF16), pltpu.VMEM((f2 // 2, d), BF16)],
    )
    return pl.pallas_call(
        _expert_kernel,
        grid_spec=grid_spec,
        out_shape=jax.ShapeDtypeStruct((rows, d), F32),
        compiler_params=_params("arbitrary"),
    )(block_e, n_used, xs, w1, b1.reshape(depth, ne, 1, f2), w2, b2.reshape(depth, ne, 1, d))


def _combine_kernel(dst_ref, x_ref, ls_ref, rw_ref, mod_ref, fg_ref, ys_hbm, xo_ref, buf, sem, *, final):
    i = pl.program_id(0)
    last = pl.num_programs(0) - 1
    cur = i % 2

    def fetch(tile, half):
        _tile_copies(dst_ref, tile, lambda j, t: pltpu.make_async_copy(
            ys_hbm.at[_rows(t)], buf.at[half, _rows(j)], sem.at[half]).start())

    @pl.when(i == 0)
    def _():
        fetch(0, 0)

    @pl.when(i < last)
    def _():
        fetch(i + 1, 1 - cur)

    _tile_waits(lambda j: pltpu.make_async_copy(
        ys_hbm.at[_rows(j)], buf.at[cur, _rows(j)], sem.at[cur]).wait())
    yb = buf[cur].astype(BF16)
    slot = lax.broadcasted_iota(I32, (TM, SORT_ROWS), 1)
    ls = ls_ref[...]
    rw = rw_ref[...]
    g = jnp.zeros((TM, SORT_ROWS), F32)
    for k in range(TOP_K):
        g = jnp.where(slot == ls[:, k:k + 1], rw[:, k:k + 1], g)
    gh, gl = _split(g)
    xn = x_ref[0] + mod_ref[0, 5:6, :] * (_dot(gh, yb) + _dot(gl, yb))
    if final:
        ms = jnp.mean(xn * xn, axis=-1, keepdims=True)
        xn = xn * lax.rsqrt(ms + NORM_EPS) * fg_ref[...]
    xo_ref[0] = xn


def _combine(xn, ls, rw, mod_l, final_gain, ys, tabs, final):
    b, s, d = xn.shape
    nt = s // TM
    tok = pl.BlockSpec((1, TM, d), lambda i, *_: (i // nt, i % nt, 0))
    lanes = pl.BlockSpec((TM, LANE), lambda i, *_: (i, 0))
    if final:
        out_spec = pl.BlockSpec((1, TM, d), lambda i, *_: (i // nt, jnp.maximum(i % nt - 1, 0), 0))
        out_shape = jax.ShapeDtypeStruct((b, s - TM, d), F32)
    else:
        out_spec, out_shape = tok, jax.ShapeDtypeStruct((b, s, d), F32)
    grid_spec = pltpu.PrefetchScalarGridSpec(
        num_scalar_prefetch=1,
        grid=(b * nt,),
        in_specs=[tok, lanes, lanes,
                  pl.BlockSpec((1, 6, d), lambda i, *_: (jnp.where(i % nt == 0, b, i // nt), 0, 0)),
                  pl.BlockSpec((1, d), lambda i, *_: (0, 0)),
                  pl.BlockSpec(memory_space=pl.ANY)],
        out_specs=out_spec,
        scratch_shapes=[pltpu.VMEM((2, SORT_ROWS, d), F32), pltpu.SemaphoreType.DMA((2,))],
    )
    return pl.pallas_call(
        functools.partial(_combine_kernel, final=final),
        grid_spec=grid_spec,
        out_shape=out_shape,
        compiler_params=_params("arbitrary"),
    )(tabs["dst"], xn, ls, rw, mod_l, final_gain.reshape(1, d), ys)


def _inproj_weight(w_in_l):
    d = w_in_l.shape[0]
    sizes = (ATT_Q, N_KV_HEADS * HEAD_DIM, N_KV_HEADS * HEAD_DIM, POOL_WIDTH,
             RET_W, RET_W, RET_W, RET_W, RET_W, d, d, d)
    parts, off = [], 0
    for sz in sizes:
        parts.append(w_in_l[:, off:off + sz])
        off += sz
    q, k, v, u, rq, rk, rv, rgf, rgb, ga, gp, gr = parts
    twice = lambda w: jnp.concatenate(
        [w[:, h * HEAD_DIM:(h + 1) * HEAD_DIM] for h in range(N_KV_HEADS) for _ in range(2)], axis=1)
    cols = [q * HEAD_DIM ** -0.5, twice(k), twice(v), rq, rk * RET_DIM ** -0.5, rv, rgf, rgb, u, ga, gp, gr]
    return jnp.concatenate(cols, axis=1).astype(BF16)


def _block_diag(blocks):
    n, r, c = blocks.shape
    out = jnp.zeros((n * r, n * c), blocks.dtype)
    for g in range(n):
        out = out.at[g * r:(g + 1) * r, g * c:(g + 1) * c].set(blocks[g])
    return out


def _routing_tables(cnt, n_blocks):
    ntt = cnt.shape[0] // SUBLANE
    counts = cnt.reshape(ntt, SUBLANE, LANE)[:, 0, :N_EXPERTS]
    run = (counts + SUBLANE - 1) // SUBLANE
    total = jnp.sum(run, axis=0)
    eb = EB // SUBLANE
    padded = (total + eb - 1) // eb * eb
    pad_end = jnp.cumsum(padded)
    pad_start = pad_end - padded
    off = pad_start[None, :] + jnp.cumsum(run, axis=0) - run
    run_end = jnp.cumsum(run, axis=1)
    j = jnp.arange(SORT_TILES)
    owner = jnp.sum(run_end[:, None, :] <= j[None, :, None], axis=2)
    mine = owner[:, :, None] == jnp.arange(N_EXPERTS)[None, None, :]
    in_region = j[None, :] + jnp.sum(jnp.where(mine, (off - (run_end - run))[:, None, :], 0), axis=2)
    spare = (n_blocks - SPARE_BLOCKS) * eb + (jnp.arange(ntt) % 2)[:, None] * SORT_TILES + j[None, :]
    dst = jnp.where(owner < N_EXPERTS, in_region, spare)
    n_used = pad_end[-1] // eb
    blk = jnp.minimum(jnp.arange(n_blocks), n_used - 1) * eb
    block_e = jnp.minimum(jnp.sum(pad_end[None, :] <= blk[:, None], axis=1), N_EXPERTS - 1)
    tabs = dict(dst=dst.reshape(-1).astype(I32), fill_start=(pad_start + total).astype(I32),
                fill_n=(padded - total).astype(I32))
    return tabs, block_e.astype(I32), n_used.reshape(1).astype(I32)


def kernel(x, c, ctx, c_ctx, w_mod, b_mod, norm1, norm2, w_in, attn_sinks, pool_w, pool_scale, ret_decay,
           w_o_attn, w_o_pool, w_o_ret, w_out, w_router, b_router, w_expert_in, b_expert_in, w_expert_out,
           b_expert_out, final_norm):
    b, seq, d = x.shape
    depth = w_mod.shape[0]
    assert ctx.shape[1] == TM and seq % TM == 0 and seq % GRID_W == 0
    s = seq + TM
    xm = jnp.concatenate([ctx, x], axis=1)
    mod = _modulation(c, c_ctx, w_mod, b_mod)
    tables = _rope_tables(seq)
    ntt = b * s // TM
    n_blocks = -(-(b * s * TOP_K + ntt * N_EXPERTS * (SUBLANE - 1)) // EB) + N_EXPERTS + SPARE_BLOCKS
    avg = _block_diag(jnp.full((RET_HEADS, RET_DIM, RET_DIM), 1.0 / RET_DIM, F32)).astype(BF16)
    tri = (jnp.arange(TM)[:, None] >= jnp.arange(TM)[None, :]).astype(BF16)
    for l in range(depth):
        q, k2, v2, rq, rk, rv, rg, u, gates = _inproj(xm, mod[l], norm1[l], _inproj_weight(w_in[l]), tables)
        attn = _attention(q, k2, v2, attn_sinks[l])
        yf, yb = _retention(rq, rk, rv, jax.nn.log_sigmoid(ret_decay[l].astype(F32)))
        wr = jnp.zeros((d, LANE), F32).at[:, :N_EXPERTS].set(w_router[l])
        wr_hilo = jnp.concatenate(_split(wr), axis=1)
        br = jnp.full((1, LANE), NEG, F32).at[0, :N_EXPERTS].set(b_router[l])
        wts = (_block_diag(pool_w[l]).astype(BF16), pool_scale[l].reshape(1, -1),
               w_o_attn[l].astype(BF16), w_o_pool[l].astype(BF16), w_o_ret[l].astype(BF16),
               w_out[l].astype(BF16), norm2[l].reshape(1, d), wr_hilo, br, avg, tri)
        xn, h2, ls, lst, rw, cnt = _merge(xm, attn, u, yf, yb, rg, gates, mod[l], seq, wts)
        tabs, block_e, n_used = _routing_tables(cnt, n_blocks)
        xs = _dispatch(h2.reshape(b * s, d), lst, tabs, n_used, n_blocks)
        ys = _experts(xs, block_e, n_used, w_expert_in, b_expert_in, w_expert_out, b_expert_out, l)
        xm = _combine(xn, ls, rw, mod[l], final_norm, ys, tabs, final=l == depth - 1)
    return xm
```

```python
import functools

import jax
import jax.numpy as jnp
from jax import lax
from jax.experimental import pallas as pl
from jax.experimental.pallas import tpu as pltpu

F32 = jnp.float32
BF16 = jnp.bfloat16
I32 = jnp.int32

GRID_W = 64
HEAD_DIM = 64
N_Q_HEADS = 8
N_KV_HEADS = 2
WINDOW = 128
ROPE_THETA = 10000.0
POOL_WIDTH = 256
POOL_WINDOWS = (2, 4, 8, 16)
RET_HEADS = 4
RET_DIM = 64
N_EXPERTS = 32
TOP_K = 4
D_FF = 1024
SWIGLU_LIMIT = 7.0
SWIGLU_ALPHA = 1.702
NORM_EPS = 1e-6
GN_EPS = 1e-5

LANE = 128
SUBLANE = 8
MXU_N = 256
TM = 256
AB = 128
IN_TILE = 640
CTX_BLOCKS = TM // AB
EB = 512
FF_CHUNK = 256
POOL_HALO = SUBLANE
SORT_ROWS = TM * TOP_K + N_EXPERTS * SUBLANE
SORT_TILES = SORT_ROWS // SUBLANE
SPARE_BLOCKS = 2 * SORT_ROWS // EB
NEG = -1e30
VMEM_LIMIT = 56 * 1024 * 1024

ATT_Q = N_Q_HEADS * HEAD_DIM
KV2 = 2 * N_KV_HEADS * HEAD_DIM
RET_W = RET_HEADS * RET_DIM


def _dot(a, b):
    return jnp.dot(a, b, preferred_element_type=F32)


def _dot_nt(a, b):
    return lax.dot_general(a, b, (((1,), (1,)), ((), ())), preferred_element_type=F32)


def _dot_tn(a, b):
    return lax.dot_general(a, b, (((0,), (0,)), ((), ())), preferred_element_type=F32)


def _split(a):
    hi = a.astype(BF16)
    lo = (a - hi.astype(F32)).astype(BF16)
    return hi, lo


def _dot_hilo(a, m):
    hi, lo = _split(a)
    return _dot(hi, m) + _dot(lo, m)


def _sigmoid(x):
    return 0.5 * jnp.tanh(0.5 * x) + 0.5


def _silu(x):
    return x * _sigmoid(x)


def _params(*sem):
    return pltpu.CompilerParams(dimension_semantics=sem, vmem_limit_bytes=VMEM_LIMIT)


def _mod_kernel(c_ref, w_ref, b_ref, o_ref):
    s = _silu(c_ref[...])
    sh, sl = _split(s)
    wh, wl = _split(w_ref[0])
    o_ref[0] = _dot(sh, wh) + _dot(sh, wl) + _dot(sl, wh) + b_ref[0]


def _modulation(c, c_ctx, w_mod, b_mod):
    depth, d, six_d = w_mod.shape
    b = c.shape[0]
    cc = jnp.zeros((SUBLANE, d), F32).at[:b].set(c).at[b].set(c_ctx)
    out = pl.pallas_call(
        _mod_kernel,
        grid=(depth, six_d // d),
        in_specs=[
            pl.BlockSpec((SUBLANE, d), lambda l, n: (0, 0)),
            pl.BlockSpec((1, d, d), lambda l, n: (l, 0, n)),
            pl.BlockSpec((1, 1, d), lambda l, n: (l, 0, n)),
        ],
        out_specs=pl.BlockSpec((1, SUBLANE, d), lambda l, n: (l, 0, n)),
        out_shape=jax.ShapeDtypeStruct((depth, SUBLANE, six_d), F32),
        compiler_params=_params("arbitrary", "arbitrary"),
    )(cc, w_mod, b_mod.reshape(depth, 1, six_d))
    return out[:, : b + 1].reshape(depth, b + 1, 6, d)


def _rope(x, cos, sin, half, first):
    partner = jnp.where(first, pltpu.roll(x, LANE - half, 1), pltpu.roll(x, half, 1))
    return x * cos + partner * sin


def _inproj_kernel(x_ref, mod_ref, modc_ref, n1_ref, w_ref, ac_ref, as_ref, rc_ref, rs_ref,
                   q_ref, k_ref, v_ref, rq_ref, rk_ref, rv_ref, rg_ref, u_ref, g_ref):
    x = x_ref[0]
    rows = x.shape[0]
    ms = jnp.mean(x * x, axis=-1, keepdims=True)
    y = x * lax.rsqrt(ms + NORM_EPS) * n1_ref[...]
    is_ctx = (lax.broadcasted_iota(I32, (rows, 1), 0) < TM) & (pl.program_id(1) == 0)
    scale = jnp.where(is_ctx, modc_ref[0, 1:2, :], mod_ref[0, 1:2, :])
    shift = jnp.where(is_ctx, modc_ref[0, 0:1, :], mod_ref[0, 0:1, :])
    hb = (y * (1.0 + scale) + shift).astype(BF16)
    lane = lax.broadcasted_iota(I32, (rows, LANE), 1)
    a_first = (lane % (HEAD_DIM // 2)) < (HEAD_DIM // 4)
    r_first = (lane % RET_DIM) < (RET_DIM // 2)
    ac, asn, rc, rsn = ac_ref[...], as_ref[...], rc_ref[...], rs_ref[...]

    def proj(off, width):
        return _dot(hb, w_ref[:, off:off + width])

    def rotated(ref, width, cos, sin, half, first, off):
        for t in range(0, width, MXU_N):
            pr = proj(off + t, MXU_N)
            for g in range(0, MXU_N, LANE):
                ref[0, :, t + g:t + g + LANE] = _rope(pr[:, g:g + LANE], cos, sin, half, first).astype(BF16)

    off = 0
    rotated(q_ref, ATT_Q, ac, asn, HEAD_DIM // 4, a_first, off)
    off += ATT_Q
    rotated(k_ref, KV2, ac, asn, HEAD_DIM // 4, a_first, off)
    off += KV2
    v_ref[0] = proj(off, KV2).astype(BF16)
    off += KV2
    for ref in (rq_ref, rk_ref):
        rotated(ref, RET_W, rc, rsn, RET_DIM // 2, r_first, off)
        off += RET_W
    rv_ref[0] = proj(off, RET_W).astype(BF16)
    off += RET_W
    rg_ref[0] = proj(off, 2 * RET_W).astype(BF16)
    off += 2 * RET_W
    u_ref[0] = proj(off, POOL_WIDTH)
    off += POOL_WIDTH
    d = x.shape[-1]
    for t in range(3):
        g_ref[0, :, t * d:(t + 1) * d] = proj(off, d).astype(BF16)
        off += d


def _inproj(xm, mod_l, norm1_l, w1, tables):
    b, s, d = xm.shape
    ti = IN_TILE if s % IN_TILE == 0 else TM
    wcols = w1.shape[1]
    tok = lambda width: pl.BlockSpec((1, ti, width), lambda bi, j: (bi, j, 0))
    tab = pl.BlockSpec((ti, LANE), lambda bi, j: (j, 0))
    widths = (ATT_Q, KV2, KV2, RET_W, RET_W, RET_W, 2 * RET_W, POOL_WIDTH, 3 * d)
    dtypes = (BF16,) * 7 + (F32, BF16)
    return pl.pallas_call(
        _inproj_kernel,
        grid=(b, s // ti),
        in_specs=[
            tok(d),
            pl.BlockSpec((1, 6, d), lambda bi, j: (bi, 0, 0)),
            pl.BlockSpec((1, 6, d), lambda bi, j: (b, 0, 0)),
            pl.BlockSpec((1, d), lambda bi, j: (0, 0)),
            pl.BlockSpec((d, wcols), lambda bi, j: (0, 0), pipeline_mode=pl.Buffered(1)),
            tab, tab, tab, tab,
        ],
        out_specs=[tok(w) for w in widths],
        out_shape=[jax.ShapeDtypeStruct((b, s, w), dt) for w, dt in zip(widths, dtypes)],
        compiler_params=_params("arbitrary", "arbitrary"),
    )(xm, mod_l, mod_l, norm1_l.reshape(1, d), w1, *tables)


def _rope_tables(seq):
    rows = seq // GRID_W
    rpos = jnp.arange(rows, dtype=F32)[:, None]
    cpos = jnp.arange(GRID_W, dtype=F32)[:, None]
    lane = jnp.arange(LANE)
    grid = lambda per_row, per_col: (per_row[:, None, :] + per_col[None, :, :]).reshape(seq, LANE)
    axis_dim = HEAD_DIM // 2
    inv_a = ROPE_THETA ** (-jnp.arange(0, axis_dim, 2, dtype=F32) / axis_dim)
    hl = lane % HEAD_DIM
    inv_al = inv_a[(hl % axis_dim) % (axis_dim // 2)][None, :]
    by_row = (hl < axis_dim)[None, :]
    a_sign = jnp.where((hl % axis_dim) < axis_dim // 2, -1.0, 1.0)[None, :]
    a_cos = grid(jnp.where(by_row, jnp.cos(rpos * inv_al), 0.0), jnp.where(by_row, 0.0, jnp.cos(cpos * inv_al)))
    a_sin = grid(jnp.where(by_row, jnp.sin(rpos * inv_al), 0.0), jnp.where(by_row, 0.0, jnp.sin(cpos * inv_al)))
    inv_r = 1.0 / (ROPE_THETA ** jnp.linspace(0.0, 1.0, RET_DIM // 2, dtype=F32))
    rl = lane % RET_DIM
    inv_rl = inv_r[rl % (RET_DIM // 2)][None, :]
    r_sign = jnp.where(rl < RET_DIM // 2, -1.0, 1.0)[None, :]
    hi = (rpos * GRID_W) * inv_rl
    lo = cpos * inv_rl
    outer = lambda a, b: (a[:, None, :] * b[None, :, :]).reshape(seq, LANE)
    r_cos = outer(jnp.cos(hi), jnp.cos(lo)) - outer(jnp.sin(hi), jnp.sin(lo))
    r_sin = outer(jnp.sin(hi), jnp.cos(lo)) + outer(jnp.cos(hi), jnp.sin(lo))
    ones = jnp.ones((TM, LANE), F32)
    zeros = jnp.zeros((TM, LANE), F32)
    cat = lambda head, body: jnp.concatenate([head, body], axis=0)
    return (cat(ones, a_cos), cat(zeros, a_sin * a_sign), cat(ones, r_cos), cat(zeros, r_sin * r_sign))


def _attn_kernel(sink_ref, bias_ref, q_ref, kp_ref, kc_ref, kn_ref, kx_ref, vp_ref, vc_ref, vn_ref, vx_ref, o_ref):
    g_heads = N_Q_HEADS // N_KV_HEADS
    lane = lax.broadcasted_iota(I32, (AB, LANE), 1)
    lo = lane < HEAD_DIM
    rows = lax.broadcasted_iota(I32, (g_heads * AB, 1), 0)
    bias = bias_ref[0]
    scores, sinks, values = [], [], []
    for g in range(N_KV_HEADS):
        ks = slice(g * LANE, (g + 1) * LANE)
        kd = jnp.concatenate([kp_ref[0, :, ks], kc_ref[0, :, ks], kn_ref[0, :, ks], kx_ref[0, :, ks]], axis=0)
        values.append(jnp.concatenate([vp_ref[0, :, ks], vc_ref[0, :, ks], vn_ref[0, :, ks], vx_ref[0, :, ks]], axis=0))
        qs = []
        for c in range(2 * g, 2 * g + 2):
            qc = q_ref[0, :, c * LANE:(c + 1) * LANE]
            zero = jnp.zeros_like(qc)
            qs += [jnp.where(lo, qc, zero), jnp.where(lo, zero, qc)]
        scores.append(_dot_nt(jnp.concatenate(qs, axis=0), kd) + bias)
        sink = jnp.full((g_heads * AB, 1), sink_ref[g_heads * g + g_heads - 1], F32)
        for t in range(g_heads - 2, -1, -1):
            sink = jnp.where(rows < (t + 1) * AB, sink_ref[g_heads * g + t], sink)
        sinks.append(sink)
    outs = []
    for s, sink, vd in zip(scores, sinks, values):
        m = jnp.maximum(jnp.max(s, axis=-1, keepdims=True), sink)
        p = jnp.exp(s - m)
        den = jnp.sum(p, axis=-1, keepdims=True) + jnp.exp(sink - m)
        o = _dot(p.astype(BF16), vd) / den
        for t in range(2):
            outs.append(jnp.where(lo, o[2 * t * AB:(2 * t + 1) * AB], o[(2 * t + 1) * AB:(2 * t + 2) * AB]))
    o_ref[0] = jnp.concatenate(outs, axis=-1).astype(BF16)


def _attn_bias():
    g_heads = N_Q_HEADS // N_KV_HEADS
    r = jnp.arange(g_heads * AB)[:, None] % AB
    j = jnp.arange(3 * AB + TM)[None, :]
    band = jnp.abs(j - AB - r) <= WINDOW
    is_ctx = j >= 3 * AB
    variants = (band, band & (j >= AB), band & (j < 2 * AB), jnp.zeros_like(band))
    return jnp.stack([jnp.where(v | is_ctx, 0.0, NEG) for v in variants]).astype(F32)


def _attention(q, k2, v2, sinks):
    b, s, _ = q.shape
    nb = s // AB
    assert nb - CTX_BLOCKS >= 2
    bias = _attn_bias()
    variant = lambda i: jnp.where(i < CTX_BLOCKS, 3, jnp.where(i == CTX_BLOCKS, 1, jnp.where(i == nb - 1, 2, 0)))
    prev = pl.BlockSpec((1, AB, KV2), lambda bi, i: (bi, jnp.maximum(i - 1, 0), 0))
    cur = pl.BlockSpec((1, AB, KV2), lambda bi, i: (bi, i, 0))
    nxt = pl.BlockSpec((1, AB, KV2), lambda bi, i: (bi, jnp.minimum(i + 1, nb - 1), 0))
    cx = pl.BlockSpec((1, TM, KV2), lambda bi, i: (bi, 0, 0))
    return pl.pallas_call(
        _attn_kernel,
        grid=(b, nb),
        in_specs=[pl.BlockSpec(memory_space=pltpu.SMEM),
                  pl.BlockSpec((1,) + bias.shape[1:], lambda bi, i: (variant(i), 0, 0)),
                  pl.BlockSpec((1, AB, ATT_Q), lambda bi, i: (bi, i, 0)),
                  prev, cur, nxt, cx, prev, cur, nxt, cx],
        out_specs=pl.BlockSpec((1, AB, ATT_Q), lambda bi, i: (bi, i, 0)),
        out_shape=jax.ShapeDtypeStruct((b, s, ATT_Q), BF16),
        compiler_params=_params("arbitrary", "arbitrary"),
    )(sinks, bias, q, k2, k2, k2, k2, v2, v2, v2, v2)


def _ret_kernel(lg_ref, qf_ref, kf_ref, vf_ref, qb_ref, kb_ref, vb_ref, yf_ref, yb_ref,
                st_ref, dm_ref, qd_ref, kd_ref, cd_ref, *, batch):
    step = pl.program_id(0)
    lane = lax.broadcasted_iota(I32, (AB, LANE), 1)
    row = lax.broadcasted_iota(I32, (AB, LANE), 0)
    lo = lane < RET_DIM
    tiles = RET_W // LANE

    @pl.when(step == 0)
    def _():
        st_ref[...] = jnp.zeros_like(st_ref)
        ii = row.astype(F32)
        jj = lane.astype(F32)
        for d in range(2):
            for c in range(tiles):
                lg0 = lg_ref[d * RET_HEADS + 2 * c]
                lg1 = lg_ref[d * RET_HEADS + 2 * c + 1]
                lgl = jnp.where(lo, lg0, lg1)
                q_exp = ii + 1.0 if d == 0 else AB - ii
                k_exp = (AB - 1.0) - ii if d == 0 else ii
                qd_ref[d * tiles + c] = jnp.exp(q_exp * lgl)
                kd_ref[d * tiles + c] = jnp.exp(k_exp * lgl)
                cd_ref[d * tiles + c] = jnp.exp(AB * lgl)
                rel = ii - jj if d == 0 else jj - ii
                for hh, lgh in enumerate((lg0, lg1)):
                    dm_ref[d * tiles + c, hh * AB:(hh + 1) * AB, :] = jnp.where(
                        rel >= 0, jnp.exp(jnp.maximum(rel, 0.0) * lgh), 0.0)

    same_head = (row < RET_DIM) == lo
    dirs = ((qf_ref, kf_ref, vf_ref, yf_ref), (qb_ref, kb_ref, vb_ref, yb_ref))
    for d, (q_ref, k_ref, v_ref, y_ref) in enumerate(dirs):
        for b in range(batch):
            for c in range(tiles):
                sl = slice(c * LANE, (c + 1) * LANE)
                t = d * tiles + c
                q = q_ref[b, :, sl]
                k = k_ref[b, :, sl]
                v = v_ref[b, :, sl]
                zero = jnp.zeros_like(q)
                q2 = jnp.concatenate([jnp.where(lo, q, zero), jnp.where(lo, zero, q)], axis=0)
                p = (_dot_nt(q2, k) * dm_ref[t]).astype(BF16)
                y_intra = jnp.where(lo, _dot(p[:AB], v), _dot(p[AB:], v))
                si = (d * batch + b) * tiles + c
                state = st_ref[si]
                q_dec = (q.astype(F32) * qd_ref[t]).astype(BF16)
                y_ref[b, :, sl] = y_intra + _dot(q_dec, state.astype(BF16))
                k_dec = (k.astype(F32) * kd_ref[t]).astype(BF16)
                st_ref[si] = state * cd_ref[t] + jnp.where(same_head, _dot_tn(k_dec, v), 0.0)


def _retention(rq, rk, rv, log_g):
    b, s, w = rq.shape
    nb = s // AB
    tiles = w // LANE

    def back(i):
        return jnp.where(i < CTX_BLOCKS, CTX_BLOCKS - 1 - i, nb - 1 + CTX_BLOCKS - i)

    fwd = pl.BlockSpec((b, AB, w), lambda i: (0, i, 0))
    bwd = pl.BlockSpec((b, AB, w), lambda i: (0, back(i), 0))
    return pl.pallas_call(
        functools.partial(_ret_kernel, batch=b),
        grid=(nb,),
        in_specs=[pl.BlockSpec(memory_space=pltpu.SMEM), fwd, fwd, fwd, bwd, bwd, bwd],
        out_specs=[fwd, bwd],
        out_shape=[jax.ShapeDtypeStruct((b, s, w), F32)] * 2,
        scratch_shapes=[
            pltpu.VMEM((2 * b * tiles, LANE, LANE), F32),
            pltpu.VMEM((2 * tiles, 2 * AB, LANE), F32),
            pltpu.VMEM((2 * tiles, AB, LANE), F32),
            pltpu.VMEM((2 * tiles, AB, LANE), F32),
            pltpu.VMEM((2 * tiles, AB, LANE), F32),
        ],
        compiler_params=_params("arbitrary"),
    )(log_g.reshape(-1), rq, rk, rv, rq, rk, rv)


def _merge_kernel(x_ref, attn_ref, up_ref, uc_ref, un_ref, yf_ref, yb_ref, rg_ref, g_ref, mod_ref,
                  wbd_ref, ps_ref, woa_ref, wop_ref, wor_ref, wout_ref, n2_ref, wr_ref,
                  br_ref, avg_ref, tri_ref,
                  xo_ref, h2_ref, ls_ref, lst_ref, rw_ref, cnt_ref, *, seq, nt):
    i = pl.program_id(0)
    j = i % nt

    ext = TM + 2 * POOL_HALO
    seq_len = jnp.where(j == 0, TM, seq)
    start = jnp.where(j == 0, 0, (j - 1) * TM)
    u = uc_ref[0]
    ue = jnp.concatenate([up_ref[0], u, un_ref[0]], axis=0)
    erow = lax.broadcasted_iota(I32, (ext, 1), 0) + (start - POOL_HALO)
    ue = jnp.where((erow >= 0) & (erow < seq_len), ue, 0.0)
    t_pos = lax.broadcasted_iota(I32, (TM, 1), 0) + start
    glane = lax.broadcasted_iota(I32, (TM, POOL_WIDTH), 1) // (POOL_WIDTH // len(POOL_WINDOWS))
    run = ue
    width = 1
    diff = jnp.zeros((TM, POOL_WIDTH), F32)
    for gi, w in enumerate(POOL_WINDOWS):
        while width < w:
            run = run + pltpu.roll(run, ext - width, 0)
            width *= 2
        win = pltpu.roll(run, w // 2, 0)[POOL_HALO:POOL_HALO + TM]
        cnt = jnp.minimum(t_pos - w // 2 + w, seq_len) - jnp.maximum(t_pos - w // 2, 0)
        diff = jnp.where(glane == gi, win / cnt.astype(F32) - u, diff)
    pool = _dot(diff.astype(BF16), wbd_ref[...]) * ps_ref[...]

    def head_norm(y):
        mu = _dot_hilo(y, avg_ref[...])
        dlt = y - mu
        var = _dot_hilo(dlt * dlt, avg_ref[...])
        return dlt * lax.rsqrt(var + GN_EPS)

    rg = rg_ref[0].astype(F32)
    ret = head_norm(yf_ref[0]) * _silu(rg[:, :RET_W]) + head_norm(yb_ref[0]) * _silu(rg[:, RET_W:])

    d = x_ref.shape[-1]
    gate = lambda t: _sigmoid(g_ref[0, :, t * d:(t + 1) * d])
    m = (gate(0) * _dot(attn_ref[0], woa_ref[...]).astype(BF16)
         + gate(1) * _dot(pool.astype(BF16), wop_ref[...]).astype(BF16)
         + gate(2) * _dot(ret.astype(BF16), wor_ref[...]).astype(BF16))
    xn = x_ref[0] + mod_ref[0, 2:3, :] * _dot(m, wout_ref[...])
    xo_ref[0] = xn

    ms = jnp.mean(xn * xn, axis=-1, keepdims=True)
    h2 = xn * lax.rsqrt(ms + NORM_EPS) * n2_ref[...] * (1.0 + mod_ref[0, 4:5, :]) + mod_ref[0, 3:4, :]
    h2_ref[0] = h2.astype(BF16)
    hh, hl = _split(h2)
    full = _dot(hh, wr_ref[...])
    logits = full[:, :LANE] + full[:, LANE:] + _dot(hl, wr_ref[:, :LANE]) + br_ref[...]
    lanef = lax.broadcasted_iota(I32, (TM, LANE), 1).astype(F32)
    tops, hots = [], []
    for _ in range(TOP_K):
        mx = jnp.max(logits, axis=-1, keepdims=True)
        idx = jnp.min(jnp.where(logits == mx, lanef, float(LANE)), axis=-1, keepdims=True)
        hot = lanef == idx
        logits = jnp.where(hot, NEG * 2.0, logits)
        tops.append((mx, idx))
        hots.append(hot)
    ex = [jnp.exp(mx - tops[0][0]) for mx, _ in tops]
    tot = ex[0]
    for e in ex[1:]:
        tot = tot + e
    sel = jnp.zeros((TM, LANE), F32)
    for hot in hots:
        sel = jnp.where(hot, 1.0, sel)
    incl = _dot(tri_ref[...], sel.astype(BF16))
    cnt = incl[TM - 1:TM, :]
    run = jnp.floor((cnt + (SUBLANE - 1.0)) * (1.0 / SUBLANE)) * SUBLANE
    er = lax.broadcasted_iota(I32, (LANE, LANE), 0)
    ec = lax.broadcasted_iota(I32, (LANE, LANE), 1)
    earlier = jnp.where(er < ec, 1.0, 0.0).astype(BF16)
    run_start = _dot(jnp.broadcast_to(run, (SUBLANE, LANE)).astype(BF16), earlier)[0:1]
    slot = run_start + incl - 1.0
    ls = jnp.zeros((TM, LANE), F32)
    rw = jnp.zeros((TM, LANE), F32)
    for k in range(TOP_K):
        ls = jnp.where(lanef == k, jnp.sum(jnp.where(hots[k], slot, 0.0), axis=-1, keepdims=True), ls)
        rw = jnp.where(lanef == k, ex[k] / tot, rw)
    ls = jnp.where(lanef < TOP_K, ls, -1.0)
    ls_ref[...] = ls.astype(I32)
    lst_ref[...] = jnp.transpose(ls)[:SUBLANE, :].astype(I32)
    rw_ref[...] = rw
    cnt_ref[...] = jnp.broadcast_to(cnt, cnt_ref.shape).astype(I32)


def _merge(xm, attn, u, yf, yb, rg, gates, mod_l, seq, wts):
    b, s, d = xm.shape
    nt = s // TM
    ntt = b * nt
    hb = TM // POOL_HALO
    nh = s // POOL_HALO
    tok = lambda width: pl.BlockSpec((1, TM, width), lambda i: (i // nt, i % nt, 0))
    full = lambda a: pl.BlockSpec(a.shape, lambda i: (0,) * a.ndim)
    lanes = pl.BlockSpec((TM, LANE), lambda i: (i, 0))
    in_specs = [
        tok(d), tok(ATT_Q),
        pl.BlockSpec((1, POOL_HALO, POOL_WIDTH), lambda i: (i // nt, jnp.maximum((i % nt) * hb - 1, 0), 0)),
        tok(POOL_WIDTH),
        pl.BlockSpec((1, POOL_HALO, POOL_WIDTH), lambda i: (i // nt, jnp.minimum((i % nt + 1) * hb, nh - 1), 0)),
        tok(RET_W), tok(RET_W), tok(2 * RET_W), tok(3 * d),
        pl.BlockSpec((1, 6, d), lambda i: (jnp.where(i % nt == 0, b, i // nt), 0, 0)),
    ] + [full(a) for a in wts]
    return pl.pallas_call(
        functools.partial(_merge_kernel, seq=seq, nt=nt),
        grid=(ntt,),
        in_specs=in_specs,
        out_specs=[tok(d), tok(d), lanes, pl.BlockSpec((SUBLANE, TM), lambda i: (i, 0)), lanes,
                   pl.BlockSpec((SUBLANE, LANE), lambda i: (i, 0))],
        out_shape=[jax.ShapeDtypeStruct((b, s, d), F32), jax.ShapeDtypeStruct((b, s, d), BF16),
                   jax.ShapeDtypeStruct((b * s, LANE), I32), jax.ShapeDtypeStruct((ntt * SUBLANE, TM), I32),
                   jax.ShapeDtypeStruct((b * s, LANE), F32), jax.ShapeDtypeStruct((ntt * SUBLANE, LANE), I32)],
        compiler_params=_params("arbitrary"),
    )(xm, attn, u, u, u, yf, yb, rg, gates, mod_l, *wts)


def _rows(tile_index):
    return pl.ds(pl.multiple_of(tile_index * SUBLANE, SUBLANE), SUBLANE)


def _tile_copies(table_ref, tile, copy):
    def body(j, carry):
        copy(j, table_ref[tile * SORT_TILES + j])
        return carry

    lax.fori_loop(0, SORT_TILES, body, 0, unroll=8)


def _tile_waits(wait):
    def body(j, carry):
        wait(j)
        return carry

    lax.fori_loop(0, SORT_TILES, body, 0, unroll=8)


def _dispatch_kernel(dst_ref, fs_ref, fn_ref, nu_ref, h_ref, lst_ref, xs_hbm, buf, zbuf, sem, fill_sem,
                     *, n_blocks):
    i = pl.program_id(0)
    last = pl.num_programs(0) - 1
    cur = i % 2

    def send(tile, half):
        _tile_copies(dst_ref, tile, lambda j, t: pltpu.make_async_copy(
            buf.at[half, _rows(j)], xs_hbm.at[_rows(t)], sem.at[half]).start())

    def drain(half):
        _tile_waits(lambda j: pltpu.make_async_copy(
            buf.at[half, _rows(j)], xs_hbm.at[_rows(j)], sem.at[half]).wait())

    @pl.when(i == 0)
    def _():
        zbuf[...] = jnp.zeros_like(zbuf)

        def pad_copy(e, c):
            return pltpu.make_async_copy(zbuf.at[pl.ds(0, SUBLANE)], xs_hbm.at[_rows(fs_ref[e] + c)], fill_sem)

        def blk_copy(blk):
            return pltpu.make_async_copy(zbuf, xs_hbm.at[pl.ds(pl.multiple_of(blk * EB, EB), EB)], fill_sem)

        def per_expert(fn):
            def outer(e, carry):
                def inner(c, cc):
                    fn(e, c)
                    return cc
                return lax.fori_loop(0, fn_ref[e], inner, carry)
            lax.fori_loop(0, N_EXPERTS, outer, 0)

        def per_block(fn):
            def body(blk, carry):
                fn(blk)
                return carry
            lax.fori_loop(nu_ref[0], n_blocks, body, 0)

        per_expert(lambda e, c: pad_copy(e, c).start())
        per_block(lambda blk: blk_copy(blk).start())
        per_expert(lambda e, c: pad_copy(e, c).wait())
        per_block(lambda blk: blk_copy(blk).wait())

    slot = lax.broadcasted_iota(I32, (SORT_ROWS, TM), 0)
    lst = lst_ref[...]
    p = jnp.zeros((SORT_ROWS, TM), F32)
    for k in range(TOP_K):
        p = jnp.where(slot == lst[k:k + 1, :], 1.0, p)
    srt = _dot(p.astype(BF16), h_ref[...])

    @pl.when(i >= 2)
    def _():
        drain(cur)

    buf[cur] = srt
    send(i, cur)

    @pl.when(i == last)
    def _():
        @pl.when(i >= 1)
        def _():
            drain(1 - cur)
        drain(cur)


def _dispatch(h2, lst, tabs, n_used, n_blocks):
    t, d = h2.shape
    ntt = t // TM
    grid_spec = pltpu.PrefetchScalarGridSpec(
        num_scalar_prefetch=4,
        grid=(ntt,),
        in_specs=[pl.BlockSpec((TM, d), lambda i, *_: (i, 0)),
                  pl.BlockSpec((SUBLANE, TM), lambda i, *_: (i, 0))],
        out_specs=pl.BlockSpec(memory_space=pl.ANY),
        scratch_shapes=[pltpu.VMEM((2, SORT_ROWS, d), F32), pltpu.VMEM((EB, d), F32),
                        pltpu.SemaphoreType.DMA((2,)), pltpu.SemaphoreType.DMA],
    )
    return pl.pallas_call(
        functools.partial(_dispatch_kernel, n_blocks=n_blocks),
        grid_spec=grid_spec,
        out_shape=jax.ShapeDtypeStruct((n_blocks * EB, d), F32),
        compiler_params=_params("arbitrary"),
    )(tabs["dst"], tabs["fill_start"], tabs["fill_n"], n_used, h2, lst)


def _expert_kernel(be_ref, nu_ref, x_ref, w1_ref, b1_ref, w2_ref, b2_ref, y_ref, w1b, w2b):
    bi = pl.program_id(0)
    used = bi < nu_ref[0]
    fresh = (bi == 0) | (be_ref[bi] != be_ref[jnp.maximum(bi - 1, 0)])

    @pl.when(used & fresh)
    def _():
        w1b[...] = w1_ref[0, 0].astype(BF16)
        w2b[...] = w2_ref[0, 0].astype(BF16)

    @pl.when(jnp.logical_not(used))
    def _():
        y_ref[...] = jnp.zeros_like(y_ref)

    @pl.when(used)
    def _():
        xb = x_ref[...].astype(BF16)
        y = None
        for c in range(0, D_FF, FF_CHUNK):
            glu = _dot(xb, w1b[:, c:c + FF_CHUNK]) + b1_ref[0, 0, :, c:c + FF_CHUNK]
            lin = _dot(xb, w1b[:, D_FF + c:D_FF + c + FF_CHUNK]) + b1_ref[0, 0, :, D_FF + c:D_FF + c + FF_CHUNK]
            glu = jnp.minimum(glu, SWIGLU_LIMIT)
            lin = jnp.clip(lin, -SWIGLU_LIMIT, SWIGLU_LIMIT)
            act = glu * _sigmoid(SWIGLU_ALPHA * glu) * (lin + 1.0)
            part = _dot(act.astype(BF16), w2b[c:c + FF_CHUNK, :])
            y = part if y is None else y + part
        y_ref[...] = y + b2_ref[0, 0]


def _experts(xs, block_e, n_used, w1, b1, w2, b2, layer):
    rows, d = xs.shape
    depth, ne, _, f2 = w1.shape
    row_blk = lambda bi, be, nu: (bi, 0)
    grid_spec = pltpu.PrefetchScalarGridSpec(
        num_scalar_prefetch=2,
        grid=(rows // EB,),
        in_specs=[
            pl.BlockSpec((EB, d), row_blk),
            pl.BlockSpec((1, 1, d, f2), lambda bi, be, nu: (layer, be[bi], 0, 0)),
            pl.BlockSpec((1, 1, 1, f2), lambda bi, be, nu: (layer, be[bi], 0, 0)),
            pl.BlockSpec((1, 1, f2 // 2, d), lambda bi, be, nu: (layer, be[bi], 0, 0)),
            pl.BlockSpec((1, 1, 1, d), lambda bi, be, nu: (layer, be[bi], 0, 0)),
        ],
        out_specs=pl.BlockSpec((EB, d), row_blk),
        scratch_shapes=[pltpu.VMEM((d, f2), BF16), pltpu.VMEM((f2 // 2, d), BF16)],
    )
    return pl.pallas_call(
        _expert_kernel,
        grid_spec=grid_spec,
        out_shape=jax.ShapeDtypeStruct((rows, d), F32),
        compiler_params=_params("arbitrary"),
    )(block_e, n_used, xs, w1, b1.reshape(depth, ne, 1, f2), w2, b2.reshape(depth, ne, 1, d))


def _combine_kernel(dst_ref, x_ref, ls_ref, rw_ref, mod_ref, fg_ref, ys_hbm, xo_ref, buf, sem, *, final):
    i = pl.program_id(0)
    last = pl.num_programs(0) - 1
    cur = i % 2

    def fetch(tile, half):
        _tile_copies(dst_ref, tile, lambda j, t: pltpu.make_async_copy(
            ys_hbm.at[_rows(t)], buf.at[half, _rows(j)], sem.at[half]).start())

    @pl.when(i == 0)
    def _():
        fetch(0, 0)

    @pl.when(i < last)
    def _():
        fetch(i + 1, 1 - cur)

    _tile_waits(lambda j: pltpu.make_async_copy(
        ys_hbm.at[_rows(j)], buf.at[cur, _rows(j)], sem.at[cur]).wait())
    yb = buf[cur].astype(BF16)
    slot = lax.broadcasted_iota(I32, (TM, SORT_ROWS), 1)
    ls = ls_ref[...]
    rw = rw_ref[...]
    g = jnp.zeros((TM, SORT_ROWS), F32)
    for k in range(TOP_K):
        g = jnp.where(slot == ls[:, k:k + 1], rw[:, k:k + 1], g)
    xn = x_ref[0] + mod_ref[0, 5:6, :] * _dot(g.astype(BF16), yb)
    if final:
        ms = jnp.mean(xn * xn, axis=-1, keepdims=True)
        xn = xn * lax.rsqrt(ms + NORM_EPS) * fg_ref[...]
    xo_ref[0] = xn


def _combine(xn, ls, rw, mod_l, final_gain, ys, tabs, final):
    b, s, d = xn.shape
    nt = s // TM
    tok = pl.BlockSpec((1, TM, d), lambda i, *_: (i // nt, i % nt, 0))
    lanes = pl.BlockSpec((TM, LANE), lambda i, *_: (i, 0))
    if final:
        out_spec = pl.BlockSpec((1, TM, d), lambda i, *_: (i // nt, jnp.maximum(i % nt - 1, 0), 0))
        out_shape = jax.ShapeDtypeStruct((b, s - TM, d), F32)
    else:
        out_spec, out_shape = tok, jax.ShapeDtypeStruct((b, s, d), F32)
    grid_spec = pltpu.PrefetchScalarGridSpec(
        num_scalar_prefetch=1,
        grid=(b * nt,),
        in_specs=[tok, lanes, lanes,
                  pl.BlockSpec((1, 6, d), lambda i, *_: (jnp.where(i % nt == 0, b, i // nt), 0, 0)),
                  pl.BlockSpec((1, d), lambda i, *_: (0, 0)),
                  pl.BlockSpec(memory_space=pl.ANY)],
        out_specs=out_spec,
        scratch_shapes=[pltpu.VMEM((2, SORT_ROWS, d), F32), pltpu.SemaphoreType.DMA((2,))],
    )
    return pl.pallas_call(
        functools.partial(_combine_kernel, final=final),
        grid_spec=grid_spec,
        out_shape=out_shape,
        compiler_params=_params("arbitrary"),
    )(tabs["dst"], xn, ls, rw, mod_l, final_gain.reshape(1, d), ys)


def _inproj_weight(w_in_l):
    d = w_in_l.shape[0]
    sizes = (ATT_Q, N_KV_HEADS * HEAD_DIM, N_KV_HEADS * HEAD_DIM, POOL_WIDTH,
             RET_W, RET_W, RET_W, RET_W, RET_W, d, d, d)
    parts, off = [], 0
    for sz in sizes:
        parts.append(w_in_l[:, off:off + sz])
        off += sz
    q, k, v, u, rq, rk, rv, rgf, rgb, ga, gp, gr = parts
    twice = lambda w: jnp.concatenate(
        [w[:, h * HEAD_DIM:(h + 1) * HEAD_DIM] for h in range(N_KV_HEADS) for _ in range(2)], axis=1)
    cols = [q * HEAD_DIM ** -0.5, twice(k), twice(v), rq, rk * RET_DIM ** -0.5, rv, rgf, rgb, u, ga, gp, gr]
    return jnp.concatenate(cols, axis=1).astype(BF16)


def _block_diag(blocks):
    n, r, c = blocks.shape
    out = jnp.zeros((n * r, n * c), blocks.dtype)
    for g in range(n):
        out = out.at[g * r:(g + 1) * r, g * c:(g + 1) * c].set(blocks[g])
    return out


def _routing_tables(cnt, n_blocks):
    ntt = cnt.shape[0] // SUBLANE
    counts = cnt.reshape(ntt, SUBLANE, LANE)[:, 0, :N_EXPERTS]
    run = (counts + SUBLANE - 1) // SUBLANE
    total = jnp.sum(run, axis=0)
    eb = EB // SUBLANE
    padded = (total + eb - 1) // eb * eb
    pad_end = jnp.cumsum(padded)
    pad_start = pad_end - padded
    off = pad_start[None, :] + jnp.cumsum(run, axis=0) - run
    run_end = jnp.cumsum(run, axis=1)
    j = jnp.arange(SORT_TILES)
    owner = jnp.sum(run_end[:, None, :] <= j[None, :, None], axis=2)
    mine = owner[:, :, None] == jnp.arange(N_EXPERTS)[None, None, :]
    in_region = j[None, :] + jnp.sum(jnp.where(mine, (off - (run_end - run))[:, None, :], 0), axis=2)
    spare = (n_blocks - SPARE_BLOCKS) * eb + (jnp.arange(ntt) % 2)[:, None] * SORT_TILES + j[None, :]
    dst = jnp.where(owner < N_EXPERTS, in_region, spare)
    n_used = pad_end[-1] // eb
    blk = jnp.minimum(jnp.arange(n_blocks), n_used - 1) * eb
    block_e = jnp.minimum(jnp.sum(pad_end[None, :] <= blk[:, None], axis=1), N_EXPERTS - 1)
    tabs = dict(dst=dst.reshape(-1).astype(I32), fill_start=(pad_start + total).astype(I32),
                fill_n=(padded - total).astype(I32))
    return tabs, block_e.astype(I32), n_used.reshape(1).astype(I32)


def kernel(x, c, ctx, c_ctx, w_mod, b_mod, norm1, norm2, w_in, attn_sinks, pool_w, pool_scale, ret_decay,
           w_o_attn, w_o_pool, w_o_ret, w_out, w_router, b_router, w_expert_in, b_expert_in, w_expert_out,
           b_expert_out, final_norm):
    b, seq, d = x.shape
    depth = w_mod.shape[0]
    assert ctx.shape[1] == TM and seq % TM == 0 and seq % GRID_W == 0
    s = seq + TM
    xm = jnp.concatenate([ctx, x], axis=1)
    mod = _modulation(c, c_ctx, w_mod, b_mod)
    tables = _rope_tables(seq)
    ntt = b * s // TM
    n_blocks = -(-(b * s * TOP_K + ntt * N_EXPERTS * (SUBLANE - 1)) // EB) + N_EXPERTS + SPARE_BLOCKS
    avg = _block_diag(jnp.full((RET_HEADS, RET_DIM, RET_DIM), 1.0 / RET_DIM, F32)).astype(BF16)
    tri = (jnp.arange(TM)[:, None] >= jnp.arange(TM)[None, :]).astype(BF16)
    for l in range(depth):
        q, k2, v2, rq, rk, rv, rg, u, gates = _inproj(xm, mod[l], norm1[l], _inproj_weight(w_in[l]), tables)
        attn = _attention(q, k2, v2, attn_sinks[l])
        yf, yb = _retention(rq, rk, rv, jax.nn.log_sigmoid(ret_decay[l].astype(F32)))
        wr = jnp.zeros((d, LANE), F32).at[:, :N_EXPERTS].set(w_router[l])
        wr_hilo = jnp.concatenate(_split(wr), axis=1)
        br = jnp.full((1, LANE), NEG, F32).at[0, :N_EXPERTS].set(b_router[l])
        wts = (_block_diag(pool_w[l]).astype(BF16), pool_scale[l].reshape(1, -1),
               w_o_attn[l].astype(BF16), w_o_pool[l].astype(BF16), w_o_ret[l].astype(BF16),
               w_out[l].astype(BF16), norm2[l].reshape(1, d), wr_hilo, br, avg, tri)
        xn, h2, ls, lst, rw, cnt = _merge(xm, attn, u, yf, yb, rg, gates, mod[l], seq, wts)
        tabs, block_e, n_used = _routing_tables(cnt, n_blocks)
        xs = _dispatch(h2.reshape(b * s, d), lst, tabs, n_used, n_blocks)
        ys = _experts(xs, block_e, n_used, w_expert_in, b_expert_in, w_expert_out, b_expert_out, l)
        xm = _combine(xn, ls, rw, mod[l], final_norm, ys, tabs, final=l == depth - 1)
    return xm
```

```python
import functools

import jax
import jax.numpy as jnp
from jax import lax
from jax.experimental import pallas as pl
from jax.experimental.pallas import tpu as pltpu

F32 = jnp.float32
BF16 = jnp.bfloat16
I32 = jnp.int32

GRID_W = 64
HEAD_DIM = 64
N_Q_HEADS = 8
N_KV_HEADS = 2
WINDOW = 128
ROPE_THETA = 10000.0
POOL_WIDTH = 256
POOL_WINDOWS = (2, 4, 8, 16)
RET_HEADS = 4
RET_DIM = 64
N_EXPERTS = 32
TOP_K = 4
D_FF = 1024
SWIGLU_LIMIT = 7.0
SWIGLU_ALPHA = 1.702
NORM_EPS = 1e-6
GN_EPS = 1e-5

LANE = 128
SUBLANE = 8
MXU_N = 256
TM = 256
AB = 128
IN_TILE = 640
CTX_BLOCKS = TM // AB
EB = 512
FF_CHUNK = 256
POOL_HALO = SUBLANE
SORT_ROWS = TM * TOP_K + N_EXPERTS * SUBLANE
SORT_TILES = SORT_ROWS // SUBLANE
SPARE_BLOCKS = 2 * SORT_ROWS // EB
NEG = -1e30
VMEM_LIMIT = 56 * 1024 * 1024

ATT_Q = N_Q_HEADS * HEAD_DIM
KV2 = 2 * N_KV_HEADS * HEAD_DIM
RET_W = RET_HEADS * RET_DIM


def _dot(a, b):
    return jnp.dot(a, b, preferred_element_type=F32)


def _dot_nt(a, b):
    return lax.dot_general(a, b, (((1,), (1,)), ((), ())), preferred_element_type=F32)


def _dot_tn(a, b):
    return lax.dot_general(a, b, (((0,), (0,)), ((), ())), preferred_element_type=F32)


def _split(a):
    hi = a.astype(BF16)
    lo = (a - hi.astype(F32)).astype(BF16)
    return hi, lo


def _dot_hilo(a, m):
    hi, lo = _split(a)
    return _dot(hi, m) + _dot(lo, m)


def _sigmoid(x):
    return 0.5 * jnp.tanh(0.5 * x) + 0.5


def _silu(x):
    return x * _sigmoid(x)


def _params(*sem):
    return pltpu.CompilerParams(dimension_semantics=sem, vmem_limit_bytes=VMEM_LIMIT)


def _mod_kernel(c_ref, w_ref, b_ref, o_ref):
    s = _silu(c_ref[...])
    sh, sl = _split(s)
    wh, wl = _split(w_ref[0])
    o_ref[0] = _dot(sh, wh) + _dot(sh, wl) + _dot(sl, wh) + b_ref[0]


def _modulation(c, c_ctx, w_mod, b_mod):
    depth, d, six_d = w_mod.shape
    b = c.shape[0]
    cc = jnp.zeros((SUBLANE, d), F32).at[:b].set(c).at[b].set(c_ctx)
    out = pl.pallas_call(
        _mod_kernel,
        grid=(depth, six_d // d),
        in_specs=[
            pl.BlockSpec((SUBLANE, d), lambda l, n: (0, 0)),
            pl.BlockSpec((1, d, d), lambda l, n: (l, 0, n)),
            pl.BlockSpec((1, 1, d), lambda l, n: (l, 0, n)),
        ],
        out_specs=pl.BlockSpec((1, SUBLANE, d), lambda l, n: (l, 0, n)),
        out_shape=jax.ShapeDtypeStruct((depth, SUBLANE, six_d), F32),
        compiler_params=_params("arbitrary", "arbitrary"),
    )(cc, w_mod, b_mod.reshape(depth, 1, six_d))
    return out[:, : b + 1].reshape(depth, b + 1, 6, d)


def _rope(x, cos, sin, half, first):
    partner = jnp.where(first, pltpu.roll(x, LANE - half, 1), pltpu.roll(x, half, 1))
    return x * cos + partner * sin


def _inproj_kernel(x_ref, mod_ref, modc_ref, n1_ref, w_ref, ac_ref, as_ref, rc_ref, rs_ref,
                   q_ref, k_ref, v_ref, rq_ref, rk_ref, rv_ref, rg_ref, u_ref, g_ref):
    x = x_ref[0]
    rows = x.shape[0]
    ms = jnp.mean(x * x, axis=-1, keepdims=True)
    y = x * lax.rsqrt(ms + NORM_EPS) * n1_ref[...]
    is_ctx = (lax.broadcasted_iota(I32, (rows, 1), 0) < TM) & (pl.program_id(1) == 0)
    scale = jnp.where(is_ctx, modc_ref[0, 1:2, :], mod_ref[0, 1:2, :])
    shift = jnp.where(is_ctx, modc_ref[0, 0:1, :], mod_ref[0, 0:1, :])
    hb = (y * (1.0 + scale) + shift).astype(BF16)
    lane = lax.broadcasted_iota(I32, (rows, LANE), 1)
    a_first = (lane % (HEAD_DIM // 2)) < (HEAD_DIM // 4)
    r_first = (lane % RET_DIM) < (RET_DIM // 2)
    ac, asn, rc, rsn = ac_ref[...], as_ref[...], rc_ref[...], rs_ref[...]

    def proj(off, width):
        return _dot(hb, w_ref[:, off:off + width])

    def rotated(ref, width, cos, sin, half, first, off):
        for t in range(0, width, MXU_N):
            pr = proj(off + t, MXU_N)
            for g in range(0, MXU_N, LANE):
                ref[0, :, t + g:t + g + LANE] = _rope(pr[:, g:g + LANE], cos, sin, half, first).astype(BF16)

    off = 0
    rotated(q_ref, ATT_Q, ac, asn, HEAD_DIM // 4, a_first, off)
    off += ATT_Q
    rotated(k_ref, KV2, ac, asn, HEAD_DIM // 4, a_first, off)
    off += KV2
    v_ref[0] = proj(off, KV2).astype(BF16)
    off += KV2
    for ref in (rq_ref, rk_ref):
        rotated(ref, RET_W, rc, rsn, RET_DIM // 2, r_first, off)
        off += RET_W
    rv_ref[0] = proj(off, RET_W).astype(BF16)
    off += RET_W
    rg_ref[0] = proj(off, 2 * RET_W).astype(BF16)
    off += 2 * RET_W
    u_ref[0] = proj(off, POOL_WIDTH)
    off += POOL_WIDTH
    d = x.shape[-1]
    for t in range(3):
        g_ref[0, :, t * d:(t + 1) * d] = proj(off, d).astype(BF16)
        off += d


def _inproj(xm, mod_l, norm1_l, w1, tables):
    b, s, d = xm.shape
    ti = IN_TILE if s % IN_TILE == 0 else TM
    wcols = w1.shape[1]
    tok = lambda width: pl.BlockSpec((1, ti, width), lambda bi, j: (bi, j, 0))
    tab = pl.BlockSpec((ti, LANE), lambda bi, j: (j, 0))
    widths = (ATT_Q, KV2, KV2, RET_W, RET_W, RET_W, 2 * RET_W, POOL_WIDTH, 3 * d)
    dtypes = (BF16,) * 7 + (F32, BF16)
    return pl.pallas_call(
        _inproj_kernel,
        grid=(b, s // ti),
        in_specs=[
            tok(d),
            pl.BlockSpec((1, 6, d), lambda bi, j: (bi, 0, 0)),
            pl.BlockSpec((1, 6, d), lambda bi, j: (b, 0, 0)),
            pl.BlockSpec((1, d), lambda bi, j: (0, 0)),
            pl.BlockSpec((d, wcols), lambda bi, j: (0, 0), pipeline_mode=pl.Buffered(1)),
            tab, tab, tab, tab,
        ],
        out_specs=[tok(w) for w in widths],
        out_shape=[jax.ShapeDtypeStruct((b, s, w), dt) for w, dt in zip(widths, dtypes)],
        compiler_params=_params("arbitrary", "arbitrary"),
    )(xm, mod_l, mod_l, norm1_l.reshape(1, d), w1, *tables)


def _rope_tables(seq):
    rows = seq // GRID_W
    rpos = jnp.arange(rows, dtype=F32)[:, None]
    cpos = jnp.arange(GRID_W, dtype=F32)[:, None]
    lane = jnp.arange(LANE)
    grid = lambda per_row, per_col: (per_row[:, None, :] + per_col[None, :, :]).reshape(seq, LANE)
    axis_dim = HEAD_DIM // 2
    inv_a = ROPE_THETA ** (-jnp.arange(0, axis_dim, 2, dtype=F32) / axis_dim)
    hl = lane % HEAD_DIM
    inv_al = inv_a[(hl % axis_dim) % (axis_dim // 2)][None, :]
    by_row = (hl < axis_dim)[None, :]
    a_sign = jnp.where((hl % axis_dim) < axis_dim // 2, -1.0, 1.0)[None, :]
    a_cos = grid(jnp.where(by_row, jnp.cos(rpos * inv_al), 0.0), jnp.where(by_row, 0.0, jnp.cos(cpos * inv_al)))
    a_sin = grid(jnp.where(by_row, jnp.sin(rpos * inv_al), 0.0), jnp.where(by_row, 0.0, jnp.sin(cpos * inv_al)))
    inv_r = 1.0 / (ROPE_THETA ** jnp.linspace(0.0, 1.0, RET_DIM // 2, dtype=F32))
    rl = lane % RET_DIM
    inv_rl = inv_r[rl % (RET_DIM // 2)][None, :]
    r_sign = jnp.where(rl < RET_DIM // 2, -1.0, 1.0)[None, :]
    hi = (rpos * GRID_W) * inv_rl
    lo = cpos * inv_rl
    outer = lambda a, b: (a[:, None, :] * b[None, :, :]).reshape(seq, LANE)
    r_cos = outer(jnp.cos(hi), jnp.cos(lo)) - outer(jnp.sin(hi), jnp.sin(lo))
    r_sin = outer(jnp.sin(hi), jnp.cos(lo)) + outer(jnp.cos(hi), jnp.sin(lo))
    ones = jnp.ones((TM, LANE), F32)
    zeros = jnp.zeros((TM, LANE), F32)
    cat = lambda head, body: jnp.concatenate([head, body], axis=0)
    return (cat(ones, a_cos), cat(zeros, a_sin * a_sign), cat(ones, r_cos), cat(zeros, r_sin * r_sign))


def _attn_kernel(sink_ref, bias_ref, q_ref, kp_ref, kc_ref, kn_ref, kx_ref, vp_ref, vc_ref, vn_ref, vx_ref, o_ref):
    g_heads = N_Q_HEADS // N_KV_HEADS
    lane = lax.broadcasted_iota(I32, (AB, LANE), 1)
    lo = lane < HEAD_DIM
    rows = lax.broadcasted_iota(I32, (g_heads * AB, 1), 0)
    bias = bias_ref[0]
    scores, sinks, values = [], [], []
    for g in range(N_KV_HEADS):
        ks = slice(g * LANE, (g + 1) * LANE)
        kd = jnp.concatenate([kp_ref[0, :, ks], kc_ref[0, :, ks], kn_ref[0, :, ks], kx_ref[0, :, ks]], axis=0)
        values.append(jnp.concatenate([vp_ref[0, :, ks], vc_ref[0, :, ks], vn_ref[0, :, ks], vx_ref[0, :, ks]], axis=0))
        qs = []
        for c in range(2 * g, 2 * g + 2):
            qc = q_ref[0, :, c * LANE:(c + 1) * LANE]
            zero = jnp.zeros_like(qc)
            qs += [jnp.where(lo, qc, zero), jnp.where(lo, zero, qc)]
        scores.append(_dot_nt(jnp.concatenate(qs, axis=0), kd) + bias)
        sink = jnp.full((g_heads * AB, 1), sink_ref[g_heads * g + g_heads - 1], F32)
        for t in range(g_heads - 2, -1, -1):
            sink = jnp.where(rows < (t + 1) * AB, sink_ref[g_heads * g + t], sink)
        sinks.append(sink)
    outs = []
    for s, sink, vd in zip(scores, sinks, values):
        m = jnp.maximum(jnp.max(s, axis=-1, keepdims=True), sink)
        p = jnp.exp(s - m)
        den = jnp.sum(p, axis=-1, keepdims=True) + jnp.exp(sink - m)
        o = _dot(p.astype(BF16), vd) / den
        for t in range(2):
            outs.append(jnp.where(lo, o[2 * t * AB:(2 * t + 1) * AB], o[(2 * t + 1) * AB:(2 * t + 2) * AB]))
    o_ref[0] = jnp.concatenate(outs, axis=-1).astype(BF16)


def _attn_bias():
    g_heads = N_Q_HEADS // N_KV_HEADS
    r = jnp.arange(g_heads * AB)[:, None] % AB
    j = jnp.arange(3 * AB + TM)[None, :]
    band = jnp.abs(j - AB - r) <= WINDOW
    is_ctx = j >= 3 * AB
    variants = (band, band & (j >= AB), band & (j < 2 * AB), jnp.zeros_like(band))
    return jnp.stack([jnp.where(v | is_ctx, 0.0, NEG) for v in variants]).astype(F32)


def _attention(q, k2, v2, sinks):
    b, s, _ = q.shape
    nb = s // AB
    assert nb - CTX_BLOCKS >= 2
    bias = _attn_bias()
    variant = lambda i: jnp.where(i < CTX_BLOCKS, 3, jnp.where(i == CTX_BLOCKS, 1, jnp.where(i == nb - 1, 2, 0)))
    prev = pl.BlockSpec((1, AB, KV2), lambda bi, i: (bi, jnp.maximum(i - 1, 0), 0))
    cur = pl.BlockSpec((1, AB, KV2), lambda bi, i: (bi, i, 0))
    nxt = pl.BlockSpec((1, AB, KV2), lambda bi, i: (bi, jnp.minimum(i + 1, nb - 1), 0))
    cx = pl.BlockSpec((1, TM, KV2), lambda bi, i: (bi, 0, 0))
    return pl.pallas_call(
        _attn_kernel,
        grid=(b, nb),
        in_specs=[pl.BlockSpec(memory_space=pltpu.SMEM),
                  pl.BlockSpec((1,) + bias.shape[1:], lambda bi, i: (variant(i), 0, 0)),
                  pl.BlockSpec((1, AB, ATT_Q), lambda bi, i: (bi, i, 0)),
                  prev, cur, nxt, cx, prev, cur, nxt, cx],
        out_specs=pl.BlockSpec((1, AB, ATT_Q), lambda bi, i: (bi, i, 0)),
        out_shape=jax.ShapeDtypeStruct((b, s, ATT_Q), BF16),
        compiler_params=_params("arbitrary", "arbitrary"),
    )(sinks, bias, q, k2, k2, k2, k2, v2, v2, v2, v2)


def _ret_kernel(lg_ref, qf_ref, kf_ref, vf_ref, qb_ref, kb_ref, vb_ref, yf_ref, yb_ref,
                st_ref, dm_ref, qd_ref, kd_ref, cd_ref, *, batch):
    step = pl.program_id(0)
    lane = lax.broadcasted_iota(I32, (AB, LANE), 1)
    row = lax.broadcasted_iota(I32, (AB, LANE), 0)
    lo = lane < RET_DIM
    tiles = RET_W // LANE

    @pl.when(step == 0)
    def _():
        st_ref[...] = jnp.zeros_like(st_ref)
        ii = row.astype(F32)
        jj = lane.astype(F32)
        for d in range(2):
            for c in range(tiles):
                lg0 = lg_ref[d * RET_HEADS + 2 * c]
                lg1 = lg_ref[d * RET_HEADS + 2 * c + 1]
                lgl = jnp.where(lo, lg0, lg1)
                q_exp = ii + 1.0 if d == 0 else AB - ii
                k_exp = (AB - 1.0) - ii if d == 0 else ii
                qd_ref[d * tiles + c] = jnp.exp(q_exp * lgl)
                kd_ref[d * tiles + c] = jnp.exp(k_exp * lgl)
                cd_ref[d * tiles + c] = jnp.exp(AB * lgl)
                rel = ii - jj if d == 0 else jj - ii
                for hh, lgh in enumerate((lg0, lg1)):
                    dm_ref[d * tiles + c, hh * AB:(hh + 1) * AB, :] = jnp.where(
                        rel >= 0, jnp.exp(jnp.maximum(rel, 0.0) * lgh), 0.0)

    same_head = (row < RET_DIM) == lo
    dirs = ((qf_ref, kf_ref, vf_ref, yf_ref), (qb_ref, kb_ref, vb_ref, yb_ref))
    chains = [(d, b, c) for d in range(2) for b in range(batch) for c in range(tiles)]

    def operands(d, b, c):
        q_ref, k_ref, v_ref, _ = dirs[d]
        sl = slice(c * LANE, (c + 1) * LANE)
        return q_ref[b, :, sl], k_ref[b, :, sl], v_ref[b, :, sl]

    probs = []
    for d, b, c in chains:
        q, k, _ = operands(d, b, c)
        zero = jnp.zeros_like(q)
        q2 = jnp.concatenate([jnp.where(lo, q, zero), jnp.where(lo, zero, q)], axis=0)
        probs.append((_dot_nt(q2, k) * dm_ref[d * tiles + c]).astype(BF16))
    for (d, b, c), p in zip(chains, probs):
        q, _, v = operands(d, b, c)
        t = d * tiles + c
        si = (d * batch + b) * tiles + c
        y_intra = jnp.where(lo, _dot(p[:AB], v), _dot(p[AB:], v))
        q_dec = (q.astype(F32) * qd_ref[t]).astype(BF16)
        dirs[d][3][b, :, c * LANE:(c + 1) * LANE] = y_intra + _dot(q_dec, st_ref[si].astype(BF16))
    for d, b, c in chains:
        _, k, v = operands(d, b, c)
        t = d * tiles + c
        si = (d * batch + b) * tiles + c
        k_dec = (k.astype(F32) * kd_ref[t]).astype(BF16)
        st_ref[si] = st_ref[si] * cd_ref[t] + jnp.where(same_head, _dot_tn(k_dec, v), 0.0)


def _retention(rq, rk, rv, log_g):
    b, s, w = rq.shape
    nb = s // AB
    tiles = w // LANE

    def back(i):
        return jnp.where(i < CTX_BLOCKS, CTX_BLOCKS - 1 - i, nb - 1 + CTX_BLOCKS - i)

    fwd = pl.BlockSpec((b, AB, w), lambda i: (0, i, 0))
    bwd = pl.BlockSpec((b, AB, w), lambda i: (0, back(i), 0))
    return pl.pallas_call(
        functools.partial(_ret_kernel, batch=b),
        grid=(nb,),
        in_specs=[pl.BlockSpec(memory_space=pltpu.SMEM), fwd, fwd, fwd, bwd, bwd, bwd],
        out_specs=[fwd, bwd],
        out_shape=[jax.ShapeDtypeStruct((b, s, w), F32)] * 2,
        scratch_shapes=[
            pltpu.VMEM((2 * b * tiles, LANE, LANE), F32),
            pltpu.VMEM((2 * tiles, 2 * AB, LANE), F32),
            pltpu.VMEM((2 * tiles, AB, LANE), F32),
            pltpu.VMEM((2 * tiles, AB, LANE), F32),
            pltpu.VMEM((2 * tiles, AB, LANE), F32),
        ],
        compiler_params=_params("arbitrary"),
    )(log_g.reshape(-1), rq, rk, rv, rq, rk, rv)


def _merge_kernel(x_ref, attn_ref, up_ref, uc_ref, un_ref, yf_ref, yb_ref, rg_ref, g_ref, mod_ref,
                  wbd_ref, ps_ref, woa_ref, wop_ref, wor_ref, wout_ref, n2_ref, wr_ref,
                  br_ref, avg_ref, tri_ref,
                  xo_ref, h2_ref, ls_ref, lst_ref, rw_ref, cnt_ref, *, seq, nt):
    i = pl.program_id(0)
    j = i % nt

    ext = TM + 2 * POOL_HALO
    seq_len = jnp.where(j == 0, TM, seq)
    start = jnp.where(j == 0, 0, (j - 1) * TM)
    u = uc_ref[0]
    ue = jnp.concatenate([up_ref[0], u, un_ref[0]], axis=0)
    erow = lax.broadcasted_iota(I32, (ext, 1), 0) + (start - POOL_HALO)
    ue = jnp.where((erow >= 0) & (erow < seq_len), ue, 0.0)
    t_pos = lax.broadcasted_iota(I32, (TM, 1), 0) + start
    glane = lax.broadcasted_iota(I32, (TM, POOL_WIDTH), 1) // (POOL_WIDTH // len(POOL_WINDOWS))
    run = ue
    width = 1
    diff = jnp.zeros((TM, POOL_WIDTH), F32)
    for gi, w in enumerate(POOL_WINDOWS):
        while width < w:
            run = run + pltpu.roll(run, ext - width, 0)
            width *= 2
        win = pltpu.roll(run, w // 2, 0)[POOL_HALO:POOL_HALO + TM]
        cnt = jnp.minimum(t_pos - w // 2 + w, seq_len) - jnp.maximum(t_pos - w // 2, 0)
        diff = jnp.where(glane == gi, win / cnt.astype(F32) - u, diff)
    pool = _dot(diff.astype(BF16), wbd_ref[...]) * ps_ref[...]

    def head_norm(y):
        mu = _dot_hilo(y, avg_ref[...])
        dlt = y - mu
        var = _dot_hilo(dlt * dlt, avg_ref[...])
        return dlt * lax.rsqrt(var + GN_EPS)

    rg = rg_ref[0].astype(F32)
    ret = head_norm(yf_ref[0]) * _silu(rg[:, :RET_W]) + head_norm(yb_ref[0]) * _silu(rg[:, RET_W:])

    d = x_ref.shape[-1]
    gate = lambda t: _sigmoid(g_ref[0, :, t * d:(t + 1) * d])
    m = (gate(0) * _dot(attn_ref[0], woa_ref[...]).astype(BF16)
         + gate(1) * _dot(pool.astype(BF16), wop_ref[...]).astype(BF16)
         + gate(2) * _dot(ret.astype(BF16), wor_ref[...]).astype(BF16))
    xn = x_ref[0] + mod_ref[0, 2:3, :] * _dot(m, wout_ref[...])
    xo_ref[0] = xn

    ms = jnp.mean(xn * xn, axis=-1, keepdims=True)
    h2 = xn * lax.rsqrt(ms + NORM_EPS) * n2_ref[...] * (1.0 + mod_ref[0, 4:5, :]) + mod_ref[0, 3:4, :]
    h2_ref[0] = h2.astype(BF16)
    hh, hl = _split(h2)
    full = _dot(hh, wr_ref[...])
    logits = full[:, :LANE] + full[:, LANE:] + _dot(hl, wr_ref[:, :LANE]) + br_ref[...]
    lanef = lax.broadcasted_iota(I32, (TM, LANE), 1).astype(F32)
    tops, hots = [], []
    for _ in range(TOP_K):
        mx = jnp.max(logits, axis=-1, keepdims=True)
        idx = jnp.min(jnp.where(logits == mx, lanef, float(LANE)), axis=-1, keepdims=True)
        hot = lanef == idx
        logits = jnp.where(hot, NEG * 2.0, logits)
        tops.append((mx, idx))
        hots.append(hot)
    ex = [jnp.exp(mx - tops[0][0]) for mx, _ in tops]
    tot = ex[0]
    for e in ex[1:]:
        tot = tot + e
    sel = jnp.zeros((TM, LANE), F32)
    for hot in hots:
        sel = jnp.where(hot, 1.0, sel)
    incl = _dot(tri_ref[...], sel.astype(BF16))
    cnt = incl[TM - 1:TM, :]
    run = jnp.floor((cnt + (SUBLANE - 1.0)) * (1.0 / SUBLANE)) * SUBLANE
    er = lax.broadcasted_iota(I32, (LANE, LANE), 0)
    ec = lax.broadcasted_iota(I32, (LANE, LANE), 1)
    earlier = jnp.where(er < ec, 1.0, 0.0).astype(BF16)
    run_start = _dot(jnp.broadcast_to(run, (SUBLANE, LANE)).astype(BF16), earlier)[0:1]
    slot = run_start + incl - 1.0
    ls = jnp.zeros((TM, LANE), F32)
    rw = jnp.zeros((TM, LANE), F32)
    for k in range(TOP_K):
        ls = jnp.where(lanef == k, jnp.sum(jnp.where(hots[k], slot, 0.0), axis=-1, keepdims=True), ls)
        rw = jnp.where(lanef == k, ex[k] / tot, rw)
    ls = jnp.where(lanef < TOP_K, ls, -1.0)
    ls_ref[...] = ls.astype(I32)
    lst_ref[...] = jnp.transpose(ls)[:SUBLANE, :].astype(I32)
    rw_ref[...] = rw
    cnt_ref[...] = jnp.broadcast_to(cnt, cnt_ref.shape).astype(I32)


def _merge(xm, attn, u, yf, yb, rg, gates, mod_l, seq, wts):
    b, s, d = xm.shape
    nt = s // TM
    ntt = b * nt
    hb = TM // POOL_HALO
    nh = s // POOL_HALO
    tok = lambda width: pl.BlockSpec((1, TM, width), lambda i: (i // nt, i % nt, 0))
    full = lambda a: pl.BlockSpec(a.shape, lambda i: (0,) * a.ndim)
    lanes = pl.BlockSpec((TM, LANE), lambda i: (i, 0))
    in_specs = [
        tok(d), tok(ATT_Q),
        pl.BlockSpec((1, POOL_HALO, POOL_WIDTH), lambda i: (i // nt, jnp.maximum((i % nt) * hb - 1, 0), 0)),
        tok(POOL_WIDTH),
        pl.BlockSpec((1, POOL_HALO, POOL_WIDTH), lambda i: (i // nt, jnp.minimum((i % nt + 1) * hb, nh - 1), 0)),
        tok(RET_W), tok(RET_W), tok(2 * RET_W), tok(3 * d),
        pl.BlockSpec((1, 6, d), lambda i: (jnp.where(i % nt == 0, b, i // nt), 0, 0)),
    ] + [full(a) for a in wts]
    return pl.pallas_call(
        functools.partial(_merge_kernel, seq=seq, nt=nt),
        grid=(ntt,),
        in_specs=in_specs,
        out_specs=[tok(d), tok(d), lanes, pl.BlockSpec((SUBLANE, TM), lambda i: (i, 0)), lanes,
                   pl.BlockSpec((SUBLANE, LANE), lambda i: (i, 0))],
        out_shape=[jax.ShapeDtypeStruct((b, s, d), F32), jax.ShapeDtypeStruct((b, s, d), BF16),
                   jax.ShapeDtypeStruct((b * s, LANE), I32), jax.ShapeDtypeStruct((ntt * SUBLANE, TM), I32),
                   jax.ShapeDtypeStruct((b * s, LANE), F32), jax.ShapeDtypeStruct((ntt * SUBLANE, LANE), I32)],
        compiler_params=_params("arbitrary"),
    )(xm, attn, u, u, u, yf, yb, rg, gates, mod_l, *wts)


def _rows(tile_index):
    return pl.ds(pl.multiple_of(tile_index * SUBLANE, SUBLANE), SUBLANE)


def _tile_copies(table_ref, tile, copy):
    def body(j, carry):
        copy(j, table_ref[tile * SORT_TILES + j])
        return carry

    lax.fori_loop(0, SORT_TILES, body, 0, unroll=8)


def _tile_waits(wait):
    def body(j, carry):
        wait(j)
        return carry

    lax.fori_loop(0, SORT_TILES, body, 0, unroll=8)


def _dispatch_kernel(dst_ref, fs_ref, fn_ref, nu_ref, h_ref, lst_ref, xs_hbm, buf, zbuf, sem, fill_sem,
                     *, n_blocks):
    i = pl.program_id(0)
    last = pl.num_programs(0) - 1
    cur = i % 2

    def send(tile, half):
        _tile_copies(dst_ref, tile, lambda j, t: pltpu.make_async_copy(
            buf.at[half, _rows(j)], xs_hbm.at[_rows(t)], sem.at[half]).start())

    def drain(half):
        _tile_waits(lambda j: pltpu.make_async_copy(
            buf.at[half, _rows(j)], xs_hbm.at[_rows(j)], sem.at[half]).wait())

    @pl.when(i == 0)
    def _():
        zbuf[...] = jnp.zeros_like(zbuf)

        def pad_copy(e, c):
            return pltpu.make_async_copy(zbuf.at[pl.ds(0, SUBLANE)], xs_hbm.at[_rows(fs_ref[e] + c)], fill_sem)

        def blk_copy(blk):
            return pltpu.make_async_copy(zbuf, xs_hbm.at[pl.ds(pl.multiple_of(blk * EB, EB), EB)], fill_sem)

        def per_expert(fn):
            def outer(e, carry):
                def inner(c, cc):
                    fn(e, c)
                    return cc
                return lax.fori_loop(0, fn_ref[e], inner, carry)
            lax.fori_loop(0, N_EXPERTS, outer, 0)

        def per_block(fn):
            def body(blk, carry):
                fn(blk)
                return carry
            lax.fori_loop(nu_ref[0], n_blocks, body, 0)

        per_expert(lambda e, c: pad_copy(e, c).start())
        per_block(lambda blk: blk_copy(blk).start())
        per_expert(lambda e, c: pad_copy(e, c).wait())
        per_block(lambda blk: blk_copy(blk).wait())

    slot = lax.broadcasted_iota(I32, (SORT_ROWS, TM), 0)
    lst = lst_ref[...]
    p = jnp.zeros((SORT_ROWS, TM), F32)
    for k in range(TOP_K):
        p = jnp.where(slot == lst[k:k + 1, :], 1.0, p)
    srt = _dot(p.astype(BF16), h_ref[...])

    @pl.when(i >= 2)
    def _():
        drain(cur)

    buf[cur] = srt
    send(i, cur)

    @pl.when(i == last)
    def _():
        @pl.when(i >= 1)
        def _():
            drain(1 - cur)
        drain(cur)


def _dispatch(h2, lst, tabs, n_used, n_blocks):
    t, d = h2.shape
    ntt = t // TM
    grid_spec = pltpu.PrefetchScalarGridSpec(
        num_scalar_prefetch=4,
        grid=(ntt,),
        in_specs=[pl.BlockSpec((TM, d), lambda i, *_: (i, 0)),
                  pl.BlockSpec((SUBLANE, TM), lambda i, *_: (i, 0))],
        out_specs=pl.BlockSpec(memory_space=pl.ANY),
        scratch_shapes=[pltpu.VMEM((2, SORT_ROWS, d), F32), pltpu.VMEM((EB, d), F32),
                        pltpu.SemaphoreType.DMA((2,)), pltpu.SemaphoreType.DMA],
    )
    return pl.pallas_call(
        functools.partial(_dispatch_kernel, n_blocks=n_blocks),
        grid_spec=grid_spec,
        out_shape=jax.ShapeDtypeStruct((n_blocks * EB, d), F32),
        compiler_params=_params("arbitrary"),
    )(tabs["dst"], tabs["fill_start"], tabs["fill_n"], n_used, h2, lst)


def _expert_kernel(be_ref, nu_ref, x_ref, w1_ref, b1_ref, w2_ref, b2_ref, y_ref, w1b, w2b):
    bi = pl.program_id(0)
    used = bi < nu_ref[0]
    fresh = (bi == 0) | (be_ref[bi] != be_ref[jnp.maximum(bi - 1, 0)])

    @pl.when(used & fresh)
    def _():
        w1b[...] = w1_ref[0, 0].astype(BF16)
        w2b[...] = w2_ref[0, 0].astype(BF16)

    @pl.when(jnp.logical_not(used))
    def _():
        y_ref[...] = jnp.zeros_like(y_ref)

    @pl.when(used)
    def _():
        xb = x_ref[...].astype(BF16)
        y = None
        for c in range(0, D_FF, FF_CHUNK):
            glu = _dot(xb, w1b[:, c:c + FF_CHUNK]) + b1_ref[0, 0, :, c:c + FF_CHUNK]
            lin = _dot(xb, w1b[:, D_FF + c:D_FF + c + FF_CHUNK]) + b1_ref[0, 0, :, D_FF + c:D_FF + c + FF_CHUNK]
            glu = jnp.minimum(glu, SWIGLU_LIMIT)
            lin = jnp.clip(lin, -SWIGLU_LIMIT, SWIGLU_LIMIT)
            act = glu * _sigmoid(SWIGLU_ALPHA * glu) * (lin + 1.0)
            part = _dot(act.astype(BF16), w2b[c:c + FF_CHUNK, :])
            y = part if y is None else y + part
        y_ref[...] = y + b2_ref[0, 0]


def _experts(xs, block_e, n_used, w1, b1, w2, b2, layer):
    rows, d = xs.shape
    depth, ne, _, f2 = w1.shape
    row_blk = lambda bi, be, nu: (bi, 0)
    grid_spec = pltpu.PrefetchScalarGridSpec(
        num_scalar_prefetch=2,
        grid=(rows // EB,),
        in_specs=[
            pl.BlockSpec((EB, d), row_blk),
            pl.BlockSpec((1, 1, d, f2), lambda bi, be, nu: (layer, be[bi], 0, 0)),
            pl.BlockSpec((1, 1, 1, f2), lambda bi, be, nu: (layer, be[bi], 0, 0)),
            pl.BlockSpec((1, 1, f2 // 2, d), lambda bi, be, nu: (layer, be[bi], 0, 0)),
            pl.BlockSpec((1, 1, 1, d), lambda bi, be, nu: (layer, be[bi], 0, 0)),
        ],
        out_specs=pl.BlockSpec((EB, d), row_blk),
        scratch_shapes=[pltpu.VMEM((d, f2), BF16), pltpu.VMEM((f2 // 2, d), BF16)],
    )
    return pl.pallas_call(
        _expert_kernel,
        grid_spec=grid_spec,
        out_shape=jax.ShapeDtypeStruct((rows, d), F32),
        compiler_params=_params("arbitrary"),
    )(block_e, n_used, xs, w1, b1.reshape(depth, ne, 1, f2), w2, b2.reshape(depth, ne, 1, d))


def _combine_kernel(dst_ref, x_ref, ls_ref, rw_ref, mod_ref, fg_ref, ys_hbm, xo_ref, buf, sem, *, final):
    i = pl.program_id(0)
    last = pl.num_programs(0) - 1
    cur = i % 2

    def fetch(tile, half):
        _tile_copies(dst_ref, tile, lambda j, t: pltpu.make_async_copy(
            ys_hbm.at[_rows(t)], buf.at[half, _rows(j)], sem.at[half]).start())

    @pl.when(i == 0)
    def _():
        fetch(0, 0)

    @pl.when(i < last)
    def _():
        fetch(i + 1, 1 - cur)

    _tile_waits(lambda j: pltpu.make_async_copy(
        ys_hbm.at[_rows(j)], buf.at[cur, _rows(j)], sem.at[cur]).wait())
    yb = buf[cur].astype(BF16)
    slot = lax.broadcasted_iota(I32, (TM, SORT_ROWS), 1)
    ls = ls_ref[...]
    rw = rw_ref[...]
    g = jnp.zeros((TM, SORT_ROWS), F32)
    for k in range(TOP_K):
        g = jnp.where(slot == ls[:, k:k + 1], rw[:, k:k + 1], g)
    xn = x_ref[0] + mod_ref[0, 5:6, :] * _dot(g.astype(BF16), yb)
    if final:
        ms = jnp.mean(xn * xn, axis=-1, keepdims=True)
        xn = xn * lax.rsqrt(ms + NORM_EPS) * fg_ref[...]
    xo_ref[0] = xn


def _combine(xn, ls, rw, mod_l, final_gain, ys, tabs, final):
    b, s, d = xn.shape
    nt = s // TM
    tok = pl.BlockSpec((1, TM, d), lambda i, *_: (i // nt, i % nt, 0))
    lanes = pl.BlockSpec((TM, LANE), lambda i, *_: (i, 0))
    if final:
        out_spec = pl.BlockSpec((1, TM, d), lambda i, *_: (i // nt, jnp.maximum(i % nt - 1, 0), 0))
        out_shape = jax.ShapeDtypeStruct((b, s - TM, d), F32)
    else:
        out_spec, out_shape = tok, jax.ShapeDtypeStruct((b, s, d), F32)
    grid_spec = pltpu.PrefetchScalarGridSpec(
        num_scalar_prefetch=1,
        grid=(b * nt,),
        in_specs=[tok, lanes, lanes,
                  pl.BlockSpec((1, 6, d), lambda i, *_: (jnp.where(i % nt == 0, b, i // nt), 0, 0)),
                  pl.BlockSpec((1, d), lambda i, *_: (0, 0)),
                  pl.BlockSpec(memory_space=pl.ANY)],
        out_specs=out_spec,
        scratch_shapes=[pltpu.VMEM((2, SORT_ROWS, d), F32), pltpu.SemaphoreType.DMA((2,))],
    )
    return pl.pallas_call(
        functools.partial(_combine_kernel, final=final),
        grid_spec=grid_spec,
        out_shape=out_shape,
        compiler_params=_params("arbitrary"),
    )(tabs["dst"], xn, ls, rw, mod_l, final_gain.reshape(1, d), ys)


def _inproj_weight(w_in_l):
    d = w_in_l.shape[0]
    sizes = (ATT_Q, N_KV_HEADS * HEAD_DIM, N_KV_HEADS * HEAD_DIM, POOL_WIDTH,
             RET_W, RET_W, RET_W, RET_W, RET_W, d, d, d)
    parts, off = [], 0
    for sz in sizes:
        parts.append(w_in_l[:, off:off + sz])
        off += sz
    q, k, v, u, rq, rk, rv, rgf, rgb, ga, gp, gr = parts
    twice = lambda w: jnp.concatenate(
        [w[:, h * HEAD_DIM:(h + 1) * HEAD_DIM] for h in range(N_KV_HEADS) for _ in range(2)], axis=1)
    cols = [q * HEAD_DIM ** -0.5, twice(k), twice(v), rq, rk * RET_DIM ** -0.5, rv, rgf, rgb, u, ga, gp, gr]
    return jnp.concatenate(cols, axis=1).astype(BF16)


def _block_diag(blocks):
    n, r, c = blocks.shape
    out = jnp.zeros((n * r, n * c), blocks.dtype)
    for g in range(n):
        out = out.at[g * r:(g + 1) * r, g * c:(g + 1) * c].set(blocks[g])
    return out


def _routing_tables(cnt, n_blocks):
    ntt = cnt.shape[0] // SUBLANE
    counts = cnt.reshape(ntt, SUBLANE, LANE)[:, 0, :N_EXPERTS]
    run = (counts + SUBLANE - 1) // SUBLANE
    total = jnp.sum(run, axis=0)
    eb = EB // SUBLANE
    padded = (total + eb - 1) // eb * eb
    pad_end = jnp.cumsum(padded)
    pad_start = pad_end - padded
    off = pad_start[None, :] + jnp.cumsum(run, axis=0) - run
    run_end = jnp.cumsum(run, axis=1)
    j = jnp.arange(SORT_TILES)
    owner = jnp.sum(run_end[:, None, :] <= j[None, :, None], axis=2)
    mine = owner[:, :, None] == jnp.arange(N_EXPERTS)[None, None, :]
    in_region = j[None, :] + jnp.sum(jnp.where(mine, (off - (run_end - run))[:, None, :], 0), axis=2)
    spare = (n_blocks - SPARE_BLOCKS) * eb + (jnp.arange(ntt) % 2)[:, None] * SORT_TILES + j[None, :]
    dst = jnp.where(owner < N_EXPERTS, in_region, spare)
    n_used = pad_end[-1] // eb
    blk = jnp.minimum(jnp.arange(n_blocks), n_used - 1) * eb
    block_e = jnp.minimum(jnp.sum(pad_end[None, :] <= blk[:, None], axis=1), N_EXPERTS - 1)
    tabs = dict(dst=dst.reshape(-1).astype(I32), fill_start=(pad_start + total).astype(I32),
                fill_n=(padded - total).astype(I32))
    return tabs, block_e.astype(I32), n_used.reshape(1).astype(I32)


def kernel(x, c, ctx, c_ctx, w_mod, b_mod, norm1, norm2, w_in, attn_sinks, pool_w, pool_scale, ret_decay,
           w_o_attn, w_o_pool, w_o_ret, w_out, w_router, b_router, w_expert_in, b_expert_in, w_expert_out,
           b_expert_out, final_norm):
    b, seq, d = x.shape
    depth = w_mod.shape[0]
    assert ctx.shape[1] == TM and seq % TM == 0 and seq % GRID_W == 0
    s = seq + TM
    xm = jnp.concatenate([ctx, x], axis=1)
    mod = _modulation(c, c_ctx, w_mod, b_mod)
    tables = _rope_tables(seq)
    ntt = b * s // TM
    n_blocks = -(-(b * s * TOP_K + ntt * N_EXPERTS * (SUBLANE - 1)) // EB) + N_EXPERTS + SPARE_BLOCKS
    avg = _block_diag(jnp.full((RET_HEADS, RET_DIM, RET_DIM), 1.0 / RET_DIM, F32)).astype(BF16)
    tri = (jnp.arange(TM)[:, None] >= jnp.arange(TM)[None, :]).astype(BF16)
    for l in range(depth):
        q, k2, v2, rq, rk, rv, rg, u, gates = _inproj(xm, mod[l], norm1[l], _inproj_weight(w_in[l]), tables)
        attn = _attention(q, k2, v2, attn_sinks[l])
        yf, yb = _retention(rq, rk, rv, jax.nn.log_sigmoid(ret_decay[l].astype(F32)))
        wr = jnp.zeros((d, LANE), F32).at[:, :N_EXPERTS].set(w_router[l])
        wr_hilo = jnp.concatenate(_split(wr), axis=1)
        br = jnp.full((1, LANE), NEG, F32).at[0, :N_EXPERTS].set(b_router[l])
        wts = (_block_diag(pool_w[l]).astype(BF16), pool_scale[l].reshape(1, -1),
               w_o_attn[l].astype(BF16), w_o_pool[l].astype(BF16), w_o_ret[l].astype(BF16),
               w_out[l].astype(BF16), norm2[l].reshape(1, d), wr_hilo, br, avg, tri)
        xn, h2, ls, lst, rw, cnt = _merge(xm, attn, u, yf, yb, rg, gates, mod[l], seq, wts)
        tabs, block_e, n_used = _routing_tables(cnt, n_blocks)
        xs = _dispatch(h2.reshape(b * s, d), lst, tabs, n_used, n_blocks)
        ys = _experts(xs, block_e, n_used, w_expert_in, b_expert_in, w_expert_out, b_expert_out, l)
        xm = _combine(xn, ls, rw, mod[l], final_norm, ys, tabs, final=l == depth - 1)
    return xm
```

```python
import functools

import jax
import jax.numpy as jnp
from jax import lax
from jax.experimental import pallas as pl
from jax.experimental.pallas import tpu as pltpu

F32 = jnp.float32
BF16 = jnp.bfloat16
I32 = jnp.int32

GRID_W = 64
HEAD_DIM = 64
N_Q_HEADS = 8
N_KV_HEADS = 2
WINDOW = 128
ROPE_THETA = 10000.0
POOL_WIDTH = 256
POOL_WINDOWS = (2, 4, 8, 16)
RET_HEADS = 4
RET_DIM = 64
N_EXPERTS = 32
TOP_K = 4
D_FF = 1024
SWIGLU_LIMIT = 7.0
SWIGLU_ALPHA = 1.702
NORM_EPS = 1e-6
GN_EPS = 1e-5

LANE = 128
SUBLANE = 8
MXU_N = 256
TM = 256
AB = 128
IN_TILE = 640
CTX_BLOCKS = TM // AB
EB = 512
FF_CHUNK = 256
POOL_HALO = SUBLANE
SORT_ROWS = TM * TOP_K + N_EXPERTS * SUBLANE
SORT_TILES = SORT_ROWS // SUBLANE
MIN_TILES = TM * TOP_K // SUBLANE
NEG = -1e30
VMEM_LIMIT = 56 * 1024 * 1024

ATT_Q = N_Q_HEADS * HEAD_DIM
KV2 = 2 * N_KV_HEADS * HEAD_DIM
RET_W = RET_HEADS * RET_DIM


def _dot(a, b):
    return jnp.dot(a, b, preferred_element_type=F32)


def _dot_nt(a, b):
    return lax.dot_general(a, b, (((1,), (1,)), ((), ())), preferred_element_type=F32)


def _dot_tn(a, b):
    return lax.dot_general(a, b, (((0,), (0,)), ((), ())), preferred_element_type=F32)


def _split(a):
    hi = a.astype(BF16)
    lo = (a - hi.astype(F32)).astype(BF16)
    return hi, lo


def _dot_hilo(a, m):
    hi, lo = _split(a)
    return _dot(hi, m) + _dot(lo, m)


def _sigmoid(x):
    return 0.5 * jnp.tanh(0.5 * x) + 0.5


def _silu(x):
    return x * _sigmoid(x)


def _params(*sem):
    return pltpu.CompilerParams(dimension_semantics=sem, vmem_limit_bytes=VMEM_LIMIT)


def _mod_kernel(c_ref, w_ref, b_ref, o_ref):
    s = _silu(c_ref[...])
    sh, sl = _split(s)
    wh, wl = _split(w_ref[0])
    o_ref[0] = _dot(sh, wh) + _dot(sh, wl) + _dot(sl, wh) + b_ref[0]


def _modulation(c, c_ctx, w_mod, b_mod):
    depth, d, six_d = w_mod.shape
    b = c.shape[0]
    cc = jnp.zeros((SUBLANE, d), F32).at[:b].set(c).at[b].set(c_ctx)
    out = pl.pallas_call(
        _mod_kernel,
        grid=(depth, six_d // d),
        in_specs=[
            pl.BlockSpec((SUBLANE, d), lambda l, n: (0, 0)),
            pl.BlockSpec((1, d, d), lambda l, n: (l, 0, n)),
            pl.BlockSpec((1, 1, d), lambda l, n: (l, 0, n)),
        ],
        out_specs=pl.BlockSpec((1, SUBLANE, d), lambda l, n: (l, 0, n)),
        out_shape=jax.ShapeDtypeStruct((depth, SUBLANE, six_d), F32),
        compiler_params=_params("arbitrary", "arbitrary"),
    )(cc, w_mod, b_mod.reshape(depth, 1, six_d))
    return out[:, : b + 1].reshape(depth, b + 1, 6, d)


def _rope(x, cos, sin, half, first):
    partner = jnp.where(first, pltpu.roll(x, LANE - half, 1), pltpu.roll(x, half, 1))
    return x * cos + partner * sin


def _inproj_kernel(x_ref, mod_ref, modc_ref, n1_ref, w_ref, ac_ref, as_ref, rc_ref, rs_ref,
                   q_ref, k_ref, v_ref, rq_ref, rk_ref, rv_ref, rg_ref, u_ref, g_ref):
    x = x_ref[0]
    rows = x.shape[0]
    ms = jnp.mean(x * x, axis=-1, keepdims=True)
    y = x * lax.rsqrt(ms + NORM_EPS) * n1_ref[...]
    is_ctx = (lax.broadcasted_iota(I32, (rows, 1), 0) < TM) & (pl.program_id(1) == 0)
    scale = jnp.where(is_ctx, modc_ref[0, 1:2, :], mod_ref[0, 1:2, :])
    shift = jnp.where(is_ctx, modc_ref[0, 0:1, :], mod_ref[0, 0:1, :])
    hb = (y * (1.0 + scale) + shift).astype(BF16)
    lane = lax.broadcasted_iota(I32, (rows, LANE), 1)
    a_first = (lane % (HEAD_DIM // 2)) < (HEAD_DIM // 4)
    r_first = (lane % RET_DIM) < (RET_DIM // 2)
    ac, asn, rc, rsn = ac_ref[...], as_ref[...], rc_ref[...], rs_ref[...]

    def proj(off, width):
        return _dot(hb, w_ref[:, off:off + width])

    def rotated(ref, width, cos, sin, half, first, off):
        for t in range(0, width, MXU_N):
            pr = proj(off + t, MXU_N)
            for g in range(0, MXU_N, LANE):
                ref[0, :, t + g:t + g + LANE] = _rope(pr[:, g:g + LANE], cos, sin, half, first).astype(BF16)

    off = 0
    rotated(q_ref, ATT_Q, ac, asn, HEAD_DIM // 4, a_first, off)
    off += ATT_Q
    rotated(k_ref, KV2, ac, asn, HEAD_DIM // 4, a_first, off)
    off += KV2
    v_ref[0] = proj(off, KV2).astype(BF16)
    off += KV2
    for ref in (rq_ref, rk_ref):
        rotated(ref, RET_W, rc, rsn, RET_DIM // 2, r_first, off)
        off += RET_W
    rv_ref[0] = proj(off, RET_W).astype(BF16)
    off += RET_W
    rg_ref[0] = proj(off, 2 * RET_W).astype(BF16)
    off += 2 * RET_W
    u_ref[0] = proj(off, POOL_WIDTH)
    off += POOL_WIDTH
    d = x.shape[-1]
    for t in range(3):
        g_ref[0, :, t * d:(t + 1) * d] = proj(off, d).astype(BF16)
        off += d


def _inproj(xm, mod_l, norm1_l, w1, tables):
    b, s, d = xm.shape
    ti = IN_TILE if s % IN_TILE == 0 else TM
    wcols = w1.shape[1]
    tok = lambda width: pl.BlockSpec((1, ti, width), lambda bi, j: (bi, j, 0))
    tab = pl.BlockSpec((ti, LANE), lambda bi, j: (j, 0))
    widths = (ATT_Q, KV2, KV2, RET_W, RET_W, RET_W, 2 * RET_W, POOL_WIDTH, 3 * d)
    dtypes = (BF16,) * 7 + (F32, BF16)
    return pl.pallas_call(
        _inproj_kernel,
        grid=(b, s // ti),
        in_specs=[
            tok(d),
            pl.BlockSpec((1, 6, d), lambda bi, j: (bi, 0, 0)),
            pl.BlockSpec((1, 6, d), lambda bi, j: (b, 0, 0)),
            pl.BlockSpec((1, d), lambda bi, j: (0, 0)),
            pl.BlockSpec((d, wcols), lambda bi, j: (0, 0), pipeline_mode=pl.Buffered(1)),
            tab, tab, tab, tab,
        ],
        out_specs=[tok(w) for w in widths],
        out_shape=[jax.ShapeDtypeStruct((b, s, w), dt) for w, dt in zip(widths, dtypes)],
        compiler_params=_params("arbitrary", "arbitrary"),
    )(xm, mod_l, mod_l, norm1_l.reshape(1, d), w1, *tables)


def _rope_tables(seq):
    rows = seq // GRID_W
    rpos = jnp.arange(rows, dtype=F32)[:, None]
    cpos = jnp.arange(GRID_W, dtype=F32)[:, None]
    lane = jnp.arange(LANE)
    grid = lambda per_row, per_col: (per_row[:, None, :] + per_col[None, :, :]).reshape(seq, LANE)
    axis_dim = HEAD_DIM // 2
    inv_a = ROPE_THETA ** (-jnp.arange(0, axis_dim, 2, dtype=F32) / axis_dim)
    hl = lane % HEAD_DIM
    inv_al = inv_a[(hl % axis_dim) % (axis_dim // 2)][None, :]
    by_row = (hl < axis_dim)[None, :]
    a_sign = jnp.where((hl % axis_dim) < axis_dim // 2, -1.0, 1.0)[None, :]
    a_cos = grid(jnp.where(by_row, jnp.cos(rpos * inv_al), 0.0), jnp.where(by_row, 0.0, jnp.cos(cpos * inv_al)))
    a_sin = grid(jnp.where(by_row, jnp.sin(rpos * inv_al), 0.0), jnp.where(by_row, 0.0, jnp.sin(cpos * inv_al)))
    inv_r = 1.0 / (ROPE_THETA ** jnp.linspace(0.0, 1.0, RET_DIM // 2, dtype=F32))
    rl = lane % RET_DIM
    inv_rl = inv_r[rl % (RET_DIM // 2)][None, :]
    r_sign = jnp.where(rl < RET_DIM // 2, -1.0, 1.0)[None, :]
    hi = (rpos * GRID_W) * inv_rl
    lo = cpos * inv_rl
    outer = lambda a, b: (a[:, None, :] * b[None, :, :]).reshape(seq, LANE)
    r_cos = outer(jnp.cos(hi), jnp.cos(lo)) - outer(jnp.sin(hi), jnp.sin(lo))
    r_sin = outer(jnp.sin(hi), jnp.cos(lo)) + outer(jnp.cos(hi), jnp.sin(lo))
    ones = jnp.ones((TM, LANE), F32)
    zeros = jnp.zeros((TM, LANE), F32)
    cat = lambda head, body: jnp.concatenate([head, body], axis=0)
    return (cat(ones, a_cos), cat(zeros, a_sin * a_sign), cat(ones, r_cos), cat(zeros, r_sin * r_sign))


def _attn_kernel(sink_ref, bias_ref, q_ref, kp_ref, kc_ref, kn_ref, kx_ref, vp_ref, vc_ref, vn_ref, vx_ref, o_ref):
    g_heads = N_Q_HEADS // N_KV_HEADS
    lane = lax.broadcasted_iota(I32, (AB, LANE), 1)
    lo = lane < HEAD_DIM
    rows = lax.broadcasted_iota(I32, (g_heads * AB, 1), 0)
    bias = bias_ref[0]
    scores, sinks, values = [], [], []
    for g in range(N_KV_HEADS):
        ks = slice(g * LANE, (g + 1) * LANE)
        kd = jnp.concatenate([kp_ref[0, :, ks], kc_ref[0, :, ks], kn_ref[0, :, ks], kx_ref[0, :, ks]], axis=0)
        values.append(jnp.concatenate([vp_ref[0, :, ks], vc_ref[0, :, ks], vn_ref[0, :, ks], vx_ref[0, :, ks]], axis=0))
        qs = []
        for c in range(2 * g, 2 * g + 2):
            qc = q_ref[0, :, c * LANE:(c + 1) * LANE]
            zero = jnp.zeros_like(qc)
            qs += [jnp.where(lo, qc, zero), jnp.where(lo, zero, qc)]
        scores.append(_dot_nt(jnp.concatenate(qs, axis=0), kd) + bias)
        sink = jnp.full((g_heads * AB, 1), sink_ref[g_heads * g + g_heads - 1], F32)
        for t in range(g_heads - 2, -1, -1):
            sink = jnp.where(rows < (t + 1) * AB, sink_ref[g_heads * g + t], sink)
        sinks.append(sink)
    outs = []
    for s, sink, vd in zip(scores, sinks, values):
        m = jnp.maximum(jnp.max(s, axis=-1, keepdims=True), sink)
        p = jnp.exp(s - m)
        den = jnp.sum(p, axis=-1, keepdims=True) + jnp.exp(sink - m)
        o = _dot(p.astype(BF16), vd) / den
        for t in range(2):
            outs.append(jnp.where(lo, o[2 * t * AB:(2 * t + 1) * AB], o[(2 * t + 1) * AB:(2 * t + 2) * AB]))
    o_ref[0] = jnp.concatenate(outs, axis=-1).astype(BF16)


def _attn_bias():
    g_heads = N_Q_HEADS // N_KV_HEADS
    r = jnp.arange(g_heads * AB)[:, None] % AB
    j = jnp.arange(3 * AB + TM)[None, :]
    band = jnp.abs(j - AB - r) <= WINDOW
    is_ctx = j >= 3 * AB
    variants = (band, band & (j >= AB), band & (j < 2 * AB), jnp.zeros_like(band))
    return jnp.stack([jnp.where(v | is_ctx, 0.0, NEG) for v in variants]).astype(F32)


def _attention(q, k2, v2, sinks):
    b, s, _ = q.shape
    nb = s // AB
    assert nb - CTX_BLOCKS >= 2
    bias = _attn_bias()
    variant = lambda i: jnp.where(i < CTX_BLOCKS, 3, jnp.where(i == CTX_BLOCKS, 1, jnp.where(i == nb - 1, 2, 0)))
    prev = pl.BlockSpec((1, AB, KV2), lambda bi, i: (bi, jnp.maximum(i - 1, 0), 0))
    cur = pl.BlockSpec((1, AB, KV2), lambda bi, i: (bi, i, 0))
    nxt = pl.BlockSpec((1, AB, KV2), lambda bi, i: (bi, jnp.minimum(i + 1, nb - 1), 0))
    cx = pl.BlockSpec((1, TM, KV2), lambda bi, i: (bi, 0, 0))
    return pl.pallas_call(
        _attn_kernel,
        grid=(b, nb),
        in_specs=[pl.BlockSpec(memory_space=pltpu.SMEM),
                  pl.BlockSpec((1,) + bias.shape[1:], lambda bi, i: (variant(i), 0, 0)),
                  pl.BlockSpec((1, AB, ATT_Q), lambda bi, i: (bi, i, 0)),
                  prev, cur, nxt, cx, prev, cur, nxt, cx],
        out_specs=pl.BlockSpec((1, AB, ATT_Q), lambda bi, i: (bi, i, 0)),
        out_shape=jax.ShapeDtypeStruct((b, s, ATT_Q), BF16),
        compiler_params=_params("arbitrary", "arbitrary"),
    )(sinks, bias, q, k2, k2, k2, k2, v2, v2, v2, v2)


def _ret_kernel(lg_ref, qf_ref, kf_ref, vf_ref, qb_ref, kb_ref, vb_ref, yf_ref, yb_ref,
                st_ref, dm_ref, qd_ref, kd_ref, cd_ref, *, batch):
    step = pl.program_id(0)
    lane = lax.broadcasted_iota(I32, (AB, LANE), 1)
    row = lax.broadcasted_iota(I32, (AB, LANE), 0)
    lo = lane < RET_DIM
    tiles = RET_W // LANE

    @pl.when(step == 0)
    def _():
        st_ref[...] = jnp.zeros_like(st_ref)
        ii = row.astype(F32)
        jj = lane.astype(F32)
        for d in range(2):
            for c in range(tiles):
                lg0 = lg_ref[d * RET_HEADS + 2 * c]
                lg1 = lg_ref[d * RET_HEADS + 2 * c + 1]
                lgl = jnp.where(lo, lg0, lg1)
                q_exp = ii + 1.0 if d == 0 else AB - ii
                k_exp = (AB - 1.0) - ii if d == 0 else ii
                qd_ref[d * tiles + c] = jnp.exp(q_exp * lgl)
                kd_ref[d * tiles + c] = jnp.exp(k_exp * lgl)
                cd_ref[d * tiles + c] = jnp.exp(AB * lgl)
                rel = ii - jj if d == 0 else jj - ii
                for hh, lgh in enumerate((lg0, lg1)):
                    dm_ref[d * tiles + c, hh * AB:(hh + 1) * AB, :] = jnp.where(
                        rel >= 0, jnp.exp(jnp.maximum(rel, 0.0) * lgh), 0.0)

    same_head = (row < RET_DIM) == lo
    dirs = ((qf_ref, kf_ref, vf_ref, yf_ref), (qb_ref, kb_ref, vb_ref, yb_ref))
    chains = [(d, b, c) for d in range(2) for b in range(batch) for c in range(tiles)]

    def operands(d, b, c):
        q_ref, k_ref, v_ref, _ = dirs[d]
        sl = slice(c * LANE, (c + 1) * LANE)
        return q_ref[b, :, sl], k_ref[b, :, sl], v_ref[b, :, sl]

    probs = []
    for d, b, c in chains:
        q, k, _ = operands(d, b, c)
        zero = jnp.zeros_like(q)
        q2 = jnp.concatenate([jnp.where(lo, q, zero), jnp.where(lo, zero, q)], axis=0)
        probs.append((_dot_nt(q2, k) * dm_ref[d * tiles + c]).astype(BF16))
    for (d, b, c), p in zip(chains, probs):
        q, _, v = operands(d, b, c)
        t = d * tiles + c
        si = (d * batch + b) * tiles + c
        y_intra = jnp.where(lo, _dot(p[:AB], v), _dot(p[AB:], v))
        q_dec = (q.astype(F32) * qd_ref[t]).astype(BF16)
        dirs[d][3][b, :, c * LANE:(c + 1) * LANE] = y_intra + _dot(q_dec, st_ref[si].astype(BF16))
    for d, b, c in chains:
        _, k, v = operands(d, b, c)
        t = d * tiles + c
        si = (d * batch + b) * tiles + c
        k_dec = (k.astype(F32) * kd_ref[t]).astype(BF16)
        st_ref[si] = st_ref[si] * cd_ref[t] + jnp.where(same_head, _dot_tn(k_dec, v), 0.0)


def _retention(rq, rk, rv, log_g):
    b, s, w = rq.shape
    nb = s // AB
    tiles = w // LANE

    def back(i):
        return jnp.where(i < CTX_BLOCKS, CTX_BLOCKS - 1 - i, nb - 1 + CTX_BLOCKS - i)

    fwd = pl.BlockSpec((b, AB, w), lambda i: (0, i, 0))
    bwd = pl.BlockSpec((b, AB, w), lambda i: (0, back(i), 0))
    return pl.pallas_call(
        functools.partial(_ret_kernel, batch=b),
        grid=(nb,),
        in_specs=[pl.BlockSpec(memory_space=pltpu.SMEM), fwd, fwd, fwd, bwd, bwd, bwd],
        out_specs=[fwd, bwd],
        out_shape=[jax.ShapeDtypeStruct((b, s, w), F32)] * 2,
        scratch_shapes=[
            pltpu.VMEM((2 * b * tiles, LANE, LANE), F32),
            pltpu.VMEM((2 * tiles, 2 * AB, LANE), F32),
            pltpu.VMEM((2 * tiles, AB, LANE), F32),
            pltpu.VMEM((2 * tiles, AB, LANE), F32),
            pltpu.VMEM((2 * tiles, AB, LANE), F32),
        ],
        compiler_params=_params("arbitrary"),
    )(log_g.reshape(-1), rq, rk, rv, rq, rk, rv)


def _merge_kernel(x_ref, attn_ref, up_ref, uc_ref, un_ref, yf_ref, yb_ref, rg_ref, g_ref, mod_ref,
                  wbd_ref, ps_ref, woa_ref, wop_ref, wor_ref, wout_ref, n2_ref, wr_ref,
                  br_ref, avg_ref, tri_ref,
                  xo_ref, h2_ref, ls_ref, lst_ref, rw_ref, cnt_ref, *, seq, nt):
    i = pl.program_id(0)
    j = i % nt

    ext = TM + 2 * POOL_HALO
    seq_len = jnp.where(j == 0, TM, seq)
    start = jnp.where(j == 0, 0, (j - 1) * TM)
    u = uc_ref[0]
    ue = jnp.concatenate([up_ref[0], u, un_ref[0]], axis=0)
    erow = lax.broadcasted_iota(I32, (ext, 1), 0) + (start - POOL_HALO)
    ue = jnp.where((erow >= 0) & (erow < seq_len), ue, 0.0)
    t_pos = lax.broadcasted_iota(I32, (TM, 1), 0) + start
    glane = lax.broadcasted_iota(I32, (TM, POOL_WIDTH), 1) // (POOL_WIDTH // len(POOL_WINDOWS))
    run = ue
    width = 1
    diff = jnp.zeros((TM, POOL_WIDTH), F32)
    for gi, w in enumerate(POOL_WINDOWS):
        while width < w:
            run = run + pltpu.roll(run, ext - width, 0)
            width *= 2
        win = pltpu.roll(run, w // 2, 0)[POOL_HALO:POOL_HALO + TM]
        cnt = jnp.minimum(t_pos - w // 2 + w, seq_len) - jnp.maximum(t_pos - w // 2, 0)
        diff = jnp.where(glane == gi, win / cnt.astype(F32) - u, diff)
    pool = _dot(diff.astype(BF16), wbd_ref[...]) * ps_ref[...]

    def head_norm(y):
        mu = _dot_hilo(y, avg_ref[...])
        dlt = y - mu
        var = _dot_hilo(dlt * dlt, avg_ref[...])
        return dlt * lax.rsqrt(var + GN_EPS)

    rg = rg_ref[0].astype(F32)
    ret = head_norm(yf_ref[0]) * _silu(rg[:, :RET_W]) + head_norm(yb_ref[0]) * _silu(rg[:, RET_W:])

    d = x_ref.shape[-1]
    gate = lambda t: _sigmoid(g_ref[0, :, t * d:(t + 1) * d])
    m = (gate(0) * _dot(attn_ref[0], woa_ref[...]).astype(BF16)
         + gate(1) * _dot(pool.astype(BF16), wop_ref[...]).astype(BF16)
         + gate(2) * _dot(ret.astype(BF16), wor_ref[...]).astype(BF16))
    xn = x_ref[0] + mod_ref[0, 2:3, :] * _dot(m, wout_ref[...])
    xo_ref[0] = xn

    ms = jnp.mean(xn * xn, axis=-1, keepdims=True)
    h2 = xn * lax.rsqrt(ms + NORM_EPS) * n2_ref[...] * (1.0 + mod_ref[0, 4:5, :]) + mod_ref[0, 3:4, :]
    h2_ref[0] = h2.astype(BF16)
    hh, hl = _split(h2)
    full = _dot(hh, wr_ref[...])
    logits = full[:, :LANE] + full[:, LANE:] + _dot(hl, wr_ref[:, :LANE]) + br_ref[...]
    lanef = lax.broadcasted_iota(I32, (TM, LANE), 1).astype(F32)
    tops, hots = [], []
    for _ in range(TOP_K):
        mx = jnp.max(logits, axis=-1, keepdims=True)
        idx = jnp.min(jnp.where(logits == mx, lanef, float(LANE)), axis=-1, keepdims=True)
        hot = lanef == idx
        logits = jnp.where(hot, NEG * 2.0, logits)
        tops.append((mx, idx))
        hots.append(hot)
    ex = [jnp.exp(mx - tops[0][0]) for mx, _ in tops]
    tot = ex[0]
    for e in ex[1:]:
        tot = tot + e
    sel = jnp.zeros((TM, LANE), F32)
    for hot in hots:
        sel = jnp.where(hot, 1.0, sel)
    incl = _dot(tri_ref[...], sel.astype(BF16))
    cnt = incl[TM - 1:TM, :]
    run = jnp.floor((cnt + (SUBLANE - 1.0)) * (1.0 / SUBLANE)) * SUBLANE
    er = lax.broadcasted_iota(I32, (LANE, LANE), 0)
    ec = lax.broadcasted_iota(I32, (LANE, LANE), 1)
    earlier = jnp.where(er < ec, 1.0, 0.0).astype(BF16)
    run_start = _dot(jnp.broadcast_to(run, (SUBLANE, LANE)).astype(BF16), earlier)[0:1]
    slot = run_start + incl - 1.0
    ls = jnp.zeros((TM, LANE), F32)
    rw = jnp.zeros((TM, LANE), F32)
    for k in range(TOP_K):
        ls = jnp.where(lanef == k, jnp.sum(jnp.where(hots[k], slot, 0.0), axis=-1, keepdims=True), ls)
        rw = jnp.where(lanef == k, ex[k] / tot, rw)
    ls = jnp.where(lanef < TOP_K, ls, -1.0)
    ls_ref[...] = ls.astype(I32)
    lst_ref[...] = jnp.transpose(ls)[:SUBLANE, :].astype(I32)
    rw_ref[...] = rw
    cnt_ref[...] = jnp.broadcast_to(cnt, cnt_ref.shape).astype(I32)


def _merge(xm, attn, u, yf, yb, rg, gates, mod_l, seq, wts):
    b, s, d = xm.shape
    nt = s // TM
    ntt = b * nt
    hb = TM // POOL_HALO
    nh = s // POOL_HALO
    tok = lambda width: pl.BlockSpec((1, TM, width), lambda i: (i // nt, i % nt, 0))
    full = lambda a: pl.BlockSpec(a.shape, lambda i: (0,) * a.ndim)
    lanes = pl.BlockSpec((TM, LANE), lambda i: (i, 0))
    in_specs = [
        tok(d), tok(ATT_Q),
        pl.BlockSpec((1, POOL_HALO, POOL_WIDTH), lambda i: (i // nt, jnp.maximum((i % nt) * hb - 1, 0), 0)),
        tok(POOL_WIDTH),
        pl.BlockSpec((1, POOL_HALO, POOL_WIDTH), lambda i: (i // nt, jnp.minimum((i % nt + 1) * hb, nh - 1), 0)),
        tok(RET_W), tok(RET_W), tok(2 * RET_W), tok(3 * d),
        pl.BlockSpec((1, 6, d), lambda i: (jnp.where(i % nt == 0, b, i // nt), 0, 0)),
    ] + [full(a) for a in wts]
    return pl.pallas_call(
        functools.partial(_merge_kernel, seq=seq, nt=nt),
        grid=(ntt,),
        in_specs=in_specs,
        out_specs=[tok(d), tok(d), lanes, pl.BlockSpec((SUBLANE, TM), lambda i: (i, 0)), lanes,
                   pl.BlockSpec((SUBLANE, LANE), lambda i: (i, 0))],
        out_shape=[jax.ShapeDtypeStruct((b, s, d), F32), jax.ShapeDtypeStruct((b, s, d), BF16),
                   jax.ShapeDtypeStruct((b * s, LANE), I32), jax.ShapeDtypeStruct((ntt * SUBLANE, TM), I32),
                   jax.ShapeDtypeStruct((b * s, LANE), F32), jax.ShapeDtypeStruct((ntt * SUBLANE, LANE), I32)],
        compiler_params=_params("arbitrary"),
    )(xm, attn, u, u, u, yf, yb, rg, gates, mod_l, *wts)


def _rows(tile_index):
    return pl.ds(pl.multiple_of(tile_index * SUBLANE, SUBLANE), SUBLANE)


def _tile_loop(count_ref, tile, fn):
    def body(j, carry):
        fn(j)
        return carry

    lax.fori_loop(0, MIN_TILES, body, 0, unroll=8)
    lax.fori_loop(MIN_TILES, count_ref[tile], body, 0)


def _tile_copies(table_ref, count_ref, tile, copy):
    _tile_loop(count_ref, tile, lambda j: copy(j, table_ref[tile * SORT_TILES + j]))


def _dispatch_kernel(dst_ref, cnt_ref, fs_ref, fn_ref, nu_ref, h_ref, lst_ref, xs_hbm, buf, zbuf, sem, fill_sem,
                     *, n_blocks):
    i = pl.program_id(0)
    last = pl.num_programs(0) - 1
    cur = i % 2

    def send(tile, half):
        _tile_copies(dst_ref, cnt_ref, tile, lambda j, t: pltpu.make_async_copy(
            buf.at[half, _rows(j)], xs_hbm.at[_rows(t)], sem.at[half]).start())

    def drain(tile, half):
        _tile_loop(cnt_ref, tile, lambda j: pltpu.make_async_copy(
            buf.at[half, _rows(j)], xs_hbm.at[_rows(j)], sem.at[half]).wait())

    @pl.when(i == 0)
    def _():
        zbuf[...] = jnp.zeros_like(zbuf)

        def pad_copy(e, c):
            return pltpu.make_async_copy(zbuf.at[pl.ds(0, SUBLANE)], xs_hbm.at[_rows(fs_ref[e] + c)], fill_sem)

        def blk_copy(blk):
            return pltpu.make_async_copy(zbuf, xs_hbm.at[pl.ds(pl.multiple_of(blk * EB, EB), EB)], fill_sem)

        def per_expert(fn):
            def outer(e, carry):
                def inner(c, cc):
                    fn(e, c)
                    return cc
                return lax.fori_loop(0, fn_ref[e], inner, carry)
            lax.fori_loop(0, N_EXPERTS, outer, 0)

        def per_block(fn):
            def body(blk, carry):
                fn(blk)
                return carry
            lax.fori_loop(nu_ref[0], n_blocks, body, 0)

        per_expert(lambda e, c: pad_copy(e, c).start())
        per_block(lambda blk: blk_copy(blk).start())
        per_expert(lambda e, c: pad_copy(e, c).wait())
        per_block(lambda blk: blk_copy(blk).wait())

    slot = lax.broadcasted_iota(I32, (SORT_ROWS, TM), 0)
    lst = lst_ref[...]
    p = jnp.zeros((SORT_ROWS, TM), F32)
    for k in range(TOP_K):
        p = jnp.where(slot == lst[k:k + 1, :], 1.0, p)
    srt = _dot(p.astype(BF16), h_ref[...])

    @pl.when(i >= 2)
    def _():
        drain(i - 2, cur)

    buf[cur] = srt
    send(i, cur)

    @pl.when(i == last)
    def _():
        @pl.when(i >= 1)
        def _():
            drain(i - 1, 1 - cur)
        drain(i, cur)


def _dispatch(h2, lst, tabs, n_used, n_blocks):
    t, d = h2.shape
    ntt = t // TM
    grid_spec = pltpu.PrefetchScalarGridSpec(
        num_scalar_prefetch=5,
        grid=(ntt,),
        in_specs=[pl.BlockSpec((TM, d), lambda i, *_: (i, 0)),
                  pl.BlockSpec((SUBLANE, TM), lambda i, *_: (i, 0))],
        out_specs=pl.BlockSpec(memory_space=pl.ANY),
        scratch_shapes=[pltpu.VMEM((2, SORT_ROWS, d), F32), pltpu.VMEM((EB, d), F32),
                        pltpu.SemaphoreType.DMA((2,)), pltpu.SemaphoreType.DMA],
    )
    return pl.pallas_call(
        functools.partial(_dispatch_kernel, n_blocks=n_blocks),
        grid_spec=grid_spec,
        out_shape=jax.ShapeDtypeStruct((n_blocks * EB, d), F32),
        compiler_params=_params("arbitrary"),
    )(tabs["dst"], tabs["count"], tabs["fill_start"], tabs["fill_n"], n_used, h2, lst)


def _expert_kernel(be_ref, nu_ref, x_ref, w1_ref, b1_ref, w2_ref, b2_ref, y_ref, w1b, w2b):
    bi = pl.program_id(0)
    used = bi < nu_ref[0]
    fresh = (bi == 0) | (be_ref[bi] != be_ref[jnp.maximum(bi - 1, 0)])

    @pl.when(used & fresh)
    def _():
        w1b[...] = w1_ref[0, 0].astype(BF16)
        w2b[...] = w2_ref[0, 0].astype(BF16)

    @pl.when(jnp.logical_not(used))
    def _():
        y_ref[...] = jnp.zeros_like(y_ref)

    @pl.when(used)
    def _():
        xb = x_ref[...].astype(BF16)
        y = None
        for c in range(0, D_FF, FF_CHUNK):
            glu = _dot(xb, w1b[:, c:c + FF_CHUNK]) + b1_ref[0, 0, :, c:c + FF_CHUNK]
            lin = _dot(xb, w1b[:, D_FF + c:D_FF + c + FF_CHUNK]) + b1_ref[0, 0, :, D_FF + c:D_FF + c + FF_CHUNK]
            glu = jnp.minimum(glu, SWIGLU_LIMIT)
            lin = jnp.clip(lin, -SWIGLU_LIMIT, SWIGLU_LIMIT)
            act = glu * _sigmoid(SWIGLU_ALPHA * glu) * (lin + 1.0)
            part = _dot(act.astype(BF16), w2b[c:c + FF_CHUNK, :])
            y = part if y is None else y + part
        y_ref[...] = y + b2_ref[0, 0]


def _experts(xs, block_e, n_used, w1, b1, w2, b2, layer):
    rows, d = xs.shape
    depth, ne, _, f2 = w1.shape
    row_blk = lambda bi, be, nu: (bi, 0)
    used_blk = lambda bi, be, nu: (jnp.minimum(bi, nu[0] - 1), 0)
    grid_spec = pltpu.PrefetchScalarGridSpec(
        num_scalar_prefetch=2,
        grid=(rows // EB,),
        in_specs=[
            pl.BlockSpec((EB, d), used_blk),
            pl.BlockSpec((1, 1, d, f2), lambda bi, be, nu: (layer, be[bi], 0, 0)),
            pl.BlockSpec((1, 1, 1, f2), lambda bi, be, nu: (layer, be[bi], 0, 0)),
            pl.BlockSpec((1, 1, f2 // 2, d), lambda bi, be, nu: (layer, be[bi], 0, 0)),
            pl.BlockSpec((1, 1, 1, d), lambda bi, be, nu: (layer, be[bi], 0, 0)),
        ],
        out_specs=pl.BlockSpec((EB, d), row_blk),
        scratch_shapes=[pltpu.VMEM((d, f2), BF16), pltpu.VMEM((f2 // 2, d), BF16)],
    )
    return pl.pallas_call(
        _expert_kernel,
        grid_spec=grid_spec,
        out_shape=jax.ShapeDtypeStruct((rows, d), F32),
        compiler_params=_params("arbitrary"),
    )(block_e, n_used, xs, w1, b1.reshape(depth, ne, 1, f2), w2, b2.reshape(depth, ne, 1, d))


def _combine_kernel(dst_ref, cnt_ref, x_ref, ls_ref, rw_ref, mod_ref, fg_ref, ys_hbm, xo_ref, buf, sem, *, final):
    i = pl.program_id(0)
    last = pl.num_programs(0) - 1
    cur = i % 2

    def fetch(tile, half):
        _tile_copies(dst_ref, cnt_ref, tile, lambda j, t: pltpu.make_async_copy(
            ys_hbm.at[_rows(t)], buf.at[half, _rows(j)], sem.at[half]).start())

    @pl.when(i == 0)
    def _():
        buf[...] = jnp.zeros_like(buf)
        fetch(0, 0)

    @pl.when(i < last)
    def _():
        fetch(i + 1, 1 - cur)

    _tile_loop(cnt_ref, i, lambda j: pltpu.make_async_copy(
        ys_hbm.at[_rows(j)], buf.at[cur, _rows(j)], sem.at[cur]).wait())
    yb = buf[cur].astype(BF16)
    slot = lax.broadcasted_iota(I32, (TM, SORT_ROWS), 1)
    ls = ls_ref[...]
    rw = rw_ref[...]
    g = jnp.zeros((TM, SORT_ROWS), F32)
    for k in range(TOP_K):
        g = jnp.where(slot == ls[:, k:k + 1], rw[:, k:k + 1], g)
    xn = x_ref[0] + mod_ref[0, 5:6, :] * _dot(g.astype(BF16), yb)
    if final:
        ms = jnp.mean(xn * xn, axis=-1, keepdims=True)
        xn = xn * lax.rsqrt(ms + NORM_EPS) * fg_ref[...]
    xo_ref[0] = xn


def _combine(xn, ls, rw, mod_l, final_gain, ys, tabs, final):
    b, s, d = xn.shape
    nt = s // TM
    tok = pl.BlockSpec((1, TM, d), lambda i, *_: (i // nt, i % nt, 0))
    lanes = pl.BlockSpec((TM, LANE), lambda i, *_: (i, 0))
    if final:
        out_spec = pl.BlockSpec((1, TM, d), lambda i, *_: (i // nt, jnp.maximum(i % nt - 1, 0), 0))
        out_shape = jax.ShapeDtypeStruct((b, s - TM, d), F32)
    else:
        out_spec, out_shape = tok, jax.ShapeDtypeStruct((b, s, d), F32)
    grid_spec = pltpu.PrefetchScalarGridSpec(
        num_scalar_prefetch=2,
        grid=(b * nt,),
        in_specs=[tok, lanes, lanes,
                  pl.BlockSpec((1, 6, d), lambda i, *_: (jnp.where(i % nt == 0, b, i // nt), 0, 0)),
                  pl.BlockSpec((1, d), lambda i, *_: (0, 0)),
                  pl.BlockSpec(memory_space=pl.ANY)],
        out_specs=out_spec,
        scratch_shapes=[pltpu.VMEM((2, SORT_ROWS, d), F32), pltpu.SemaphoreType.DMA((2,))],
    )
    return pl.pallas_call(
        functools.partial(_combine_kernel, final=final),
        grid_spec=grid_spec,
        out_shape=out_shape,
        compiler_params=_params("arbitrary"),
    )(tabs["dst"], tabs["count"], xn, ls, rw, mod_l, final_gain.reshape(1, d), ys)


def _inproj_weight(w_in_l):
    d = w_in_l.shape[0]
    sizes = (ATT_Q, N_KV_HEADS * HEAD_DIM, N_KV_HEADS * HEAD_DIM, POOL_WIDTH,
             RET_W, RET_W, RET_W, RET_W, RET_W, d, d, d)
    parts, off = [], 0
    for sz in sizes:
        parts.append(w_in_l[:, off:off + sz])
        off += sz
    q, k, v, u, rq, rk, rv, rgf, rgb, ga, gp, gr = parts
    twice = lambda w: jnp.concatenate(
        [w[:, h * HEAD_DIM:(h + 1) * HEAD_DIM] for h in range(N_KV_HEADS) for _ in range(2)], axis=1)
    cols = [q * HEAD_DIM ** -0.5, twice(k), twice(v), rq, rk * RET_DIM ** -0.5, rv, rgf, rgb, u, ga, gp, gr]
    return jnp.concatenate(cols, axis=1).astype(BF16)


def _block_diag(blocks):
    n, r, c = blocks.shape
    out = jnp.zeros((n * r, n * c), blocks.dtype)
    for g in range(n):
        out = out.at[g * r:(g + 1) * r, g * c:(g + 1) * c].set(blocks[g])
    return out


def _routing_tables(cnt, n_blocks):
    ntt = cnt.shape[0] // SUBLANE
    counts = cnt.reshape(ntt, SUBLANE, LANE)[:, 0, :N_EXPERTS]
    run = (counts + SUBLANE - 1) // SUBLANE
    total = jnp.sum(run, axis=0)
    eb = EB // SUBLANE
    padded = (total + eb - 1) // eb * eb
    pad_end = jnp.cumsum(padded)
    pad_start = pad_end - padded
    off = pad_start[None, :] + jnp.cumsum(run, axis=0) - run
    run_end = jnp.cumsum(run, axis=1)
    j = jnp.arange(SORT_TILES)
    owner = jnp.sum(run_end[:, None, :] <= j[None, :, None], axis=2)
    mine = owner[:, :, None] == jnp.arange(N_EXPERTS)[None, None, :]
    in_region = j[None, :] + jnp.sum(jnp.where(mine, (off - (run_end - run))[:, None, :], 0), axis=2)
    dst = jnp.where(owner < N_EXPERTS, in_region, 0)
    n_used = pad_end[-1] // eb
    blk = jnp.minimum(jnp.arange(n_blocks), n_used - 1) * eb
    block_e = jnp.minimum(jnp.sum(pad_end[None, :] <= blk[:, None], axis=1), N_EXPERTS - 1)
    tabs = dict(dst=dst.reshape(-1).astype(I32), count=run_end[:, -1].astype(I32),
                fill_start=(pad_start + total).astype(I32),
                fill_n=(padded - total).astype(I32))
    return tabs, block_e.astype(I32), n_used.reshape(1).astype(I32)


def kernel(x, c, ctx, c_ctx, w_mod, b_mod, norm1, norm2, w_in, attn_sinks, pool_w, pool_scale, ret_decay,
           w_o_attn, w_o_pool, w_o_ret, w_out, w_router, b_router, w_expert_in, b_expert_in, w_expert_out,
           b_expert_out, final_norm):
    b, seq, d = x.shape
    depth = w_mod.shape[0]
    assert ctx.shape[1] == TM and seq % TM == 0 and seq % GRID_W == 0
    s = seq + TM
    xm = jnp.concatenate([ctx, x], axis=1)
    mod = _modulation(c, c_ctx, w_mod, b_mod)
    tables = _rope_tables(seq)
    ntt = b * s // TM
    n_blocks = -(-(b * s * TOP_K + ntt * N_EXPERTS * (SUBLANE - 1)) // EB) + N_EXPERTS
    avg = _block_diag(jnp.full((RET_HEADS, RET_DIM, RET_DIM), 1.0 / RET_DIM, F32)).astype(BF16)
    tri = (jnp.arange(TM)[:, None] >= jnp.arange(TM)[None, :]).astype(BF16)
    for l in range(depth):
        q, k2, v2, rq, rk, rv, rg, u, gates = _inproj(xm, mod[l], norm1[l], _inproj_weight(w_in[l]), tables)
        attn = _attention(q, k2, v2, attn_sinks[l])
        yf, yb = _retention(rq, rk, rv, jax.nn.log_sigmoid(ret_decay[l].astype(F32)))
        wr = jnp.zeros((d, LANE), F32).at[:, :N_EXPERTS].set(w_router[l])
        wr_hilo = jnp.concatenate(_split(wr), axis=1)
        br = jnp.full((1, LANE), NEG, F32).at[0, :N_EXPERTS].set(b_router[l])
        wts = (_block_diag(pool_w[l]).astype(BF16), pool_scale[l].reshape(1, -1),
               w_o_attn[l].astype(BF16), w_o_pool[l].astype(BF16), w_o_ret[l].astype(BF16),
               w_out[l].astype(BF16), norm2[l].reshape(1, d), wr_hilo, br, avg, tri)
        xn, h2, ls, lst, rw, cnt = _merge(xm, attn, u, yf, yb, rg, gates, mod[l], seq, wts)
        tabs, block_e, n_used = _routing_tables(cnt, n_blocks)
        xs = _dispatch(h2.reshape(b * s, d), lst, tabs, n_used, n_blocks)
        ys = _experts(xs, block_e, n_used, w_expert_in, b_expert_in, w_expert_out, b_expert_out, l)
        xm = _combine(xn, ls, rw, mod[l], final_norm, ys, tabs, final=l == depth - 1)
    return xm
```

```python
import functools

import jax
import jax.numpy as jnp
from jax import lax
from jax.experimental import pallas as pl
from jax.experimental.pallas import tpu as pltpu

F32 = jnp.float32
BF16 = jnp.bfloat16
I32 = jnp.int32

GRID_W = 64
HEAD_DIM = 64
N_Q_HEADS = 8
N_KV_HEADS = 2
WINDOW = 128
ROPE_THETA = 10000.0
POOL_WIDTH = 256
POOL_WINDOWS = (2, 4, 8, 16)
RET_HEADS = 4
RET_DIM = 64
N_EXPERTS = 32
TOP_K = 4
D_FF = 1024
SWIGLU_LIMIT = 7.0
SWIGLU_ALPHA = 1.702
NORM_EPS = 1e-6
GN_EPS = 1e-5

LANE = 128
SUBLANE = 8
MXU_N = 256
TM = 256
AB = 128
IN_TILE = 640
CTX_BLOCKS = TM // AB
EB = 512
FF_CHUNK = 256
POOL_HALO = SUBLANE
SORT_ROWS = TM * TOP_K + N_EXPERTS * SUBLANE
SORT_TILES = SORT_ROWS // SUBLANE
MIN_TILES = TM * TOP_K // SUBLANE
NEG = -1e30
VMEM_LIMIT = 56 * 1024 * 1024

ATT_Q = N_Q_HEADS * HEAD_DIM
KV2 = 2 * N_KV_HEADS * HEAD_DIM
RET_W = RET_HEADS * RET_DIM


def _dot(a, b):
    return jnp.dot(a, b, preferred_element_type=F32)


def _dot_nt(a, b):
    return lax.dot_general(a, b, (((1,), (1,)), ((), ())), preferred_element_type=F32)


def _dot_tn(a, b):
    return lax.dot_general(a, b, (((0,), (0,)), ((), ())), preferred_element_type=F32)


def _split(a):
    hi = a.astype(BF16)
    lo = (a - hi.astype(F32)).astype(BF16)
    return hi, lo


def _dot_hilo(a, m):
    hi, lo = _split(a)
    return _dot(hi, m) + _dot(lo, m)


def _sigmoid(x):
    return 0.5 * jnp.tanh(0.5 * x) + 0.5


def _silu(x):
    return x * _sigmoid(x)


def _params(*sem):
    return pltpu.CompilerParams(dimension_semantics=sem, vmem_limit_bytes=VMEM_LIMIT)


def _mod_kernel(c_ref, w_ref, b_ref, o_ref):
    s = _silu(c_ref[...])
    sh, sl = _split(s)
    wh, wl = _split(w_ref[0])
    o_ref[0] = _dot(sh, wh) + _dot(sh, wl) + _dot(sl, wh) + b_ref[0]


def _modulation(c, c_ctx, w_mod, b_mod):
    depth, d, six_d = w_mod.shape
    b = c.shape[0]
    cc = jnp.zeros((SUBLANE, d), F32).at[:b].set(c).at[b].set(c_ctx)
    out = pl.pallas_call(
        _mod_kernel,
        grid=(depth, six_d // d),
        in_specs=[
            pl.BlockSpec((SUBLANE, d), lambda l, n: (0, 0)),
            pl.BlockSpec((1, d, d), lambda l, n: (l, 0, n)),
            pl.BlockSpec((1, 1, d), lambda l, n: (l, 0, n)),
        ],
        out_specs=pl.BlockSpec((1, SUBLANE, d), lambda l, n: (l, 0, n)),
        out_shape=jax.ShapeDtypeStruct((depth, SUBLANE, six_d), F32),
        compiler_params=_params("arbitrary", "arbitrary"),
    )(cc, w_mod, b_mod.reshape(depth, 1, six_d))
    return out[:, : b + 1].reshape(depth, b + 1, 6, d)


def _rope(x, cos, sin, half, first):
    partner = jnp.where(first, pltpu.roll(x, LANE - half, 1), pltpu.roll(x, half, 1))
    return x * cos + partner * sin


def _inproj_kernel(x_ref, mod_ref, modc_ref, n1_ref, w_ref, ac_ref, as_ref, rc_ref, rs_ref,
                   q_ref, k_ref, v_ref, rq_ref, rk_ref, rv_ref, rg_ref, u_ref, g_ref):
    x = x_ref[0]
    rows = x.shape[0]
    ms = jnp.mean(x * x, axis=-1, keepdims=True)
    y = x * lax.rsqrt(ms + NORM_EPS) * n1_ref[...]
    is_ctx = (lax.broadcasted_iota(I32, (rows, 1), 0) < TM) & (pl.program_id(1) == 0)
    scale = jnp.where(is_ctx, modc_ref[0, 1:2, :], mod_ref[0, 1:2, :])
    shift = jnp.where(is_ctx, modc_ref[0, 0:1, :], mod_ref[0, 0:1, :])
    hb = (y * (1.0 + scale) + shift).astype(BF16)
    lane = lax.broadcasted_iota(I32, (rows, LANE), 1)
    a_first = (lane % (HEAD_DIM // 2)) < (HEAD_DIM // 4)
    r_first = (lane % RET_DIM) < (RET_DIM // 2)
    ac, asn, rc, rsn = ac_ref[...], as_ref[...], rc_ref[...], rs_ref[...]

    def proj(off, width):
        return _dot(hb, w_ref[:, off:off + width])

    def rotated(ref, width, cos, sin, half, first, off):
        for t in range(0, width, MXU_N):
            pr = proj(off + t, MXU_N)
            for g in range(0, MXU_N, LANE):
                ref[0, :, t + g:t + g + LANE] = _rope(pr[:, g:g + LANE], cos, sin, half, first).astype(BF16)

    off = 0
    rotated(q_ref, ATT_Q, ac, asn, HEAD_DIM // 4, a_first, off)
    off += ATT_Q
    rotated(k_ref, KV2, ac, asn, HEAD_DIM // 4, a_first, off)
    off += KV2
    v_ref[0] = proj(off, KV2).astype(BF16)
    off += KV2
    for ref in (rq_ref, rk_ref):
        rotated(ref, RET_W, rc, rsn, RET_DIM // 2, r_first, off)
        off += RET_W
    rv_ref[0] = proj(off, RET_W).astype(BF16)
    off += RET_W
    rg_ref[0] = proj(off, 2 * RET_W).astype(BF16)
    off += 2 * RET_W
    u_ref[0] = proj(off, POOL_WIDTH)
    off += POOL_WIDTH
    d = x.shape[-1]
    for t in range(3):
        g_ref[0, :, t * d:(t + 1) * d] = proj(off, d).astype(BF16)
        off += d


def _inproj(xm, mod_l, norm1_l, w1, tables):
    b, s, d = xm.shape
    ti = IN_TILE if s % IN_TILE == 0 else TM
    wcols = w1.shape[1]
    tok = lambda width: pl.BlockSpec((1, ti, width), lambda bi, j: (bi, j, 0))
    tab = pl.BlockSpec((ti, LANE), lambda bi, j: (j, 0))
    widths = (ATT_Q, KV2, KV2, RET_W, RET_W, RET_W, 2 * RET_W, POOL_WIDTH, 3 * d)
    dtypes = (BF16,) * 7 + (F32, BF16)
    return pl.pallas_call(
        _inproj_kernel,
        grid=(b, s // ti),
        in_specs=[
            tok(d),
            pl.BlockSpec((1, 6, d), lambda bi, j: (bi, 0, 0)),
            pl.BlockSpec((1, 6, d), lambda bi, j: (b, 0, 0)),
            pl.BlockSpec((1, d), lambda bi, j: (0, 0)),
            pl.BlockSpec((d, wcols), lambda bi, j: (0, 0), pipeline_mode=pl.Buffered(1)),
            tab, tab, tab, tab,
        ],
        out_specs=[tok(w) for w in widths],
        out_shape=[jax.ShapeDtypeStruct((b, s, w), dt) for w, dt in zip(widths, dtypes)],
        compiler_params=_params("arbitrary", "arbitrary"),
    )(xm, mod_l, mod_l, norm1_l.reshape(1, d), w1, *tables)


def _rope_tables(seq):
    rows = seq // GRID_W
    rpos = jnp.arange(rows, dtype=F32)[:, None]
    cpos = jnp.arange(GRID_W, dtype=F32)[:, None]
    lane = jnp.arange(LANE)
    grid = lambda per_row, per_col: (per_row[:, None, :] + per_col[None, :, :]).reshape(seq, LANE)
    axis_dim = HEAD_DIM // 2
    inv_a = ROPE_THETA ** (-jnp.arange(0, axis_dim, 2, dtype=F32) / axis_dim)
    hl = lane % HEAD_DIM
    inv_al = inv_a[(hl % axis_dim) % (axis_dim // 2)][None, :]
    by_row = (hl < axis_dim)[None, :]
    a_sign = jnp.where((hl % axis_dim) < axis_dim // 2, -1.0, 1.0)[None, :]
    a_cos = grid(jnp.where(by_row, jnp.cos(rpos * inv_al), 0.0), jnp.where(by_row, 0.0, jnp.cos(cpos * inv_al)))
    a_sin = grid(jnp.where(by_row, jnp.sin(rpos * inv_al), 0.0), jnp.where(by_row, 0.0, jnp.sin(cpos * inv_al)))
    inv_r = 1.0 / (ROPE_THETA ** jnp.linspace(0.0, 1.0, RET_DIM // 2, dtype=F32))
    rl = lane % RET_DIM
    inv_rl = inv_r[rl % (RET_DIM // 2)][None, :]
    r_sign = jnp.where(rl < RET_DIM // 2, -1.0, 1.0)[None, :]
    hi = (rpos * GRID_W) * inv_rl
    lo = cpos * inv_rl
    outer = lambda a, b: (a[:, None, :] * b[None, :, :]).reshape(seq, LANE)
    r_cos = outer(jnp.cos(hi), jnp.cos(lo)) - outer(jnp.sin(hi), jnp.sin(lo))
    r_sin = outer(jnp.sin(hi), jnp.cos(lo)) + outer(jnp.cos(hi), jnp.sin(lo))
    ones = jnp.ones((TM, LANE), F32)
    zeros = jnp.zeros((TM, LANE), F32)
    cat = lambda head, body: jnp.concatenate([head, body], axis=0)
    return (cat(ones, a_cos), cat(zeros, a_sin * a_sign), cat(ones, r_cos), cat(zeros, r_sin * r_sign))


def _attn_kernel(sink_ref, bias_ref, q_ref, kp_ref, kc_ref, kn_ref, kx_ref, vp_ref, vc_ref, vn_ref, vx_ref, o_ref):
    g_heads = N_Q_HEADS // N_KV_HEADS
    lane = lax.broadcasted_iota(I32, (AB, LANE), 1)
    lo = lane < HEAD_DIM
    rows = lax.broadcasted_iota(I32, (g_heads * AB, 1), 0)
    bias = bias_ref[0]
    scores, sinks, values = [], [], []
    for g in range(N_KV_HEADS):
        ks = slice(g * LANE, (g + 1) * LANE)
        kd = jnp.concatenate([kp_ref[0, :, ks], kc_ref[0, :, ks], kn_ref[0, :, ks], kx_ref[0, :, ks]], axis=0)
        values.append(jnp.concatenate([vp_ref[0, :, ks], vc_ref[0, :, ks], vn_ref[0, :, ks], vx_ref[0, :, ks]], axis=0))
        qs = []
        for c in range(2 * g, 2 * g + 2):
            qc = q_ref[0, :, c * LANE:(c + 1) * LANE]
            zero = jnp.zeros_like(qc)
            qs += [jnp.where(lo, qc, zero), jnp.where(lo, zero, qc)]
        scores.append(_dot_nt(jnp.concatenate(qs, axis=0), kd) + bias)
        sink = jnp.full((g_heads * AB, 1), sink_ref[g_heads * g + g_heads - 1], F32)
        for t in range(g_heads - 2, -1, -1):
            sink = jnp.where(rows < (t + 1) * AB, sink_ref[g_heads * g + t], sink)
        sinks.append(sink)
    outs = []
    for s, sink, vd in zip(scores, sinks, values):
        m = jnp.maximum(jnp.max(s, axis=-1, keepdims=True), sink)
        p = jnp.exp(s - m)
        den = jnp.sum(p, axis=-1, keepdims=True) + jnp.exp(sink - m)
        o = _dot(p.astype(BF16), vd) / den
        for t in range(2):
            outs.append(jnp.where(lo, o[2 * t * AB:(2 * t + 1) * AB], o[(2 * t + 1) * AB:(2 * t + 2) * AB]))
    o_ref[0] = jnp.concatenate(outs, axis=-1).astype(BF16)


def _attn_bias():
    g_heads = N_Q_HEADS // N_KV_HEADS
    r = jnp.arange(g_heads * AB)[:, None] % AB
    j = jnp.arange(3 * AB + TM)[None, :]
    band = jnp.abs(j - AB - r) <= WINDOW
    is_ctx = j >= 3 * AB
    variants = (band, band & (j >= AB), band & (j < 2 * AB), jnp.zeros_like(band))
    return jnp.stack([jnp.where(v | is_ctx, 0.0, NEG) for v in variants]).astype(F32)


def _attention(q, k2, v2, sinks):
    b, s, _ = q.shape
    nb = s // AB
    assert nb - CTX_BLOCKS >= 2
    bias = _attn_bias()
    variant = lambda i: jnp.where(i < CTX_BLOCKS, 3, jnp.where(i == CTX_BLOCKS, 1, jnp.where(i == nb - 1, 2, 0)))
    prev = pl.BlockSpec((1, AB, KV2), lambda bi, i: (bi, jnp.maximum(i - 1, 0), 0))
    cur = pl.BlockSpec((1, AB, KV2), lambda bi, i: (bi, i, 0))
    nxt = pl.BlockSpec((1, AB, KV2), lambda bi, i: (bi, jnp.minimum(i + 1, nb - 1), 0))
    cx = pl.BlockSpec((1, TM, KV2), lambda bi, i: (bi, 0, 0))
    return pl.pallas_call(
        _attn_kernel,
        grid=(b, nb),
        in_specs=[pl.BlockSpec(memory_space=pltpu.SMEM),
                  pl.BlockSpec((1,) + bias.shape[1:], lambda bi, i: (variant(i), 0, 0)),
                  pl.BlockSpec((1, AB, ATT_Q), lambda bi, i: (bi, i, 0)),
                  prev, cur, nxt, cx, prev, cur, nxt, cx],
        out_specs=pl.BlockSpec((1, AB, ATT_Q), lambda bi, i: (bi, i, 0)),
        out_shape=jax.ShapeDtypeStruct((b, s, ATT_Q), BF16),
        compiler_params=_params("arbitrary", "arbitrary"),
    )(sinks, bias, q, k2, k2, k2, k2, v2, v2, v2, v2)


def _ret_kernel(lg_ref, qf_ref, kf_ref, vf_ref, qb_ref, kb_ref, vb_ref, yf_ref, yb_ref,
                st_ref, dm_ref, qd_ref, kd_ref, cd_ref, *, batch):
    step = pl.program_id(0)
    lane = lax.broadcasted_iota(I32, (AB, LANE), 1)
    row = lax.broadcasted_iota(I32, (AB, LANE), 0)
    lo = lane < RET_DIM
    tiles = RET_W // LANE

    @pl.when(step == 0)
    def _():
        st_ref[...] = jnp.zeros_like(st_ref)
        ii = row.astype(F32)
        jj = lane.astype(F32)
        for d in range(2):
            for c in range(tiles):
                lg0 = lg_ref[d * RET_HEADS + 2 * c]
                lg1 = lg_ref[d * RET_HEADS + 2 * c + 1]
                lgl = jnp.where(lo, lg0, lg1)
                q_exp = ii + 1.0 if d == 0 else AB - ii
                k_exp = (AB - 1.0) - ii if d == 0 else ii
                qd_ref[d * tiles + c] = jnp.exp(q_exp * lgl)
                kd_ref[d * tiles + c] = jnp.exp(k_exp * lgl)
                cd_ref[d * tiles + c] = jnp.exp(AB * lgl)
                rel = ii - jj if d == 0 else jj - ii
                for hh, lgh in enumerate((lg0, lg1)):
                    dm_ref[d * tiles + c, hh * AB:(hh + 1) * AB, :] = jnp.where(
                        rel >= 0, jnp.exp(jnp.maximum(rel, 0.0) * lgh), 0.0)

    same_head = (row < RET_DIM) == lo
    dirs = ((qf_ref, kf_ref, vf_ref, yf_ref), (qb_ref, kb_ref, vb_ref, yb_ref))
    chains = [(d, b, c) for d in range(2) for b in range(batch) for c in range(tiles)]

    def operands(d, b, c):
        q_ref, k_ref, v_ref, _ = dirs[d]
        sl = slice(c * LANE, (c + 1) * LANE)
        return q_ref[b, :, sl], k_ref[b, :, sl], v_ref[b, :, sl]

    probs = []
    for d, b, c in chains:
        q, k, _ = operands(d, b, c)
        zero = jnp.zeros_like(q)
        q2 = jnp.concatenate([jnp.where(lo, q, zero), jnp.where(lo, zero, q)], axis=0)
        probs.append((_dot_nt(q2, k) * dm_ref[d * tiles + c]).astype(BF16))
    for (d, b, c), p in zip(chains, probs):
        q, _, v = operands(d, b, c)
        t = d * tiles + c
        si = (d * batch + b) * tiles + c
        y_intra = jnp.where(lo, _dot(p[:AB], v), _dot(p[AB:], v))
        q_dec = (q.astype(F32) * qd_ref[t]).astype(BF16)
        dirs[d][3][b, :, c * LANE:(c + 1) * LANE] = y_intra + _dot(q_dec, st_ref[si].astype(BF16))
    for d, b, c in chains:
        _, k, v = operands(d, b, c)
        t = d * tiles + c
        si = (d * batch + b) * tiles + c
        k_dec = (k.astype(F32) * kd_ref[t]).astype(BF16)
        st_ref[si] = st_ref[si] * cd_ref[t] + jnp.where(same_head, _dot_tn(k_dec, v), 0.0)


def _retention(rq, rk, rv, log_g):
    b, s, w = rq.shape
    nb = s // AB
    tiles = w // LANE

    def back(i):
        return jnp.where(i < CTX_BLOCKS, CTX_BLOCKS - 1 - i, nb - 1 + CTX_BLOCKS - i)

    fwd = pl.BlockSpec((b, AB, w), lambda i: (0, i, 0))
    bwd = pl.BlockSpec((b, AB, w), lambda i: (0, back(i), 0))
    return pl.pallas_call(
        functools.partial(_ret_kernel, batch=b),
        grid=(nb,),
        in_specs=[pl.BlockSpec(memory_space=pltpu.SMEM), fwd, fwd, fwd, bwd, bwd, bwd],
        out_specs=[fwd, bwd],
        out_shape=[jax.ShapeDtypeStruct((b, s, w), F32)] * 2,
        scratch_shapes=[
            pltpu.VMEM((2 * b * tiles, LANE, LANE), F32),
            pltpu.VMEM((2 * tiles, 2 * AB, LANE), F32),
            pltpu.VMEM((2 * tiles, AB, LANE), F32),
            pltpu.VMEM((2 * tiles, AB, LANE), F32),
            pltpu.VMEM((2 * tiles, AB, LANE), F32),
        ],
        compiler_params=_params("arbitrary"),
    )(log_g.reshape(-1), rq, rk, rv, rq, rk, rv)


def _route_tile(logits, tri_ref, ls_ref, lst_ref, rw_ref, cnt_ref):
    ne = N_EXPERTS
    eidx = lax.broadcasted_iota(I32, (ne, TM), 0).astype(F32)
    tops, hots = [], []
    for _ in range(TOP_K):
        mx = jnp.max(logits, axis=0, keepdims=True)
        idx = jnp.min(jnp.where(logits == mx, eidx, float(ne)), axis=0, keepdims=True)
        hot = eidx == idx
        logits = jnp.where(hot, NEG * 2.0, logits)
        tops.append(mx)
        hots.append(hot)
    ex = [jnp.exp(mx - tops[0]) for mx in tops]
    tot = ex[0]
    for e in ex[1:]:
        tot = tot + e
    sel = jnp.zeros((ne, TM), F32)
    for hot in hots:
        sel = jnp.where(hot, 1.0, sel)
    incl = _dot(sel.astype(BF16), tri_ref[...])
    cnt = incl[:, TM - 1:TM]
    run = jnp.floor((cnt + (SUBLANE - 1.0)) * (1.0 / SUBLANE)) * SUBLANE
    ends = jnp.broadcast_to(run, (ne, LANE))
    erow = lax.broadcasted_iota(I32, (ne, LANE), 0)
    shift = 1
    while shift < ne:
        ends = ends + jnp.where(erow >= shift, pltpu.roll(ends, shift, 0), 0.0)
        shift *= 2
    slot = (ends[:, :1] - run) + incl - 1.0
    ls_rows = [jnp.sum(jnp.where(hot, slot, 0.0), axis=0, keepdims=True) for hot in hots]
    rw_rows = [e / tot for e in ex]
    lst = jnp.concatenate(ls_rows + [jnp.full((SUBLANE - TOP_K, TM), -1.0, F32)], axis=0)
    lst_ref[...] = lst.astype(I32)
    ls_ref[...] = jnp.transpose(
        jnp.concatenate([lst, jnp.full((LANE - SUBLANE, TM), -1.0, F32)], axis=0)).astype(I32)
    rw_ref[...] = jnp.transpose(jnp.concatenate(rw_rows + [jnp.zeros((LANE - TOP_K, TM), F32)], axis=0))
    cnt_ref[...] = jnp.broadcast_to(cnt, cnt_ref.shape).astype(I32)


def _merge_kernel(x_ref, attn_ref, up_ref, uc_ref, un_ref, yf_ref, yb_ref, rg_ref, g_ref, mod_ref,
                  wbd_ref, ps_ref, woa_ref, wop_ref, wor_ref, wout_ref, n2_ref, wr_ref,
                  br_ref, avg_ref, tri_ref,
                  xo_ref, h2_ref, ls_ref, lst_ref, rw_ref, cnt_ref, lg_ref, *, seq, nt, ntt):
    i = pl.program_id(0)
    j = jnp.minimum(i, ntt - 1) % nt

    @pl.when(i == 0)
    def _():
        lg_ref[...] = jnp.zeros_like(lg_ref)

    _route_tile(lg_ref[...], tri_ref, ls_ref, lst_ref, rw_ref, cnt_ref)

    ext = TM + 2 * POOL_HALO
    seq_len = jnp.where(j == 0, TM, seq)
    start = jnp.where(j == 0, 0, (j - 1) * TM)
    u = uc_ref[0]
    ue = jnp.concatenate([up_ref[0], u, un_ref[0]], axis=0)
    erow = lax.broadcasted_iota(I32, (ext, 1), 0) + (start - POOL_HALO)
    ue = jnp.where((erow >= 0) & (erow < seq_len), ue, 0.0)
    t_pos = lax.broadcasted_iota(I32, (TM, 1), 0) + start
    glane = lax.broadcasted_iota(I32, (TM, POOL_WIDTH), 1) // (POOL_WIDTH // len(POOL_WINDOWS))
    run = ue
    width = 1
    diff = jnp.zeros((TM, POOL_WIDTH), F32)
    for gi, w in enumerate(POOL_WINDOWS):
        while width < w:
            run = run + pltpu.roll(run, ext - width, 0)
            width *= 2
        win = pltpu.roll(run, w // 2, 0)[POOL_HALO:POOL_HALO + TM]
        cnt = jnp.minimum(t_pos - w // 2 + w, seq_len) - jnp.maximum(t_pos - w // 2, 0)
        diff = jnp.where(glane == gi, win / cnt.astype(F32) - u, diff)
    pool = _dot(diff.astype(BF16), wbd_ref[...]) * ps_ref[...]

    def head_norm(y):
        mu = _dot_hilo(y, avg_ref[...])
        dlt = y - mu
        var = _dot_hilo(dlt * dlt, avg_ref[...])
        return dlt * lax.rsqrt(var + GN_EPS)

    rg = rg_ref[0].astype(F32)
    ret = head_norm(yf_ref[0]) * _silu(rg[:, :RET_W]) + head_norm(yb_ref[0]) * _silu(rg[:, RET_W:])

    d = x_ref.shape[-1]
    gate = lambda t: _sigmoid(g_ref[0, :, t * d:(t + 1) * d])
    m = (gate(0) * _dot(attn_ref[0], woa_ref[...]).astype(BF16)
         + gate(1) * _dot(pool.astype(BF16), wop_ref[...]).astype(BF16)
         + gate(2) * _dot(ret.astype(BF16), wor_ref[...]).astype(BF16))
    xn = x_ref[0] + mod_ref[0, 2:3, :] * _dot(m, wout_ref[...])
    xo_ref[0] = xn

    ms = jnp.mean(xn * xn, axis=-1, keepdims=True)
    h2 = xn * lax.rsqrt(ms + NORM_EPS) * n2_ref[...] * (1.0 + mod_ref[0, 4:5, :]) + mod_ref[0, 3:4, :]
    h2_ref[0] = h2.astype(BF16)
    hh, hl = _split(h2)
    ne = N_EXPERTS
    full = _dot_nt(wr_ref[...], hh)
    logits = full[:ne] + full[ne:] + _dot_nt(wr_ref[:ne], hl) + br_ref[...]
    lg_ref[...] = logits


def _merge(xm, attn, u, yf, yb, rg, gates, mod_l, seq, wts):
    b, s, d = xm.shape
    nt = s // TM
    ntt = b * nt
    hb = TM // POOL_HALO
    nh = s // POOL_HALO
    cur = lambda i: jnp.minimum(i, ntt - 1)
    lag = lambda i: jnp.maximum(i - 1, 0)
    tok = lambda width: pl.BlockSpec((1, TM, width), lambda i: (cur(i) // nt, cur(i) % nt, 0))
    full = lambda a: pl.BlockSpec(a.shape, lambda i: (0,) * a.ndim)
    lanes = pl.BlockSpec((TM, LANE), lambda i: (lag(i), 0))
    in_specs = [
        tok(d), tok(ATT_Q),
        pl.BlockSpec((1, POOL_HALO, POOL_WIDTH),
                     lambda i: (cur(i) // nt, jnp.maximum((cur(i) % nt) * hb - 1, 0), 0)),
        tok(POOL_WIDTH),
        pl.BlockSpec((1, POOL_HALO, POOL_WIDTH),
                     lambda i: (cur(i) // nt, jnp.minimum((cur(i) % nt + 1) * hb, nh - 1), 0)),
        tok(RET_W), tok(RET_W), tok(2 * RET_W), tok(3 * d),
        pl.BlockSpec((1, 6, d), lambda i: (jnp.where(cur(i) % nt == 0, b, cur(i) // nt), 0, 0)),
    ] + [full(a) for a in wts]
    return pl.pallas_call(
        functools.partial(_merge_kernel, seq=seq, nt=nt, ntt=ntt),
        grid=(ntt + 1,),
        in_specs=in_specs,
        out_specs=[tok(d), tok(d), lanes, pl.BlockSpec((SUBLANE, TM), lambda i: (lag(i), 0)), lanes,
                   pl.BlockSpec((N_EXPERTS, LANE), lambda i: (lag(i), 0))],
        out_shape=[jax.ShapeDtypeStruct((b, s, d), F32), jax.ShapeDtypeStruct((b, s, d), BF16),
                   jax.ShapeDtypeStruct((b * s, LANE), I32), jax.ShapeDtypeStruct((ntt * SUBLANE, TM), I32),
                   jax.ShapeDtypeStruct((b * s, LANE), F32), jax.ShapeDtypeStruct((ntt * N_EXPERTS, LANE), I32)],
        scratch_shapes=[pltpu.VMEM((N_EXPERTS, TM), F32)],
        compiler_params=_params("arbitrary"),
    )(xm, attn, u, u, u, yf, yb, rg, gates, mod_l, *wts)


def _rows(tile_index):
    return pl.ds(pl.multiple_of(tile_index * SUBLANE, SUBLANE), SUBLANE)


def _tile_loop(count_ref, tile, fn):
    def body(j, carry):
        fn(j)
        return carry

    lax.fori_loop(0, MIN_TILES, body, 0, unroll=8)
    lax.fori_loop(MIN_TILES, count_ref[tile], body, 0)


def _tile_copies(table_ref, count_ref, tile, copy):
    _tile_loop(count_ref, tile, lambda j: copy(j, table_ref[tile * SORT_TILES + j]))


def _dispatch_kernel(dst_ref, cnt_ref, fs_ref, fn_ref, nu_ref, h_ref, lst_ref, xs_hbm, buf, zbuf, sem, fill_sem,
                     *, n_blocks):
    i = pl.program_id(0)
    last = pl.num_programs(0) - 1
    cur = i % 2

    def send(tile, half):
        _tile_copies(dst_ref, cnt_ref, tile, lambda j, t: pltpu.make_async_copy(
            buf.at[half, _rows(j)], xs_hbm.at[_rows(t)], sem.at[half]).start())

    def drain(tile, half):
        _tile_loop(cnt_ref, tile, lambda j: pltpu.make_async_copy(
            buf.at[half, _rows(j)], xs_hbm.at[_rows(j)], sem.at[half]).wait())

    @pl.when(i == 0)
    def _():
        zbuf[...] = jnp.zeros_like(zbuf)

        def pad_copy(e, c):
            return pltpu.make_async_copy(zbuf.at[pl.ds(0, SUBLANE)], xs_hbm.at[_rows(fs_ref[e] + c)], fill_sem)

        def blk_copy(blk):
            return pltpu.make_async_copy(zbuf, xs_hbm.at[pl.ds(pl.multiple_of(blk * EB, EB), EB)], fill_sem)

        def per_expert(fn):
            def outer(e, carry):
                def inner(c, cc):
                    fn(e, c)
                    return cc
                return lax.fori_loop(0, fn_ref[e], inner, carry)
            lax.fori_loop(0, N_EXPERTS, outer, 0)

        def per_block(fn):
            def body(blk, carry):
                fn(blk)
                return carry
            lax.fori_loop(nu_ref[0], n_blocks, body, 0)

        per_expert(lambda e, c: pad_copy(e, c).start())
        per_block(lambda blk: blk_copy(blk).start())
        per_expert(lambda e, c: pad_copy(e, c).wait())
        per_block(lambda blk: blk_copy(blk).wait())

    slot = lax.broadcasted_iota(I32, (SORT_ROWS, TM), 0)
    lst = lst_ref[...]
    p = jnp.zeros((SORT_ROWS, TM), F32)
    for k in range(TOP_K):
        p = jnp.where(slot == lst[k:k + 1, :], 1.0, p)
    srt = _dot(p.astype(BF16), h_ref[...])

    @pl.when(i >= 2)
    def _():
        drain(i - 2, cur)

    buf[cur] = srt
    send(i, cur)

    @pl.when(i == last)
    def _():
        @pl.when(i >= 1)
        def _():
            drain(i - 1, 1 - cur)
        drain(i, cur)


def _dispatch(h2, lst, tabs, n_used, n_blocks):
    t, d = h2.shape
    ntt = t // TM
    grid_spec = pltpu.PrefetchScalarGridSpec(
        num_scalar_prefetch=5,
        grid=(ntt,),
        in_specs=[pl.BlockSpec((TM, d), lambda i, *_: (i, 0)),
                  pl.BlockSpec((SUBLANE, TM), lambda i, *_: (i, 0))],
        out_specs=pl.BlockSpec(memory_space=pl.ANY),
        scratch_shapes=[pltpu.VMEM((2, SORT_ROWS, d), F32), pltpu.VMEM((EB, d), F32),
                        pltpu.SemaphoreType.DMA((2,)), pltpu.SemaphoreType.DMA],
    )
    return pl.pallas_call(
        functools.partial(_dispatch_kernel, n_blocks=n_blocks),
        grid_spec=grid_spec,
        out_shape=jax.ShapeDtypeStruct((n_blocks * EB, d), F32),
        compiler_params=_params("arbitrary"),
    )(tabs["dst"], tabs["count"], tabs["fill_start"], tabs["fill_n"], n_used, h2, lst)


def _expert_kernel(be_ref, nu_ref, x_ref, w1_ref, b1_ref, w2_ref, b2_ref, y_ref, w1b, w2b):
    bi = pl.program_id(0)
    used = bi < nu_ref[0]
    fresh = (bi == 0) | (be_ref[bi] != be_ref[jnp.maximum(bi - 1, 0)])

    @pl.when(used & fresh)
    def _():
        w1b[...] = w1_ref[0, 0].astype(BF16)
        w2b[...] = w2_ref[0, 0].astype(BF16)

    @pl.when(jnp.logical_not(used))
    def _():
        y_ref[...] = jnp.zeros_like(y_ref)

    @pl.when(used)
    def _():
        xb = x_ref[...].astype(BF16)
        y = None
        for c in range(0, D_FF, FF_CHUNK):
            glu = _dot(xb, w1b[:, c:c + FF_CHUNK]) + b1_ref[0, 0, :, c:c + FF_CHUNK]
            lin = _dot(xb, w1b[:, D_FF + c:D_FF + c + FF_CHUNK]) + b1_ref[0, 0, :, D_FF + c:D_FF + c + FF_CHUNK]
            glu = jnp.minimum(glu, SWIGLU_LIMIT)
            lin = jnp.clip(lin, -SWIGLU_LIMIT, SWIGLU_LIMIT)
            act = glu * _sigmoid(SWIGLU_ALPHA * glu) * (lin + 1.0)
            part = _dot(act.astype(BF16), w2b[c:c + FF_CHUNK, :])
            y = part if y is None else y + part
        y_ref[...] = y + b2_ref[0, 0]


def _experts(xs, block_e, n_used, w1, b1, w2, b2, layer):
    rows, d = xs.shape
    depth, ne, _, f2 = w1.shape
    row_blk = lambda bi, be, nu: (bi, 0)
    used_blk = lambda bi, be, nu: (jnp.minimum(bi, nu[0] - 1), 0)
    grid_spec = pltpu.PrefetchScalarGridSpec(
        num_scalar_prefetch=2,
        grid=(rows // EB,),
        in_specs=[
            pl.BlockSpec((EB, d), used_blk),
            pl.BlockSpec((1, 1, d, f2), lambda bi, be, nu: (layer, be[bi], 0, 0)),
            pl.BlockSpec((1, 1, 1, f2), lambda bi, be, nu: (layer, be[bi], 0, 0)),
            pl.BlockSpec((1, 1, f2 // 2, d), lambda bi, be, nu: (layer, be[bi], 0, 0)),
            pl.BlockSpec((1, 1, 1, d), lambda bi, be, nu: (layer, be[bi], 0, 0)),
        ],
        out_specs=pl.BlockSpec((EB, d), row_blk),
        scratch_shapes=[pltpu.VMEM((d, f2), BF16), pltpu.VMEM((f2 // 2, d), BF16)],
    )
    return pl.pallas_call(
        _expert_kernel,
        grid_spec=grid_spec,
        out_shape=jax.ShapeDtypeStruct((rows, d), F32),
        compiler_params=_params("arbitrary"),
    )(block_e, n_used, xs, w1, b1.reshape(depth, ne, 1, f2), w2, b2.reshape(depth, ne, 1, d))


def _combine_kernel(dst_ref, cnt_ref, x_ref, ls_ref, rw_ref, mod_ref, fg_ref, ys_hbm, xo_ref, buf, sem, *, final):
    i = pl.program_id(0)
    last = pl.num_programs(0) - 1
    cur = i % 2

    def fetch(tile, half):
        _tile_copies(dst_ref, cnt_ref, tile, lambda j, t: pltpu.make_async_copy(
            ys_hbm.at[_rows(t)], buf.at[half, _rows(j)], sem.at[half]).start())

    @pl.when(i == 0)
    def _():
        buf[...] = jnp.zeros_like(buf)
        fetch(0, 0)

    @pl.when(i < last)
    def _():
        fetch(i + 1, 1 - cur)

    _tile_loop(cnt_ref, i, lambda j: pltpu.make_async_copy(
        ys_hbm.at[_rows(j)], buf.at[cur, _rows(j)], sem.at[cur]).wait())
    yb = buf[cur].astype(BF16)
    slot = lax.broadcasted_iota(I32, (TM, SORT_ROWS), 1)
    ls = ls_ref[...]
    rw = rw_ref[...]
    g = jnp.zeros((TM, SORT_ROWS), F32)
    for k in range(TOP_K):
        g = jnp.where(slot == ls[:, k:k + 1], rw[:, k:k + 1], g)
    xn = x_ref[0] + mod_ref[0, 5:6, :] * _dot(g.astype(BF16), yb)
    if final:
        ms = jnp.mean(xn * xn, axis=-1, keepdims=True)
        xn = xn * lax.rsqrt(ms + NORM_EPS) * fg_ref[...]
    xo_ref[0] = xn


def _combine(xn, ls, rw, mod_l, final_gain, ys, tabs, final):
    b, s, d = xn.shape
    nt = s // TM
    tok = pl.BlockSpec((1, TM, d), lambda i, *_: (i // nt, i % nt, 0))
    lanes = pl.BlockSpec((TM, LANE), lambda i, *_: (i, 0))
    if final:
        out_spec = pl.BlockSpec((1, TM, d), lambda i, *_: (i // nt, jnp.maximum(i % nt - 1, 0), 0))
        out_shape = jax.ShapeDtypeStruct((b, s - TM, d), F32)
    else:
        out_spec, out_shape = tok, jax.ShapeDtypeStruct((b, s, d), F32)
    grid_spec = pltpu.PrefetchScalarGridSpec(
        num_scalar_prefetch=2,
        grid=(b * nt,),
        in_specs=[tok, lanes, lanes,
                  pl.BlockSpec((1, 6, d), lambda i, *_: (jnp.where(i % nt == 0, b, i // nt), 0, 0)),
                  pl.BlockSpec((1, d), lambda i, *_: (0, 0)),
                  pl.BlockSpec(memory_space=pl.ANY)],
        out_specs=out_spec,
        scratch_shapes=[pltpu.VMEM((2, SORT_ROWS, d), F32), pltpu.SemaphoreType.DMA((2,))],
    )
    return pl.pallas_call(
        functools.partial(_combine_kernel, final=final),
        grid_spec=grid_spec,
        out_shape=out_shape,
        compiler_params=_params("arbitrary"),
    )(tabs["dst"], tabs["count"], xn, ls, rw, mod_l, final_gain.reshape(1, d), ys)


def _inproj_weight(w_in_l):
    d = w_in_l.shape[0]
    sizes = (ATT_Q, N_KV_HEADS * HEAD_DIM, N_KV_HEADS * HEAD_DIM, POOL_WIDTH,
             RET_W, RET_W, RET_W, RET_W, RET_W, d, d, d)
    parts, off = [], 0
    for sz in sizes:
        parts.append(w_in_l[:, off:off + sz])
        off += sz
    q, k, v, u, rq, rk, rv, rgf, rgb, ga, gp, gr = parts
    twice = lambda w: jnp.concatenate(
        [w[:, h * HEAD_DIM:(h + 1) * HEAD_DIM] for h in range(N_KV_HEADS) for _ in range(2)], axis=1)
    cols = [q * HEAD_DIM ** -0.5, twice(k), twice(v), rq, rk * RET_DIM ** -0.5, rv, rgf, rgb, u, ga, gp, gr]
    return jnp.concatenate(cols, axis=1).astype(BF16)


def _block_diag(blocks):
    n, r, c = blocks.shape
    out = jnp.zeros((n * r, n * c), blocks.dtype)
    for g in range(n):
        out = out.at[g * r:(g + 1) * r, g * c:(g + 1) * c].set(blocks[g])
    return out


def _routing_tables(cnt, n_blocks):
    ntt = cnt.shape[0] // N_EXPERTS
    counts = cnt.reshape(ntt, N_EXPERTS, LANE)[:, :, 0]
    run = (counts + SUBLANE - 1) // SUBLANE
    total = jnp.sum(run, axis=0)
    eb = EB // SUBLANE
    padded = (total + eb - 1) // eb * eb
    pad_end = jnp.cumsum(padded)
    pad_start = pad_end - padded
    off = pad_start[None, :] + jnp.cumsum(run, axis=0) - run
    run_end = jnp.cumsum(run, axis=1)
    j = jnp.arange(SORT_TILES)
    owner = jnp.sum(run_end[:, None, :] <= j[None, :, None], axis=2)
    mine = owner[:, :, None] == jnp.arange(N_EXPERTS)[None, None, :]
    in_region = j[None, :] + jnp.sum(jnp.where(mine, (off - (run_end - run))[:, None, :], 0), axis=2)
    dst = jnp.where(owner < N_EXPERTS, in_region, 0)
    n_used = pad_end[-1] // eb
    blk = jnp.minimum(jnp.arange(n_blocks), n_used - 1) * eb
    block_e = jnp.minimum(jnp.sum(pad_end[None, :] <= blk[:, None], axis=1), N_EXPERTS - 1)
    tabs = dict(dst=dst.reshape(-1).astype(I32), count=run_end[:, -1].astype(I32),
                fill_start=(pad_start + total).astype(I32),
                fill_n=(padded - total).astype(I32))
    return tabs, block_e.astype(I32), n_used.reshape(1).astype(I32)


def kernel(x, c, ctx, c_ctx, w_mod, b_mod, norm1, norm2, w_in, attn_sinks, pool_w, pool_scale, ret_decay,
           w_o_attn, w_o_pool, w_o_ret, w_out, w_router, b_router, w_expert_in, b_expert_in, w_expert_out,
           b_expert_out, final_norm):
    b, seq, d = x.shape
    depth = w_mod.shape[0]
    assert ctx.shape[1] == TM and seq % TM == 0 and seq % GRID_W == 0
    s = seq + TM
    xm = jnp.concatenate([ctx, x], axis=1)
    mod = _modulation(c, c_ctx, w_mod, b_mod)
    tables = _rope_tables(seq)
    ntt = b * s // TM
    n_blocks = -(-(b * s * TOP_K + ntt * N_EXPERTS * (SUBLANE - 1)) // EB) + N_EXPERTS
    avg = _block_diag(jnp.full((RET_HEADS, RET_DIM, RET_DIM), 1.0 / RET_DIM, F32)).astype(BF16)
    tri = (jnp.arange(TM)[:, None] <= jnp.arange(TM)[None, :]).astype(BF16)
    for l in range(depth):
        q, k2, v2, rq, rk, rv, rg, u, gates = _inproj(xm, mod[l], norm1[l], _inproj_weight(w_in[l]), tables)
        attn = _attention(q, k2, v2, attn_sinks[l])
        yf, yb = _retention(rq, rk, rv, jax.nn.log_sigmoid(ret_decay[l].astype(F32)))
        wr_hilo = jnp.concatenate(_split(w_router[l].T), axis=0)
        br = jnp.broadcast_to(b_router[l][:, None], (N_EXPERTS, TM))
        wts = (_block_diag(pool_w[l]).astype(BF16), pool_scale[l].reshape(1, -1),
               w_o_attn[l].astype(BF16), w_o_pool[l].astype(BF16), w_o_ret[l].astype(BF16),
               w_out[l].astype(BF16), norm2[l].reshape(1, d), wr_hilo, br, avg, tri)
        xn, h2, ls, lst, rw, cnt = _merge(xm, attn, u, yf, yb, rg, gates, mod[l], seq, wts)
        tabs, block_e, n_used = _routing_tables(cnt, n_blocks)
        xs = _dispatch(h2.reshape(b * s, d), lst, tabs, n_used, n_blocks)
        ys = _experts(xs, block_e, n_used, w_expert_in, b_expert_in, w_expert_out, b_expert_out, l)
        xm = _combine(xn, ls, rw, mod[l], final_norm, ys, tabs, final=l == depth - 1)
    return xm
```

```python
import functools

import jax
import jax.numpy as jnp
from jax import lax
from jax.experimental import pallas as pl
from jax.experimental.pallas import tpu as pltpu

F32 = jnp.float32
BF16 = jnp.bfloat16
I32 = jnp.int32

GRID_W = 64
HEAD_DIM = 64
N_Q_HEADS = 8
N_KV_HEADS = 2
WINDOW = 128
ROPE_THETA = 10000.0
POOL_WIDTH = 256
POOL_WINDOWS = (2, 4, 8, 16)
RET_HEADS = 4
RET_DIM = 64
N_EXPERTS = 32
TOP_K = 4
D_FF = 1024
SWIGLU_LIMIT = 7.0
SWIGLU_ALPHA = 1.702
NORM_EPS = 1e-6
GN_EPS = 1e-5

LANE = 128
SUBLANE = 8
MXU_N = 256
TM = 256
AB = 128
IN_TILE = 640
CTX_BLOCKS = TM // AB
Q_BLOCKS = 2
EB = 512
FF_CHUNK = 256
POOL_HALO = SUBLANE
SORT_ROWS = TM * TOP_K + N_EXPERTS * SUBLANE
SORT_TILES = SORT_ROWS // SUBLANE
MIN_TILES = TM * TOP_K // SUBLANE
NEG = -1e30
VMEM_LIMIT = 56 * 1024 * 1024

ATT_Q = N_Q_HEADS * HEAD_DIM
KV2 = 2 * N_KV_HEADS * HEAD_DIM
RET_W = RET_HEADS * RET_DIM


def _dot(a, b):
    return jnp.dot(a, b, preferred_element_type=F32)


def _dot_nt(a, b):
    return lax.dot_general(a, b, (((1,), (1,)), ((), ())), preferred_element_type=F32)


def _dot_tn(a, b):
    return lax.dot_general(a, b, (((0,), (0,)), ((), ())), preferred_element_type=F32)


def _split(a):
    hi = a.astype(BF16)
    lo = (a - hi.astype(F32)).astype(BF16)
    return hi, lo


def _dot_hilo(a, m):
    hi, lo = _split(a)
    return _dot(hi, m) + _dot(lo, m)


def _sigmoid(x):
    return 0.5 * jnp.tanh(0.5 * x) + 0.5


def _silu(x):
    return x * _sigmoid(x)


def _params(*sem):
    return pltpu.CompilerParams(dimension_semantics=sem, vmem_limit_bytes=VMEM_LIMIT)


def _mod_kernel(c_ref, w_ref, b_ref, o_ref):
    s = _silu(c_ref[...])
    sh, sl = _split(s)
    wh, wl = _split(w_ref[0])
    o_ref[0] = _dot(sh, wh) + _dot(sh, wl) + _dot(sl, wh) + b_ref[0]


def _modulation(c, c_ctx, w_mod, b_mod):
    depth, d, six_d = w_mod.shape
    b = c.shape[0]
    cc = jnp.zeros((SUBLANE, d), F32).at[:b].set(c).at[b].set(c_ctx)
    out = pl.pallas_call(
        _mod_kernel,
        grid=(depth, six_d // d),
        in_specs=[
            pl.BlockSpec((SUBLANE, d), lambda l, n: (0, 0)),
            pl.BlockSpec((1, d, d), lambda l, n: (l, 0, n)),
            pl.BlockSpec((1, 1, d), lambda l, n: (l, 0, n)),
        ],
        out_specs=pl.BlockSpec((1, SUBLANE, d), lambda l, n: (l, 0, n)),
        out_shape=jax.ShapeDtypeStruct((depth, SUBLANE, six_d), F32),
        compiler_params=_params("arbitrary", "arbitrary"),
    )(cc, w_mod, b_mod.reshape(depth, 1, six_d))
    return out[:, : b + 1].reshape(depth, b + 1, 6, d)


def _rope(x, cos, sin, half, first):
    partner = jnp.where(first, pltpu.roll(x, LANE - half, 1), pltpu.roll(x, half, 1))
    return x * cos + partner * sin


def _inproj_kernel(x_ref, mod_ref, modc_ref, n1_ref, w_ref, ac_ref, as_ref, rc_ref, rs_ref,
                   q_ref, k_ref, v_ref, rq_ref, rk_ref, rv_ref, rg_ref, u_ref, g_ref):
    x = x_ref[0]
    rows = x.shape[0]
    ms = jnp.mean(x * x, axis=-1, keepdims=True)
    y = x * lax.rsqrt(ms + NORM_EPS) * n1_ref[...]
    is_ctx = (lax.broadcasted_iota(I32, (rows, 1), 0) < TM) & (pl.program_id(1) == 0)
    scale = jnp.where(is_ctx, modc_ref[0, 1:2, :], mod_ref[0, 1:2, :])
    shift = jnp.where(is_ctx, modc_ref[0, 0:1, :], mod_ref[0, 0:1, :])
    hb = (y * (1.0 + scale) + shift).astype(BF16)
    lane = lax.broadcasted_iota(I32, (rows, LANE), 1)
    a_first = (lane % (HEAD_DIM // 2)) < (HEAD_DIM // 4)
    r_first = (lane % RET_DIM) < (RET_DIM // 2)
    ac, asn, rc, rsn = ac_ref[...], as_ref[...], rc_ref[...], rs_ref[...]

    def proj(off, width):
        return _dot(hb, w_ref[:, off:off + width])

    def rotated(ref, width, cos, sin, half, first, off):
        for t in range(0, width, MXU_N):
            pr = proj(off + t, MXU_N)
            for g in range(0, MXU_N, LANE):
                ref[0, :, t + g:t + g + LANE] = _rope(pr[:, g:g + LANE], cos, sin, half, first).astype(BF16)

    off = 0
    rotated(q_ref, ATT_Q, ac, asn, HEAD_DIM // 4, a_first, off)
    off += ATT_Q
    rotated(k_ref, KV2, ac, asn, HEAD_DIM // 4, a_first, off)
    off += KV2
    v_ref[0] = proj(off, KV2).astype(BF16)
    off += KV2
    for ref in (rq_ref, rk_ref):
        rotated(ref, RET_W, rc, rsn, RET_DIM // 2, r_first, off)
        off += RET_W
    rv_ref[0] = proj(off, RET_W).astype(BF16)
    off += RET_W
    rg_ref[0] = proj(off, 2 * RET_W).astype(BF16)
    off += 2 * RET_W
    u_ref[0] = proj(off, POOL_WIDTH)
    off += POOL_WIDTH
    d = x.shape[-1]
    for t in range(3):
        g_ref[0, :, t * d:(t + 1) * d] = proj(off, d).astype(BF16)
        off += d


def _inproj(xm, mod_l, norm1_l, w1, tables):
    b, s, d = xm.shape
    ti = IN_TILE if s % IN_TILE == 0 else TM
    wcols = w1.shape[1]
    tok = lambda width: pl.BlockSpec((1, ti, width), lambda bi, j: (bi, j, 0))
    tab = pl.BlockSpec((ti, LANE), lambda bi, j: (j, 0))
    widths = (ATT_Q, KV2, KV2, RET_W, RET_W, RET_W, 2 * RET_W, POOL_WIDTH, 3 * d)
    dtypes = (BF16,) * 7 + (F32, BF16)
    return pl.pallas_call(
        _inproj_kernel,
        grid=(b, s // ti),
        in_specs=[
            tok(d),
            pl.BlockSpec((1, 6, d), lambda bi, j: (bi, 0, 0)),
            pl.BlockSpec((1, 6, d), lambda bi, j: (b, 0, 0)),
            pl.BlockSpec((1, d), lambda bi, j: (0, 0)),
            pl.BlockSpec((d, wcols), lambda bi, j: (0, 0), pipeline_mode=pl.Buffered(1)),
            tab, tab, tab, tab,
        ],
        out_specs=[tok(w) for w in widths],
        out_shape=[jax.ShapeDtypeStruct((b, s, w), dt) for w, dt in zip(widths, dtypes)],
        compiler_params=_params("arbitrary", "arbitrary"),
    )(xm, mod_l, mod_l, norm1_l.reshape(1, d), w1, *tables)


def _rope_tables(seq):
    rows = seq // GRID_W
    rpos = jnp.arange(rows, dtype=F32)[:, None]
    cpos = jnp.arange(GRID_W, dtype=F32)[:, None]
    lane = jnp.arange(LANE)
    grid = lambda per_row, per_col: (per_row[:, None, :] + per_col[None, :, :]).reshape(seq, LANE)
    axis_dim = HEAD_DIM // 2
    inv_a = ROPE_THETA ** (-jnp.arange(0, axis_dim, 2, dtype=F32) / axis_dim)
    hl = lane % HEAD_DIM
    inv_al = inv_a[(hl % axis_dim) % (axis_dim // 2)][None, :]
    by_row = (hl < axis_dim)[None, :]
    a_sign = jnp.where((hl % axis_dim) < axis_dim // 2, -1.0, 1.0)[None, :]
    a_cos = grid(jnp.where(by_row, jnp.cos(rpos * inv_al), 0.0), jnp.where(by_row, 0.0, jnp.cos(cpos * inv_al)))
    a_sin = grid(jnp.where(by_row, jnp.sin(rpos * inv_al), 0.0), jnp.where(by_row, 0.0, jnp.sin(cpos * inv_al)))
    inv_r = 1.0 / (ROPE_THETA ** jnp.linspace(0.0, 1.0, RET_DIM // 2, dtype=F32))
    rl = lane % RET_DIM
    inv_rl = inv_r[rl % (RET_DIM // 2)][None, :]
    r_sign = jnp.where(rl < RET_DIM // 2, -1.0, 1.0)[None, :]
    hi = (rpos * GRID_W) * inv_rl
    lo = cpos * inv_rl
    outer = lambda a, b: (a[:, None, :] * b[None, :, :]).reshape(seq, LANE)
    r_cos = outer(jnp.cos(hi), jnp.cos(lo)) - outer(jnp.sin(hi), jnp.sin(lo))
    r_sin = outer(jnp.sin(hi), jnp.cos(lo)) + outer(jnp.cos(hi), jnp.sin(lo))
    ones = jnp.ones((TM, LANE), F32)
    zeros = jnp.zeros((TM, LANE), F32)
    cat = lambda head, body: jnp.concatenate([head, body], axis=0)
    return (cat(ones, a_cos), cat(zeros, a_sin * a_sign), cat(ones, r_cos), cat(zeros, r_sin * r_sign))


def _attn_kernel(sink_ref, bias0_ref, bias1_ref, q_ref, k0_ref, k1_ref, k2_ref, k3_ref, kx_ref,
                 v0_ref, v1_ref, v2_ref, v3_ref, vx_ref, o_ref):
    g_heads = N_Q_HEADS // N_KV_HEADS
    lane = lax.broadcasted_iota(I32, (AB, LANE), 1)
    lo = lane < HEAD_DIM
    rows = lax.broadcasted_iota(I32, (g_heads * AB, 1), 0)
    k_refs = (k0_ref, k1_ref, k2_ref, k3_ref)
    v_refs = (v0_ref, v1_ref, v2_ref, v3_ref)
    biases = (bias0_ref[0], bias1_ref[0])
    chains = [(t, g) for t in range(Q_BLOCKS) for g in range(N_KV_HEADS)]
    scores, sinks, values = [], [], []
    for t, g in chains:
        ks = slice(g * LANE, (g + 1) * LANE)
        kd = jnp.concatenate([r[0, :, ks] for r in k_refs[t:t + 3]] + [kx_ref[0, :, ks]], axis=0)
        values.append(jnp.concatenate([r[0, :, ks] for r in v_refs[t:t + 3]] + [vx_ref[0, :, ks]], axis=0))
        qs = []
        for c in range(2 * g, 2 * g + 2):
            qc = q_ref[0, t * AB:(t + 1) * AB, c * LANE:(c + 1) * LANE]
            zero = jnp.zeros_like(qc)
            qs += [jnp.where(lo, qc, zero), jnp.where(lo, zero, qc)]
        scores.append(_dot_nt(jnp.concatenate(qs, axis=0), kd) + biases[t])
        sink = jnp.full((g_heads * AB, 1), sink_ref[g_heads * g + g_heads - 1], F32)
        for h in range(g_heads - 2, -1, -1):
            sink = jnp.where(rows < (h + 1) * AB, sink_ref[g_heads * g + h], sink)
        sinks.append(sink)
    for (t, g), s, sink, vd in zip(chains, scores, sinks, values):
        m = jnp.maximum(jnp.max(s, axis=-1, keepdims=True), sink)
        p = jnp.exp(s - m)
        den = jnp.sum(p, axis=-1, keepdims=True) + jnp.exp(sink - m)
        o = _dot(p.astype(BF16), vd) / den
        for h in range(2):
            c = 2 * g + h
            o_ref[0, t * AB:(t + 1) * AB, c * LANE:(c + 1) * LANE] = jnp.where(
                lo, o[2 * h * AB:(2 * h + 1) * AB], o[(2 * h + 1) * AB:(2 * h + 2) * AB]).astype(BF16)


def _attn_bias():
    g_heads = N_Q_HEADS // N_KV_HEADS
    r = jnp.arange(g_heads * AB)[:, None] % AB
    j = jnp.arange(3 * AB + TM)[None, :]
    band = jnp.abs(j - AB - r) <= WINDOW
    is_ctx = j >= 3 * AB
    variants = (band, band & (j >= AB), band & (j < 2 * AB), jnp.zeros_like(band))
    return jnp.stack([jnp.where(v | is_ctx, 0.0, NEG) for v in variants]).astype(F32)


def _attention(q, k2, v2, sinks):
    b, s, _ = q.shape
    nb = s // AB
    assert nb - CTX_BLOCKS >= 2 and nb % Q_BLOCKS == 0 and CTX_BLOCKS % Q_BLOCKS == 0
    bias = _attn_bias()

    def variant(blk):
        return jnp.where(blk < CTX_BLOCKS, 3, jnp.where(blk == CTX_BLOCKS, 1, jnp.where(blk == nb - 1, 2, 0)))

    def kv(offset):
        return pl.BlockSpec((1, AB, KV2), lambda bi, i: (bi, jnp.clip(Q_BLOCKS * i + offset, 0, nb - 1), 0))

    def bias_spec(t):
        return pl.BlockSpec((1,) + bias.shape[1:], lambda bi, i: (variant(Q_BLOCKS * i + t), 0, 0))

    cx = pl.BlockSpec((1, TM, KV2), lambda bi, i: (bi, 0, 0))
    kvs = [kv(o) for o in range(-1, Q_BLOCKS + 1)] + [cx]
    qo = pl.BlockSpec((1, Q_BLOCKS * AB, ATT_Q), lambda bi, i: (bi, i, 0))
    return pl.pallas_call(
        _attn_kernel,
        grid=(b, nb // Q_BLOCKS),
        in_specs=[pl.BlockSpec(memory_space=pltpu.SMEM), bias_spec(0), bias_spec(1), qo] + kvs + kvs,
        out_specs=qo,
        out_shape=jax.ShapeDtypeStruct((b, s, ATT_Q), BF16),
        compiler_params=_params("arbitrary", "arbitrary"),
    )(sinks, bias, bias, q, *([k2] * len(kvs)), *([v2] * len(kvs)))


def _ret_kernel(lg_ref, qf_ref, kf_ref, vf_ref, qb_ref, kb_ref, vb_ref, yf_ref, yb_ref,
                st_ref, dm_ref, qd_ref, kd_ref, cd_ref, *, batch):
    step = pl.program_id(0)
    lane = lax.broadcasted_iota(I32, (AB, LANE), 1)
    row = lax.broadcasted_iota(I32, (AB, LANE), 0)
    lo = lane < RET_DIM
    tiles = RET_W // LANE

    @pl.when(step == 0)
    def _():
        st_ref[...] = jnp.zeros_like(st_ref)
        ii = row.astype(F32)
        jj = lane.astype(F32)
        for d in range(2):
            for c in range(tiles):
                lg0 = lg_ref[d * RET_HEADS + 2 * c]
                lg1 = lg_ref[d * RET_HEADS + 2 * c + 1]
                lgl = jnp.where(lo, lg0, lg1)
                q_exp = ii + 1.0 if d == 0 else AB - ii
                k_exp = (AB - 1.0) - ii if d == 0 else ii
                qd_ref[d * tiles + c] = jnp.exp(q_exp * lgl)
                kd_ref[d * tiles + c] = jnp.exp(k_exp * lgl)
                cd_ref[d * tiles + c] = jnp.exp(AB * lgl)
                rel = ii - jj if d == 0 else jj - ii
                for hh, lgh in enumerate((lg0, lg1)):
                    dm_ref[d * tiles + c, hh * AB:(hh + 1) * AB, :] = jnp.where(
                        rel >= 0, jnp.exp(jnp.maximum(rel, 0.0) * lgh), 0.0)

    same_head = (row < RET_DIM) == lo
    dirs = ((qf_ref, kf_ref, vf_ref, yf_ref), (qb_ref, kb_ref, vb_ref, yb_ref))
    chains = [(d, b, c) for d in range(2) for b in range(batch) for c in range(tiles)]

    def operands(d, b, c):
        q_ref, k_ref, v_ref, _ = dirs[d]
        sl = slice(c * LANE, (c + 1) * LANE)
        return q_ref[b, :, sl], k_ref[b, :, sl], v_ref[b, :, sl]

    probs = []
    for d, b, c in chains:
        q, k, _ = operands(d, b, c)
        zero = jnp.zeros_like(q)
        q2 = jnp.concatenate([jnp.where(lo, q, zero), jnp.where(lo, zero, q)], axis=0)
        probs.append((_dot_nt(q2, k) * dm_ref[d * tiles + c]).astype(BF16))
    for (d, b, c), p in zip(chains, probs):
        q, _, v = operands(d, b, c)
        t = d * tiles + c
        si = (d * batch + b) * tiles + c
        y_intra = jnp.where(lo, _dot(p[:AB], v), _dot(p[AB:], v))
        q_dec = (q.astype(F32) * qd_ref[t]).astype(BF16)
        dirs[d][3][b, :, c * LANE:(c + 1) * LANE] = (y_intra + _dot(q_dec, st_ref[si].astype(BF16))).astype(BF16)
    for d, b, c in chains:
        _, k, v = operands(d, b, c)
        t = d * tiles + c
        si = (d * batch + b) * tiles + c
        k_dec = (k.astype(F32) * kd_ref[t]).astype(BF16)
        st_ref[si] = st_ref[si] * cd_ref[t] + jnp.where(same_head, _dot_tn(k_dec, v), 0.0)


def _retention(rq, rk, rv, log_g):
    b, s, w = rq.shape
    nb = s // AB
    tiles = w // LANE

    def back(i):
        return jnp.where(i < CTX_BLOCKS, CTX_BLOCKS - 1 - i, nb - 1 + CTX_BLOCKS - i)

    fwd = pl.BlockSpec((b, AB, w), lambda i: (0, i, 0))
    bwd = pl.BlockSpec((b, AB, w), lambda i: (0, back(i), 0))
    return pl.pallas_call(
        functools.partial(_ret_kernel, batch=b),
        grid=(nb,),
        in_specs=[pl.BlockSpec(memory_space=pltpu.SMEM), fwd, fwd, fwd, bwd, bwd, bwd],
        out_specs=[fwd, bwd],
        out_shape=[jax.ShapeDtypeStruct((b, s, w), BF16)] * 2,
        scratch_shapes=[
            pltpu.VMEM((2 * b * tiles, LANE, LANE), F32),
            pltpu.VMEM((2 * tiles, 2 * AB, LANE), F32),
            pltpu.VMEM((2 * tiles, AB, LANE), F32),
            pltpu.VMEM((2 * tiles, AB, LANE), F32),
            pltpu.VMEM((2 * tiles, AB, LANE), F32),
        ],
        compiler_params=_params("arbitrary"),
    )(log_g.reshape(-1), rq, rk, rv, rq, rk, rv)


def _route_tile(logits, tri_ref, ls_ref, lst_ref, rw_ref, cnt_ref):
    ne = N_EXPERTS
    eidx = lax.broadcasted_iota(I32, (ne, TM), 0).astype(F32)
    tops, hots = [], []
    for _ in range(TOP_K):
        mx = jnp.max(logits, axis=0, keepdims=True)
        idx = jnp.min(jnp.where(logits == mx, eidx, float(ne)), axis=0, keepdims=True)
        hot = eidx == idx
        logits = jnp.where(hot, NEG * 2.0, logits)
        tops.append(mx)
        hots.append(hot)
    ex = [jnp.exp(mx - tops[0]) for mx in tops]
    tot = ex[0]
    for e in ex[1:]:
        tot = tot + e
    sel = jnp.zeros((ne, TM), F32)
    for hot in hots:
        sel = jnp.where(hot, 1.0, sel)
    incl = _dot(sel.astype(BF16), tri_ref[...])
    cnt = incl[:, TM - 1:TM]
    run = jnp.floor((cnt + (SUBLANE - 1.0)) * (1.0 / SUBLANE)) * SUBLANE
    ends = jnp.broadcast_to(run, (ne, LANE))
    erow = lax.broadcasted_iota(I32, (ne, LANE), 0)
    shift = 1
    while shift < ne:
        ends = ends + jnp.where(erow >= shift, pltpu.roll(ends, shift, 0), 0.0)
        shift *= 2
    slot = (ends[:, :1] - run) + incl - 1.0
    ls_rows = [jnp.sum(jnp.where(hot, slot, 0.0), axis=0, keepdims=True) for hot in hots]
    rw_rows = [e / tot for e in ex]
    lst = jnp.concatenate(ls_rows + [jnp.full((SUBLANE - TOP_K, TM), -1.0, F32)], axis=0)
    lst_ref[...] = lst.astype(I32)
    ls_ref[...] = jnp.transpose(
        jnp.concatenate([lst, jnp.full((LANE - SUBLANE, TM), -1.0, F32)], axis=0)).astype(I32)
    rw_ref[...] = jnp.transpose(jnp.concatenate(rw_rows + [jnp.zeros((LANE - TOP_K, TM), F32)], axis=0))
    cnt_ref[...] = jnp.broadcast_to(cnt, cnt_ref.shape).astype(I32)


def _merge_kernel(x_ref, attn_ref, up_ref, uc_ref, un_ref, yf_ref, yb_ref, rg_ref, g_ref, mod_ref,
                  wbd_ref, ps_ref, woa_ref, wop_ref, wor_ref, wout_ref, n2_ref, wr_ref,
                  br_ref, avg_ref, tri_ref,
                  xo_ref, h2_ref, ls_ref, lst_ref, rw_ref, cnt_ref, lg_ref, *, seq, nt, ntt):
    i = pl.program_id(0)
    j = jnp.minimum(i, ntt - 1) % nt

    @pl.when(i == 0)
    def _():
        lg_ref[...] = jnp.zeros_like(lg_ref)

    _route_tile(lg_ref[...], tri_ref, ls_ref, lst_ref, rw_ref, cnt_ref)

    ext = TM + 2 * POOL_HALO
    seq_len = jnp.where(j == 0, TM, seq)
    start = jnp.where(j == 0, 0, (j - 1) * TM)
    u = uc_ref[0]
    ue = jnp.concatenate([up_ref[0], u, un_ref[0]], axis=0)
    erow = lax.broadcasted_iota(I32, (ext, 1), 0) + (start - POOL_HALO)
    ue = jnp.where((erow >= 0) & (erow < seq_len), ue, 0.0)
    t_pos = lax.broadcasted_iota(I32, (TM, 1), 0) + start
    glane = lax.broadcasted_iota(I32, (TM, POOL_WIDTH), 1) // (POOL_WIDTH // len(POOL_WINDOWS))
    run = ue
    width = 1
    diff = jnp.zeros((TM, POOL_WIDTH), F32)
    for gi, w in enumerate(POOL_WINDOWS):
        while width < w:
            run = run + pltpu.roll(run, ext - width, 0)
            width *= 2
        win = pltpu.roll(run, w // 2, 0)[POOL_HALO:POOL_HALO + TM]
        cnt = jnp.minimum(t_pos - w // 2 + w, seq_len) - jnp.maximum(t_pos - w // 2, 0)
        diff = jnp.where(glane == gi, win / cnt.astype(F32) - u, diff)
    pool = _dot(diff.astype(BF16), wbd_ref[...]) * ps_ref[...]

    def head_norm(y):
        dlt = y.astype(F32) - _dot(y, avg_ref[...])
        var = _dot_hilo(dlt * dlt, avg_ref[...])
        return dlt * lax.rsqrt(var + GN_EPS)

    rg = rg_ref[0].astype(F32)
    ret = head_norm(yf_ref[0]) * _silu(rg[:, :RET_W]) + head_norm(yb_ref[0]) * _silu(rg[:, RET_W:])

    d = x_ref.shape[-1]
    gate = lambda t: _sigmoid(g_ref[0, :, t * d:(t + 1) * d])
    m = (gate(0) * _dot(attn_ref[0], woa_ref[...]).astype(BF16)
         + gate(1) * _dot(pool.astype(BF16), wop_ref[...]).astype(BF16)
         + gate(2) * _dot(ret.astype(BF16), wor_ref[...]).astype(BF16))
    xn = x_ref[0] + mod_ref[0, 2:3, :] * _dot(m, wout_ref[...])
    xo_ref[0] = xn

    ms = jnp.mean(xn * xn, axis=-1, keepdims=True)
    h2 = xn * lax.rsqrt(ms + NORM_EPS) * n2_ref[...] * (1.0 + mod_ref[0, 4:5, :]) + mod_ref[0, 3:4, :]
    h2_ref[0] = h2.astype(BF16)
    hh, hl = _split(h2)
    ne = N_EXPERTS
    full = _dot_nt(wr_ref[...], hh)
    logits = full[:ne] + full[ne:] + _dot_nt(wr_ref[:ne], hl) + br_ref[...]
    lg_ref[...] = logits


def _merge(xm, attn, u, yf, yb, rg, gates, mod_l, seq, wts):
    b, s, d = xm.shape
    nt = s // TM
    ntt = b * nt
    hb = TM // POOL_HALO
    nh = s // POOL_HALO
    cur = lambda i: jnp.minimum(i, ntt - 1)
    lag = lambda i: jnp.maximum(i - 1, 0)
    tok = lambda width: pl.BlockSpec((1, TM, width), lambda i: (cur(i) // nt, cur(i) % nt, 0))
    full = lambda a: pl.BlockSpec(a.shape, lambda i: (0,) * a.ndim)
    lanes = pl.BlockSpec((TM, LANE), lambda i: (lag(i), 0))
    in_specs = [
        tok(d), tok(ATT_Q),
        pl.BlockSpec((1, POOL_HALO, POOL_WIDTH),
                     lambda i: (cur(i) // nt, jnp.maximum((cur(i) % nt) * hb - 1, 0), 0)),
        tok(POOL_WIDTH),
        pl.BlockSpec((1, POOL_HALO, POOL_WIDTH),
                     lambda i: (cur(i) // nt, jnp.minimum((cur(i) % nt + 1) * hb, nh - 1), 0)),
        tok(RET_W), tok(RET_W), tok(2 * RET_W), tok(3 * d),
        pl.BlockSpec((1, 6, d), lambda i: (jnp.where(cur(i) % nt == 0, b, cur(i) // nt), 0, 0)),
    ] + [full(a) for a in wts]
    return pl.pallas_call(
        functools.partial(_merge_kernel, seq=seq, nt=nt, ntt=ntt),
        grid=(ntt + 1,),
        in_specs=in_specs,
        out_specs=[tok(d), tok(d), lanes, pl.BlockSpec((SUBLANE, TM), lambda i: (lag(i), 0)), lanes,
                   pl.BlockSpec((N_EXPERTS, LANE), lambda i: (lag(i), 0))],
        out_shape=[jax.ShapeDtypeStruct((b, s, d), F32), jax.ShapeDtypeStruct((b, s, d), BF16),
                   jax.ShapeDtypeStruct((b * s, LANE), I32), jax.ShapeDtypeStruct((ntt * SUBLANE, TM), I32),
                   jax.ShapeDtypeStruct((b * s, LANE), F32), jax.ShapeDtypeStruct((ntt * N_EXPERTS, LANE), I32)],
        scratch_shapes=[pltpu.VMEM((N_EXPERTS, TM), F32)],
        compiler_params=_params("arbitrary"),
    )(xm, attn, u, u, u, yf, yb, rg, gates, mod_l, *wts)


def _rows(tile_index):
    return pl.ds(pl.multiple_of(tile_index * SUBLANE, SUBLANE), SUBLANE)


def _tile_loop(count_ref, tile, fn):
    def body(j, carry):
        fn(j)
        return carry

    lax.fori_loop(0, MIN_TILES, body, 0, unroll=8)
    lax.fori_loop(MIN_TILES, count_ref[tile], body, 0)


def _tile_copies(table_ref, count_ref, tile, copy):
    _tile_loop(count_ref, tile, lambda j: copy(j, table_ref[tile * SORT_TILES + j]))


def _dispatch_kernel(dst_ref, cnt_ref, fs_ref, fn_ref, nu_ref, h_ref, lst_ref, xs_hbm, buf, zbuf, sem, fill_sem,
                     *, n_blocks):
    i = pl.program_id(0)
    last = pl.num_programs(0) - 1
    cur = i % 2

    def send(tile, half):
        _tile_copies(dst_ref, cnt_ref, tile, lambda j, t: pltpu.make_async_copy(
            buf.at[half, _rows(j)], xs_hbm.at[_rows(t)], sem.at[half]).start())

    def drain(tile, half):
        _tile_loop(cnt_ref, tile, lambda j: pltpu.make_async_copy(
            buf.at[half, _rows(j)], xs_hbm.at[_rows(j)], sem.at[half]).wait())

    @pl.when(i == 0)
    def _():
        zbuf[...] = jnp.zeros_like(zbuf)

        def pad_copy(e, c):
            return pltpu.make_async_copy(zbuf.at[pl.ds(0, SUBLANE)], xs_hbm.at[_rows(fs_ref[e] + c)], fill_sem)

        def blk_copy(blk):
            return pltpu.make_async_copy(zbuf, xs_hbm.at[pl.ds(pl.multiple_of(blk * EB, EB), EB)], fill_sem)

        def per_expert(fn):
            def outer(e, carry):
                def inner(c, cc):
                    fn(e, c)
                    return cc
                return lax.fori_loop(0, fn_ref[e], inner, carry)
            lax.fori_loop(0, N_EXPERTS, outer, 0)

        def per_block(fn):
            def body(blk, carry):
                fn(blk)
                return carry
            lax.fori_loop(nu_ref[0], n_blocks, body, 0)

        per_expert(lambda e, c: pad_copy(e, c).start())
        per_block(lambda blk: blk_copy(blk).start())
        per_expert(lambda e, c: pad_copy(e, c).wait())
        per_block(lambda blk: blk_copy(blk).wait())

    slot = lax.broadcasted_iota(I32, (SORT_ROWS, TM), 0)
    lst = lst_ref[...]
    p = jnp.zeros((SORT_ROWS, TM), F32)
    for k in range(TOP_K):
        p = jnp.where(slot == lst[k:k + 1, :], 1.0, p)
    srt = _dot(p.astype(BF16), h_ref[...])

    @pl.when(i >= 2)
    def _():
        drain(i - 2, cur)

    buf[cur] = srt
    send(i, cur)

    @pl.when(i == last)
    def _():
        @pl.when(i >= 1)
        def _():
            drain(i - 1, 1 - cur)
        drain(i, cur)


def _dispatch(h2, lst, tabs, n_used, n_blocks):
    t, d = h2.shape
    ntt = t // TM
    grid_spec = pltpu.PrefetchScalarGridSpec(
        num_scalar_prefetch=5,
        grid=(ntt,),
        in_specs=[pl.BlockSpec((TM, d), lambda i, *_: (i, 0)),
                  pl.BlockSpec((SUBLANE, TM), lambda i, *_: (i, 0))],
        out_specs=pl.BlockSpec(memory_space=pl.ANY),
        scratch_shapes=[pltpu.VMEM((2, SORT_ROWS, d), F32), pltpu.VMEM((EB, d), F32),
                        pltpu.SemaphoreType.DMA((2,)), pltpu.SemaphoreType.DMA],
    )
    return pl.pallas_call(
        functools.partial(_dispatch_kernel, n_blocks=n_blocks),
        grid_spec=grid_spec,
        out_shape=jax.ShapeDtypeStruct((n_blocks * EB, d), F32),
        compiler_params=_params("arbitrary"),
    )(tabs["dst"], tabs["count"], tabs["fill_start"], tabs["fill_n"], n_used, h2, lst)


def _expert_kernel(be_ref, nu_ref, x_ref, w1_ref, b1_ref, w2_ref, b2_ref, y_ref, w1b, w2b):
    bi = pl.program_id(0)
    used = bi < nu_ref[0]
    fresh = (bi == 0) | (be_ref[bi] != be_ref[jnp.maximum(bi - 1, 0)])

    @pl.when(used & fresh)
    def _():
        w1b[...] = w1_ref[0, 0].astype(BF16)
        w2b[...] = w2_ref[0, 0].astype(BF16)

    @pl.when(jnp.logical_not(used))
    def _():
        y_ref[...] = jnp.zeros_like(y_ref)

    @pl.when(used)
    def _():
        xb = x_ref[...].astype(BF16)
        y = None
        for c in range(0, D_FF, FF_CHUNK):
            glu = _dot(xb, w1b[:, c:c + FF_CHUNK]) + b1_ref[0, 0, :, c:c + FF_CHUNK]
            lin = _dot(xb, w1b[:, D_FF + c:D_FF + c + FF_CHUNK]) + b1_ref[0, 0, :, D_FF + c:D_FF + c + FF_CHUNK]
            glu = jnp.minimum(glu, SWIGLU_LIMIT)
            lin = jnp.clip(lin, -SWIGLU_LIMIT, SWIGLU_LIMIT)
            act = glu * _sigmoid(SWIGLU_ALPHA * glu) * (lin + 1.0)
            part = _dot(act.astype(BF16), w2b[c:c + FF_CHUNK, :])
            y = part if y is None else y + part
        y_ref[...] = y + b2_ref[0, 0]


def _experts(xs, block_e, n_used, w1, b1, w2, b2, layer):
    rows, d = xs.shape
    depth, ne, _, f2 = w1.shape
    row_blk = lambda bi, be, nu: (bi, 0)
    used_blk = lambda bi, be, nu: (jnp.minimum(bi, nu[0] - 1), 0)
    grid_spec = pltpu.PrefetchScalarGridSpec(
        num_scalar_prefetch=2,
        grid=(rows // EB,),
        in_specs=[
            pl.BlockSpec((EB, d), used_blk),
            pl.BlockSpec((1, 1, d, f2), lambda bi, be, nu: (layer, be[bi], 0, 0)),
            pl.BlockSpec((1, 1, 1, f2), lambda bi, be, nu: (layer, be[bi], 0, 0)),
            pl.BlockSpec((1, 1, f2 // 2, d), lambda bi, be, nu: (layer, be[bi], 0, 0)),
            pl.BlockSpec((1, 1, 1, d), lambda bi, be, nu: (layer, be[bi], 0, 0)),
        ],
        out_specs=pl.BlockSpec((EB, d), row_blk),
        scratch_shapes=[pltpu.VMEM((d, f2), BF16), pltpu.VMEM((f2 // 2, d), BF16)],
    )
    return pl.pallas_call(
        _expert_kernel,
        grid_spec=grid_spec,
        out_shape=jax.ShapeDtypeStruct((rows, d), F32),
        compiler_params=_params("arbitrary"),
    )(block_e, n_used, xs, w1, b1.reshape(depth, ne, 1, f2), w2, b2.reshape(depth, ne, 1, d))


def _combine_kernel(dst_ref, cnt_ref, x_ref, ls_ref, rw_ref, mod_ref, fg_ref, ys_hbm, xo_ref, buf, sem, *, final):
    i = pl.program_id(0)
    last = pl.num_programs(0) - 1
    cur = i % 2

    def fetch(tile, half):
        _tile_copies(dst_ref, cnt_ref, tile, lambda j, t: pltpu.make_async_copy(
            ys_hbm.at[_rows(t)], buf.at[half, _rows(j)], sem.at[half]).start())

    @pl.when(i == 0)
    def _():
        buf[...] = jnp.zeros_like(buf)
        fetch(0, 0)

    @pl.when(i < last)
    def _():
        fetch(i + 1, 1 - cur)

    _tile_loop(cnt_ref, i, lambda j: pltpu.make_async_copy(
        ys_hbm.at[_rows(j)], buf.at[cur, _rows(j)], sem.at[cur]).wait())
    yb = buf[cur].astype(BF16)
    slot = lax.broadcasted_iota(I32, (TM, SORT_ROWS), 1)
    ls = ls_ref[...]
    rw = rw_ref[...]
    g = jnp.zeros((TM, SORT_ROWS), F32)
    for k in range(TOP_K):
        g = jnp.where(slot == ls[:, k:k + 1], rw[:, k:k + 1], g)
    xn = x_ref[0] + mod_ref[0, 5:6, :] * _dot(g.astype(BF16), yb)
    if final:
        ms = jnp.mean(xn * xn, axis=-1, keepdims=True)
        xn = xn * lax.rsqrt(ms + NORM_EPS) * fg_ref[...]
    xo_ref[0] = xn


def _combine(xn, ls, rw, mod_l, final_gain, ys, tabs, final):
    b, s, d = xn.shape
    nt = s // TM
    tok = pl.BlockSpec((1, TM, d), lambda i, *_: (i // nt, i % nt, 0))
    lanes = pl.BlockSpec((TM, LANE), lambda i, *_: (i, 0))
    if final:
        out_spec = pl.BlockSpec((1, TM, d), lambda i, *_: (i // nt, jnp.maximum(i % nt - 1, 0), 0))
        out_shape = jax.ShapeDtypeStruct((b, s - TM, d), F32)
    else:
        out_spec, out_shape = tok, jax.ShapeDtypeStruct((b, s, d), F32)
    grid_spec = pltpu.PrefetchScalarGridSpec(
        num_scalar_prefetch=2,
        grid=(b * nt,),
        in_specs=[tok, lanes, lanes,
                  pl.BlockSpec((1, 6, d), lambda i, *_: (jnp.where(i % nt == 0, b, i // nt), 0, 0)),
                  pl.BlockSpec((1, d), lambda i, *_: (0, 0)),
                  pl.BlockSpec(memory_space=pl.ANY)],
        out_specs=out_spec,
        scratch_shapes=[pltpu.VMEM((2, SORT_ROWS, d), F32), pltpu.SemaphoreType.DMA((2,))],
    )
    return pl.pallas_call(
        functools.partial(_combine_kernel, final=final),
        grid_spec=grid_spec,
        out_shape=out_shape,
        compiler_params=_params("arbitrary"),
    )(tabs["dst"], tabs["count"], xn, ls, rw, mod_l, final_gain.reshape(1, d), ys)


def _inproj_weight(w_in_l):
    d = w_in_l.shape[0]
    sizes = (ATT_Q, N_KV_HEADS * HEAD_DIM, N_KV_HEADS * HEAD_DIM, POOL_WIDTH,
             RET_W, RET_W, RET_W, RET_W, RET_W, d, d, d)
    parts, off = [], 0
    for sz in sizes:
        parts.append(w_in_l[:, off:off + sz])
        off += sz
    q, k, v, u, rq, rk, rv, rgf, rgb, ga, gp, gr = parts
    twice = lambda w: jnp.concatenate(
        [w[:, h * HEAD_DIM:(h + 1) * HEAD_DIM] for h in range(N_KV_HEADS) for _ in range(2)], axis=1)
    cols = [q * HEAD_DIM ** -0.5, twice(k), twice(v), rq, rk * RET_DIM ** -0.5, rv, rgf, rgb, u, ga, gp, gr]
    return jnp.concatenate(cols, axis=1).astype(BF16)


def _block_diag(blocks):
    n, r, c = blocks.shape
    out = jnp.zeros((n * r, n * c), blocks.dtype)
    for g in range(n):
        out = out.at[g * r:(g + 1) * r, g * c:(g + 1) * c].set(blocks[g])
    return out


def _routing_tables(cnt, n_blocks):
    ntt = cnt.shape[0] // N_EXPERTS
    counts = cnt.reshape(ntt, N_EXPERTS, LANE)[:, :, 0]
    run = (counts + SUBLANE - 1) // SUBLANE
    total = jnp.sum(run, axis=0)
    eb = EB // SUBLANE
    padded = (total + eb - 1) // eb * eb
    pad_end = jnp.cumsum(padded)
    pad_start = pad_end - padded
    off = pad_start[None, :] + jnp.cumsum(run, axis=0) - run
    run_end = jnp.cumsum(run, axis=1)
    j = jnp.arange(SORT_TILES)
    owner = jnp.sum(run_end[:, None, :] <= j[None, :, None], axis=2)
    mine = owner[:, :, None] == jnp.arange(N_EXPERTS)[None, None, :]
    in_region = j[None, :] + jnp.sum(jnp.where(mine, (off - (run_end - run))[:, None, :], 0), axis=2)
    dst = jnp.where(owner < N_EXPERTS, in_region, 0)
    n_used = pad_end[-1] // eb
    blk = jnp.minimum(jnp.arange(n_blocks), n_used - 1) * eb
    block_e = jnp.minimum(jnp.sum(pad_end[None, :] <= blk[:, None], axis=1), N_EXPERTS - 1)
    tabs = dict(dst=dst.reshape(-1).astype(I32), count=run_end[:, -1].astype(I32),
                fill_start=(pad_start + total).astype(I32),
                fill_n=(padded - total).astype(I32))
    return tabs, block_e.astype(I32), n_used.reshape(1).astype(I32)


def kernel(x, c, ctx, c_ctx, w_mod, b_mod, norm1, norm2, w_in, attn_sinks, pool_w, pool_scale, ret_decay,
           w_o_attn, w_o_pool, w_o_ret, w_out, w_router, b_router, w_expert_in, b_expert_in, w_expert_out,
           b_expert_out, final_norm):
    b, seq, d = x.shape
    depth = w_mod.shape[0]
    assert ctx.shape[1] == TM and seq % TM == 0 and seq % GRID_W == 0
    s = seq + TM
    xm = jnp.concatenate([ctx, x], axis=1)
    mod = _modulation(c, c_ctx, w_mod, b_mod)
    tables = _rope_tables(seq)
    ntt = b * s // TM
    n_blocks = -(-(b * s * TOP_K + ntt * N_EXPERTS * (SUBLANE - 1)) // EB) + N_EXPERTS
    avg = _block_diag(jnp.full((RET_HEADS, RET_DIM, RET_DIM), 1.0 / RET_DIM, F32)).astype(BF16)
    tri = (jnp.arange(TM)[:, None] <= jnp.arange(TM)[None, :]).astype(BF16)
    for l in range(depth):
        q, k2, v2, rq, rk, rv, rg, u, gates = _inproj(xm, mod[l], norm1[l], _inproj_weight(w_in[l]), tables)
        attn = _attention(q, k2, v2, attn_sinks[l])
        yf, yb = _retention(rq, rk, rv, jax.nn.log_sigmoid(ret_decay[l].astype(F32)))
        wr_hilo = jnp.concatenate(_split(w_router[l].T), axis=0)
        br = jnp.broadcast_to(b_router[l][:, None], (N_EXPERTS, TM))
        wts = (_block_diag(pool_w[l]).astype(BF16), pool_scale[l].reshape(1, -1),
               w_o_attn[l].astype(BF16), w_o_pool[l].astype(BF16), w_o_ret[l].astype(BF16),
               w_out[l].astype(BF16), norm2[l].reshape(1, d), wr_hilo, br, avg, tri)
        xn, h2, ls, lst, rw, cnt = _merge(xm, attn, u, yf, yb, rg, gates, mod[l], seq, wts)
        tabs, block_e, n_used = _routing_tables(cnt, n_blocks)
        xs = _dispatch(h2.reshape(b * s, d), lst, tabs, n_used, n_blocks)
        ys = _experts(xs, block_e, n_used, w_expert_in, b_expert_in, w_expert_out, b_expert_out, l)
        xm = _combine(xn, ls, rw, mod[l], final_norm, ys, tabs, final=l == depth - 1)
    return xm
```

```python
import functools

import jax
import jax.numpy as jnp
from jax import lax
from jax.experimental import pallas as pl
from jax.experimental.pallas import tpu as pltpu

F32 = jnp.float32
BF16 = jnp.bfloat16
I32 = jnp.int32

GRID_W = 64
HEAD_DIM = 64
N_Q_HEADS = 8
N_KV_HEADS = 2
WINDOW = 128
ROPE_THETA = 10000.0
POOL_WIDTH = 256
POOL_WINDOWS = (2, 4, 8, 16)
RET_HEADS = 4
RET_DIM = 64
N_EXPERTS = 32
TOP_K = 4
D_FF = 1024
SWIGLU_LIMIT = 7.0
SWIGLU_ALPHA = 1.702
NORM_EPS = 1e-6
GN_EPS = 1e-5

LANE = 128
SUBLANE = 8
MXU_N = 256
TM = 256
AB = 128
IN_TILE = 640
CTX_BLOCKS = TM // AB
Q_BLOCKS = 2
EB = 512
FF_CHUNK = 256
POOL_HALO = SUBLANE
SORT_ROWS = TM * TOP_K + N_EXPERTS * SUBLANE
SORT_TILES = SORT_ROWS // SUBLANE
MIN_TILES = TM * TOP_K // SUBLANE
NEG = -1e30
VMEM_LIMIT = 56 * 1024 * 1024

ATT_Q = N_Q_HEADS * HEAD_DIM
KV2 = 2 * N_KV_HEADS * HEAD_DIM
RET_W = RET_HEADS * RET_DIM


def _dot(a, b):
    return jnp.dot(a, b, preferred_element_type=F32)


def _dot_nt(a, b):
    return lax.dot_general(a, b, (((1,), (1,)), ((), ())), preferred_element_type=F32)


def _dot_tn(a, b):
    return lax.dot_general(a, b, (((0,), (0,)), ((), ())), preferred_element_type=F32)


def _split(a):
    hi = a.astype(BF16)
    lo = (a - hi.astype(F32)).astype(BF16)
    return hi, lo


def _dot_hilo(a, m):
    hi, lo = _split(a)
    return _dot(hi, m) + _dot(lo, m)


def _sigmoid(x):
    return 0.5 * jnp.tanh(0.5 * x) + 0.5


def _silu(x):
    return x * _sigmoid(x)


def _params(*sem):
    return pltpu.CompilerParams(dimension_semantics=sem, vmem_limit_bytes=VMEM_LIMIT)


def _mod_kernel(c_ref, w_ref, b_ref, o_ref):
    s = _silu(c_ref[...])
    sh, sl = _split(s)
    wh, wl = _split(w_ref[0])
    o_ref[0] = _dot(sh, wh) + _dot(sh, wl) + _dot(sl, wh) + b_ref[0]


def _modulation(c, c_ctx, w_mod, b_mod):
    depth, d, six_d = w_mod.shape
    b = c.shape[0]
    cc = jnp.zeros((SUBLANE, d), F32).at[:b].set(c).at[b].set(c_ctx)
    out = pl.pallas_call(
        _mod_kernel,
        grid=(depth, six_d // d),
        in_specs=[
            pl.BlockSpec((SUBLANE, d), lambda l, n: (0, 0)),
            pl.BlockSpec((1, d, d), lambda l, n: (l, 0, n)),
            pl.BlockSpec((1, 1, d), lambda l, n: (l, 0, n)),
        ],
        out_specs=pl.BlockSpec((1, SUBLANE, d), lambda l, n: (l, 0, n)),
        out_shape=jax.ShapeDtypeStruct((depth, SUBLANE, six_d), F32),
        compiler_params=_params("arbitrary", "arbitrary"),
    )(cc, w_mod, b_mod.reshape(depth, 1, six_d))
    return out[:, : b + 1].reshape(depth, b + 1, 6, d)


def _rope(x, cos, sin, half, first):
    partner = jnp.where(first, pltpu.roll(x, LANE - half, 1), pltpu.roll(x, half, 1))
    return x * cos + partner * sin


def _inproj_kernel(x_ref, mod_ref, modc_ref, n1_ref, w_ref, ac_ref, as_ref, rc_ref, rs_ref,
                   q_ref, k_ref, v_ref, rq_ref, rk_ref, rv_ref, rg_ref, u_ref, g_ref):
    x = x_ref[0]
    rows = x.shape[0]
    ms = jnp.mean(x * x, axis=-1, keepdims=True)
    y = x * lax.rsqrt(ms + NORM_EPS) * n1_ref[...]
    is_ctx = (lax.broadcasted_iota(I32, (rows, 1), 0) < TM) & (pl.program_id(1) == 0)
    scale = jnp.where(is_ctx, modc_ref[0, 1:2, :], mod_ref[0, 1:2, :])
    shift = jnp.where(is_ctx, modc_ref[0, 0:1, :], mod_ref[0, 0:1, :])
    hb = (y * (1.0 + scale) + shift).astype(BF16)
    lane = lax.broadcasted_iota(I32, (rows, LANE), 1)
    a_first = (lane % (HEAD_DIM // 2)) < (HEAD_DIM // 4)
    r_first = (lane % RET_DIM) < (RET_DIM // 2)
    ac, asn, rc, rsn = ac_ref[...], as_ref[...], rc_ref[...], rs_ref[...]

    def proj(off, width):
        return _dot(hb, w_ref[:, off:off + width])

    def rotated(ref, width, cos, sin, half, first, off):
        for t in range(0, width, MXU_N):
            pr = proj(off + t, MXU_N)
            for g in range(0, MXU_N, LANE):
                ref[0, :, t + g:t + g + LANE] = _rope(pr[:, g:g + LANE], cos, sin, half, first).astype(BF16)

    off = 0
    rotated(q_ref, ATT_Q, ac, asn, HEAD_DIM // 4, a_first, off)
    off += ATT_Q
    rotated(k_ref, KV2, ac, asn, HEAD_DIM // 4, a_first, off)
    off += KV2
    v_ref[0] = proj(off, KV2).astype(BF16)
    off += KV2
    for ref in (rq_ref, rk_ref):
        rotated(ref, RET_W, rc, rsn, RET_DIM // 2, r_first, off)
        off += RET_W
    rv_ref[0] = proj(off, RET_W).astype(BF16)
    off += RET_W
    rg_ref[0] = proj(off, 2 * RET_W).astype(BF16)
    off += 2 * RET_W
    u_ref[0] = proj(off, POOL_WIDTH)
    off += POOL_WIDTH
    d = x.shape[-1]
    for t in range(3):
        g_ref[0, :, t * d:(t + 1) * d] = proj(off, d).astype(BF16)
        off += d


def _inproj(xm, mod_l, norm1_l, w1, tables):
    b, s, d = xm.shape
    ti = IN_TILE if s % IN_TILE == 0 else TM
    wcols = w1.shape[1]
    tok = lambda width: pl.BlockSpec((1, ti, width), lambda bi, j: (bi, j, 0))
    tab = pl.BlockSpec((ti, LANE), lambda bi, j: (j, 0))
    widths = (ATT_Q, KV2, KV2, RET_W, RET_W, RET_W, 2 * RET_W, POOL_WIDTH, 3 * d)
    dtypes = (BF16,) * 7 + (F32, BF16)
    return pl.pallas_call(
        _inproj_kernel,
        grid=(b, s // ti),
        in_specs=[
            tok(d),
            pl.BlockSpec((1, 6, d), lambda bi, j: (bi, 0, 0)),
            pl.BlockSpec((1, 6, d), lambda bi, j: (b, 0, 0)),
            pl.BlockSpec((1, d), lambda bi, j: (0, 0)),
            pl.BlockSpec((d, wcols), lambda bi, j: (0, 0), pipeline_mode=pl.Buffered(1)),
            tab, tab, tab, tab,
        ],
        out_specs=[tok(w) for w in widths],
        out_shape=[jax.ShapeDtypeStruct((b, s, w), dt) for w, dt in zip(widths, dtypes)],
        compiler_params=_params("arbitrary", "arbitrary"),
    )(xm, mod_l, mod_l, norm1_l.reshape(1, d), w1, *tables)


def _rope_tables(seq):
    rows = seq // GRID_W
    rpos = jnp.arange(rows, dtype=F32)[:, None]
    cpos = jnp.arange(GRID_W, dtype=F32)[:, None]
    lane = jnp.arange(LANE)
    grid = lambda per_row, per_col: (per_row[:, None, :] + per_col[None, :, :]).reshape(seq, LANE)
    axis_dim = HEAD_DIM // 2
    inv_a = ROPE_THETA ** (-jnp.arange(0, axis_dim, 2, dtype=F32) / axis_dim)
    hl = lane % HEAD_DIM
    inv_al = inv_a[(hl % axis_dim) % (axis_dim // 2)][None, :]
    by_row = (hl < axis_dim)[None, :]
    a_sign = jnp.where((hl % axis_dim) < axis_dim // 2, -1.0, 1.0)[None, :]
    a_cos = grid(jnp.where(by_row, jnp.cos(rpos * inv_al), 0.0), jnp.where(by_row, 0.0, jnp.cos(cpos * inv_al)))
    a_sin = grid(jnp.where(by_row, jnp.sin(rpos * inv_al), 0.0), jnp.where(by_row, 0.0, jnp.sin(cpos * inv_al)))
    inv_r = 1.0 / (ROPE_THETA ** jnp.linspace(0.0, 1.0, RET_DIM // 2, dtype=F32))
    rl = lane % RET_DIM
    inv_rl = inv_r[rl % (RET_DIM // 2)][None, :]
    r_sign = jnp.where(rl < RET_DIM // 2, -1.0, 1.0)[None, :]
    hi = (rpos * GRID_W) * inv_rl
    lo = cpos * inv_rl
    outer = lambda a, b: (a[:, None, :] * b[None, :, :]).reshape(seq, LANE)
    r_cos = outer(jnp.cos(hi), jnp.cos(lo)) - outer(jnp.sin(hi), jnp.sin(lo))
    r_sin = outer(jnp.sin(hi), jnp.cos(lo)) + outer(jnp.cos(hi), jnp.sin(lo))
    ones = jnp.ones((TM, LANE), F32)
    zeros = jnp.zeros((TM, LANE), F32)
    cat = lambda head, body: jnp.concatenate([head, body], axis=0)
    return (cat(ones, a_cos), cat(zeros, a_sin * a_sign), cat(ones, r_cos), cat(zeros, r_sin * r_sign))


def _attn_kernel(sink_ref, bias0_ref, bias1_ref, q_ref, k0_ref, k1_ref, k2_ref, k3_ref, kx_ref,
                 v0_ref, v1_ref, v2_ref, v3_ref, vx_ref, o_ref):
    g_heads = N_Q_HEADS // N_KV_HEADS
    lane = lax.broadcasted_iota(I32, (AB, LANE), 1)
    lo = lane < HEAD_DIM
    rows = lax.broadcasted_iota(I32, (g_heads * AB, 1), 0)
    k_refs = (k0_ref, k1_ref, k2_ref, k3_ref)
    v_refs = (v0_ref, v1_ref, v2_ref, v3_ref)
    biases = (bias0_ref[0], bias1_ref[0])
    chains = [(t, g) for t in range(Q_BLOCKS) for g in range(N_KV_HEADS)]
    scores, sinks, values = [], [], []
    for t, g in chains:
        ks = slice(g * LANE, (g + 1) * LANE)
        kd = jnp.concatenate([r[0, :, ks] for r in k_refs[t:t + 3]] + [kx_ref[0, :, ks]], axis=0)
        values.append(jnp.concatenate([r[0, :, ks] for r in v_refs[t:t + 3]] + [vx_ref[0, :, ks]], axis=0))
        qs = []
        for c in range(2 * g, 2 * g + 2):
            qc = q_ref[0, t * AB:(t + 1) * AB, c * LANE:(c + 1) * LANE]
            zero = jnp.zeros_like(qc)
            qs += [jnp.where(lo, qc, zero), jnp.where(lo, zero, qc)]
        scores.append(_dot_nt(jnp.concatenate(qs, axis=0), kd) + biases[t])
        sink = jnp.full((g_heads * AB, 1), sink_ref[g_heads * g + g_heads - 1], F32)
        for h in range(g_heads - 2, -1, -1):
            sink = jnp.where(rows < (h + 1) * AB, sink_ref[g_heads * g + h], sink)
        sinks.append(sink)
    for (t, g), s, sink, vd in zip(chains, scores, sinks, values):
        m = jnp.maximum(jnp.max(s, axis=-1, keepdims=True), sink)
        p = jnp.exp(s - m)
        den = jnp.sum(p, axis=-1, keepdims=True) + jnp.exp(sink - m)
        o = _dot(p.astype(BF16), vd) / den
        for h in range(2):
            c = 2 * g + h
            o_ref[0, t * AB:(t + 1) * AB, c * LANE:(c + 1) * LANE] = jnp.where(
                lo, o[2 * h * AB:(2 * h + 1) * AB], o[(2 * h + 1) * AB:(2 * h + 2) * AB]).astype(BF16)


def _attn_bias():
    g_heads = N_Q_HEADS // N_KV_HEADS
    r = jnp.arange(g_heads * AB)[:, None] % AB
    j = jnp.arange(3 * AB + TM)[None, :]
    band = jnp.abs(j - AB - r) <= WINDOW
    is_ctx = j >= 3 * AB
    variants = (band, band & (j >= AB), band & (j < 2 * AB), jnp.zeros_like(band))
    return jnp.stack([jnp.where(v | is_ctx, 0.0, NEG) for v in variants]).astype(F32)


def _attention(q, k2, v2, sinks):
    b, s, _ = q.shape
    nb = s // AB
    assert nb - CTX_BLOCKS >= 2 and nb % Q_BLOCKS == 0 and CTX_BLOCKS % Q_BLOCKS == 0
    bias = _attn_bias()

    def variant(blk):
        return jnp.where(blk < CTX_BLOCKS, 3, jnp.where(blk == CTX_BLOCKS, 1, jnp.where(blk == nb - 1, 2, 0)))

    def kv(offset):
        return pl.BlockSpec((1, AB, KV2), lambda bi, i: (bi, jnp.clip(Q_BLOCKS * i + offset, 0, nb - 1), 0))

    def bias_spec(t):
        return pl.BlockSpec((1,) + bias.shape[1:], lambda bi, i: (variant(Q_BLOCKS * i + t), 0, 0))

    cx = pl.BlockSpec((1, TM, KV2), lambda bi, i: (bi, 0, 0))
    kvs = [kv(o) for o in range(-1, Q_BLOCKS + 1)] + [cx]
    qo = pl.BlockSpec((1, Q_BLOCKS * AB, ATT_Q), lambda bi, i: (bi, i, 0))
    return pl.pallas_call(
        _attn_kernel,
        grid=(b, nb // Q_BLOCKS),
        in_specs=[pl.BlockSpec(memory_space=pltpu.SMEM), bias_spec(0), bias_spec(1), qo] + kvs + kvs,
        out_specs=qo,
        out_shape=jax.ShapeDtypeStruct((b, s, ATT_Q), BF16),
        compiler_params=_params("arbitrary", "arbitrary"),
    )(sinks, bias, bias, q, *([k2] * len(kvs)), *([v2] * len(kvs)))


def _ret_kernel(lg_ref, qf_ref, kf_ref, vf_ref, qb_ref, kb_ref, vb_ref, yf_ref, yb_ref,
                st_ref, dm_ref, qd_ref, kd_ref, cd_ref, *, batch):
    step = pl.program_id(0)
    lane = lax.broadcasted_iota(I32, (AB, LANE), 1)
    row = lax.broadcasted_iota(I32, (AB, LANE), 0)
    lo = lane < RET_DIM
    tiles = RET_W // LANE

    @pl.when(step == 0)
    def _():
        st_ref[...] = jnp.zeros_like(st_ref)
        ii = row.astype(F32)
        jj = lane.astype(F32)
        for d in range(2):
            for c in range(tiles):
                lg0 = lg_ref[d * RET_HEADS + 2 * c]
                lg1 = lg_ref[d * RET_HEADS + 2 * c + 1]
                lgl = jnp.where(lo, lg0, lg1)
                q_exp = ii + 1.0 if d == 0 else AB - ii
                k_exp = (AB - 1.0) - ii if d == 0 else ii
                qd_ref[d * tiles + c] = jnp.exp(q_exp * lgl)
                kd_ref[d * tiles + c] = jnp.exp(k_exp * lgl)
                cd_ref[d * tiles + c] = jnp.exp(AB * lgl)
                rel = ii - jj if d == 0 else jj - ii
                for hh, lgh in enumerate((lg0, lg1)):
                    dm_ref[d * tiles + c, hh * AB:(hh + 1) * AB, :] = jnp.where(
                        rel >= 0, jnp.exp(jnp.maximum(rel, 0.0) * lgh), 0.0)

    same_head = (row < RET_DIM) == lo
    dirs = ((qf_ref, kf_ref, vf_ref, yf_ref), (qb_ref, kb_ref, vb_ref, yb_ref))
    chains = [(d, b, c) for d in range(2) for b in range(batch) for c in range(tiles)]

    def operands(d, b, c):
        q_ref, k_ref, v_ref, _ = dirs[d]
        sl = slice(c * LANE, (c + 1) * LANE)
        return q_ref[b, :, sl], k_ref[b, :, sl], v_ref[b, :, sl]

    probs = []
    for d, b, c in chains:
        q, k, _ = operands(d, b, c)
        zero = jnp.zeros_like(q)
        q2 = jnp.concatenate([jnp.where(lo, q, zero), jnp.where(lo, zero, q)], axis=0)
        probs.append((_dot_nt(q2, k) * dm_ref[d * tiles + c]).astype(BF16))
    for (d, b, c), p in zip(chains, probs):
        q, _, v = operands(d, b, c)
        t = d * tiles + c
        si = (d * batch + b) * tiles + c
        y_intra = jnp.where(lo, _dot(p[:AB], v), _dot(p[AB:], v))
        q_dec = (q.astype(F32) * qd_ref[t]).astype(BF16)
        dirs[d][3][b, :, c * LANE:(c + 1) * LANE] = (y_intra + _dot(q_dec, st_ref[si].astype(BF16))).astype(BF16)
    for d, b, c in chains:
        _, k, v = operands(d, b, c)
        t = d * tiles + c
        si = (d * batch + b) * tiles + c
        k_dec = (k.astype(F32) * kd_ref[t]).astype(BF16)
        st_ref[si] = st_ref[si] * cd_ref[t] + jnp.where(same_head, _dot_tn(k_dec, v), 0.0)


def _retention(rq, rk, rv, log_g):
    b, s, w = rq.shape
    nb = s // AB
    tiles = w // LANE

    def back(i):
        return jnp.where(i < CTX_BLOCKS, CTX_BLOCKS - 1 - i, nb - 1 + CTX_BLOCKS - i)

    fwd = pl.BlockSpec((b, AB, w), lambda i: (0, i, 0))
    bwd = pl.BlockSpec((b, AB, w), lambda i: (0, back(i), 0))
    return pl.pallas_call(
        functools.partial(_ret_kernel, batch=b),
        grid=(nb,),
        in_specs=[pl.BlockSpec(memory_space=pltpu.SMEM), fwd, fwd, fwd, bwd, bwd, bwd],
        out_specs=[fwd, bwd],
        out_shape=[jax.ShapeDtypeStruct((b, s, w), BF16)] * 2,
        scratch_shapes=[
            pltpu.VMEM((2 * b * tiles, LANE, LANE), F32),
            pltpu.VMEM((2 * tiles, 2 * AB, LANE), F32),
            pltpu.VMEM((2 * tiles, AB, LANE), F32),
            pltpu.VMEM((2 * tiles, AB, LANE), F32),
            pltpu.VMEM((2 * tiles, AB, LANE), F32),
        ],
        compiler_params=_params("arbitrary"),
    )(log_g.reshape(-1), rq, rk, rv, rq, rk, rv)


def _route_tile(logits, tri_ref, ls_ref, rw_ref, cnt_ref):
    ne = N_EXPERTS
    eidx = lax.broadcasted_iota(I32, (ne, TM), 0).astype(F32)
    tops, hots = [], []
    for _ in range(TOP_K):
        mx = jnp.max(logits, axis=0, keepdims=True)
        idx = jnp.min(jnp.where(logits == mx, eidx, float(ne)), axis=0, keepdims=True)
        hot = eidx == idx
        logits = jnp.where(hot, NEG * 2.0, logits)
        tops.append(mx)
        hots.append(hot)
    ex = [jnp.exp(mx - tops[0]) for mx in tops]
    tot = ex[0]
    for e in ex[1:]:
        tot = tot + e
    sel = jnp.zeros((ne, TM), F32)
    for hot in hots:
        sel = jnp.where(hot, 1.0, sel)
    incl = _dot(sel.astype(BF16), tri_ref[...])
    cnt = incl[:, TM - 1:TM]
    run = jnp.floor((cnt + (SUBLANE - 1.0)) * (1.0 / SUBLANE)) * SUBLANE
    ends = jnp.broadcast_to(run, (ne, LANE))
    erow = lax.broadcasted_iota(I32, (ne, LANE), 0)
    shift = 1
    while shift < ne:
        ends = ends + jnp.where(erow >= shift, pltpu.roll(ends, shift, 0), 0.0)
        shift *= 2
    slot = (ends[:, :1] - run) + incl - 1.0
    ls_rows = [jnp.sum(jnp.where(hot, slot, 0.0), axis=0, keepdims=True) for hot in hots]
    rw_rows = [e / tot for e in ex]
    lst = jnp.concatenate(ls_rows + [jnp.full((SUBLANE - TOP_K, TM), -1.0, F32)], axis=0)
    ls_ref[...] = jnp.transpose(
        jnp.concatenate([lst, jnp.full((LANE - SUBLANE, TM), -1.0, F32)], axis=0)).astype(I32)
    rw_ref[...] = jnp.transpose(jnp.concatenate(rw_rows + [jnp.zeros((LANE - TOP_K, TM), F32)], axis=0))
    cnt_ref[...] = jnp.broadcast_to(cnt, cnt_ref.shape).astype(I32)
    return lst


def _merge_kernel(x_ref, attn_ref, up_ref, uc_ref, un_ref, yf_ref, yb_ref, rg_ref, g_ref, mod_ref,
                  wbd_ref, ps_ref, woa_ref, wop_ref, wor_ref, wout_ref, n2_ref, wr_ref,
                  br_ref, avg_ref, tri_ref,
                  xo_ref, xs_ref, ls_ref, rw_ref, cnt_ref, lg_ref, h2_ref, *, seq, nt, ntt):
    i = pl.program_id(0)
    j = jnp.minimum(i, ntt - 1) % nt

    @pl.when(i == 0)
    def _():
        lg_ref[...] = jnp.zeros_like(lg_ref)
        h2_ref[...] = jnp.zeros_like(h2_ref)

    lst = _route_tile(lg_ref[...], tri_ref, ls_ref, rw_ref, cnt_ref)
    slot = lax.broadcasted_iota(I32, (SORT_ROWS, TM), 0).astype(F32)
    p = jnp.zeros((SORT_ROWS, TM), F32)
    for k in range(TOP_K):
        p = jnp.where(slot == lst[k:k + 1, :], 1.0, p)
    xs_ref[...] = _dot(p.astype(BF16), h2_ref[...])

    ext = TM + 2 * POOL_HALO
    seq_len = jnp.where(j == 0, TM, seq)
    start = jnp.where(j == 0, 0, (j - 1) * TM)
    u = uc_ref[0]
    ue = jnp.concatenate([up_ref[0], u, un_ref[0]], axis=0)
    erow = lax.broadcasted_iota(I32, (ext, 1), 0) + (start - POOL_HALO)
    ue = jnp.where((erow >= 0) & (erow < seq_len), ue, 0.0)
    t_pos = lax.broadcasted_iota(I32, (TM, 1), 0) + start
    glane = lax.broadcasted_iota(I32, (TM, POOL_WIDTH), 1) // (POOL_WIDTH // len(POOL_WINDOWS))
    run = ue
    width = 1
    diff = jnp.zeros((TM, POOL_WIDTH), F32)
    for gi, w in enumerate(POOL_WINDOWS):
        while width < w:
            run = run + pltpu.roll(run, ext - width, 0)
            width *= 2
        win = pltpu.roll(run, w // 2, 0)[POOL_HALO:POOL_HALO + TM]
        cnt = jnp.minimum(t_pos - w // 2 + w, seq_len) - jnp.maximum(t_pos - w // 2, 0)
        diff = jnp.where(glane == gi, win / cnt.astype(F32) - u, diff)
    pool = _dot(diff.astype(BF16), wbd_ref[...]) * ps_ref[...]

    def head_norm(y):
        dlt = y.astype(F32) - _dot(y, avg_ref[...])
        var = _dot_hilo(dlt * dlt, avg_ref[...])
        return dlt * lax.rsqrt(var + GN_EPS)

    rg = rg_ref[0].astype(F32)
    ret = head_norm(yf_ref[0]) * _silu(rg[:, :RET_W]) + head_norm(yb_ref[0]) * _silu(rg[:, RET_W:])

    d = x_ref.shape[-1]
    gate = lambda t: _sigmoid(g_ref[0, :, t * d:(t + 1) * d])
    m = (gate(0) * _dot(attn_ref[0], woa_ref[...]).astype(BF16)
         + gate(1) * _dot(pool.astype(BF16), wop_ref[...]).astype(BF16)
         + gate(2) * _dot(ret.astype(BF16), wor_ref[...]).astype(BF16))
    xn = x_ref[0] + mod_ref[0, 2:3, :] * _dot(m, wout_ref[...])
    xo_ref[0] = xn

    ms = jnp.mean(xn * xn, axis=-1, keepdims=True)
    h2 = xn * lax.rsqrt(ms + NORM_EPS) * n2_ref[...] * (1.0 + mod_ref[0, 4:5, :]) + mod_ref[0, 3:4, :]
    h2_ref[...] = h2.astype(BF16)
    hh, hl = _split(h2)
    ne = N_EXPERTS
    full = _dot_nt(wr_ref[...], hh)
    logits = full[:ne] + full[ne:] + _dot_nt(wr_ref[:ne], hl) + br_ref[...]
    lg_ref[...] = logits


def _merge(xm, attn, u, yf, yb, rg, gates, mod_l, seq, wts):
    b, s, d = xm.shape
    nt = s // TM
    ntt = b * nt
    hb = TM // POOL_HALO
    nh = s // POOL_HALO
    cur = lambda i: jnp.minimum(i, ntt - 1)
    lag = lambda i: jnp.maximum(i - 1, 0)
    tok = lambda width: pl.BlockSpec((1, TM, width), lambda i: (cur(i) // nt, cur(i) % nt, 0))
    full = lambda a: pl.BlockSpec(a.shape, lambda i: (0,) * a.ndim)
    lanes = pl.BlockSpec((TM, LANE), lambda i: (lag(i), 0))
    in_specs = [
        tok(d), tok(ATT_Q),
        pl.BlockSpec((1, POOL_HALO, POOL_WIDTH),
                     lambda i: (cur(i) // nt, jnp.maximum((cur(i) % nt) * hb - 1, 0), 0)),
        tok(POOL_WIDTH),
        pl.BlockSpec((1, POOL_HALO, POOL_WIDTH),
                     lambda i: (cur(i) // nt, jnp.minimum((cur(i) % nt + 1) * hb, nh - 1), 0)),
        tok(RET_W), tok(RET_W), tok(2 * RET_W), tok(3 * d),
        pl.BlockSpec((1, 6, d), lambda i: (jnp.where(cur(i) % nt == 0, b, cur(i) // nt), 0, 0)),
    ] + [full(a) for a in wts]
    return pl.pallas_call(
        functools.partial(_merge_kernel, seq=seq, nt=nt, ntt=ntt),
        grid=(ntt + 1,),
        in_specs=in_specs,
        out_specs=[tok(d), pl.BlockSpec((SORT_ROWS, d), lambda i: (lag(i), 0)), lanes, lanes,
                   pl.BlockSpec((N_EXPERTS, LANE), lambda i: (lag(i), 0))],
        out_shape=[jax.ShapeDtypeStruct((b, s, d), F32), jax.ShapeDtypeStruct((ntt * SORT_ROWS, d), F32),
                   jax.ShapeDtypeStruct((b * s, LANE), I32),
                   jax.ShapeDtypeStruct((b * s, LANE), F32), jax.ShapeDtypeStruct((ntt * N_EXPERTS, LANE), I32)],
        scratch_shapes=[pltpu.VMEM((N_EXPERTS, TM), F32), pltpu.VMEM((TM, d), BF16)],
        compiler_params=_params("arbitrary"),
    )(xm, attn, u, u, u, yf, yb, rg, gates, mod_l, *wts)


def _rows(tile_index):
    return pl.ds(pl.multiple_of(tile_index * SUBLANE, SUBLANE), SUBLANE)


def _tile_loop(count_ref, tile, fn):
    def body(j, carry):
        fn(j)
        return carry

    lax.fori_loop(0, MIN_TILES, body, 0, unroll=8)
    lax.fori_loop(MIN_TILES, count_ref[tile], body, 0)


def _tile_copies(table_ref, count_ref, tile, copy):
    _tile_loop(count_ref, tile, lambda j: copy(j, table_ref[tile * SORT_TILES + j]))


def _expert_kernel(be_ref, nu_ref, src_ref, xs_hbm, w1_ref, b1_ref, w2_ref, b2_ref, y_ref, w1b, w2b, xbuf, sem):
    bi = pl.program_id(0)
    used = bi < nu_ref[0]
    fresh = (bi == 0) | (be_ref[bi] != be_ref[jnp.maximum(bi - 1, 0)])
    cur = bi % 2
    block_tiles = EB // SUBLANE

    def each_tile(fn):
        def body(t, carry):
            fn(t)
            return carry
        lax.fori_loop(0, block_tiles, body, 0, unroll=8)

    def fetch(blk, half):
        each_tile(lambda t: pltpu.make_async_copy(
            xs_hbm.at[_rows(src_ref[blk * block_tiles + t])], xbuf.at[half, _rows(t)], sem.at[half]).start())

    @pl.when(bi == 0)
    def _():
        fetch(0, 0)

    @pl.when(bi + 1 < nu_ref[0])
    def _():
        fetch(bi + 1, 1 - cur)

    @pl.when(used & fresh)
    def _():
        w1b[...] = w1_ref[0, 0].astype(BF16)
        w2b[...] = w2_ref[0, 0].astype(BF16)

    @pl.when(jnp.logical_not(used))
    def _():
        y_ref[...] = jnp.zeros_like(y_ref)

    @pl.when(used)
    def _():
        each_tile(lambda t: pltpu.make_async_copy(
            xs_hbm.at[_rows(t)], xbuf.at[cur, _rows(t)], sem.at[cur]).wait())
        xb = xbuf[cur].astype(BF16)
        y = None
        for c in range(0, D_FF, FF_CHUNK):
            glu = _dot(xb, w1b[:, c:c + FF_CHUNK]) + b1_ref[0, 0, :, c:c + FF_CHUNK]
            lin = _dot(xb, w1b[:, D_FF + c:D_FF + c + FF_CHUNK]) + b1_ref[0, 0, :, D_FF + c:D_FF + c + FF_CHUNK]
            glu = jnp.minimum(glu, SWIGLU_LIMIT)
            lin = jnp.clip(lin, -SWIGLU_LIMIT, SWIGLU_LIMIT)
            act = glu * _sigmoid(SWIGLU_ALPHA * glu) * (lin + 1.0)
            part = _dot(act.astype(BF16), w2b[c:c + FF_CHUNK, :])
            y = part if y is None else y + part
        y_ref[...] = y + b2_ref[0, 0]


def _experts(xs, block_e, n_used, src, w1, b1, w2, b2, layer, n_blocks):
    d = xs.shape[1]
    depth, ne, _, f2 = w1.shape
    grid_spec = pltpu.PrefetchScalarGridSpec(
        num_scalar_prefetch=3,
        grid=(n_blocks,),
        in_specs=[
            pl.BlockSpec(memory_space=pl.ANY),
            pl.BlockSpec((1, 1, d, f2), lambda bi, be, *_: (layer, be[bi], 0, 0)),
            pl.BlockSpec((1, 1, 1, f2), lambda bi, be, *_: (layer, be[bi], 0, 0)),
            pl.BlockSpec((1, 1, f2 // 2, d), lambda bi, be, *_: (layer, be[bi], 0, 0)),
            pl.BlockSpec((1, 1, 1, d), lambda bi, be, *_: (layer, be[bi], 0, 0)),
        ],
        out_specs=pl.BlockSpec((EB, d), lambda bi, *_: (bi, 0)),
        scratch_shapes=[pltpu.VMEM((d, f2), BF16), pltpu.VMEM((f2 // 2, d), BF16),
                        pltpu.VMEM((2, EB, d), F32), pltpu.SemaphoreType.DMA((2,))],
    )
    return pl.pallas_call(
        _expert_kernel,
        grid_spec=grid_spec,
        out_shape=jax.ShapeDtypeStruct((n_blocks * EB, d), F32),
        compiler_params=_params("arbitrary"),
    )(block_e, n_used, src, xs, w1, b1.reshape(depth, ne, 1, f2), w2, b2.reshape(depth, ne, 1, d))


def _combine_kernel(dst_ref, cnt_ref, x_ref, ls_ref, rw_ref, mod_ref, fg_ref, ys_hbm, xo_ref, buf, sem, *, final):
    i = pl.program_id(0)
    last = pl.num_programs(0) - 1
    cur = i % 2

    def fetch(tile, half):
        _tile_copies(dst_ref, cnt_ref, tile, lambda j, t: pltpu.make_async_copy(
            ys_hbm.at[_rows(t)], buf.at[half, _rows(j)], sem.at[half]).start())

    @pl.when(i == 0)
    def _():
        buf[...] = jnp.zeros_like(buf)
        fetch(0, 0)

    @pl.when(i < last)
    def _():
        fetch(i + 1, 1 - cur)

    _tile_loop(cnt_ref, i, lambda j: pltpu.make_async_copy(
        ys_hbm.at[_rows(j)], buf.at[cur, _rows(j)], sem.at[cur]).wait())
    yb = buf[cur].astype(BF16)
    slot = lax.broadcasted_iota(I32, (TM, SORT_ROWS), 1)
    ls = ls_ref[...]
    rw = rw_ref[...]
    g = jnp.zeros((TM, SORT_ROWS), F32)
    for k in range(TOP_K):
        g = jnp.where(slot == ls[:, k:k + 1], rw[:, k:k + 1], g)
    xn = x_ref[0] + mod_ref[0, 5:6, :] * _dot(g.astype(BF16), yb)
    if final:
        ms = jnp.mean(xn * xn, axis=-1, keepdims=True)
        xn = xn * lax.rsqrt(ms + NORM_EPS) * fg_ref[...]
    xo_ref[0] = xn


def _combine(xn, ls, rw, mod_l, final_gain, ys, tabs, final):
    b, s, d = xn.shape
    nt = s // TM
    tok = pl.BlockSpec((1, TM, d), lambda i, *_: (i // nt, i % nt, 0))
    lanes = pl.BlockSpec((TM, LANE), lambda i, *_: (i, 0))
    if final:
        out_spec = pl.BlockSpec((1, TM, d), lambda i, *_: (i // nt, jnp.maximum(i % nt - 1, 0), 0))
        out_shape = jax.ShapeDtypeStruct((b, s - TM, d), F32)
    else:
        out_spec, out_shape = tok, jax.ShapeDtypeStruct((b, s, d), F32)
    grid_spec = pltpu.PrefetchScalarGridSpec(
        num_scalar_prefetch=2,
        grid=(b * nt,),
        in_specs=[tok, lanes, lanes,
                  pl.BlockSpec((1, 6, d), lambda i, *_: (jnp.where(i % nt == 0, b, i // nt), 0, 0)),
                  pl.BlockSpec((1, d), lambda i, *_: (0, 0)),
                  pl.BlockSpec(memory_space=pl.ANY)],
        out_specs=out_spec,
        scratch_shapes=[pltpu.VMEM((2, SORT_ROWS, d), F32), pltpu.SemaphoreType.DMA((2,))],
    )
    return pl.pallas_call(
        functools.partial(_combine_kernel, final=final),
        grid_spec=grid_spec,
        out_shape=out_shape,
        compiler_params=_params("arbitrary"),
    )(tabs["dst"], tabs["count"], xn, ls, rw, mod_l, final_gain.reshape(1, d), ys)


def _inproj_weight(w_in_l):
    d = w_in_l.shape[0]
    sizes = (ATT_Q, N_KV_HEADS * HEAD_DIM, N_KV_HEADS * HEAD_DIM, POOL_WIDTH,
             RET_W, RET_W, RET_W, RET_W, RET_W, d, d, d)
    parts, off = [], 0
    for sz in sizes:
        parts.append(w_in_l[:, off:off + sz])
        off += sz
    q, k, v, u, rq, rk, rv, rgf, rgb, ga, gp, gr = parts
    twice = lambda w: jnp.concatenate(
        [w[:, h * HEAD_DIM:(h + 1) * HEAD_DIM] for h in range(N_KV_HEADS) for _ in range(2)], axis=1)
    cols = [q * HEAD_DIM ** -0.5, twice(k), twice(v), rq, rk * RET_DIM ** -0.5, rv, rgf, rgb, u, ga, gp, gr]
    return jnp.concatenate(cols, axis=1).astype(BF16)


def _block_diag(blocks):
    n, r, c = blocks.shape
    out = jnp.zeros((n * r, n * c), blocks.dtype)
    for g in range(n):
        out = out.at[g * r:(g + 1) * r, g * c:(g + 1) * c].set(blocks[g])
    return out


def _routing_tables(cnt, n_blocks):
    ntt = cnt.shape[0] // N_EXPERTS
    counts = cnt.reshape(ntt, N_EXPERTS, LANE)[:, :, 0]
    run = (counts + SUBLANE - 1) // SUBLANE
    total = jnp.sum(run, axis=0)
    eb = EB // SUBLANE
    padded = (total + eb - 1) // eb * eb
    pad_end = jnp.cumsum(padded)
    pad_start = pad_end - padded
    off = pad_start[None, :] + jnp.cumsum(run, axis=0) - run
    run_end = jnp.cumsum(run, axis=1)
    j = jnp.arange(SORT_TILES)
    owner = jnp.sum(run_end[:, None, :] <= j[None, :, None], axis=2)
    mine = owner[:, :, None] == jnp.arange(N_EXPERTS)[None, None, :]
    in_region = j[None, :] + jnp.sum(jnp.where(mine, (off - (run_end - run))[:, None, :], 0), axis=2)
    dst = jnp.where(owner < N_EXPERTS, in_region, 0)
    n_used = pad_end[-1] // eb
    blk = jnp.minimum(jnp.arange(n_blocks), n_used - 1) * eb
    block_e = jnp.minimum(jnp.sum(pad_end[None, :] <= blk[:, None], axis=1), N_EXPERTS - 1)
    r = jnp.arange(n_blocks * eb)
    mine_e = block_e[r // eb][:, None] == jnp.arange(N_EXPERTS)[None, :]
    pick = lambda table: jnp.sum(jnp.where(mine_e, table[None, :], 0), axis=1)
    pos = r - pick(pad_start)
    first = jnp.cumsum(run, axis=0) - run
    first_e = jnp.sum(jnp.where(mine_e[:, None, :], first[None, :, :], 0), axis=2)
    run_e = jnp.sum(jnp.where(mine_e[:, None, :], run[None, :, :], 0), axis=2)
    local_e = jnp.sum(jnp.where(mine_e[:, None, :], (run_end - run)[None, :, :], 0), axis=2)
    inside = (pos[:, None] >= first_e) & (pos[:, None] < first_e + run_e)
    where = jnp.arange(ntt)[None, :] * SORT_TILES + local_e + pos[:, None] - first_e
    zero_tile = SORT_TILES - 1
    src = jnp.where(jnp.any(inside, axis=1), jnp.sum(jnp.where(inside, where, 0), axis=1), zero_tile)
    tabs = dict(dst=dst.reshape(-1).astype(I32), count=run_end[:, -1].astype(I32), src=src.astype(I32))
    return tabs, block_e.astype(I32), n_used.reshape(1).astype(I32)


def kernel(x, c, ctx, c_ctx, w_mod, b_mod, norm1, norm2, w_in, attn_sinks, pool_w, pool_scale, ret_decay,
           w_o_attn, w_o_pool, w_o_ret, w_out, w_router, b_router, w_expert_in, b_expert_in, w_expert_out,
           b_expert_out, final_norm):
    b, seq, d = x.shape
    depth = w_mod.shape[0]
    assert ctx.shape[1] == TM and seq % TM == 0 and seq % GRID_W == 0
    s = seq + TM
    xm = jnp.concatenate([ctx, x], axis=1)
    mod = _modulation(c, c_ctx, w_mod, b_mod)
    tables = _rope_tables(seq)
    ntt = b * s // TM
    n_blocks = -(-(b * s * TOP_K + ntt * N_EXPERTS * (SUBLANE - 1)) // EB) + N_EXPERTS
    avg = _block_diag(jnp.full((RET_HEADS, RET_DIM, RET_DIM), 1.0 / RET_DIM, F32)).astype(BF16)
    tri = (jnp.arange(TM)[:, None] <= jnp.arange(TM)[None, :]).astype(BF16)
    for l in range(depth):
        q, k2, v2, rq, rk, rv, rg, u, gates = _inproj(xm, mod[l], norm1[l], _inproj_weight(w_in[l]), tables)
        attn = _attention(q, k2, v2, attn_sinks[l])
        yf, yb = _retention(rq, rk, rv, jax.nn.log_sigmoid(ret_decay[l].astype(F32)))
        wr_hilo = jnp.concatenate(_split(w_router[l].T), axis=0)
        br = jnp.broadcast_to(b_router[l][:, None], (N_EXPERTS, TM))
        wts = (_block_diag(pool_w[l]).astype(BF16), pool_scale[l].reshape(1, -1),
               w_o_attn[l].astype(BF16), w_o_pool[l].astype(BF16), w_o_ret[l].astype(BF16),
               w_out[l].astype(BF16), norm2[l].reshape(1, d), wr_hilo, br, avg, tri)
        xn, xs, ls, rw, cnt = _merge(xm, attn, u, yf, yb, rg, gates, mod[l], seq, wts)
        tabs, block_e, n_used = _routing_tables(cnt, n_blocks)
        ys = _experts(xs, block_e, n_used, tabs["src"], w_expert_in, b_expert_in, w_expert_out, b_expert_out, l,
                      n_blocks)
        xm = _combine(xn, ls, rw, mod[l], final_norm, ys, tabs, final=l == depth - 1)
    return xm
```

```python
import functools

import jax
import jax.numpy as jnp
from jax import lax
from jax.experimental import pallas as pl
from jax.experimental.pallas import tpu as pltpu

F32 = jnp.float32
BF16 = jnp.bfloat16
I32 = jnp.int32

GRID_W = 64
HEAD_DIM = 64
N_Q_HEADS = 8
N_KV_HEADS = 2
WINDOW = 128
ROPE_THETA = 10000.0
POOL_WIDTH = 256
POOL_WINDOWS = (2, 4, 8, 16)
RET_HEADS = 4
RET_DIM = 64
N_EXPERTS = 32
TOP_K = 4
D_FF = 1024
SWIGLU_LIMIT = 7.0
SWIGLU_ALPHA = 1.702
NORM_EPS = 1e-6
GN_EPS = 1e-5

LANE = 128
SUBLANE = 8
MXU_N = 256
TM = 256
AB = 128
IN_TILE = 640
CTX_BLOCKS = TM // AB
Q_BLOCKS = 2
EB = 512
FF_CHUNK = 256
POOL_HALO = SUBLANE
SORT_ROWS = TM * TOP_K + N_EXPERTS * SUBLANE
SORT_TILES = SORT_ROWS // SUBLANE
MIN_TILES = TM * TOP_K // SUBLANE
NEG = -1e30
VMEM_LIMIT = 56 * 1024 * 1024

ATT_Q = N_Q_HEADS * HEAD_DIM
KV2 = 2 * N_KV_HEADS * HEAD_DIM
RET_W = RET_HEADS * RET_DIM


def _dot(a, b):
    return jnp.dot(a, b, preferred_element_type=F32)


def _dot_nt(a, b):
    return lax.dot_general(a, b, (((1,), (1,)), ((), ())), preferred_element_type=F32)


def _dot_tn(a, b):
    return lax.dot_general(a, b, (((0,), (0,)), ((), ())), preferred_element_type=F32)


def _split(a):
    hi = a.astype(BF16)
    lo = (a - hi.astype(F32)).astype(BF16)
    return hi, lo


def _dot_hilo(a, m):
    hi, lo = _split(a)
    return _dot(hi, m) + _dot(lo, m)


def _sigmoid(x):
    return 0.5 * jnp.tanh(0.5 * x) + 0.5


def _silu(x):
    return x * _sigmoid(x)


def _params(*sem):
    return pltpu.CompilerParams(dimension_semantics=sem, vmem_limit_bytes=VMEM_LIMIT)


def _mod_kernel(c_ref, w_ref, b_ref, o_ref):
    s = _silu(c_ref[...])
    sh, sl = _split(s)
    wh, wl = _split(w_ref[0])
    o_ref[0] = _dot(sh, wh) + _dot(sh, wl) + _dot(sl, wh) + b_ref[0]


def _modulation(c, c_ctx, w_mod, b_mod):
    depth, d, six_d = w_mod.shape
    b = c.shape[0]
    cc = jnp.zeros((SUBLANE, d), F32).at[:b].set(c).at[b].set(c_ctx)
    out = pl.pallas_call(
        _mod_kernel,
        grid=(depth, six_d // d),
        in_specs=[
            pl.BlockSpec((SUBLANE, d), lambda l, n: (0, 0)),
            pl.BlockSpec((1, d, d), lambda l, n: (l, 0, n)),
            pl.BlockSpec((1, 1, d), lambda l, n: (l, 0, n)),
        ],
        out_specs=pl.BlockSpec((1, SUBLANE, d), lambda l, n: (l, 0, n)),
        out_shape=jax.ShapeDtypeStruct((depth, SUBLANE, six_d), F32),
        compiler_params=_params("arbitrary", "arbitrary"),
    )(cc, w_mod, b_mod.reshape(depth, 1, six_d))
    return out[:, : b + 1].reshape(depth, b + 1, 6, d)


def _rope(x, cos, sin, half, first):
    partner = jnp.where(first, pltpu.roll(x, LANE - half, 1), pltpu.roll(x, half, 1))
    return x * cos + partner * sin


def _inproj_kernel(x_ref, mod_ref, modc_ref, n1_ref, w_ref, ac_ref, as_ref, rc_ref, rs_ref,
                   q_ref, k_ref, v_ref, rq_ref, rk_ref, rv_ref, rg_ref, u_ref, g_ref):
    x = x_ref[0]
    rows = x.shape[0]
    ms = jnp.mean(x * x, axis=-1, keepdims=True)
    y = x * lax.rsqrt(ms + NORM_EPS) * n1_ref[...]
    is_ctx = (lax.broadcasted_iota(I32, (rows, 1), 0) < TM) & (pl.program_id(1) == 0)
    scale = jnp.where(is_ctx, modc_ref[0, 1:2, :], mod_ref[0, 1:2, :])
    shift = jnp.where(is_ctx, modc_ref[0, 0:1, :], mod_ref[0, 0:1, :])
    hb = (y * (1.0 + scale) + shift).astype(BF16)
    lane = lax.broadcasted_iota(I32, (rows, LANE), 1)
    a_first = (lane % (HEAD_DIM // 2)) < (HEAD_DIM // 4)
    r_first = (lane % RET_DIM) < (RET_DIM // 2)
    ac, asn, rc, rsn = ac_ref[...], as_ref[...], rc_ref[...], rs_ref[...]

    def proj(off, width):
        return _dot(hb, w_ref[:, off:off + width])

    def rotated(ref, width, cos, sin, half, first, off):
        for t in range(0, width, MXU_N):
            pr = proj(off + t, MXU_N)
            for g in range(0, MXU_N, LANE):
                ref[0, :, t + g:t + g + LANE] = _rope(pr[:, g:g + LANE], cos, sin, half, first).astype(BF16)

    off = 0
    rotated(q_ref, ATT_Q, ac, asn, HEAD_DIM // 4, a_first, off)
    off += ATT_Q
    rotated(k_ref, KV2, ac, asn, HEAD_DIM // 4, a_first, off)
    off += KV2
    v_ref[0] = proj(off, KV2).astype(BF16)
    off += KV2
    for ref in (rq_ref, rk_ref):
        rotated(ref, RET_W, rc, rsn, RET_DIM // 2, r_first, off)
        off += RET_W
    rv_ref[0] = proj(off, RET_W).astype(BF16)
    off += RET_W
    rg_ref[0] = proj(off, 2 * RET_W).astype(BF16)
    off += 2 * RET_W
    u_ref[0] = proj(off, POOL_WIDTH)
    off += POOL_WIDTH
    d = x.shape[-1]
    for t in range(3):
        g_ref[0, :, t * d:(t + 1) * d] = proj(off, d).astype(BF16)
        off += d


def _inproj(xm, mod_l, norm1_l, w1, tables):
    b, s, d = xm.shape
    ti = IN_TILE if s % IN_TILE == 0 else TM
    wcols = w1.shape[1]
    tok = lambda width: pl.BlockSpec((1, ti, width), lambda bi, j: (bi, j, 0))
    tab = pl.BlockSpec((ti, LANE), lambda bi, j: (j, 0))
    widths = (ATT_Q, KV2, KV2, RET_W, RET_W, RET_W, 2 * RET_W, POOL_WIDTH, 3 * d)
    dtypes = (BF16,) * 7 + (F32, BF16)
    return pl.pallas_call(
        _inproj_kernel,
        grid=(b, s // ti),
        in_specs=[
            tok(d),
            pl.BlockSpec((1, 6, d), lambda bi, j: (bi, 0, 0)),
            pl.BlockSpec((1, 6, d), lambda bi, j: (b, 0, 0)),
            pl.BlockSpec((1, d), lambda bi, j: (0, 0)),
            pl.BlockSpec((d, wcols), lambda bi, j: (0, 0), pipeline_mode=pl.Buffered(1)),
            tab, tab, tab, tab,
        ],
        out_specs=[tok(w) for w in widths],
        out_shape=[jax.ShapeDtypeStruct((b, s, w), dt) for w, dt in zip(widths, dtypes)],
        compiler_params=_params("arbitrary", "arbitrary"),
    )(xm, mod_l, mod_l, norm1_l.reshape(1, d), w1, *tables)


def _rope_tables(seq):
    rows = seq // GRID_W
    rpos = jnp.arange(rows, dtype=F32)[:, None]
    cpos = jnp.arange(GRID_W, dtype=F32)[:, None]
    lane = jnp.arange(LANE)
    grid = lambda per_row, per_col: (per_row[:, None, :] + per_col[None, :, :]).reshape(seq, LANE)
    axis_dim = HEAD_DIM // 2
    inv_a = ROPE_THETA ** (-jnp.arange(0, axis_dim, 2, dtype=F32) / axis_dim)
    hl = lane % HEAD_DIM
    inv_al = inv_a[(hl % axis_dim) % (axis_dim // 2)][None, :]
    by_row = (hl < axis_dim)[None, :]
    a_sign = jnp.where((hl % axis_dim) < axis_dim // 2, -1.0, 1.0)[None, :]
    a_cos = grid(jnp.where(by_row, jnp.cos(rpos * inv_al), 0.0), jnp.where(by_row, 0.0, jnp.cos(cpos * inv_al)))
    a_sin = grid(jnp.where(by_row, jnp.sin(rpos * inv_al), 0.0), jnp.where(by_row, 0.0, jnp.sin(cpos * inv_al)))
    inv_r = 1.0 / (ROPE_THETA ** jnp.linspace(0.0, 1.0, RET_DIM // 2, dtype=F32))
    rl = lane % RET_DIM
    inv_rl = inv_r[rl % (RET_DIM // 2)][None, :]
    r_sign = jnp.where(rl < RET_DIM // 2, -1.0, 1.0)[None, :]
    hi = (rpos * GRID_W) * inv_rl
    lo = cpos * inv_rl
    outer = lambda a, b: (a[:, None, :] * b[None, :, :]).reshape(seq, LANE)
    r_cos = outer(jnp.cos(hi), jnp.cos(lo)) - outer(jnp.sin(hi), jnp.sin(lo))
    r_sin = outer(jnp.sin(hi), jnp.cos(lo)) + outer(jnp.cos(hi), jnp.sin(lo))
    ones = jnp.ones((TM, LANE), F32)
    zeros = jnp.zeros((TM, LANE), F32)
    cat = lambda head, body: jnp.concatenate([head, body], axis=0)
    return (cat(ones, a_cos), cat(zeros, a_sin * a_sign), cat(ones, r_cos), cat(zeros, r_sin * r_sign))


def _attn_kernel(sink_ref, bias0_ref, bias1_ref, q_ref, k0_ref, k1_ref, k2_ref, k3_ref, kx_ref,
                 v0_ref, v1_ref, v2_ref, v3_ref, vx_ref, o_ref):
    g_heads = N_Q_HEADS // N_KV_HEADS
    lane = lax.broadcasted_iota(I32, (AB, LANE), 1)
    lo = lane < HEAD_DIM
    rows = lax.broadcasted_iota(I32, (g_heads * AB, 1), 0)
    k_refs = (k0_ref, k1_ref, k2_ref, k3_ref)
    v_refs = (v0_ref, v1_ref, v2_ref, v3_ref)
    biases = (bias0_ref[0], bias1_ref[0])
    chains = [(t, g) for t in range(Q_BLOCKS) for g in range(N_KV_HEADS)]
    scores, sinks, values = [], [], []
    for t, g in chains:
        ks = slice(g * LANE, (g + 1) * LANE)
        kd = jnp.concatenate([r[0, :, ks] for r in k_refs[t:t + 3]] + [kx_ref[0, :, ks]], axis=0)
        values.append(jnp.concatenate([r[0, :, ks] for r in v_refs[t:t + 3]] + [vx_ref[0, :, ks]], axis=0))
        qs = []
        for c in range(2 * g, 2 * g + 2):
            qc = q_ref[0, t * AB:(t + 1) * AB, c * LANE:(c + 1) * LANE]
            zero = jnp.zeros_like(qc)
            qs += [jnp.where(lo, qc, zero), jnp.where(lo, zero, qc)]
        scores.append(_dot_nt(jnp.concatenate(qs, axis=0), kd) + biases[t])
        sink = jnp.full((g_heads * AB, 1), sink_ref[g_heads * g + g_heads - 1], F32)
        for h in range(g_heads - 2, -1, -1):
            sink = jnp.where(rows < (h + 1) * AB, sink_ref[g_heads * g + h], sink)
        sinks.append(sink)
    for (t, g), s, sink, vd in zip(chains, scores, sinks, values):
        m = jnp.maximum(jnp.max(s, axis=-1, keepdims=True), sink)
        p = jnp.exp(s - m)
        den = jnp.sum(p, axis=-1, keepdims=True) + jnp.exp(sink - m)
        o = _dot(p.astype(BF16), vd) / den
        for h in range(2):
            c = 2 * g + h
            o_ref[0, t * AB:(t + 1) * AB, c * LANE:(c + 1) * LANE] = jnp.where(
                lo, o[2 * h * AB:(2 * h + 1) * AB], o[(2 * h + 1) * AB:(2 * h + 2) * AB]).astype(BF16)


def _attn_bias():
    g_heads = N_Q_HEADS // N_KV_HEADS
    r = jnp.arange(g_heads * AB)[:, None] % AB
    j = jnp.arange(3 * AB + TM)[None, :]
    band = jnp.abs(j - AB - r) <= WINDOW
    is_ctx = j >= 3 * AB
    variants = (band, band & (j >= AB), band & (j < 2 * AB), jnp.zeros_like(band))
    return jnp.stack([jnp.where(v | is_ctx, 0.0, NEG) for v in variants]).astype(F32)


def _attention(q, k2, v2, sinks):
    b, s, _ = q.shape
    nb = s // AB
    assert nb - CTX_BLOCKS >= 2 and nb % Q_BLOCKS == 0 and CTX_BLOCKS % Q_BLOCKS == 0
    bias = _attn_bias()

    def variant(blk):
        return jnp.where(blk < CTX_BLOCKS, 3, jnp.where(blk == CTX_BLOCKS, 1, jnp.where(blk == nb - 1, 2, 0)))

    def kv(offset):
        return pl.BlockSpec((1, AB, KV2), lambda bi, i: (bi, jnp.clip(Q_BLOCKS * i + offset, 0, nb - 1), 0))

    def bias_spec(t):
        return pl.BlockSpec((1,) + bias.shape[1:], lambda bi, i: (variant(Q_BLOCKS * i + t), 0, 0))

    cx = pl.BlockSpec((1, TM, KV2), lambda bi, i: (bi, 0, 0))
    kvs = [kv(o) for o in range(-1, Q_BLOCKS + 1)] + [cx]
    qo = pl.BlockSpec((1, Q_BLOCKS * AB, ATT_Q), lambda bi, i: (bi, i, 0))
    return pl.pallas_call(
        _attn_kernel,
        grid=(b, nb // Q_BLOCKS),
        in_specs=[pl.BlockSpec(memory_space=pltpu.SMEM), bias_spec(0), bias_spec(1), qo] + kvs + kvs,
        out_specs=qo,
        out_shape=jax.ShapeDtypeStruct((b, s, ATT_Q), BF16),
        compiler_params=_params("arbitrary", "arbitrary"),
    )(sinks, bias, bias, q, *([k2] * len(kvs)), *([v2] * len(kvs)))


def _ret_kernel(lg_ref, qf_ref, kf_ref, vf_ref, qb_ref, kb_ref, vb_ref, yf_ref, yb_ref,
                st_ref, dm_ref, qd_ref, kd_ref, cd_ref, *, batch):
    step = pl.program_id(0)
    lane = lax.broadcasted_iota(I32, (AB, LANE), 1)
    row = lax.broadcasted_iota(I32, (AB, LANE), 0)
    lo = lane < RET_DIM
    tiles = RET_W // LANE

    @pl.when(step == 0)
    def _():
        st_ref[...] = jnp.zeros_like(st_ref)
        ii = row.astype(F32)
        jj = lane.astype(F32)
        for d in range(2):
            for c in range(tiles):
                lg0 = lg_ref[d * RET_HEADS + 2 * c]
                lg1 = lg_ref[d * RET_HEADS + 2 * c + 1]
                lgl = jnp.where(lo, lg0, lg1)
                q_exp = ii + 1.0 if d == 0 else AB - ii
                k_exp = (AB - 1.0) - ii if d == 0 else ii
                qd_ref[d * tiles + c] = jnp.exp(q_exp * lgl)
                kd_ref[d * tiles + c] = jnp.exp(k_exp * lgl)
                cd_ref[d * tiles + c] = jnp.exp(AB * lgl)
                rel = ii - jj if d == 0 else jj - ii
                for hh, lgh in enumerate((lg0, lg1)):
                    dm_ref[d * tiles + c, hh * AB:(hh + 1) * AB, :] = jnp.where(
                        rel >= 0, jnp.exp(jnp.maximum(rel, 0.0) * lgh), 0.0)

    same_head = (row < RET_DIM) == lo
    dirs = ((qf_ref, kf_ref, vf_ref, yf_ref), (qb_ref, kb_ref, vb_ref, yb_ref))
    chains = [(d, b, c) for d in range(2) for b in range(batch) for c in range(tiles)]

    def operands(d, b, c):
        q_ref, k_ref, v_ref, _ = dirs[d]
        sl = slice(c * LANE, (c + 1) * LANE)
        return q_ref[b, :, sl], k_ref[b, :, sl], v_ref[b, :, sl]

    probs = []
    for d, b, c in chains:
        q, k, _ = operands(d, b, c)
        zero = jnp.zeros_like(q)
        q2 = jnp.concatenate([jnp.where(lo, q, zero), jnp.where(lo, zero, q)], axis=0)
        probs.append((_dot_nt(q2, k) * dm_ref[d * tiles + c]).astype(BF16))
    for (d, b, c), p in zip(chains, probs):
        q, _, v = operands(d, b, c)
        t = d * tiles + c
        si = (d * batch + b) * tiles + c
        y_intra = jnp.where(lo, _dot(p[:AB], v), _dot(p[AB:], v))
        q_dec = (q.astype(F32) * qd_ref[t]).astype(BF16)
        dirs[d][3][b, :, c * LANE:(c + 1) * LANE] = (y_intra + _dot(q_dec, st_ref[si].astype(BF16))).astype(BF16)
    for d, b, c in chains:
        _, k, v = operands(d, b, c)
        t = d * tiles + c
        si = (d * batch + b) * tiles + c
        k_dec = (k.astype(F32) * kd_ref[t]).astype(BF16)
        st_ref[si] = st_ref[si] * cd_ref[t] + jnp.where(same_head, _dot_tn(k_dec, v), 0.0)


def _retention(rq, rk, rv, log_g):
    b, s, w = rq.shape
    nb = s // AB
    tiles = w // LANE

    def back(i):
        return jnp.where(i < CTX_BLOCKS, CTX_BLOCKS - 1 - i, nb - 1 + CTX_BLOCKS - i)

    fwd = pl.BlockSpec((b, AB, w), lambda i: (0, i, 0))
    bwd = pl.BlockSpec((b, AB, w), lambda i: (0, back(i), 0))
    return pl.pallas_call(
        functools.partial(_ret_kernel, batch=b),
        grid=(nb,),
        in_specs=[pl.BlockSpec(memory_space=pltpu.SMEM), fwd, fwd, fwd, bwd, bwd, bwd],
        out_specs=[fwd, bwd],
        out_shape=[jax.ShapeDtypeStruct((b, s, w), BF16)] * 2,
        scratch_shapes=[
            pltpu.VMEM((2 * b * tiles, LANE, LANE), F32),
            pltpu.VMEM((2 * tiles, 2 * AB, LANE), F32),
            pltpu.VMEM((2 * tiles, AB, LANE), F32),
            pltpu.VMEM((2 * tiles, AB, LANE), F32),
            pltpu.VMEM((2 * tiles, AB, LANE), F32),
        ],
        compiler_params=_params("arbitrary"),
    )(log_g.reshape(-1), rq, rk, rv, rq, rk, rv)


def _route_tile(logits, tri_ref, ls_ref, rw_ref, cnt_ref):
    ne = N_EXPERTS
    eidx = lax.broadcasted_iota(I32, (ne, TM), 0).astype(F32)
    tops, hots = [], []
    for _ in range(TOP_K):
        mx = jnp.max(logits, axis=0, keepdims=True)
        idx = jnp.min(jnp.where(logits == mx, eidx, float(ne)), axis=0, keepdims=True)
        hot = eidx == idx
        logits = jnp.where(hot, NEG * 2.0, logits)
        tops.append(mx)
        hots.append(hot)
    ex = [jnp.exp(mx - tops[0]) for mx in tops]
    tot = ex[0]
    for e in ex[1:]:
        tot = tot + e
    sel = jnp.zeros((ne, TM), F32)
    for hot in hots:
        sel = jnp.where(hot, 1.0, sel)
    incl = _dot(sel.astype(BF16), tri_ref[...])
    cnt = incl[:, TM - 1:TM]
    run = jnp.floor((cnt + (SUBLANE - 1.0)) * (1.0 / SUBLANE)) * SUBLANE
    ends = jnp.broadcast_to(run, (ne, LANE))
    erow = lax.broadcasted_iota(I32, (ne, LANE), 0)
    shift = 1
    while shift < ne:
        ends = ends + jnp.where(erow >= shift, pltpu.roll(ends, shift, 0), 0.0)
        shift *= 2
    slot = (ends[:, :1] - run) + incl - 1.0
    ls_rows = [jnp.sum(jnp.where(hot, slot, 0.0), axis=0, keepdims=True) for hot in hots]
    rw_rows = [e / tot for e in ex]
    lst = jnp.concatenate(ls_rows + [jnp.full((SUBLANE - TOP_K, TM), -1.0, F32)], axis=0)
    ls_ref[...] = jnp.transpose(
        jnp.concatenate([lst, jnp.full((LANE - SUBLANE, TM), -1.0, F32)], axis=0)).astype(I32)
    rw_ref[...] = jnp.transpose(jnp.concatenate(rw_rows + [jnp.zeros((LANE - TOP_K, TM), F32)], axis=0))
    cnt_ref[...] = jnp.broadcast_to(cnt, cnt_ref.shape).astype(I32)
    return lst


def _merge_kernel(x_ref, attn_ref, up_ref, uc_ref, un_ref, yf_ref, yb_ref, rg_ref, g_ref, mod_ref,
                  wbd_ref, ps_ref, woa_ref, wop_ref, wor_ref, wout_ref, n2_ref, wr_ref,
                  br_ref, avg_ref, tri_ref,
                  xo_ref, xs_ref, ls_ref, rw_ref, cnt_ref, lg_ref, h2_ref, *, seq, nt, ntt):
    i = pl.program_id(0)
    j = jnp.minimum(i, ntt - 1) % nt

    @pl.when(i == 0)
    def _():
        lg_ref[...] = jnp.zeros_like(lg_ref)
        h2_ref[...] = jnp.zeros_like(h2_ref)

    lst = _route_tile(lg_ref[...], tri_ref, ls_ref, rw_ref, cnt_ref)
    slot = lax.broadcasted_iota(I32, (SORT_ROWS, TM), 0).astype(F32)
    p = jnp.zeros((SORT_ROWS, TM), F32)
    for k in range(TOP_K):
        p = jnp.where(slot == lst[k:k + 1, :], 1.0, p)
    xs_ref[...] = _dot(p.astype(BF16), h2_ref[...])

    ext = TM + 2 * POOL_HALO
    seq_len = jnp.where(j == 0, TM, seq)
    start = jnp.where(j == 0, 0, (j - 1) * TM)
    u = uc_ref[0]
    ue = jnp.concatenate([up_ref[0], u, un_ref[0]], axis=0)
    erow = lax.broadcasted_iota(I32, (ext, 1), 0) + (start - POOL_HALO)
    ue = jnp.where((erow >= 0) & (erow < seq_len), ue, 0.0)
    t_pos = lax.broadcasted_iota(I32, (TM, 1), 0) + start
    glane = lax.broadcasted_iota(I32, (TM, POOL_WIDTH), 1) // (POOL_WIDTH // len(POOL_WINDOWS))
    run = ue
    width = 1
    diff = jnp.zeros((TM, POOL_WIDTH), F32)
    for gi, w in enumerate(POOL_WINDOWS):
        while width < w:
            run = run + pltpu.roll(run, ext - width, 0)
            width *= 2
        win = pltpu.roll(run, w // 2, 0)[POOL_HALO:POOL_HALO + TM]
        cnt = jnp.minimum(t_pos - w // 2 + w, seq_len) - jnp.maximum(t_pos - w // 2, 0)
        diff = jnp.where(glane == gi, win / cnt.astype(F32) - u, diff)
    pool = _dot(diff.astype(BF16), wbd_ref[...]) * ps_ref[...]

    def head_norm(y):
        dlt = y.astype(F32) - _dot(y, avg_ref[...])
        var = _dot_hilo(dlt * dlt, avg_ref[...])
        return dlt * lax.rsqrt(var + GN_EPS)

    rg = rg_ref[0].astype(F32)
    ret = head_norm(yf_ref[0]) * _silu(rg[:, :RET_W]) + head_norm(yb_ref[0]) * _silu(rg[:, RET_W:])

    d = x_ref.shape[-1]
    gate = lambda t: _sigmoid(g_ref[0, :, t * d:(t + 1) * d])
    m = (gate(0) * _dot(attn_ref[0], woa_ref[...]).astype(BF16)
         + gate(1) * _dot(pool.astype(BF16), wop_ref[...]).astype(BF16)
         + gate(2) * _dot(ret.astype(BF16), wor_ref[...]).astype(BF16))
    xn = x_ref[0] + mod_ref[0, 2:3, :] * _dot(m, wout_ref[...])
    xo_ref[0] = xn

    ms = jnp.mean(xn * xn, axis=-1, keepdims=True)
    h2 = xn * lax.rsqrt(ms + NORM_EPS) * n2_ref[...] * (1.0 + mod_ref[0, 4:5, :]) + mod_ref[0, 3:4, :]
    h2_ref[...] = h2.astype(BF16)
    hh, hl = _split(h2)
    ne = N_EXPERTS
    full = _dot_nt(wr_ref[...], hh)
    logits = full[:ne] + full[ne:] + _dot_nt(wr_ref[:ne], hl) + br_ref[...]
    lg_ref[...] = logits


def _merge(xm, attn, u, yf, yb, rg, gates, mod_l, seq, wts):
    b, s, d = xm.shape
    nt = s // TM
    ntt = b * nt
    hb = TM // POOL_HALO
    nh = s // POOL_HALO
    cur = lambda i: jnp.minimum(i, ntt - 1)
    lag = lambda i: jnp.maximum(i - 1, 0)
    tok = lambda width: pl.BlockSpec((1, TM, width), lambda i: (cur(i) // nt, cur(i) % nt, 0))
    full = lambda a: pl.BlockSpec(a.shape, lambda i: (0,) * a.ndim)
    lanes = pl.BlockSpec((TM, LANE), lambda i: (lag(i), 0))
    in_specs = [
        tok(d), tok(ATT_Q),
        pl.BlockSpec((1, POOL_HALO, POOL_WIDTH),
                     lambda i: (cur(i) // nt, jnp.maximum((cur(i) % nt) * hb - 1, 0), 0)),
        tok(POOL_WIDTH),
        pl.BlockSpec((1, POOL_HALO, POOL_WIDTH),
                     lambda i: (cur(i) // nt, jnp.minimum((cur(i) % nt + 1) * hb, nh - 1), 0)),
        tok(RET_W), tok(RET_W), tok(2 * RET_W), tok(3 * d),
        pl.BlockSpec((1, 6, d), lambda i: (jnp.where(cur(i) % nt == 0, b, cur(i) // nt), 0, 0)),
    ] + [full(a) for a in wts]
    return pl.pallas_call(
        functools.partial(_merge_kernel, seq=seq, nt=nt, ntt=ntt),
        grid=(ntt + 1,),
        in_specs=in_specs,
        out_specs=[tok(d), pl.BlockSpec((SORT_ROWS, d), lambda i: (lag(i), 0)), lanes, lanes,
                   pl.BlockSpec((N_EXPERTS, LANE), lambda i: (lag(i), 0))],
        out_shape=[jax.ShapeDtypeStruct((b, s, d), F32), jax.ShapeDtypeStruct((ntt * SORT_ROWS, d), F32),
                   jax.ShapeDtypeStruct((b * s, LANE), I32),
                   jax.ShapeDtypeStruct((b * s, LANE), F32), jax.ShapeDtypeStruct((ntt * N_EXPERTS, LANE), I32)],
        scratch_shapes=[pltpu.VMEM((N_EXPERTS, TM), F32), pltpu.VMEM((TM, d), BF16)],
        compiler_params=_params("arbitrary"),
    )(xm, attn, u, u, u, yf, yb, rg, gates, mod_l, *wts)


def _rows(tile_index):
    return pl.ds(pl.multiple_of(tile_index * SUBLANE, SUBLANE), SUBLANE)


def _tile_loop(count_ref, tile, fn):
    def body(j, carry):
        fn(j)
        return carry

    lax.fori_loop(0, MIN_TILES, body, 0, unroll=8)
    lax.fori_loop(MIN_TILES, count_ref[tile], body, 0)


def _tile_copies(table_ref, count_ref, tile, copy):
    _tile_loop(count_ref, tile, lambda j: copy(j, table_ref[tile * SORT_TILES + j]))


def _expert_kernel(be_ref, nu_ref, src_ref, xs_hbm, w1_ref, b1_ref, w2_ref, b2_ref, y_ref, w1b, w2b, xbuf, sem):
    bi = pl.program_id(0)
    used = bi < nu_ref[0]
    fresh = (bi == 0) | (be_ref[bi] != be_ref[jnp.maximum(bi - 1, 0)])
    cur = bi % 2
    block_tiles = EB // SUBLANE

    def each_tile(fn):
        def body(t, carry):
            fn(t)
            return carry
        lax.fori_loop(0, block_tiles, body, 0, unroll=8)

    def fetch(blk, half):
        each_tile(lambda t: pltpu.make_async_copy(
            xs_hbm.at[_rows(src_ref[blk * block_tiles + t])], xbuf.at[half, _rows(t)], sem.at[half]).start())

    @pl.when(bi == 0)
    def _():
        fetch(0, 0)

    @pl.when(bi + 1 < nu_ref[0])
    def _():
        fetch(bi + 1, 1 - cur)

    @pl.when(used & fresh)
    def _():
        w1b[...] = w1_ref[0, 0].astype(BF16)
        w2b[...] = w2_ref[0, 0].astype(BF16)

    @pl.when(jnp.logical_not(used))
    def _():
        y_ref[...] = jnp.zeros_like(y_ref)

    @pl.when(used)
    def _():
        each_tile(lambda t: pltpu.make_async_copy(
            xs_hbm.at[_rows(t)], xbuf.at[cur, _rows(t)], sem.at[cur]).wait())
        xb = xbuf[cur].astype(BF16)
        y = None
        for c in range(0, D_FF, FF_CHUNK):
            glu = _dot(xb, w1b[:, c:c + FF_CHUNK]) + b1_ref[0, 0, :, c:c + FF_CHUNK]
            lin = _dot(xb, w1b[:, D_FF + c:D_FF + c + FF_CHUNK]) + b1_ref[0, 0, :, D_FF + c:D_FF + c + FF_CHUNK]
            glu = jnp.minimum(glu, SWIGLU_LIMIT)
            lin = jnp.clip(lin, -SWIGLU_LIMIT, SWIGLU_LIMIT)
            act = glu * _sigmoid(SWIGLU_ALPHA * glu) * (lin + 1.0)
            part = _dot(act.astype(BF16), w2b[c:c + FF_CHUNK, :])
            y = part if y is None else y + part
        y_ref[...] = y + b2_ref[0, 0]


def _experts(xs, block_e, n_used, src, w1, b1, w2, b2, layer, n_blocks):
    d = xs.shape[1]
    depth, ne, _, f2 = w1.shape
    grid_spec = pltpu.PrefetchScalarGridSpec(
        num_scalar_prefetch=3,
        grid=(n_blocks,),
        in_specs=[
            pl.BlockSpec(memory_space=pl.ANY),
            pl.BlockSpec((1, 1, d, f2), lambda bi, be, *_: (layer, be[bi], 0, 0)),
            pl.BlockSpec((1, 1, 1, f2), lambda bi, be, *_: (layer, be[bi], 0, 0)),
            pl.BlockSpec((1, 1, f2 // 2, d), lambda bi, be, *_: (layer, be[bi], 0, 0)),
            pl.BlockSpec((1, 1, 1, d), lambda bi, be, *_: (layer, be[bi], 0, 0)),
        ],
        out_specs=pl.BlockSpec((EB, d), lambda bi, *_: (bi, 0)),
        scratch_shapes=[pltpu.VMEM((d, f2), BF16), pltpu.VMEM((f2 // 2, d), BF16),
                        pltpu.VMEM((2, EB, d), F32), pltpu.SemaphoreType.DMA((2,))],
    )
    return pl.pallas_call(
        _expert_kernel,
        grid_spec=grid_spec,
        out_shape=jax.ShapeDtypeStruct((n_blocks * EB, d), F32),
        compiler_params=_params("arbitrary"),
    )(block_e, n_used, src, xs, w1, b1.reshape(depth, ne, 1, f2), w2, b2.reshape(depth, ne, 1, d))


def _combine_kernel(dst_ref, cnt_ref, x_ref, ls_ref, rw_ref, mod_ref, fg_ref, ys_hbm, xo_ref, buf, sem, *, final):
    i = pl.program_id(0)
    last = pl.num_programs(0) - 1
    cur = i % 2

    def fetch(tile, half):
        _tile_copies(dst_ref, cnt_ref, tile, lambda j, t: pltpu.make_async_copy(
            ys_hbm.at[_rows(t)], buf.at[half, _rows(j)], sem.at[half]).start())

    @pl.when(i == 0)
    def _():
        buf[...] = jnp.zeros_like(buf)
        fetch(0, 0)

    @pl.when(i < last)
    def _():
        fetch(i + 1, 1 - cur)

    _tile_loop(cnt_ref, i, lambda j: pltpu.make_async_copy(
        ys_hbm.at[_rows(j)], buf.at[cur, _rows(j)], sem.at[cur]).wait())
    yb = buf[cur].astype(BF16)
    slot = lax.broadcasted_iota(I32, (TM, SORT_ROWS), 1)
    ls = ls_ref[...]
    rw = rw_ref[...]
    g = jnp.zeros((TM, SORT_ROWS), F32)
    for k in range(TOP_K):
        g = jnp.where(slot == ls[:, k:k + 1], rw[:, k:k + 1], g)
    xn = x_ref[0] + mod_ref[0, 5:6, :] * _dot(g.astype(BF16), yb)
    if final:
        ms = jnp.mean(xn * xn, axis=-1, keepdims=True)
        xn = xn * lax.rsqrt(ms + NORM_EPS) * fg_ref[...]
    xo_ref[0] = xn


def _combine(xn, ls, rw, mod_l, final_gain, ys, tabs, final):
    b, s, d = xn.shape
    nt = s // TM
    tok = pl.BlockSpec((1, TM, d), lambda i, *_: (i // nt, i % nt, 0))
    lanes = pl.BlockSpec((TM, LANE), lambda i, *_: (i, 0))
    if final:
        out_spec = pl.BlockSpec((1, TM, d), lambda i, *_: (i // nt, jnp.maximum(i % nt - 1, 0), 0))
        out_shape = jax.ShapeDtypeStruct((b, s - TM, d), F32)
    else:
        out_spec, out_shape = tok, jax.ShapeDtypeStruct((b, s, d), F32)
    grid_spec = pltpu.PrefetchScalarGridSpec(
        num_scalar_prefetch=2,
        grid=(b * nt,),
        in_specs=[tok, lanes, lanes,
                  pl.BlockSpec((1, 6, d), lambda i, *_: (jnp.where(i % nt == 0, b, i // nt), 0, 0)),
                  pl.BlockSpec((1, d), lambda i, *_: (0, 0)),
                  pl.BlockSpec(memory_space=pl.ANY)],
        out_specs=out_spec,
        scratch_shapes=[pltpu.VMEM((2, SORT_ROWS, d), F32), pltpu.SemaphoreType.DMA((2,))],
    )
    return pl.pallas_call(
        functools.partial(_combine_kernel, final=final),
        grid_spec=grid_spec,
        out_shape=out_shape,
        compiler_params=_params("arbitrary"),
    )(tabs["dst"], tabs["count"], xn, ls, rw, mod_l, final_gain.reshape(1, d), ys)


def _inproj_weight(w_in_l):
    d = w_in_l.shape[0]
    sizes = (ATT_Q, N_KV_HEADS * HEAD_DIM, N_KV_HEADS * HEAD_DIM, POOL_WIDTH,
             RET_W, RET_W, RET_W, RET_W, RET_W, d, d, d)
    parts, off = [], 0
    for sz in sizes:
        parts.append(w_in_l[:, off:off + sz])
        off += sz
    q, k, v, u, rq, rk, rv, rgf, rgb, ga, gp, gr = parts
    twice = lambda w: jnp.concatenate(
        [w[:, h * HEAD_DIM:(h + 1) * HEAD_DIM] for h in range(N_KV_HEADS) for _ in range(2)], axis=1)
    cols = [q * HEAD_DIM ** -0.5, twice(k), twice(v), rq, rk * RET_DIM ** -0.5, rv, rgf, rgb, u, ga, gp, gr]
    return jnp.concatenate(cols, axis=1).astype(BF16)


def _block_diag(blocks):
    n, r, c = blocks.shape
    out = jnp.zeros((n * r, n * c), blocks.dtype)
    for g in range(n):
        out = out.at[g * r:(g + 1) * r, g * c:(g + 1) * c].set(blocks[g])
    return out


def _routing_tables(cnt, n_blocks):
    ntt = cnt.shape[0] // N_EXPERTS
    counts = cnt.reshape(ntt, N_EXPERTS, LANE)[:, :, 0]
    run = (counts + SUBLANE - 1) // SUBLANE
    total = jnp.sum(run, axis=0)
    eb = EB // SUBLANE
    padded = (total + eb - 1) // eb * eb
    pad_end = jnp.cumsum(padded)
    pad_start = pad_end - padded
    off = pad_start[None, :] + jnp.cumsum(run, axis=0) - run
    run_end = jnp.cumsum(run, axis=1)
    j = jnp.arange(SORT_TILES)
    owner = jnp.sum(run_end[:, None, :] <= j[None, :, None], axis=2)
    mine = owner[:, :, None] == jnp.arange(N_EXPERTS)[None, None, :]
    in_region = j[None, :] + jnp.sum(jnp.where(mine, (off - (run_end - run))[:, None, :], 0), axis=2)
    dst = jnp.where(owner < N_EXPERTS, in_region, 0)
    n_used = pad_end[-1] // eb
    blk = jnp.minimum(jnp.arange(n_blocks), n_used - 1) * eb
    block_e = jnp.minimum(jnp.sum(pad_end[None, :] <= blk[:, None], axis=1), N_EXPERTS - 1)
    mine_b = block_e[:, None] == jnp.arange(N_EXPERTS)[None, :]
    of_block = lambda table: jnp.sum(jnp.where(mine_b[:, None, :], table[None, :, :], 0), axis=2)
    first = of_block(jnp.cumsum(run, axis=0) - run)
    length = of_block(run)
    local = of_block(run_end - run)
    start_b = jnp.sum(jnp.where(mine_b, pad_start[None, :], 0), axis=1)
    pos = (jnp.arange(n_blocks) * eb - start_b)[:, None] + jnp.arange(eb)[None, :]
    inside = (pos[:, :, None] >= first[:, None, :]) & (pos[:, :, None] < (first + length)[:, None, :])
    where = (jnp.arange(ntt) * SORT_TILES)[None, None, :] + (local - first)[:, None, :] + pos[:, :, None]
    zero_tile = SORT_TILES - 1
    src = jnp.where(jnp.any(inside, axis=2), jnp.sum(jnp.where(inside, where, 0), axis=2), zero_tile)
    tabs = dict(dst=dst.reshape(-1).astype(I32), count=run_end[:, -1].astype(I32),
                src=src.reshape(-1).astype(I32))
    return tabs, block_e.astype(I32), n_used.reshape(1).astype(I32)


def kernel(x, c, ctx, c_ctx, w_mod, b_mod, norm1, norm2, w_in, attn_sinks, pool_w, pool_scale, ret_decay,
           w_o_attn, w_o_pool, w_o_ret, w_out, w_router, b_router, w_expert_in, b_expert_in, w_expert_out,
           b_expert_out, final_norm):
    b, seq, d = x.shape
    depth = w_mod.shape[0]
    assert ctx.shape[1] == TM and seq % TM == 0 and seq % GRID_W == 0
    s = seq + TM
    xm = jnp.concatenate([ctx, x], axis=1)
    mod = _modulation(c, c_ctx, w_mod, b_mod)
    tables = _rope_tables(seq)
    ntt = b * s // TM
    n_blocks = -(-(b * s * TOP_K + ntt * N_EXPERTS * (SUBLANE - 1)) // EB) + N_EXPERTS
    avg = _block_diag(jnp.full((RET_HEADS, RET_DIM, RET_DIM), 1.0 / RET_DIM, F32)).astype(BF16)
    tri = (jnp.arange(TM)[:, None] <= jnp.arange(TM)[None, :]).astype(BF16)
    for l in range(depth):
        q, k2, v2, rq, rk, rv, rg, u, gates = _inproj(xm, mod[l], norm1[l], _inproj_weight(w_in[l]), tables)
        attn = _attention(q, k2, v2, attn_sinks[l])
        yf, yb = _retention(rq, rk, rv, jax.nn.log_sigmoid(ret_decay[l].astype(F32)))
        wr_hilo = jnp.concatenate(_split(w_router[l].T), axis=0)
        br = jnp.broadcast_to(b_router[l][:, None], (N_EXPERTS, TM))
        wts = (_block_diag(pool_w[l]).astype(BF16), pool_scale[l].reshape(1, -1),
               w_o_attn[l].astype(BF16), w_o_pool[l].astype(BF16), w_o_ret[l].astype(BF16),
               w_out[l].astype(BF16), norm2[l].reshape(1, d), wr_hilo, br, avg, tri)
        xn, xs, ls, rw, cnt = _merge(xm, attn, u, yf, yb, rg, gates, mod[l], seq, wts)
        tabs, block_e, n_used = _routing_tables(cnt, n_blocks)
        ys = _experts(xs, block_e, n_used, tabs["src"], w_expert_in, b_expert_in, w_expert_out, b_expert_out, l,
                      n_blocks)
        xm = _combine(xn, ls, rw, mod[l], final_norm, ys, tabs, final=l == depth - 1)
    return xm
```

```python
import functools

import jax
import jax.numpy as jnp
from jax import lax
from jax.experimental import pallas as pl
from jax.experimental.pallas import tpu as pltpu

F32 = jnp.float32
BF16 = jnp.bfloat16
I32 = jnp.int32

GRID_W = 64
HEAD_DIM = 64
N_Q_HEADS = 8
N_KV_HEADS = 2
WINDOW = 128
ROPE_THETA = 10000.0
POOL_WIDTH = 256
POOL_WINDOWS = (2, 4, 8, 16)
RET_HEADS = 4
RET_DIM = 64
N_EXPERTS = 32
TOP_K = 4
D_FF = 1024
SWIGLU_LIMIT = 7.0
SWIGLU_ALPHA = 1.702
NORM_EPS = 1e-6
GN_EPS = 1e-5

LANE = 128
SUBLANE = 8
MXU_N = 256
TM = 256
AB = 128
IN_TILE = 640
CTX_BLOCKS = TM // AB
Q_BLOCKS = 2
EB = 512
FF_CHUNK = 256
POOL_HALO = SUBLANE
SORT_ROWS = TM * TOP_K + N_EXPERTS * SUBLANE
SORT_TILES = SORT_ROWS // SUBLANE
MIN_TILES = TM * TOP_K // SUBLANE
NEG = -1e30
VMEM_LIMIT = 56 * 1024 * 1024

ATT_Q = N_Q_HEADS * HEAD_DIM
KV2 = 2 * N_KV_HEADS * HEAD_DIM
RET_W = RET_HEADS * RET_DIM


def _dot(a, b):
    return jnp.dot(a, b, preferred_element_type=F32)


def _dot_nt(a, b):
    return lax.dot_general(a, b, (((1,), (1,)), ((), ())), preferred_element_type=F32)


def _dot_tn(a, b):
    return lax.dot_general(a, b, (((0,), (0,)), ((), ())), preferred_element_type=F32)


def _split(a):
    hi = a.astype(BF16)
    lo = (a - hi.astype(F32)).astype(BF16)
    return hi, lo


def _dot_hilo(a, m):
    hi, lo = _split(a)
    return _dot(hi, m) + _dot(lo, m)


def _sigmoid(x):
    return 0.5 * jnp.tanh(0.5 * x) + 0.5


def _silu(x):
    return x * _sigmoid(x)


def _params(*sem):
    return pltpu.CompilerParams(dimension_semantics=sem, vmem_limit_bytes=VMEM_LIMIT)


def _mod_kernel(c_ref, w_ref, b_ref, o_ref):
    s = _silu(c_ref[...])
    sh, sl = _split(s)
    wh, wl = _split(w_ref[0])
    o_ref[0] = _dot(sh, wh) + _dot(sh, wl) + _dot(sl, wh) + b_ref[0]


def _modulation(c, c_ctx, w_mod, b_mod):
    depth, d, six_d = w_mod.shape
    b = c.shape[0]
    cc = jnp.zeros((SUBLANE, d), F32).at[:b].set(c).at[b].set(c_ctx)
    out = pl.pallas_call(
        _mod_kernel,
        grid=(depth, six_d // d),
        in_specs=[
            pl.BlockSpec((SUBLANE, d), lambda l, n: (0, 0)),
            pl.BlockSpec((1, d, d), lambda l, n: (l, 0, n)),
            pl.BlockSpec((1, 1, d), lambda l, n: (l, 0, n)),
        ],
        out_specs=pl.BlockSpec((1, SUBLANE, d), lambda l, n: (l, 0, n)),
        out_shape=jax.ShapeDtypeStruct((depth, SUBLANE, six_d), F32),
        compiler_params=_params("arbitrary", "arbitrary"),
    )(cc, w_mod, b_mod.reshape(depth, 1, six_d))
    return out[:, : b + 1].reshape(depth, b + 1, 6, d)


def _rope(x, cos, sin, half, first):
    partner = jnp.where(first, pltpu.roll(x, LANE - half, 1), pltpu.roll(x, half, 1))
    return x * cos + partner * sin


def _inproj_kernel(x_ref, mod_ref, modc_ref, n1_ref, w_ref, ac_ref, as_ref, rc_ref, rs_ref,
                   q_ref, k_ref, v_ref, rq_ref, rk_ref, rv_ref, rg_ref, u_ref, g_ref):
    x = x_ref[0]
    rows = x.shape[0]
    ms = jnp.mean(x * x, axis=-1, keepdims=True)
    y = x * lax.rsqrt(ms + NORM_EPS) * n1_ref[...]
    is_ctx = (lax.broadcasted_iota(I32, (rows, 1), 0) < TM) & (pl.program_id(1) == 0)
    scale = jnp.where(is_ctx, modc_ref[0, 1:2, :], mod_ref[0, 1:2, :])
    shift = jnp.where(is_ctx, modc_ref[0, 0:1, :], mod_ref[0, 0:1, :])
    hb = (y * (1.0 + scale) + shift).astype(BF16)
    lane = lax.broadcasted_iota(I32, (rows, LANE), 1)
    a_first = (lane % (HEAD_DIM // 2)) < (HEAD_DIM // 4)
    r_first = (lane % RET_DIM) < (RET_DIM // 2)
    ac, asn, rc, rsn = ac_ref[...], as_ref[...], rc_ref[...], rs_ref[...]

    def proj(off, width):
        return _dot(hb, w_ref[:, off:off + width])

    def rotated(ref, width, cos, sin, half, first, off):
        for t in range(0, width, MXU_N):
            pr = proj(off + t, MXU_N)
            for g in range(0, MXU_N, LANE):
                ref[0, :, t + g:t + g + LANE] = _rope(pr[:, g:g + LANE], cos, sin, half, first).astype(BF16)

    off = 0
    rotated(q_ref, ATT_Q, ac, asn, HEAD_DIM // 4, a_first, off)
    off += ATT_Q
    rotated(k_ref, KV2, ac, asn, HEAD_DIM // 4, a_first, off)
    off += KV2
    v_ref[0] = proj(off, KV2).astype(BF16)
    off += KV2
    for ref in (rq_ref, rk_ref):
        rotated(ref, RET_W, rc, rsn, RET_DIM // 2, r_first, off)
        off += RET_W
    rv_ref[0] = proj(off, RET_W).astype(BF16)
    off += RET_W
    rg_ref[0] = proj(off, 2 * RET_W).astype(BF16)
    off += 2 * RET_W
    u_ref[0] = proj(off, POOL_WIDTH)
    off += POOL_WIDTH
    d = x.shape[-1]
    for t in range(3):
        g_ref[0, :, t * d:(t + 1) * d] = proj(off, d).astype(BF16)
        off += d


def _inproj(xm, mod_l, norm1_l, w1, tables):
    b, s, d = xm.shape
    ti = IN_TILE if s % IN_TILE == 0 else TM
    wcols = w1.shape[1]
    tok = lambda width: pl.BlockSpec((1, ti, width), lambda bi, j: (bi, j, 0))
    tab = pl.BlockSpec((ti, LANE), lambda bi, j: (j, 0))
    widths = (ATT_Q, KV2, KV2, RET_W, RET_W, RET_W, 2 * RET_W, POOL_WIDTH, 3 * d)
    dtypes = (BF16,) * 7 + (F32, BF16)
    return pl.pallas_call(
        _inproj_kernel,
        grid=(b, s // ti),
        in_specs=[
            tok(d),
            pl.BlockSpec((1, 6, d), lambda bi, j: (bi, 0, 0)),
            pl.BlockSpec((1, 6, d), lambda bi, j: (b, 0, 0)),
            pl.BlockSpec((1, d), lambda bi, j: (0, 0)),
            pl.BlockSpec((d, wcols), lambda bi, j: (0, 0), pipeline_mode=pl.Buffered(1)),
            tab, tab, tab, tab,
        ],
        out_specs=[tok(w) for w in widths],
        out_shape=[jax.ShapeDtypeStruct((b, s, w), dt) for w, dt in zip(widths, dtypes)],
        compiler_params=_params("arbitrary", "arbitrary"),
    )(xm, mod_l, mod_l, norm1_l.reshape(1, d), w1, *tables)


def _rope_tables(seq):
    rows = seq // GRID_W
    rpos = jnp.arange(rows, dtype=F32)[:, None]
    cpos = jnp.arange(GRID_W, dtype=F32)[:, None]
    lane = jnp.arange(LANE)
    grid = lambda per_row, per_col: (per_row[:, None, :] + per_col[None, :, :]).reshape(seq, LANE)
    axis_dim = HEAD_DIM // 2
    inv_a = ROPE_THETA ** (-jnp.arange(0, axis_dim, 2, dtype=F32) / axis_dim)
    hl = lane % HEAD_DIM
    inv_al = inv_a[(hl % axis_dim) % (axis_dim // 2)][None, :]
    by_row = (hl < axis_dim)[None, :]
    a_sign = jnp.where((hl % axis_dim) < axis_dim // 2, -1.0, 1.0)[None, :]
    a_cos = grid(jnp.where(by_row, jnp.cos(rpos * inv_al), 0.0), jnp.where(by_row, 0.0, jnp.cos(cpos * inv_al)))
    a_sin = grid(jnp.where(by_row, jnp.sin(rpos * inv_al), 0.0), jnp.where(by_row, 0.0, jnp.sin(cpos * inv_al)))
    inv_r = 1.0 / (ROPE_THETA ** jnp.linspace(0.0, 1.0, RET_DIM // 2, dtype=F32))
    rl = lane % RET_DIM
    inv_rl = inv_r[rl % (RET_DIM // 2)][None, :]
    r_sign = jnp.where(rl < RET_DIM // 2, -1.0, 1.0)[None, :]
    hi = (rpos * GRID_W) * inv_rl
    lo = cpos * inv_rl
    outer = lambda a, b: (a[:, None, :] * b[None, :, :]).reshape(seq, LANE)
    r_cos = outer(jnp.cos(hi), jnp.cos(lo)) - outer(jnp.sin(hi), jnp.sin(lo))
    r_sin = outer(jnp.sin(hi), jnp.cos(lo)) + outer(jnp.cos(hi), jnp.sin(lo))
    ones = jnp.ones((TM, LANE), F32)
    zeros = jnp.zeros((TM, LANE), F32)
    cat = lambda head, body: jnp.concatenate([head, body], axis=0)
    return (cat(ones, a_cos), cat(zeros, a_sin * a_sign), cat(ones, r_cos), cat(zeros, r_sin * r_sign))


def _attn_kernel(sink_ref, bias0_ref, bias1_ref, q_ref, k0_ref, k1_ref, k2_ref, k3_ref, kx_ref,
                 v0_ref, v1_ref, v2_ref, v3_ref, vx_ref, o_ref):
    g_heads = N_Q_HEADS // N_KV_HEADS
    lane = lax.broadcasted_iota(I32, (AB, LANE), 1)
    lo = lane < HEAD_DIM
    rows = lax.broadcasted_iota(I32, (g_heads * AB, 1), 0)
    k_refs = (k0_ref, k1_ref, k2_ref, k3_ref)
    v_refs = (v0_ref, v1_ref, v2_ref, v3_ref)
    biases = (bias0_ref[0], bias1_ref[0])
    chains = [(t, g) for t in range(Q_BLOCKS) for g in range(N_KV_HEADS)]
    scores, sinks, values = [], [], []
    for t, g in chains:
        ks = slice(g * LANE, (g + 1) * LANE)
        kd = jnp.concatenate([r[0, :, ks] for r in k_refs[t:t + 3]] + [kx_ref[0, :, ks]], axis=0)
        values.append(jnp.concatenate([r[0, :, ks] for r in v_refs[t:t + 3]] + [vx_ref[0, :, ks]], axis=0))
        qs = []
        for c in range(2 * g, 2 * g + 2):
            qc = q_ref[0, t * AB:(t + 1) * AB, c * LANE:(c + 1) * LANE]
            zero = jnp.zeros_like(qc)
            qs += [jnp.where(lo, qc, zero), jnp.where(lo, zero, qc)]
        scores.append(_dot_nt(jnp.concatenate(qs, axis=0), kd) + biases[t])
        sink = jnp.full((g_heads * AB, 1), sink_ref[g_heads * g + g_heads - 1], F32)
        for h in range(g_heads - 2, -1, -1):
            sink = jnp.where(rows < (h + 1) * AB, sink_ref[g_heads * g + h], sink)
        sinks.append(sink)
    for (t, g), s, sink, vd in zip(chains, scores, sinks, values):
        m = jnp.maximum(jnp.max(s, axis=-1, keepdims=True), sink)
        p = jnp.exp(s - m)
        den = jnp.sum(p, axis=-1, keepdims=True) + jnp.exp(sink - m)
        o = _dot(p.astype(BF16), vd) / den
        for h in range(2):
            c = 2 * g + h
            o_ref[0, t * AB:(t + 1) * AB, c * LANE:(c + 1) * LANE] = jnp.where(
                lo, o[2 * h * AB:(2 * h + 1) * AB], o[(2 * h + 1) * AB:(2 * h + 2) * AB]).astype(BF16)


def _attn_bias():
    g_heads = N_Q_HEADS // N_KV_HEADS
    r = jnp.arange(g_heads * AB)[:, None] % AB
    j = jnp.arange(3 * AB + TM)[None, :]
    band = jnp.abs(j - AB - r) <= WINDOW
    is_ctx = j >= 3 * AB
    variants = (band, band & (j >= AB), band & (j < 2 * AB), jnp.zeros_like(band))
    return jnp.stack([jnp.where(v | is_ctx, 0.0, NEG) for v in variants]).astype(F32)


def _attention(q, k2, v2, sinks):
    b, s, _ = q.shape
    nb = s // AB
    assert nb - CTX_BLOCKS >= 2 and nb % Q_BLOCKS == 0 and CTX_BLOCKS % Q_BLOCKS == 0
    bias = _attn_bias()

    def variant(blk):
        return jnp.where(blk < CTX_BLOCKS, 3, jnp.where(blk == CTX_BLOCKS, 1, jnp.where(blk == nb - 1, 2, 0)))

    def kv(offset):
        return pl.BlockSpec((1, AB, KV2), lambda bi, i: (bi, jnp.clip(Q_BLOCKS * i + offset, 0, nb - 1), 0))

    def bias_spec(t):
        return pl.BlockSpec((1,) + bias.shape[1:], lambda bi, i: (variant(Q_BLOCKS * i + t), 0, 0))

    cx = pl.BlockSpec((1, TM, KV2), lambda bi, i: (bi, 0, 0))
    kvs = [kv(o) for o in range(-1, Q_BLOCKS + 1)] + [cx]
    qo = pl.BlockSpec((1, Q_BLOCKS * AB, ATT_Q), lambda bi, i: (bi, i, 0))
    return pl.pallas_call(
        _attn_kernel,
        grid=(b, nb // Q_BLOCKS),
        in_specs=[pl.BlockSpec(memory_space=pltpu.SMEM), bias_spec(0), bias_spec(1), qo] + kvs + kvs,
        out_specs=qo,
        out_shape=jax.ShapeDtypeStruct((b, s, ATT_Q), BF16),
        compiler_params=_params("arbitrary", "arbitrary"),
    )(sinks, bias, bias, q, *([k2] * len(kvs)), *([v2] * len(kvs)))


def _ret_kernel(lg_ref, qf_ref, kf_ref, vf_ref, qb_ref, kb_ref, vb_ref, yf_ref, yb_ref,
                st_ref, dm_ref, qd_ref, kd_ref, cd_ref, *, batch):
    step = pl.program_id(0)
    lane = lax.broadcasted_iota(I32, (AB, LANE), 1)
    row = lax.broadcasted_iota(I32, (AB, LANE), 0)
    lo = lane < RET_DIM
    tiles = RET_W // LANE

    @pl.when(step == 0)
    def _():
        st_ref[...] = jnp.zeros_like(st_ref)
        ii = row.astype(F32)
        jj = lane.astype(F32)
        for d in range(2):
            for c in range(tiles):
                lg0 = lg_ref[d * RET_HEADS + 2 * c]
                lg1 = lg_ref[d * RET_HEADS + 2 * c + 1]
                lgl = jnp.where(lo, lg0, lg1)
                q_exp = ii + 1.0 if d == 0 else AB - ii
                k_exp = (AB - 1.0) - ii if d == 0 else ii
                qd_ref[d * tiles + c] = jnp.exp(q_exp * lgl)
                kd_ref[d * tiles + c] = jnp.exp(k_exp * lgl)
                cd_ref[d * tiles + c] = jnp.exp(AB * lgl)
                rel = ii - jj if d == 0 else jj - ii
                for hh, lgh in enumerate((lg0, lg1)):
                    dm_ref[d * tiles + c, hh * AB:(hh + 1) * AB, :] = jnp.where(
                        rel >= 0, jnp.exp(jnp.maximum(rel, 0.0) * lgh), 0.0)

    same_head = (row < RET_DIM) == lo
    dirs = ((qf_ref, kf_ref, vf_ref, yf_ref), (qb_ref, kb_ref, vb_ref, yb_ref))
    chains = [(d, b, c) for d in range(2) for b in range(batch) for c in range(tiles)]

    def operands(d, b, c):
        q_ref, k_ref, v_ref, _ = dirs[d]
        sl = slice(c * LANE, (c + 1) * LANE)
        return q_ref[b, :, sl], k_ref[b, :, sl], v_ref[b, :, sl]

    probs = []
    for d, b, c in chains:
        q, k, _ = operands(d, b, c)
        zero = jnp.zeros_like(q)
        q2 = jnp.concatenate([jnp.where(lo, q, zero), jnp.where(lo, zero, q)], axis=0)
        probs.append((_dot_nt(q2, k) * dm_ref[d * tiles + c]).astype(BF16))
    for (d, b, c), p in zip(chains, probs):
        q, _, v = operands(d, b, c)
        t = d * tiles + c
        si = (d * batch + b) * tiles + c
        y_intra = jnp.where(lo, _dot(p[:AB], v), _dot(p[AB:], v))
        q_dec = (q.astype(F32) * qd_ref[t]).astype(BF16)
        dirs[d][3][b, :, c * LANE:(c + 1) * LANE] = (y_intra + _dot(q_dec, st_ref[si].astype(BF16))).astype(BF16)
    for d, b, c in chains:
        _, k, v = operands(d, b, c)
        t = d * tiles + c
        si = (d * batch + b) * tiles + c
        k_dec = (k.astype(F32) * kd_ref[t]).astype(BF16)
        st_ref[si] = st_ref[si] * cd_ref[t] + jnp.where(same_head, _dot_tn(k_dec, v), 0.0)


def _retention(rq, rk, rv, log_g):
    b, s, w = rq.shape
    nb = s // AB
    tiles = w // LANE

    def back(i):
        return jnp.where(i < CTX_BLOCKS, CTX_BLOCKS - 1 - i, nb - 1 + CTX_BLOCKS - i)

    fwd = pl.BlockSpec((b, AB, w), lambda i: (0, i, 0))
    bwd = pl.BlockSpec((b, AB, w), lambda i: (0, back(i), 0))
    return pl.pallas_call(
        functools.partial(_ret_kernel, batch=b),
        grid=(nb,),
        in_specs=[pl.BlockSpec(memory_space=pltpu.SMEM), fwd, fwd, fwd, bwd, bwd, bwd],
        out_specs=[fwd, bwd],
        out_shape=[jax.ShapeDtypeStruct((b, s, w), BF16)] * 2,
        scratch_shapes=[
            pltpu.VMEM((2 * b * tiles, LANE, LANE), F32),
            pltpu.VMEM((2 * tiles, 2 * AB, LANE), F32),
            pltpu.VMEM((2 * tiles, AB, LANE), F32),
            pltpu.VMEM((2 * tiles, AB, LANE), F32),
            pltpu.VMEM((2 * tiles, AB, LANE), F32),
        ],
        compiler_params=_params("arbitrary"),
    )(log_g.reshape(-1), rq, rk, rv, rq, rk, rv)


def _route_tile(logits, tri_ref, ls_ref, rw_ref, cnt_ref):
    ne = N_EXPERTS
    eidx = lax.broadcasted_iota(I32, (ne, TM), 0).astype(F32)
    tops, hots = [], []
    for _ in range(TOP_K):
        mx = jnp.max(logits, axis=0, keepdims=True)
        idx = jnp.min(jnp.where(logits == mx, eidx, float(ne)), axis=0, keepdims=True)
        hot = eidx == idx
        logits = jnp.where(hot, NEG * 2.0, logits)
        tops.append(mx)
        hots.append(hot)
    ex = [jnp.exp(mx - tops[0]) for mx in tops]
    tot = ex[0]
    for e in ex[1:]:
        tot = tot + e
    sel = jnp.zeros((ne, TM), F32)
    for hot in hots:
        sel = jnp.where(hot, 1.0, sel)
    incl = _dot(sel.astype(BF16), tri_ref[...])
    cnt = incl[:, TM - 1:TM]
    run = jnp.floor((cnt + (SUBLANE - 1.0)) * (1.0 / SUBLANE)) * SUBLANE
    ends = jnp.broadcast_to(run, (ne, LANE))
    erow = lax.broadcasted_iota(I32, (ne, LANE), 0)
    shift = 1
    while shift < ne:
        ends = ends + jnp.where(erow >= shift, pltpu.roll(ends, shift, 0), 0.0)
        shift *= 2
    slot = (ends[:, :1] - run) + incl - 1.0
    ls_rows = [jnp.sum(jnp.where(hot, slot, 0.0), axis=0, keepdims=True) for hot in hots]
    rw_rows = [e / tot for e in ex]
    lst = jnp.concatenate(ls_rows + [jnp.full((SUBLANE - TOP_K, TM), -1.0, F32)], axis=0)
    ls_ref[...] = jnp.transpose(
        jnp.concatenate([lst, jnp.full((LANE - SUBLANE, TM), -1.0, F32)], axis=0)).astype(I32)
    rw_ref[...] = jnp.transpose(jnp.concatenate(rw_rows + [jnp.zeros((LANE - TOP_K, TM), F32)], axis=0))
    cnt_ref[...] = jnp.broadcast_to(cnt, cnt_ref.shape).astype(I32)
    return lst


def _merge_kernel(x_ref, attn_ref, up_ref, uc_ref, un_ref, yf_ref, yb_ref, rg_ref, g_ref, mod_ref,
                  wbd_ref, ps_ref, woa_ref, wop_ref, wor_ref, wout_ref, n2_ref, wr_ref,
                  br_ref, avg_ref, tri_ref,
                  xo_ref, xs_ref, ls_ref, rw_ref, cnt_ref, lg_ref, h2_ref, *, seq, nt, ntt):
    i = pl.program_id(0)
    j = jnp.minimum(i, ntt - 1) % nt

    @pl.when(i == 0)
    def _():
        lg_ref[...] = jnp.zeros_like(lg_ref)
        h2_ref[...] = jnp.zeros_like(h2_ref)

    lst = _route_tile(lg_ref[...], tri_ref, ls_ref, rw_ref, cnt_ref)

    ext = TM + 2 * POOL_HALO
    seq_len = jnp.where(j == 0, TM, seq)
    start = jnp.where(j == 0, 0, (j - 1) * TM)
    u = uc_ref[0]
    ue = jnp.concatenate([up_ref[0], u, un_ref[0]], axis=0)
    erow = lax.broadcasted_iota(I32, (ext, 1), 0) + (start - POOL_HALO)
    ue = jnp.where((erow >= 0) & (erow < seq_len), ue, 0.0)
    t_pos = lax.broadcasted_iota(I32, (TM, 1), 0) + start
    glane = lax.broadcasted_iota(I32, (TM, POOL_WIDTH), 1) // (POOL_WIDTH // len(POOL_WINDOWS))
    run = ue
    width = 1
    diff = jnp.zeros((TM, POOL_WIDTH), F32)
    for gi, w in enumerate(POOL_WINDOWS):
        while width < w:
            run = run + pltpu.roll(run, ext - width, 0)
            width *= 2
        win = pltpu.roll(run, w // 2, 0)[POOL_HALO:POOL_HALO + TM]
        cnt = jnp.minimum(t_pos - w // 2 + w, seq_len) - jnp.maximum(t_pos - w // 2, 0)
        diff = jnp.where(glane == gi, win / cnt.astype(F32) - u, diff)
    pool = _dot(diff.astype(BF16), wbd_ref[...]) * ps_ref[...]

    def head_norm(y):
        dlt = y.astype(F32) - _dot(y, avg_ref[...])
        var = _dot_hilo(dlt * dlt, avg_ref[...])
        return dlt * lax.rsqrt(var + GN_EPS)

    rg = rg_ref[0].astype(F32)
    ret = head_norm(yf_ref[0]) * _silu(rg[:, :RET_W]) + head_norm(yb_ref[0]) * _silu(rg[:, RET_W:])

    slot = lax.broadcasted_iota(I32, (SORT_ROWS, TM), 0).astype(F32)
    p = jnp.zeros((SORT_ROWS, TM), F32)
    for k in range(TOP_K):
        p = jnp.where(slot == lst[k:k + 1, :], 1.0, p)
    xs_ref[...] = _dot(p.astype(BF16), h2_ref[...])

    d = x_ref.shape[-1]
    gate = lambda t: _sigmoid(g_ref[0, :, t * d:(t + 1) * d])
    m = (gate(0) * _dot(attn_ref[0], woa_ref[...]).astype(BF16)
         + gate(1) * _dot(pool.astype(BF16), wop_ref[...]).astype(BF16)
         + gate(2) * _dot(ret.astype(BF16), wor_ref[...]).astype(BF16))
    xn = x_ref[0] + mod_ref[0, 2:3, :] * _dot(m, wout_ref[...])
    xo_ref[0] = xn

    ms = jnp.mean(xn * xn, axis=-1, keepdims=True)
    h2 = xn * lax.rsqrt(ms + NORM_EPS) * n2_ref[...] * (1.0 + mod_ref[0, 4:5, :]) + mod_ref[0, 3:4, :]
    h2_ref[...] = h2.astype(BF16)
    hh, hl = _split(h2)
    ne = N_EXPERTS
    full = _dot_nt(wr_ref[...], hh)
    logits = full[:ne] + full[ne:] + _dot_nt(wr_ref[:ne], hl) + br_ref[...]
    lg_ref[...] = logits


def _merge(xm, attn, u, yf, yb, rg, gates, mod_l, seq, wts):
    b, s, d = xm.shape
    nt = s // TM
    ntt = b * nt
    hb = TM // POOL_HALO
    nh = s // POOL_HALO
    cur = lambda i: jnp.minimum(i, ntt - 1)
    lag = lambda i: jnp.maximum(i - 1, 0)
    tok = lambda width: pl.BlockSpec((1, TM, width), lambda i: (cur(i) // nt, cur(i) % nt, 0))
    full = lambda a: pl.BlockSpec(a.shape, lambda i: (0,) * a.ndim)
    lanes = pl.BlockSpec((TM, LANE), lambda i: (lag(i), 0))
    in_specs = [
        tok(d), tok(ATT_Q),
        pl.BlockSpec((1, POOL_HALO, POOL_WIDTH),
                     lambda i: (cur(i) // nt, jnp.maximum((cur(i) % nt) * hb - 1, 0), 0)),
        tok(POOL_WIDTH),
        pl.BlockSpec((1, POOL_HALO, POOL_WIDTH),
                     lambda i: (cur(i) // nt, jnp.minimum((cur(i) % nt + 1) * hb, nh - 1), 0)),
        tok(RET_W), tok(RET_W), tok(2 * RET_W), tok(3 * d),
        pl.BlockSpec((1, 6, d), lambda i: (jnp.where(cur(i) % nt == 0, b, cur(i) // nt), 0, 0)),
    ] + [full(a) for a in wts]
    return pl.pallas_call(
        functools.partial(_merge_kernel, seq=seq, nt=nt, ntt=ntt),
        grid=(ntt + 1,),
        in_specs=in_specs,
        out_specs=[tok(d), pl.BlockSpec((SORT_ROWS, d), lambda i: (lag(i), 0)), lanes, lanes,
                   pl.BlockSpec((N_EXPERTS, LANE), lambda i: (lag(i), 0))],
        out_shape=[jax.ShapeDtypeStruct((b, s, d), F32), jax.ShapeDtypeStruct((ntt * SORT_ROWS, d), F32),
                   jax.ShapeDtypeStruct((b * s, LANE), I32),
                   jax.ShapeDtypeStruct((b * s, LANE), F32), jax.ShapeDtypeStruct((ntt * N_EXPERTS, LANE), I32)],
        scratch_shapes=[pltpu.VMEM((N_EXPERTS, TM), F32), pltpu.VMEM((TM, d), BF16)],
        compiler_params=_params("arbitrary"),
    )(xm, attn, u, u, u, yf, yb, rg, gates, mod_l, *wts)


def _rows(tile_index):
    return pl.ds(pl.multiple_of(tile_index * SUBLANE, SUBLANE), SUBLANE)


def _tile_loop(count_ref, tile, fn):
    def body(j, carry):
        fn(j)
        return carry

    lax.fori_loop(0, MIN_TILES, body, 0, unroll=8)
    lax.fori_loop(MIN_TILES, count_ref[tile], body, 0)


def _tile_copies(table_ref, count_ref, tile, copy):
    _tile_loop(count_ref, tile, lambda j: copy(j, table_ref[tile * SORT_TILES + j]))


def _expert_kernel(be_ref, nu_ref, src_ref, xs_hbm, w1_ref, b1_ref, w2_ref, b2_ref, y_ref, w1b, w2b, xbuf, sem):
    bi = pl.program_id(0)
    used = bi < nu_ref[0]
    fresh = (bi == 0) | (be_ref[bi] != be_ref[jnp.maximum(bi - 1, 0)])
    cur = bi % 2
    block_tiles = EB // SUBLANE

    def each_tile(fn):
        def body(t, carry):
            fn(t)
            return carry
        lax.fori_loop(0, block_tiles, body, 0, unroll=8)

    def fetch(blk, half):
        each_tile(lambda t: pltpu.make_async_copy(
            xs_hbm.at[_rows(src_ref[blk * block_tiles + t])], xbuf.at[half, _rows(t)], sem.at[half]).start())

    @pl.when(bi == 0)
    def _():
        fetch(0, 0)

    @pl.when(bi + 1 < nu_ref[0])
    def _():
        fetch(bi + 1, 1 - cur)

    @pl.when(used & fresh)
    def _():
        w1b[...] = w1_ref[0, 0].astype(BF16)
        w2b[...] = w2_ref[0, 0].astype(BF16)

    @pl.when(jnp.logical_not(used))
    def _():
        y_ref[...] = jnp.zeros_like(y_ref)

    @pl.when(used)
    def _():
        each_tile(lambda t: pltpu.make_async_copy(
            xs_hbm.at[_rows(t)], xbuf.at[cur, _rows(t)], sem.at[cur]).wait())
        xb = xbuf[cur].astype(BF16)
        y = None
        for c in range(0, D_FF, FF_CHUNK):
            glu = _dot(xb, w1b[:, c:c + FF_CHUNK]) + b1_ref[0, 0, :, c:c + FF_CHUNK]
            lin = _dot(xb, w1b[:, D_FF + c:D_FF + c + FF_CHUNK]) + b1_ref[0, 0, :, D_FF + c:D_FF + c + FF_CHUNK]
            glu = jnp.minimum(glu, SWIGLU_LIMIT)
            lin = jnp.clip(lin, -SWIGLU_LIMIT, SWIGLU_LIMIT)
            act = glu * _sigmoid(SWIGLU_ALPHA * glu) * (lin + 1.0)
            part = _dot(act.astype(BF16), w2b[c:c + FF_CHUNK, :])
            y = part if y is None else y + part
        y_ref[...] = y + b2_ref[0, 0]


def _experts(xs, block_e, n_used, src, w1, b1, w2, b2, layer, n_blocks):
    d = xs.shape[1]
    depth, ne, _, f2 = w1.shape
    grid_spec = pltpu.PrefetchScalarGridSpec(
        num_scalar_prefetch=3,
        grid=(n_blocks,),
        in_specs=[
            pl.BlockSpec(memory_space=pl.ANY),
            pl.BlockSpec((1, 1, d, f2), lambda bi, be, *_: (layer, be[bi], 0, 0)),
            pl.BlockSpec((1, 1, 1, f2), lambda bi, be, *_: (layer, be[bi], 0, 0)),
            pl.BlockSpec((1, 1, f2 // 2, d), lambda bi, be, *_: (layer, be[bi], 0, 0)),
            pl.BlockSpec((1, 1, 1, d), lambda bi, be, *_: (layer, be[bi], 0, 0)),
        ],
        out_specs=pl.BlockSpec((EB, d), lambda bi, *_: (bi, 0)),
        scratch_shapes=[pltpu.VMEM((d, f2), BF16), pltpu.VMEM((f2 // 2, d), BF16),
                        pltpu.VMEM((2, EB, d), F32), pltpu.SemaphoreType.DMA((2,))],
    )
    return pl.pallas_call(
        _expert_kernel,
        grid_spec=grid_spec,
        out_shape=jax.ShapeDtypeStruct((n_blocks * EB, d), F32),
        compiler_params=_params("arbitrary"),
    )(block_e, n_used, src, xs, w1, b1.reshape(depth, ne, 1, f2), w2, b2.reshape(depth, ne, 1, d))


def _combine_kernel(dst_ref, cnt_ref, x_ref, ls_ref, rw_ref, mod_ref, fg_ref, ys_hbm, xo_ref, buf, sem, *, final):
    i = pl.program_id(0)
    last = pl.num_programs(0) - 1
    cur = i % 2

    def fetch(tile, half):
        _tile_copies(dst_ref, cnt_ref, tile, lambda j, t: pltpu.make_async_copy(
            ys_hbm.at[_rows(t)], buf.at[half, _rows(j)], sem.at[half]).start())

    @pl.when(i == 0)
    def _():
        buf[...] = jnp.zeros_like(buf)
        fetch(0, 0)

    @pl.when(i < last)
    def _():
        fetch(i + 1, 1 - cur)

    _tile_loop(cnt_ref, i, lambda j: pltpu.make_async_copy(
        ys_hbm.at[_rows(j)], buf.at[cur, _rows(j)], sem.at[cur]).wait())
    yb = buf[cur].astype(BF16)
    slot = lax.broadcasted_iota(I32, (TM, SORT_ROWS), 1)
    ls = ls_ref[...]
    rw = rw_ref[...]
    g = jnp.zeros((TM, SORT_ROWS), F32)
    for k in range(TOP_K):
        g = jnp.where(slot == ls[:, k:k + 1], rw[:, k:k + 1], g)
    xn = x_ref[0] + mod_ref[0, 5:6, :] * _dot(g.astype(BF16), yb)
    if final:
        ms = jnp.mean(xn * xn, axis=-1, keepdims=True)
        xn = xn * lax.rsqrt(ms + NORM_EPS) * fg_ref[...]
    xo_ref[0] = xn


def _combine(xn, ls, rw, mod_l, final_gain, ys, tabs, final):
    b, s, d = xn.shape
    nt = s // TM
    tok = pl.BlockSpec((1, TM, d), lambda i, *_: (i // nt, i % nt, 0))
    lanes = pl.BlockSpec((TM, LANE), lambda i, *_: (i, 0))
    if final:
        out_spec = pl.BlockSpec((1, TM, d), lambda i, *_: (i // nt, jnp.maximum(i % nt - 1, 0), 0))
        out_shape = jax.ShapeDtypeStruct((b, s - TM, d), F32)
    else:
        out_spec, out_shape = tok, jax.ShapeDtypeStruct((b, s, d), F32)
    grid_spec = pltpu.PrefetchScalarGridSpec(
        num_scalar_prefetch=2,
        grid=(b * nt,),
        in_specs=[tok, lanes, lanes,
                  pl.BlockSpec((1, 6, d), lambda i, *_: (jnp.where(i % nt == 0, b, i // nt), 0, 0)),
                  pl.BlockSpec((1, d), lambda i, *_: (0, 0)),
                  pl.BlockSpec(memory_space=pl.ANY)],
        out_specs=out_spec,
        scratch_shapes=[pltpu.VMEM((2, SORT_ROWS, d), F32), pltpu.SemaphoreType.DMA((2,))],
    )
    return pl.pallas_call(
        functools.partial(_combine_kernel, final=final),
        grid_spec=grid_spec,
        out_shape=out_shape,
        compiler_params=_params("arbitrary"),
    )(tabs["dst"], tabs["count"], xn, ls, rw, mod_l, final_gain.reshape(1, d), ys)


def _inproj_weight(w_in_l):
    d = w_in_l.shape[0]
    sizes = (ATT_Q, N_KV_HEADS * HEAD_DIM, N_KV_HEADS * HEAD_DIM, POOL_WIDTH,
             RET_W, RET_W, RET_W, RET_W, RET_W, d, d, d)
    parts, off = [], 0
    for sz in sizes:
        parts.append(w_in_l[:, off:off + sz])
        off += sz
    q, k, v, u, rq, rk, rv, rgf, rgb, ga, gp, gr = parts
    twice = lambda w: jnp.concatenate(
        [w[:, h * HEAD_DIM:(h + 1) * HEAD_DIM] for h in range(N_KV_HEADS) for _ in range(2)], axis=1)
    cols = [q * HEAD_DIM ** -0.5, twice(k), twice(v), rq, rk * RET_DIM ** -0.5, rv, rgf, rgb, u, ga, gp, gr]
    return jnp.concatenate(cols, axis=1).astype(BF16)


def _block_diag(blocks):
    n, r, c = blocks.shape
    out = jnp.zeros((n * r, n * c), blocks.dtype)
    for g in range(n):
        out = out.at[g * r:(g + 1) * r, g * c:(g + 1) * c].set(blocks[g])
    return out


def _routing_tables(cnt, n_blocks):
    ntt = cnt.shape[0] // N_EXPERTS
    counts = cnt.reshape(ntt, N_EXPERTS, LANE)[:, :, 0]
    run = (counts + SUBLANE - 1) // SUBLANE
    total = jnp.sum(run, axis=0)
    eb = EB // SUBLANE
    padded = (total + eb - 1) // eb * eb
    pad_end = jnp.cumsum(padded)
    pad_start = pad_end - padded
    off = pad_start[None, :] + jnp.cumsum(run, axis=0) - run
    run_end = jnp.cumsum(run, axis=1)
    j = jnp.arange(SORT_TILES)
    owner = jnp.sum(run_end[:, None, :] <= j[None, :, None], axis=2)
    mine = owner[:, :, None] == jnp.arange(N_EXPERTS)[None, None, :]
    in_region = j[None, :] + jnp.sum(jnp.where(mine, (off - (run_end - run))[:, None, :], 0), axis=2)
    dst = jnp.where(owner < N_EXPERTS, in_region, 0)
    n_used = pad_end[-1] // eb
    blk = jnp.minimum(jnp.arange(n_blocks), n_used - 1) * eb
    block_e = jnp.minimum(jnp.sum(pad_end[None, :] <= blk[:, None], axis=1), N_EXPERTS - 1)
    mine_b = block_e[:, None] == jnp.arange(N_EXPERTS)[None, :]
    of_block = lambda table: jnp.sum(jnp.where(mine_b[:, None, :], table[None, :, :], 0), axis=2)
    first = of_block(jnp.cumsum(run, axis=0) - run)
    length = of_block(run)
    local = of_block(run_end - run)
    start_b = jnp.sum(jnp.where(mine_b, pad_start[None, :], 0), axis=1)
    pos = (jnp.arange(n_blocks) * eb - start_b)[:, None] + jnp.arange(eb)[None, :]
    inside = (pos[:, :, None] >= first[:, None, :]) & (pos[:, :, None] < (first + length)[:, None, :])
    where = (jnp.arange(ntt) * SORT_TILES)[None, None, :] + (local - first)[:, None, :] + pos[:, :, None]
    zero_tile = SORT_TILES - 1
    src = jnp.where(jnp.any(inside, axis=2), jnp.sum(jnp.where(inside, where, 0), axis=2), zero_tile)
    tabs = dict(dst=dst.reshape(-1).astype(I32), count=run_end[:, -1].astype(I32),
                src=src.reshape(-1).astype(I32))
    return tabs, block_e.astype(I32), n_used.reshape(1).astype(I32)


def kernel(x, c, ctx, c_ctx, w_mod, b_mod, norm1, norm2, w_in, attn_sinks, pool_w, pool_scale, ret_decay,
           w_o_attn, w_o_pool, w_o_ret, w_out, w_router, b_router, w_expert_in, b_expert_in, w_expert_out,
           b_expert_out, final_norm):
    b, seq, d = x.shape
    depth = w_mod.shape[0]
    assert ctx.shape[1] == TM and seq % TM == 0 and seq % GRID_W == 0
    s = seq + TM
    xm = jnp.concatenate([ctx, x], axis=1)
    mod = _modulation(c, c_ctx, w_mod, b_mod)
    tables = _rope_tables(seq)
    ntt = b * s // TM
    n_blocks = -(-(b * s * TOP_K + ntt * N_EXPERTS * (SUBLANE - 1)) // EB) + N_EXPERTS
    avg = _block_diag(jnp.full((RET_HEADS, RET_DIM, RET_DIM), 1.0 / RET_DIM, F32)).astype(BF16)
    tri = (jnp.arange(TM)[:, None] <= jnp.arange(TM)[None, :]).astype(BF16)
    for l in range(depth):
        q, k2, v2, rq, rk, rv, rg, u, gates = _inproj(xm, mod[l], norm1[l], _inproj_weight(w_in[l]), tables)
        attn = _attention(q, k2, v2, attn_sinks[l])
        yf, yb = _retention(rq, rk, rv, jax.nn.log_sigmoid(ret_decay[l].astype(F32)))
        wr_hilo = jnp.concatenate(_split(w_router[l].T), axis=0)
        br = jnp.broadcast_to(b_router[l][:, None], (N_EXPERTS, TM))
        wts = (_block_diag(pool_w[l]).astype(BF16), pool_scale[l].reshape(1, -1),
               w_o_attn[l].astype(BF16), w_o_pool[l].astype(BF16), w_o_ret[l].astype(BF16),
               w_out[l].astype(BF16), norm2[l].reshape(1, d), wr_hilo, br, avg, tri)
        xn, xs, ls, rw, cnt = _merge(xm, attn, u, yf, yb, rg, gates, mod[l], seq, wts)
        tabs, block_e, n_used = _routing_tables(cnt, n_blocks)
        ys = _experts(xs, block_e, n_used, tabs["src"], w_expert_in, b_expert_in, w_expert_out, b_expert_out, l,
                      n_blocks)
        xm = _combine(xn, ls, rw, mod[l], final_norm, ys, tabs, final=l == depth - 1)
    return xm
```

```python
import functools

import jax
import jax.numpy as jnp
from jax import lax
from jax.experimental import pallas as pl
from jax.experimental.pallas import tpu as pltpu

F32 = jnp.float32
BF16 = jnp.bfloat16
I32 = jnp.int32

GRID_W = 64
HEAD_DIM = 64
N_Q_HEADS = 8
N_KV_HEADS = 2
WINDOW = 128
ROPE_THETA = 10000.0
POOL_WIDTH = 256
POOL_WINDOWS = (2, 4, 8, 16)
RET_HEADS = 4
RET_DIM = 64
N_EXPERTS = 32
TOP_K = 4
D_FF = 1024
SWIGLU_LIMIT = 7.0
SWIGLU_ALPHA = 1.702
NORM_EPS = 1e-6
GN_EPS = 1e-5

LANE = 128
SUBLANE = 8
MXU_N = 256
TM = 256
AB = 128
IN_TILE = 640
CTX_BLOCKS = TM // AB
Q_BLOCKS = 2
EB = 512
FF_CHUNK = 256
POOL_HALO = SUBLANE
SORT_ROWS = TM * TOP_K + N_EXPERTS * SUBLANE
SORT_TILES = SORT_ROWS // SUBLANE
MIN_TILES = TM * TOP_K // SUBLANE
NEG = -1e30
VMEM_LIMIT = 56 * 1024 * 1024

ATT_Q = N_Q_HEADS * HEAD_DIM
KV2 = 2 * N_KV_HEADS * HEAD_DIM
RET_W = RET_HEADS * RET_DIM


def _dot(a, b):
    return jnp.dot(a, b, preferred_element_type=F32)


def _dot_nt(a, b):
    return lax.dot_general(a, b, (((1,), (1,)), ((), ())), preferred_element_type=F32)


def _dot_tn(a, b):
    return lax.dot_general(a, b, (((0,), (0,)), ((), ())), preferred_element_type=F32)


def _split(a):
    hi = a.astype(BF16)
    lo = (a - hi.astype(F32)).astype(BF16)
    return hi, lo


def _dot_hilo(a, m):
    hi, lo = _split(a)
    return _dot(hi, m) + _dot(lo, m)


def _sigmoid(x):
    return 0.5 * jnp.tanh(0.5 * x) + 0.5


def _silu(x):
    return x * _sigmoid(x)


def _params(*sem):
    return pltpu.CompilerParams(dimension_semantics=sem, vmem_limit_bytes=VMEM_LIMIT)


def _mod_kernel(c_ref, w_ref, b_ref, o_ref):
    s = _silu(c_ref[...])
    sh, sl = _split(s)
    wh, wl = _split(w_ref[0])
    o_ref[0] = _dot(sh, wh) + _dot(sh, wl) + _dot(sl, wh) + b_ref[0]


def _modulation(c, c_ctx, w_mod, b_mod):
    depth, d, six_d = w_mod.shape
    b = c.shape[0]
    cc = jnp.zeros((SUBLANE, d), F32).at[:b].set(c).at[b].set(c_ctx)
    out = pl.pallas_call(
        _mod_kernel,
        grid=(depth, six_d // d),
        in_specs=[
            pl.BlockSpec((SUBLANE, d), lambda l, n: (0, 0)),
            pl.BlockSpec((1, d, d), lambda l, n: (l, 0, n)),
            pl.BlockSpec((1, 1, d), lambda l, n: (l, 0, n)),
        ],
        out_specs=pl.BlockSpec((1, SUBLANE, d), lambda l, n: (l, 0, n)),
        out_shape=jax.ShapeDtypeStruct((depth, SUBLANE, six_d), F32),
        compiler_params=_params("arbitrary", "arbitrary"),
    )(cc, w_mod, b_mod.reshape(depth, 1, six_d))
    return out[:, : b + 1].reshape(depth, b + 1, 6, d)


def _rope(x, cos, sin, half, first):
    partner = jnp.where(first, pltpu.roll(x, LANE - half, 1), pltpu.roll(x, half, 1))
    return x * cos + partner * sin


def _merged_rows(x_refs, c_refs, first):
    parts = []
    for n, ref in enumerate(x_refs):
        part = ref[0]
        if n < len(c_refs):
            part = jnp.where(first, c_refs[n][0], part)
        parts.append(part)
    return parts[0] if len(parts) == 1 else jnp.concatenate(parts, axis=0)


def _inproj_kernel(*refs, n_sub, n_ctx):
    x_refs, c_refs, refs = refs[:n_sub], refs[n_sub:n_sub + n_ctx], refs[n_sub + n_ctx:]
    mod_ref, modc_ref, n1_ref, w_ref, ac_ref, as_ref, rc_ref, rs_ref = refs[:8]
    q_ref, k_ref, v_ref, rq_ref, rk_ref, rv_ref, rg_ref, u_ref, g_ref = refs[8:]
    x = _merged_rows(x_refs, c_refs, pl.program_id(1) == 0)
    rows = x.shape[0]
    ms = jnp.mean(x * x, axis=-1, keepdims=True)
    y = x * lax.rsqrt(ms + NORM_EPS) * n1_ref[...]
    is_ctx = (lax.broadcasted_iota(I32, (rows, 1), 0) < TM) & (pl.program_id(1) == 0)
    scale = jnp.where(is_ctx, modc_ref[0, 1:2, :], mod_ref[0, 1:2, :])
    shift = jnp.where(is_ctx, modc_ref[0, 0:1, :], mod_ref[0, 0:1, :])
    hb = (y * (1.0 + scale) + shift).astype(BF16)
    lane = lax.broadcasted_iota(I32, (rows, LANE), 1)
    a_first = (lane % (HEAD_DIM // 2)) < (HEAD_DIM // 4)
    r_first = (lane % RET_DIM) < (RET_DIM // 2)
    ac, asn, rc, rsn = ac_ref[...], as_ref[...], rc_ref[...], rs_ref[...]

    def proj(off, width):
        return _dot(hb, w_ref[:, off:off + width])

    def rotated(ref, width, cos, sin, half, first, off):
        for t in range(0, width, MXU_N):
            pr = proj(off + t, MXU_N)
            for g in range(0, MXU_N, LANE):
                ref[0, :, t + g:t + g + LANE] = _rope(pr[:, g:g + LANE], cos, sin, half, first).astype(BF16)

    off = 0
    rotated(q_ref, ATT_Q, ac, asn, HEAD_DIM // 4, a_first, off)
    off += ATT_Q
    rotated(k_ref, KV2, ac, asn, HEAD_DIM // 4, a_first, off)
    off += KV2
    v_ref[0] = proj(off, KV2).astype(BF16)
    off += KV2
    for ref in (rq_ref, rk_ref):
        rotated(ref, RET_W, rc, rsn, RET_DIM // 2, r_first, off)
        off += RET_W
    rv_ref[0] = proj(off, RET_W).astype(BF16)
    off += RET_W
    rg_ref[0] = proj(off, 2 * RET_W).astype(BF16)
    off += 2 * RET_W
    u_ref[0] = proj(off, POOL_WIDTH)
    off += POOL_WIDTH
    d = x.shape[-1]
    for t in range(3):
        g_ref[0, :, t * d:(t + 1) * d] = proj(off, d).astype(BF16)
        off += d


def _inproj(x, ctx, mod_l, norm1_l, w1, tables):
    b, _, d = x.shape
    s = x.shape[1] + (0 if ctx is None else ctx.shape[1])
    ti = IN_TILE if s % IN_TILE == 0 else TM
    n_sub = ti // AB
    n_ctx = 0 if ctx is None else CTX_BLOCKS
    pieces = [pl.BlockSpec((1, AB, d), lambda bi, j, n=n: (bi, jnp.maximum(n_sub * j + n - n_ctx, 0), 0))
              for n in range(n_sub)]
    ctx_pieces = [pl.BlockSpec((1, AB, d), lambda bi, j, n=n: (bi, n, 0)) for n in range(n_ctx)]
    wcols = w1.shape[1]
    tok = lambda width: pl.BlockSpec((1, ti, width), lambda bi, j: (bi, j, 0))
    tab = pl.BlockSpec((ti, LANE), lambda bi, j: (j, 0))
    widths = (ATT_Q, KV2, KV2, RET_W, RET_W, RET_W, 2 * RET_W, POOL_WIDTH, 3 * d)
    dtypes = (BF16,) * 7 + (F32, BF16)
    return pl.pallas_call(
        functools.partial(_inproj_kernel, n_sub=n_sub, n_ctx=n_ctx),
        grid=(b, s // ti),
        in_specs=pieces + ctx_pieces + [
            pl.BlockSpec((1, 6, d), lambda bi, j: (bi, 0, 0)),
            pl.BlockSpec((1, 6, d), lambda bi, j: (b, 0, 0)),
            pl.BlockSpec((1, d), lambda bi, j: (0, 0)),
            pl.BlockSpec((d, wcols), lambda bi, j: (0, 0), pipeline_mode=pl.Buffered(1)),
            tab, tab, tab, tab,
        ],
        out_specs=[tok(w) for w in widths],
        out_shape=[jax.ShapeDtypeStruct((b, s, w), dt) for w, dt in zip(widths, dtypes)],
        compiler_params=_params("arbitrary", "arbitrary"),
    )(*([x] * n_sub), *([ctx] * n_ctx), mod_l, mod_l, norm1_l.reshape(1, d), w1, *tables)


def _rope_tables(seq):
    rows = seq // GRID_W
    rpos = jnp.arange(rows, dtype=F32)[:, None]
    cpos = jnp.arange(GRID_W, dtype=F32)[:, None]
    lane = jnp.arange(LANE)
    grid = lambda per_row, per_col: (per_row[:, None, :] + per_col[None, :, :]).reshape(seq, LANE)
    axis_dim = HEAD_DIM // 2
    inv_a = ROPE_THETA ** (-jnp.arange(0, axis_dim, 2, dtype=F32) / axis_dim)
    hl = lane % HEAD_DIM
    inv_al = inv_a[(hl % axis_dim) % (axis_dim // 2)][None, :]
    by_row = (hl < axis_dim)[None, :]
    a_sign = jnp.where((hl % axis_dim) < axis_dim // 2, -1.0, 1.0)[None, :]
    a_cos = grid(jnp.where(by_row, jnp.cos(rpos * inv_al), 0.0), jnp.where(by_row, 0.0, jnp.cos(cpos * inv_al)))
    a_sin = grid(jnp.where(by_row, jnp.sin(rpos * inv_al), 0.0), jnp.where(by_row, 0.0, jnp.sin(cpos * inv_al)))
    inv_r = 1.0 / (ROPE_THETA ** jnp.linspace(0.0, 1.0, RET_DIM // 2, dtype=F32))
    rl = lane % RET_DIM
    inv_rl = inv_r[rl % (RET_DIM // 2)][None, :]
    r_sign = jnp.where(rl < RET_DIM // 2, -1.0, 1.0)[None, :]
    hi = (rpos * GRID_W) * inv_rl
    lo = cpos * inv_rl
    outer = lambda a, b: (a[:, None, :] * b[None, :, :]).reshape(seq, LANE)
    r_cos = outer(jnp.cos(hi), jnp.cos(lo)) - outer(jnp.sin(hi), jnp.sin(lo))
    r_sin = outer(jnp.sin(hi), jnp.cos(lo)) + outer(jnp.cos(hi), jnp.sin(lo))
    ones = jnp.ones((TM, LANE), F32)
    zeros = jnp.zeros((TM, LANE), F32)
    cat = lambda head, body: jnp.concatenate([head, body], axis=0)
    return (cat(ones, a_cos), cat(zeros, a_sin * a_sign), cat(ones, r_cos), cat(zeros, r_sin * r_sign))


def _attn_kernel(sink_ref, bias0_ref, bias1_ref, q_ref, k0_ref, k1_ref, k2_ref, k3_ref, kx_ref,
                 v0_ref, v1_ref, v2_ref, v3_ref, vx_ref, o_ref):
    g_heads = N_Q_HEADS // N_KV_HEADS
    lane = lax.broadcasted_iota(I32, (AB, LANE), 1)
    lo = lane < HEAD_DIM
    rows = lax.broadcasted_iota(I32, (g_heads * AB, 1), 0)
    k_refs = (k0_ref, k1_ref, k2_ref, k3_ref)
    v_refs = (v0_ref, v1_ref, v2_ref, v3_ref)
    biases = (bias0_ref[0], bias1_ref[0])
    chains = [(t, g) for t in range(Q_BLOCKS) for g in range(N_KV_HEADS)]
    scores, sinks, values = [], [], []
    for t, g in chains:
        ks = slice(g * LANE, (g + 1) * LANE)
        kd = jnp.concatenate([r[0, :, ks] for r in k_refs[t:t + 3]] + [kx_ref[0, :, ks]], axis=0)
        values.append(jnp.concatenate([r[0, :, ks] for r in v_refs[t:t + 3]] + [vx_ref[0, :, ks]], axis=0))
        qs = []
        for c in range(2 * g, 2 * g + 2):
            qc = q_ref[0, t * AB:(t + 1) * AB, c * LANE:(c + 1) * LANE]
            zero = jnp.zeros_like(qc)
            qs += [jnp.where(lo, qc, zero), jnp.where(lo, zero, qc)]
        scores.append(_dot_nt(jnp.concatenate(qs, axis=0), kd) + biases[t])
        sink = jnp.full((g_heads * AB, 1), sink_ref[g_heads * g + g_heads - 1], F32)
        for h in range(g_heads - 2, -1, -1):
            sink = jnp.where(rows < (h + 1) * AB, sink_ref[g_heads * g + h], sink)
        sinks.append(sink)
    for (t, g), s, sink, vd in zip(chains, scores, sinks, values):
        m = jnp.maximum(jnp.max(s, axis=-1, keepdims=True), sink)
        p = jnp.exp(s - m)
        den = jnp.sum(p, axis=-1, keepdims=True) + jnp.exp(sink - m)
        o = _dot(p.astype(BF16), vd) / den
        for h in range(2):
            c = 2 * g + h
            o_ref[0, t * AB:(t + 1) * AB, c * LANE:(c + 1) * LANE] = jnp.where(
                lo, o[2 * h * AB:(2 * h + 1) * AB], o[(2 * h + 1) * AB:(2 * h + 2) * AB]).astype(BF16)


def _attn_bias():
    g_heads = N_Q_HEADS // N_KV_HEADS
    r = jnp.arange(g_heads * AB)[:, None] % AB
    j = jnp.arange(3 * AB + TM)[None, :]
    band = jnp.abs(j - AB - r) <= WINDOW
    is_ctx = j >= 3 * AB
    variants = (band, band & (j >= AB), band & (j < 2 * AB), jnp.zeros_like(band))
    return jnp.stack([jnp.where(v | is_ctx, 0.0, NEG) for v in variants]).astype(F32)


def _attention(q, k2, v2, sinks):
    b, s, _ = q.shape
    nb = s // AB
    assert nb - CTX_BLOCKS >= 2 and nb % Q_BLOCKS == 0 and CTX_BLOCKS % Q_BLOCKS == 0
    bias = _attn_bias()

    def variant(blk):
        return jnp.where(blk < CTX_BLOCKS, 3, jnp.where(blk == CTX_BLOCKS, 1, jnp.where(blk == nb - 1, 2, 0)))

    def kv(offset):
        return pl.BlockSpec((1, AB, KV2), lambda bi, i: (bi, jnp.clip(Q_BLOCKS * i + offset, 0, nb - 1), 0))

    def bias_spec(t):
        return pl.BlockSpec((1,) + bias.shape[1:], lambda bi, i: (variant(Q_BLOCKS * i + t), 0, 0))

    cx = pl.BlockSpec((1, TM, KV2), lambda bi, i: (bi, 0, 0))
    kvs = [kv(o) for o in range(-1, Q_BLOCKS + 1)] + [cx]
    qo = pl.BlockSpec((1, Q_BLOCKS * AB, ATT_Q), lambda bi, i: (bi, i, 0))
    return pl.pallas_call(
        _attn_kernel,
        grid=(b, nb // Q_BLOCKS),
        in_specs=[pl.BlockSpec(memory_space=pltpu.SMEM), bias_spec(0), bias_spec(1), qo] + kvs + kvs,
        out_specs=qo,
        out_shape=jax.ShapeDtypeStruct((b, s, ATT_Q), BF16),
        compiler_params=_params("arbitrary", "arbitrary"),
    )(sinks, bias, bias, q, *([k2] * len(kvs)), *([v2] * len(kvs)))


def _ret_kernel(lg_ref, qf_ref, kf_ref, vf_ref, qb_ref, kb_ref, vb_ref, yf_ref, yb_ref,
                st_ref, dm_ref, qd_ref, kd_ref, cd_ref, *, batch):
    step = pl.program_id(0)
    lane = lax.broadcasted_iota(I32, (AB, LANE), 1)
    row = lax.broadcasted_iota(I32, (AB, LANE), 0)
    lo = lane < RET_DIM
    tiles = RET_W // LANE

    @pl.when(step == 0)
    def _():
        st_ref[...] = jnp.zeros_like(st_ref)
        ii = row.astype(F32)
        jj = lane.astype(F32)
        for d in range(2):
            for c in range(tiles):
                lg0 = lg_ref[d * RET_HEADS + 2 * c]
                lg1 = lg_ref[d * RET_HEADS + 2 * c + 1]
                lgl = jnp.where(lo, lg0, lg1)
                q_exp = ii + 1.0 if d == 0 else AB - ii
                k_exp = (AB - 1.0) - ii if d == 0 else ii
                qd_ref[d * tiles + c] = jnp.exp(q_exp * lgl)
                kd_ref[d * tiles + c] = jnp.exp(k_exp * lgl)
                cd_ref[d * tiles + c] = jnp.exp(AB * lgl)
                rel = ii - jj if d == 0 else jj - ii
                for hh, lgh in enumerate((lg0, lg1)):
                    dm_ref[d * tiles + c, hh * AB:(hh + 1) * AB, :] = jnp.where(
                        rel >= 0, jnp.exp(jnp.maximum(rel, 0.0) * lgh), 0.0)

    same_head = (row < RET_DIM) == lo
    dirs = ((qf_ref, kf_ref, vf_ref, yf_ref), (qb_ref, kb_ref, vb_ref, yb_ref))
    chains = [(d, b, c) for d in range(2) for b in range(batch) for c in range(tiles)]

    def operands(d, b, c):
        q_ref, k_ref, v_ref, _ = dirs[d]
        sl = slice(c * LANE, (c + 1) * LANE)
        return q_ref[b, :, sl], k_ref[b, :, sl], v_ref[b, :, sl]

    probs = []
    for d, b, c in chains:
        q, k, _ = operands(d, b, c)
        zero = jnp.zeros_like(q)
        q2 = jnp.concatenate([jnp.where(lo, q, zero), jnp.where(lo, zero, q)], axis=0)
        probs.append((_dot_nt(q2, k) * dm_ref[d * tiles + c]).astype(BF16))
    for (d, b, c), p in zip(chains, probs):
        q, _, v = operands(d, b, c)
        t = d * tiles + c
        si = (d * batch + b) * tiles + c
        y_intra = jnp.where(lo, _dot(p[:AB], v), _dot(p[AB:], v))
        q_dec = (q.astype(F32) * qd_ref[t]).astype(BF16)
        dirs[d][3][b, :, c * LANE:(c + 1) * LANE] = (y_intra + _dot(q_dec, st_ref[si].astype(BF16))).astype(BF16)
    for d, b, c in chains:
        _, k, v = operands(d, b, c)
        t = d * tiles + c
        si = (d * batch + b) * tiles + c
        k_dec = (k.astype(F32) * kd_ref[t]).astype(BF16)
        st_ref[si] = st_ref[si] * cd_ref[t] + jnp.where(same_head, _dot_tn(k_dec, v), 0.0)


def _retention(rq, rk, rv, log_g):
    b, s, w = rq.shape
    nb = s // AB
    tiles = w // LANE

    def back(i):
        return jnp.where(i < CTX_BLOCKS, CTX_BLOCKS - 1 - i, nb - 1 + CTX_BLOCKS - i)

    fwd = pl.BlockSpec((b, AB, w), lambda i: (0, i, 0))
    bwd = pl.BlockSpec((b, AB, w), lambda i: (0, back(i), 0))
    return pl.pallas_call(
        functools.partial(_ret_kernel, batch=b),
        grid=(nb,),
        in_specs=[pl.BlockSpec(memory_space=pltpu.SMEM), fwd, fwd, fwd, bwd, bwd, bwd],
        out_specs=[fwd, bwd],
        out_shape=[jax.ShapeDtypeStruct((b, s, w), BF16)] * 2,
        scratch_shapes=[
            pltpu.VMEM((2 * b * tiles, LANE, LANE), F32),
            pltpu.VMEM((2 * tiles, 2 * AB, LANE), F32),
            pltpu.VMEM((2 * tiles, AB, LANE), F32),
            pltpu.VMEM((2 * tiles, AB, LANE), F32),
            pltpu.VMEM((2 * tiles, AB, LANE), F32),
        ],
        compiler_params=_params("arbitrary"),
    )(log_g.reshape(-1), rq, rk, rv, rq, rk, rv)


def _route_tile(logits, tri_ref, ls_ref, rw_ref, cnt_ref):
    ne = N_EXPERTS
    eidx = lax.broadcasted_iota(I32, (ne, TM), 0).astype(F32)
    tops, hots = [], []
    for _ in range(TOP_K):
        mx = jnp.max(logits, axis=0, keepdims=True)
        idx = jnp.min(jnp.where(logits == mx, eidx, float(ne)), axis=0, keepdims=True)
        hot = eidx == idx
        logits = jnp.where(hot, NEG * 2.0, logits)
        tops.append(mx)
        hots.append(hot)
    ex = [jnp.exp(mx - tops[0]) for mx in tops]
    tot = ex[0]
    for e in ex[1:]:
        tot = tot + e
    sel = jnp.zeros((ne, TM), F32)
    for hot in hots:
        sel = jnp.where(hot, 1.0, sel)
    incl = _dot(sel.astype(BF16), tri_ref[...])
    cnt = incl[:, TM - 1:TM]
    run = jnp.floor((cnt + (SUBLANE - 1.0)) * (1.0 / SUBLANE)) * SUBLANE
    ends = jnp.broadcast_to(run, (ne, LANE))
    erow = lax.broadcasted_iota(I32, (ne, LANE), 0)
    shift = 1
    while shift < ne:
        ends = ends + jnp.where(erow >= shift, pltpu.roll(ends, shift, 0), 0.0)
        shift *= 2
    slot = (ends[:, :1] - run) + incl - 1.0
    ls_rows = [jnp.sum(jnp.where(hot, slot, 0.0), axis=0, keepdims=True) for hot in hots]
    rw_rows = [e / tot for e in ex]
    lst = jnp.concatenate(ls_rows + [jnp.full((SUBLANE - TOP_K, TM), -1.0, F32)], axis=0)
    ls_ref[...] = jnp.transpose(
        jnp.concatenate([lst, jnp.full((LANE - SUBLANE, TM), -1.0, F32)], axis=0)).astype(I32)
    rw_ref[...] = jnp.transpose(jnp.concatenate(rw_rows + [jnp.zeros((LANE - TOP_K, TM), F32)], axis=0))
    cnt_ref[...] = jnp.broadcast_to(cnt, cnt_ref.shape).astype(I32)
    return lst


def _merge_kernel(*refs, seq, nt, ntt, n_ctx):
    _merge_body(refs[n_ctx], refs[:n_ctx], *refs[n_ctx + 1:], seq=seq, nt=nt, ntt=ntt)


def _merge_body(x_ref, c_refs, attn_ref, up_ref, uc_ref, un_ref, yf_ref, yb_ref, rg_ref, g_ref, mod_ref,
                  wbd_ref, ps_ref, woa_ref, wop_ref, wor_ref, wout_ref, n2_ref, wr_ref,
                  br_ref, avg_ref, tri_ref,
                  xo_ref, xs_ref, ls_ref, rw_ref, cnt_ref, lg_ref, h2_ref, *, seq, nt, ntt):
    i = pl.program_id(0)
    j = jnp.minimum(i, ntt - 1) % nt

    @pl.when(i == 0)
    def _():
        lg_ref[...] = jnp.zeros_like(lg_ref)
        h2_ref[...] = jnp.zeros_like(h2_ref)

    lst = _route_tile(lg_ref[...], tri_ref, ls_ref, rw_ref, cnt_ref)

    ext = TM + 2 * POOL_HALO
    seq_len = jnp.where(j == 0, TM, seq)
    start = jnp.where(j == 0, 0, (j - 1) * TM)
    u = uc_ref[0]
    ue = jnp.concatenate([up_ref[0], u, un_ref[0]], axis=0)
    erow = lax.broadcasted_iota(I32, (ext, 1), 0) + (start - POOL_HALO)
    ue = jnp.where((erow >= 0) & (erow < seq_len), ue, 0.0)
    t_pos = lax.broadcasted_iota(I32, (TM, 1), 0) + start
    glane = lax.broadcasted_iota(I32, (TM, POOL_WIDTH), 1) // (POOL_WIDTH // len(POOL_WINDOWS))
    run = ue
    width = 1
    diff = jnp.zeros((TM, POOL_WIDTH), F32)
    for gi, w in enumerate(POOL_WINDOWS):
        while width < w:
            run = run + pltpu.roll(run, ext - width, 0)
            width *= 2
        win = pltpu.roll(run, w // 2, 0)[POOL_HALO:POOL_HALO + TM]
        cnt = jnp.minimum(t_pos - w // 2 + w, seq_len) - jnp.maximum(t_pos - w // 2, 0)
        diff = jnp.where(glane == gi, win / cnt.astype(F32) - u, diff)
    pool = _dot(diff.astype(BF16), wbd_ref[...]) * ps_ref[...]

    def head_norm(y):
        dlt = y.astype(F32) - _dot(y, avg_ref[...])
        var = _dot_hilo(dlt * dlt, avg_ref[...])
        return dlt * lax.rsqrt(var + GN_EPS)

    rg = rg_ref[0].astype(F32)
    ret = head_norm(yf_ref[0]) * _silu(rg[:, :RET_W]) + head_norm(yb_ref[0]) * _silu(rg[:, RET_W:])

    slot = lax.broadcasted_iota(I32, (SORT_ROWS, TM), 0).astype(F32)
    p = jnp.zeros((SORT_ROWS, TM), F32)
    for k in range(TOP_K):
        p = jnp.where(slot == lst[k:k + 1, :], 1.0, p)
    xs_ref[...] = _dot(p.astype(BF16), h2_ref[...])

    d = x_ref.shape[-1]
    gate = lambda t: _sigmoid(g_ref[0, :, t * d:(t + 1) * d])
    m = (gate(0) * _dot(attn_ref[0], woa_ref[...]).astype(BF16)
         + gate(1) * _dot(pool.astype(BF16), wop_ref[...]).astype(BF16)
         + gate(2) * _dot(ret.astype(BF16), wor_ref[...]).astype(BF16))
    xn = _merged_rows((x_ref,), c_refs, j == 0) + mod_ref[0, 2:3, :] * _dot(m, wout_ref[...])
    xo_ref[0] = xn

    ms = jnp.mean(xn * xn, axis=-1, keepdims=True)
    h2 = xn * lax.rsqrt(ms + NORM_EPS) * n2_ref[...] * (1.0 + mod_ref[0, 4:5, :]) + mod_ref[0, 3:4, :]
    h2_ref[...] = h2.astype(BF16)
    hh, hl = _split(h2)
    ne = N_EXPERTS
    full = _dot_nt(wr_ref[...], hh)
    logits = full[:ne] + full[ne:] + _dot_nt(wr_ref[:ne], hl) + br_ref[...]
    lg_ref[...] = logits


def _merge(x, ctx, attn, u, yf, yb, rg, gates, mod_l, seq, wts):
    b, s, _ = attn.shape
    d = x.shape[-1]
    nt = s // TM
    ntt = b * nt
    hb = TM // POOL_HALO
    nh = s // POOL_HALO
    cur = lambda i: jnp.minimum(i, ntt - 1)
    lag = lambda i: jnp.maximum(i - 1, 0)
    tok = lambda width: pl.BlockSpec((1, TM, width), lambda i: (cur(i) // nt, cur(i) % nt, 0))
    full = lambda a: pl.BlockSpec(a.shape, lambda i: (0,) * a.ndim)
    lanes = pl.BlockSpec((TM, LANE), lambda i: (lag(i), 0))
    if ctx is None:
        stream = [tok(d)]
    else:
        stream = [pl.BlockSpec((1, TM, d), lambda i: (cur(i) // nt, 0, 0)),
                  pl.BlockSpec((1, TM, d), lambda i: (cur(i) // nt, jnp.maximum(cur(i) % nt - 1, 0), 0))]
    in_specs = stream + [
        tok(ATT_Q),
        pl.BlockSpec((1, POOL_HALO, POOL_WIDTH),
                     lambda i: (cur(i) // nt, jnp.maximum((cur(i) % nt) * hb - 1, 0), 0)),
        tok(POOL_WIDTH),
        pl.BlockSpec((1, POOL_HALO, POOL_WIDTH),
                     lambda i: (cur(i) // nt, jnp.minimum((cur(i) % nt + 1) * hb, nh - 1), 0)),
        tok(RET_W), tok(RET_W), tok(2 * RET_W), tok(3 * d),
        pl.BlockSpec((1, 6, d), lambda i: (jnp.where(cur(i) % nt == 0, b, cur(i) // nt), 0, 0)),
    ] + [full(a) for a in wts]
    return pl.pallas_call(
        functools.partial(_merge_kernel, seq=seq, nt=nt, ntt=ntt, n_ctx=len(stream) - 1),
        grid=(ntt + 1,),
        in_specs=in_specs,
        out_specs=[tok(d), pl.BlockSpec((SORT_ROWS, d), lambda i: (lag(i), 0)), lanes, lanes,
                   pl.BlockSpec((N_EXPERTS, LANE), lambda i: (lag(i), 0))],
        out_shape=[jax.ShapeDtypeStruct((b, s, d), F32), jax.ShapeDtypeStruct((ntt * SORT_ROWS, d), F32),
                   jax.ShapeDtypeStruct((b * s, LANE), I32),
                   jax.ShapeDtypeStruct((b * s, LANE), F32), jax.ShapeDtypeStruct((ntt * N_EXPERTS, LANE), I32)],
        scratch_shapes=[pltpu.VMEM((N_EXPERTS, TM), F32), pltpu.VMEM((TM, d), BF16)],
        compiler_params=_params("arbitrary"),
    )(*([] if ctx is None else [ctx]), x, attn, u, u, u, yf, yb, rg, gates, mod_l, *wts)


def _rows(tile_index):
    return pl.ds(pl.multiple_of(tile_index * SUBLANE, SUBLANE), SUBLANE)


def _tile_loop(count_ref, tile, fn):
    def body(j, carry):
        fn(j)
        return carry

    lax.fori_loop(0, MIN_TILES, body, 0, unroll=8)
    lax.fori_loop(MIN_TILES, count_ref[tile], body, 0)


def _tile_copies(table_ref, count_ref, tile, copy):
    _tile_loop(count_ref, tile, lambda j: copy(j, table_ref[tile * SORT_TILES + j]))


def _expert_kernel(be_ref, nu_ref, src_ref, xs_hbm, w1_ref, b1_ref, w2_ref, b2_ref, y_ref, w1b, w2b, xbuf, sem):
    bi = pl.program_id(0)
    used = bi < nu_ref[0]
    fresh = (bi == 0) | (be_ref[bi] != be_ref[jnp.maximum(bi - 1, 0)])
    cur = bi % 2
    block_tiles = EB // SUBLANE

    def each_tile(fn):
        def body(t, carry):
            fn(t)
            return carry
        lax.fori_loop(0, block_tiles, body, 0, unroll=8)

    def fetch(blk, half):
        each_tile(lambda t: pltpu.make_async_copy(
            xs_hbm.at[_rows(src_ref[blk * block_tiles + t])], xbuf.at[half, _rows(t)], sem.at[half]).start())

    @pl.when(bi == 0)
    def _():
        fetch(0, 0)

    @pl.when(bi + 1 < nu_ref[0])
    def _():
        fetch(bi + 1, 1 - cur)

    @pl.when(used & fresh)
    def _():
        w1b[...] = w1_ref[0, 0].astype(BF16)
        w2b[...] = w2_ref[0, 0].astype(BF16)

    @pl.when(jnp.logical_not(used))
    def _():
        y_ref[...] = jnp.zeros_like(y_ref)

    @pl.when(used)
    def _():
        each_tile(lambda t: pltpu.make_async_copy(
            xs_hbm.at[_rows(t)], xbuf.at[cur, _rows(t)], sem.at[cur]).wait())
        xb = xbuf[cur].astype(BF16)
        y = None
        for c in range(0, D_FF, FF_CHUNK):
            glu = _dot(xb, w1b[:, c:c + FF_CHUNK]) + b1_ref[0, 0, :, c:c + FF_CHUNK]
            lin = _dot(xb, w1b[:, D_FF + c:D_FF + c + FF_CHUNK]) + b1_ref[0, 0, :, D_FF + c:D_FF + c + FF_CHUNK]
            glu = jnp.minimum(glu, SWIGLU_LIMIT)
            lin = jnp.clip(lin, -SWIGLU_LIMIT, SWIGLU_LIMIT)
            act = glu * _sigmoid(SWIGLU_ALPHA * glu) * (lin + 1.0)
            part = _dot(act.astype(BF16), w2b[c:c + FF_CHUNK, :])
            y = part if y is None else y + part
        y_ref[...] = y + b2_ref[0, 0]


def _experts(xs, block_e, n_used, src, w1, b1, w2, b2, layer, n_blocks):
    d = xs.shape[1]
    depth, ne, _, f2 = w1.shape
    grid_spec = pltpu.PrefetchScalarGridSpec(
        num_scalar_prefetch=3,
        grid=(n_blocks,),
        in_specs=[
            pl.BlockSpec(memory_space=pl.ANY),
            pl.BlockSpec((1, 1, d, f2), lambda bi, be, *_: (layer, be[bi], 0, 0)),
            pl.BlockSpec((1, 1, 1, f2), lambda bi, be, *_: (layer, be[bi], 0, 0)),
            pl.BlockSpec((1, 1, f2 // 2, d), lambda bi, be, *_: (layer, be[bi], 0, 0)),
            pl.BlockSpec((1, 1, 1, d), lambda bi, be, *_: (layer, be[bi], 0, 0)),
        ],
        out_specs=pl.BlockSpec((EB, d), lambda bi, *_: (bi, 0)),
        scratch_shapes=[pltpu.VMEM((d, f2), BF16), pltpu.VMEM((f2 // 2, d), BF16),
                        pltpu.VMEM((2, EB, d), F32), pltpu.SemaphoreType.DMA((2,))],
    )
    return pl.pallas_call(
        _expert_kernel,
        grid_spec=grid_spec,
        out_shape=jax.ShapeDtypeStruct((n_blocks * EB, d), F32),
        compiler_params=_params("arbitrary"),
    )(block_e, n_used, src, xs, w1, b1.reshape(depth, ne, 1, f2), w2, b2.reshape(depth, ne, 1, d))


def _combine_kernel(dst_ref, cnt_ref, x_ref, ls_ref, rw_ref, mod_ref, fg_ref, ys_hbm, xo_ref, buf, sem, *, final):
    i = pl.program_id(0)
    last = pl.num_programs(0) - 1
    cur = i % 2

    def fetch(tile, half):
        _tile_copies(dst_ref, cnt_ref, tile, lambda j, t: pltpu.make_async_copy(
            ys_hbm.at[_rows(t)], buf.at[half, _rows(j)], sem.at[half]).start())

    @pl.when(i == 0)
    def _():
        buf[...] = jnp.zeros_like(buf)
        fetch(0, 0)

    @pl.when(i < last)
    def _():
        fetch(i + 1, 1 - cur)

    _tile_loop(cnt_ref, i, lambda j: pltpu.make_async_copy(
        ys_hbm.at[_rows(j)], buf.at[cur, _rows(j)], sem.at[cur]).wait())
    yb = buf[cur].astype(BF16)
    slot = lax.broadcasted_iota(I32, (TM, SORT_ROWS), 1)
    ls = ls_ref[...]
    rw = rw_ref[...]
    g = jnp.zeros((TM, SORT_ROWS), F32)
    for k in range(TOP_K):
        g = jnp.where(slot == ls[:, k:k + 1], rw[:, k:k + 1], g)
    xn = x_ref[0] + mod_ref[0, 5:6, :] * _dot(g.astype(BF16), yb)
    if final:
        ms = jnp.mean(xn * xn, axis=-1, keepdims=True)
        xn = xn * lax.rsqrt(ms + NORM_EPS) * fg_ref[...]
    xo_ref[0] = xn


def _combine(xn, ls, rw, mod_l, final_gain, ys, tabs, final):
    b, s, d = xn.shape
    nt = s // TM
    tok = pl.BlockSpec((1, TM, d), lambda i, *_: (i // nt, i % nt, 0))
    lanes = pl.BlockSpec((TM, LANE), lambda i, *_: (i, 0))
    if final:
        out_spec = pl.BlockSpec((1, TM, d), lambda i, *_: (i // nt, jnp.maximum(i % nt - 1, 0), 0))
        out_shape = jax.ShapeDtypeStruct((b, s - TM, d), F32)
    else:
        out_spec, out_shape = tok, jax.ShapeDtypeStruct((b, s, d), F32)
    grid_spec = pltpu.PrefetchScalarGridSpec(
        num_scalar_prefetch=2,
        grid=(b * nt,),
        in_specs=[tok, lanes, lanes,
                  pl.BlockSpec((1, 6, d), lambda i, *_: (jnp.where(i % nt == 0, b, i // nt), 0, 0)),
                  pl.BlockSpec((1, d), lambda i, *_: (0, 0)),
                  pl.BlockSpec(memory_space=pl.ANY)],
        out_specs=out_spec,
        scratch_shapes=[pltpu.VMEM((2, SORT_ROWS, d), F32), pltpu.SemaphoreType.DMA((2,))],
    )
    return pl.pallas_call(
        functools.partial(_combine_kernel, final=final),
        grid_spec=grid_spec,
        out_shape=out_shape,
        compiler_params=_params("arbitrary"),
    )(tabs["dst"], tabs["count"], xn, ls, rw, mod_l, final_gain.reshape(1, d), ys)


def _inproj_weight(w_in_l):
    d = w_in_l.shape[0]
    sizes = (ATT_Q, N_KV_HEADS * HEAD_DIM, N_KV_HEADS * HEAD_DIM, POOL_WIDTH,
             RET_W, RET_W, RET_W, RET_W, RET_W, d, d, d)
    parts, off = [], 0
    for sz in sizes:
        parts.append(w_in_l[:, off:off + sz])
        off += sz
    q, k, v, u, rq, rk, rv, rgf, rgb, ga, gp, gr = parts
    twice = lambda w: jnp.concatenate(
        [w[:, h * HEAD_DIM:(h + 1) * HEAD_DIM] for h in range(N_KV_HEADS) for _ in range(2)], axis=1)
    cols = [q * HEAD_DIM ** -0.5, twice(k), twice(v), rq, rk * RET_DIM ** -0.5, rv, rgf, rgb, u, ga, gp, gr]
    return jnp.concatenate(cols, axis=1).astype(BF16)


def _block_diag(blocks):
    n, r, c = blocks.shape
    out = jnp.zeros((n * r, n * c), blocks.dtype)
    for g in range(n):
        out = out.at[g * r:(g + 1) * r, g * c:(g + 1) * c].set(blocks[g])
    return out


def _routing_tables(cnt, n_blocks):
    ntt = cnt.shape[0] // N_EXPERTS
    counts = cnt.reshape(ntt, N_EXPERTS, LANE)[:, :, 0]
    run = (counts + SUBLANE - 1) // SUBLANE
    total = jnp.sum(run, axis=0)
    eb = EB // SUBLANE
    padded = (total + eb - 1) // eb * eb
    pad_end = jnp.cumsum(padded)
    pad_start = pad_end - padded
    off = pad_start[None, :] + jnp.cumsum(run, axis=0) - run
    run_end = jnp.cumsum(run, axis=1)
    j = jnp.arange(SORT_TILES)
    owner = jnp.sum(run_end[:, None, :] <= j[None, :, None], axis=2)
    mine = owner[:, :, None] == jnp.arange(N_EXPERTS)[None, None, :]
    in_region = j[None, :] + jnp.sum(jnp.where(mine, (off - (run_end - run))[:, None, :], 0), axis=2)
    dst = jnp.where(owner < N_EXPERTS, in_region, 0)
    n_used = pad_end[-1] // eb
    blk = jnp.minimum(jnp.arange(n_blocks), n_used - 1) * eb
    block_e = jnp.minimum(jnp.sum(pad_end[None, :] <= blk[:, None], axis=1), N_EXPERTS - 1)
    mine_b = block_e[:, None] == jnp.arange(N_EXPERTS)[None, :]
    of_block = lambda table: jnp.sum(jnp.where(mine_b[:, None, :], table[None, :, :], 0), axis=2)
    first = of_block(jnp.cumsum(run, axis=0) - run)
    length = of_block(run)
    local = of_block(run_end - run)
    start_b = jnp.sum(jnp.where(mine_b, pad_start[None, :], 0), axis=1)
    pos = (jnp.arange(n_blocks) * eb - start_b)[:, None] + jnp.arange(eb)[None, :]
    inside = (pos[:, :, None] >= first[:, None, :]) & (pos[:, :, None] < (first + length)[:, None, :])
    where = (jnp.arange(ntt) * SORT_TILES)[None, None, :] + (local - first)[:, None, :] + pos[:, :, None]
    zero_tile = SORT_TILES - 1
    src = jnp.where(jnp.any(inside, axis=2), jnp.sum(jnp.where(inside, where, 0), axis=2), zero_tile)
    tabs = dict(dst=dst.reshape(-1).astype(I32), count=run_end[:, -1].astype(I32),
                src=src.reshape(-1).astype(I32))
    return tabs, block_e.astype(I32), n_used.reshape(1).astype(I32)


def kernel(x, c, ctx, c_ctx, w_mod, b_mod, norm1, norm2, w_in, attn_sinks, pool_w, pool_scale, ret_decay,
           w_o_attn, w_o_pool, w_o_ret, w_out, w_router, b_router, w_expert_in, b_expert_in, w_expert_out,
           b_expert_out, final_norm):
    b, seq, d = x.shape
    depth = w_mod.shape[0]
    assert ctx.shape[1] == TM and seq % TM == 0 and seq % GRID_W == 0
    s = seq + TM
    xm = x
    mod = _modulation(c, c_ctx, w_mod, b_mod)
    tables = _rope_tables(seq)
    ntt = b * s // TM
    n_blocks = -(-(b * s * TOP_K + ntt * N_EXPERTS * (SUBLANE - 1)) // EB) + N_EXPERTS
    avg = _block_diag(jnp.full((RET_HEADS, RET_DIM, RET_DIM), 1.0 / RET_DIM, F32)).astype(BF16)
    tri = (jnp.arange(TM)[:, None] <= jnp.arange(TM)[None, :]).astype(BF16)
    for l in range(depth):
        first_ctx = ctx if l == 0 else None
        q, k2, v2, rq, rk, rv, rg, u, gates = _inproj(xm, first_ctx, mod[l], norm1[l], _inproj_weight(w_in[l]), tables)
        attn = _attention(q, k2, v2, attn_sinks[l])
        yf, yb = _retention(rq, rk, rv, jax.nn.log_sigmoid(ret_decay[l].astype(F32)))
        wr_hilo = jnp.concatenate(_split(w_router[l].T), axis=0)
        br = jnp.broadcast_to(b_router[l][:, None], (N_EXPERTS, TM))
        wts = (_block_diag(pool_w[l]).astype(BF16), pool_scale[l].reshape(1, -1),
               w_o_attn[l].astype(BF16), w_o_pool[l].astype(BF16), w_o_ret[l].astype(BF16),
               w_out[l].astype(BF16), norm2[l].reshape(1, d), wr_hilo, br, avg, tri)
        xn, xs, ls, rw, cnt = _merge(xm, first_ctx, attn, u, yf, yb, rg, gates, mod[l], seq, wts)
        tabs, block_e, n_used = _routing_tables(cnt, n_blocks)
        ys = _experts(xs, block_e, n_used, tabs["src"], w_expert_in, b_expert_in, w_expert_out, b_expert_out, l,
                      n_blocks)
        xm = _combine(xn, ls, rw, mod[l], final_norm, ys, tabs, final=l == depth - 1)
    return xm
```

```python
import functools

import jax
import jax.numpy as jnp
from jax import lax
from jax.experimental import pallas as pl
from jax.experimental.pallas import tpu as pltpu

F32 = jnp.float32
BF16 = jnp.bfloat16
I32 = jnp.int32

GRID_W = 64
HEAD_DIM = 64
N_Q_HEADS = 8
N_KV_HEADS = 2
WINDOW = 128
ROPE_THETA = 10000.0
POOL_WIDTH = 256
POOL_WINDOWS = (2, 4, 8, 16)
RET_HEADS = 4
RET_DIM = 64
N_EXPERTS = 32
TOP_K = 4
D_FF = 1024
SWIGLU_LIMIT = 7.0
SWIGLU_ALPHA = 1.702
NORM_EPS = 1e-6
GN_EPS = 1e-5

LANE = 128
SUBLANE = 8
MXU_N = 256
TM = 256
AB = 128
IN_TILE = 640
CTX_BLOCKS = TM // AB
Q_BLOCKS = 2
EB = 512
FF_CHUNK = 512
POOL_HALO = SUBLANE
SORT_ROWS = TM * TOP_K + N_EXPERTS * SUBLANE
SORT_TILES = SORT_ROWS // SUBLANE
MIN_TILES = TM * TOP_K // SUBLANE
NEG = -1e30
VMEM_LIMIT = 56 * 1024 * 1024

ATT_Q = N_Q_HEADS * HEAD_DIM
KV2 = 2 * N_KV_HEADS * HEAD_DIM
RET_W = RET_HEADS * RET_DIM


def _dot(a, b):
    return jnp.dot(a, b, preferred_element_type=F32)


def _dot_nt(a, b):
    return lax.dot_general(a, b, (((1,), (1,)), ((), ())), preferred_element_type=F32)


def _dot_tn(a, b):
    return lax.dot_general(a, b, (((0,), (0,)), ((), ())), preferred_element_type=F32)


def _split(a):
    hi = a.astype(BF16)
    lo = (a - hi.astype(F32)).astype(BF16)
    return hi, lo


def _dot_hilo(a, m):
    hi, lo = _split(a)
    return _dot(hi, m) + _dot(lo, m)


def _sigmoid(x):
    return 0.5 * jnp.tanh(0.5 * x) + 0.5


def _silu(x):
    return x * _sigmoid(x)


def _params(*sem):
    return pltpu.CompilerParams(dimension_semantics=sem, vmem_limit_bytes=VMEM_LIMIT)


def _mod_kernel(c_ref, w_ref, b_ref, o_ref):
    s = _silu(c_ref[...])
    sh, sl = _split(s)
    wh, wl = _split(w_ref[0])
    o_ref[0] = _dot(sh, wh) + _dot(sh, wl) + _dot(sl, wh) + b_ref[0]


def _modulation(c, c_ctx, w_mod, b_mod):
    depth, d, six_d = w_mod.shape
    b = c.shape[0]
    cc = jnp.zeros((SUBLANE, d), F32).at[:b].set(c).at[b].set(c_ctx)
    out = pl.pallas_call(
        _mod_kernel,
        grid=(depth, six_d // d),
        in_specs=[
            pl.BlockSpec((SUBLANE, d), lambda l, n: (0, 0)),
            pl.BlockSpec((1, d, d), lambda l, n: (l, 0, n)),
            pl.BlockSpec((1, 1, d), lambda l, n: (l, 0, n)),
        ],
        out_specs=pl.BlockSpec((1, SUBLANE, d), lambda l, n: (l, 0, n)),
        out_shape=jax.ShapeDtypeStruct((depth, SUBLANE, six_d), F32),
        compiler_params=_params("arbitrary", "arbitrary"),
    )(cc, w_mod, b_mod.reshape(depth, 1, six_d))
    return out[:, : b + 1].reshape(depth, b + 1, 6, d)


def _rope(x, cos, sin, half, first):
    partner = jnp.where(first, pltpu.roll(x, LANE - half, 1), pltpu.roll(x, half, 1))
    return x * cos + partner * sin


def _merged_rows(x_refs, c_refs, first):
    parts = []
    for n, ref in enumerate(x_refs):
        part = ref[0]
        if n < len(c_refs):
            part = jnp.where(first, c_refs[n][0], part)
        parts.append(part)
    return parts[0] if len(parts) == 1 else jnp.concatenate(parts, axis=0)


def _inproj_kernel(*refs, n_sub, n_ctx):
    x_refs, c_refs, refs = refs[:n_sub], refs[n_sub:n_sub + n_ctx], refs[n_sub + n_ctx:]
    mod_ref, modc_ref, n1_ref, w_ref, ac_ref, as_ref, rc_ref, rs_ref = refs[:8]
    q_ref, k_ref, v_ref, rq_ref, rk_ref, rv_ref, rg_ref, u_ref, g_ref = refs[8:]
    x = _merged_rows(x_refs, c_refs, pl.program_id(1) == 0)
    rows = x.shape[0]
    ms = jnp.mean(x * x, axis=-1, keepdims=True)
    y = x * lax.rsqrt(ms + NORM_EPS) * n1_ref[...]
    is_ctx = (lax.broadcasted_iota(I32, (rows, 1), 0) < TM) & (pl.program_id(1) == 0)
    scale = jnp.where(is_ctx, modc_ref[0, 1:2, :], mod_ref[0, 1:2, :])
    shift = jnp.where(is_ctx, modc_ref[0, 0:1, :], mod_ref[0, 0:1, :])
    hb = (y * (1.0 + scale) + shift).astype(BF16)
    lane = lax.broadcasted_iota(I32, (rows, LANE), 1)
    a_first = (lane % (HEAD_DIM // 2)) < (HEAD_DIM // 4)
    r_first = (lane % RET_DIM) < (RET_DIM // 2)
    ac, asn, rc, rsn = ac_ref[...], as_ref[...], rc_ref[...], rs_ref[...]

    def proj(off, width):
        return _dot(hb, w_ref[:, off:off + width])

    def rotated(ref, width, cos, sin, half, first, off):
        for t in range(0, width, MXU_N):
            pr = proj(off + t, MXU_N)
            for g in range(0, MXU_N, LANE):
                ref[0, :, t + g:t + g + LANE] = _rope(pr[:, g:g + LANE], cos, sin, half, first).astype(BF16)

    off = 0
    rotated(q_ref, ATT_Q, ac, asn, HEAD_DIM // 4, a_first, off)
    off += ATT_Q
    rotated(k_ref, KV2, ac, asn, HEAD_DIM // 4, a_first, off)
    off += KV2
    v_ref[0] = proj(off, KV2).astype(BF16)
    off += KV2
    for ref in (rq_ref, rk_ref):
        rotated(ref, RET_W, rc, rsn, RET_DIM // 2, r_first, off)
        off += RET_W
    rv_ref[0] = proj(off, RET_W).astype(BF16)
    off += RET_W
    rg_ref[0] = proj(off, 2 * RET_W).astype(BF16)
    off += 2 * RET_W
    u_ref[0] = proj(off, POOL_WIDTH)
    off += POOL_WIDTH
    d = x.shape[-1]
    for t in range(3):
        g_ref[0, :, t * d:(t + 1) * d] = proj(off, d).astype(BF16)
        off += d


def _inproj(x, ctx, mod_l, norm1_l, w1, tables):
    b, _, d = x.shape
    s = x.shape[1] + (0 if ctx is None else ctx.shape[1])
    ti = IN_TILE if s % IN_TILE == 0 else TM
    n_sub = ti // AB
    n_ctx = 0 if ctx is None else CTX_BLOCKS
    pieces = [pl.BlockSpec((1, AB, d), lambda bi, j, n=n: (bi, jnp.maximum(n_sub * j + n - n_ctx, 0), 0))
              for n in range(n_sub)]
    ctx_pieces = [pl.BlockSpec((1, AB, d), lambda bi, j, n=n: (bi, n, 0)) for n in range(n_ctx)]
    wcols = w1.shape[1]
    tok = lambda width: pl.BlockSpec((1, ti, width), lambda bi, j: (bi, j, 0))
    tab = pl.BlockSpec((ti, LANE), lambda bi, j: (j, 0))
    widths = (ATT_Q, KV2, KV2, RET_W, RET_W, RET_W, 2 * RET_W, POOL_WIDTH, 3 * d)
    dtypes = (BF16,) * 7 + (F32, BF16)
    return pl.pallas_call(
        functools.partial(_inproj_kernel, n_sub=n_sub, n_ctx=n_ctx),
        grid=(b, s // ti),
        in_specs=pieces + ctx_pieces + [
            pl.BlockSpec((1, 6, d), lambda bi, j: (bi, 0, 0)),
            pl.BlockSpec((1, 6, d), lambda bi, j: (b, 0, 0)),
            pl.BlockSpec((1, d), lambda bi, j: (0, 0)),
            pl.BlockSpec((d, wcols), lambda bi, j: (0, 0), pipeline_mode=pl.Buffered(1)),
            tab, tab, tab, tab,
        ],
        out_specs=[tok(w) for w in widths],
        out_shape=[jax.ShapeDtypeStruct((b, s, w), dt) for w, dt in zip(widths, dtypes)],
        compiler_params=_params("arbitrary", "arbitrary"),
    )(*([x] * n_sub), *([ctx] * n_ctx), mod_l, mod_l, norm1_l.reshape(1, d), w1, *tables)


def _rope_tables(seq):
    rows = seq // GRID_W
    rpos = jnp.arange(rows, dtype=F32)[:, None]
    cpos = jnp.arange(GRID_W, dtype=F32)[:, None]
    lane = jnp.arange(LANE)
    grid = lambda per_row, per_col: (per_row[:, None, :] + per_col[None, :, :]).reshape(seq, LANE)
    axis_dim = HEAD_DIM // 2
    inv_a = ROPE_THETA ** (-jnp.arange(0, axis_dim, 2, dtype=F32) / axis_dim)
    hl = lane % HEAD_DIM
    inv_al = inv_a[(hl % axis_dim) % (axis_dim // 2)][None, :]
    by_row = (hl < axis_dim)[None, :]
    a_sign = jnp.where((hl % axis_dim) < axis_dim // 2, -1.0, 1.0)[None, :]
    a_cos = grid(jnp.where(by_row, jnp.cos(rpos * inv_al), 0.0), jnp.where(by_row, 0.0, jnp.cos(cpos * inv_al)))
    a_sin = grid(jnp.where(by_row, jnp.sin(rpos * inv_al), 0.0), jnp.where(by_row, 0.0, jnp.sin(cpos * inv_al)))
    inv_r = 1.0 / (ROPE_THETA ** jnp.linspace(0.0, 1.0, RET_DIM // 2, dtype=F32))
    rl = lane % RET_DIM
    inv_rl = inv_r[rl % (RET_DIM // 2)][None, :]
    r_sign = jnp.where(rl < RET_DIM // 2, -1.0, 1.0)[None, :]
    hi = (rpos * GRID_W) * inv_rl
    lo = cpos * inv_rl
    outer = lambda a, b: (a[:, None, :] * b[None, :, :]).reshape(seq, LANE)
    r_cos = outer(jnp.cos(hi), jnp.cos(lo)) - outer(jnp.sin(hi), jnp.sin(lo))
    r_sin = outer(jnp.sin(hi), jnp.cos(lo)) + outer(jnp.cos(hi), jnp.sin(lo))
    ones = jnp.ones((TM, LANE), F32)
    zeros = jnp.zeros((TM, LANE), F32)
    cat = lambda head, body: jnp.concatenate([head, body], axis=0)
    return (cat(ones, a_cos), cat(zeros, a_sin * a_sign), cat(ones, r_cos), cat(zeros, r_sin * r_sign))


def _attn_kernel(sink_ref, bias0_ref, bias1_ref, q_ref, k0_ref, k1_ref, k2_ref, k3_ref, kx_ref,
                 v0_ref, v1_ref, v2_ref, v3_ref, vx_ref, o_ref):
    g_heads = N_Q_HEADS // N_KV_HEADS
    lane = lax.broadcasted_iota(I32, (AB, LANE), 1)
    lo = lane < HEAD_DIM
    rows = lax.broadcasted_iota(I32, (g_heads * AB, 1), 0)
    k_refs = (k0_ref, k1_ref, k2_ref, k3_ref)
    v_refs = (v0_ref, v1_ref, v2_ref, v3_ref)
    biases = (bias0_ref[0], bias1_ref[0])
    chains = [(t, g) for t in range(Q_BLOCKS) for g in range(N_KV_HEADS)]
    scores, sinks, values = [], [], []
    for t, g in chains:
        ks = slice(g * LANE, (g + 1) * LANE)
        kd = jnp.concatenate([r[0, :, ks] for r in k_refs[t:t + 3]] + [kx_ref[0, :, ks]], axis=0)
        values.append(jnp.concatenate([r[0, :, ks] for r in v_refs[t:t + 3]] + [vx_ref[0, :, ks]], axis=0))
        qs = []
        for c in range(2 * g, 2 * g + 2):
            qc = q_ref[0, t * AB:(t + 1) * AB, c * LANE:(c + 1) * LANE]
            zero = jnp.zeros_like(qc)
            qs += [jnp.where(lo, qc, zero), jnp.where(lo, zero, qc)]
        scores.append(_dot_nt(jnp.concatenate(qs, axis=0), kd) + biases[t])
        sink = jnp.full((g_heads * AB, 1), sink_ref[g_heads * g + g_heads - 1], F32)
        for h in range(g_heads - 2, -1, -1):
            sink = jnp.where(rows < (h + 1) * AB, sink_ref[g_heads * g + h], sink)
        sinks.append(sink)
    for (t, g), s, sink, vd in zip(chains, scores, sinks, values):
        m = jnp.maximum(jnp.max(s, axis=-1, keepdims=True), sink)
        p = jnp.exp(s - m)
        den = jnp.sum(p, axis=-1, keepdims=True) + jnp.exp(sink - m)
        o = _dot(p.astype(BF16), vd) / den
        for h in range(2):
            c = 2 * g + h
            o_ref[0, t * AB:(t + 1) * AB, c * LANE:(c + 1) * LANE] = jnp.where(
                lo, o[2 * h * AB:(2 * h + 1) * AB], o[(2 * h + 1) * AB:(2 * h + 2) * AB]).astype(BF16)


def _attn_bias():
    g_heads = N_Q_HEADS // N_KV_HEADS
    r = jnp.arange(g_heads * AB)[:, None] % AB
    j = jnp.arange(3 * AB + TM)[None, :]
    band = jnp.abs(j - AB - r) <= WINDOW
    is_ctx = j >= 3 * AB
    variants = (band, band & (j >= AB), band & (j < 2 * AB), jnp.zeros_like(band))
    return jnp.stack([jnp.where(v | is_ctx, 0.0, NEG) for v in variants]).astype(F32)


def _attention(q, k2, v2, sinks):
    b, s, _ = q.shape
    nb = s // AB
    assert nb - CTX_BLOCKS >= 2 and nb % Q_BLOCKS == 0 and CTX_BLOCKS % Q_BLOCKS == 0
    bias = _attn_bias()

    def variant(blk):
        return jnp.where(blk < CTX_BLOCKS, 3, jnp.where(blk == CTX_BLOCKS, 1, jnp.where(blk == nb - 1, 2, 0)))

    def kv(offset):
        return pl.BlockSpec((1, AB, KV2), lambda bi, i: (bi, jnp.clip(Q_BLOCKS * i + offset, 0, nb - 1), 0))

    def bias_spec(t):
        return pl.BlockSpec((1,) + bias.shape[1:], lambda bi, i: (variant(Q_BLOCKS * i + t), 0, 0))

    cx = pl.BlockSpec((1, TM, KV2), lambda bi, i: (bi, 0, 0))
    kvs = [kv(o) for o in range(-1, Q_BLOCKS + 1)] + [cx]
    qo = pl.BlockSpec((1, Q_BLOCKS * AB, ATT_Q), lambda bi, i: (bi, i, 0))
    return pl.pallas_call(
        _attn_kernel,
        grid=(b, nb // Q_BLOCKS),
        in_specs=[pl.BlockSpec(memory_space=pltpu.SMEM), bias_spec(0), bias_spec(1), qo] + kvs + kvs,
        out_specs=qo,
        out_shape=jax.ShapeDtypeStruct((b, s, ATT_Q), BF16),
        compiler_params=_params("arbitrary", "arbitrary"),
    )(sinks, bias, bias, q, *([k2] * len(kvs)), *([v2] * len(kvs)))


def _ret_kernel(lg_ref, qf_ref, kf_ref, vf_ref, qb_ref, kb_ref, vb_ref, yf_ref, yb_ref,
                st_ref, dm_ref, qd_ref, kd_ref, cd_ref, *, batch):
    step = pl.program_id(0)
    lane = lax.broadcasted_iota(I32, (AB, LANE), 1)
    row = lax.broadcasted_iota(I32, (AB, LANE), 0)
    lo = lane < RET_DIM
    tiles = RET_W // LANE

    @pl.when(step == 0)
    def _():
        st_ref[...] = jnp.zeros_like(st_ref)
        ii = row.astype(F32)
        jj = lane.astype(F32)
        for d in range(2):
            for c in range(tiles):
                lg0 = lg_ref[d * RET_HEADS + 2 * c]
                lg1 = lg_ref[d * RET_HEADS + 2 * c + 1]
                lgl = jnp.where(lo, lg0, lg1)
                q_exp = ii + 1.0 if d == 0 else AB - ii
                k_exp = (AB - 1.0) - ii if d == 0 else ii
                qd_ref[d * tiles + c] = jnp.exp(q_exp * lgl)
                kd_ref[d * tiles + c] = jnp.exp(k_exp * lgl)
                cd_ref[d * tiles + c] = jnp.exp(AB * lgl)
                rel = ii - jj if d == 0 else jj - ii
                for hh, lgh in enumerate((lg0, lg1)):
                    dm_ref[d * tiles + c, hh * AB:(hh + 1) * AB, :] = jnp.where(
                        rel >= 0, jnp.exp(jnp.maximum(rel, 0.0) * lgh), 0.0)

    same_head = (row < RET_DIM) == lo
    dirs = ((qf_ref, kf_ref, vf_ref, yf_ref), (qb_ref, kb_ref, vb_ref, yb_ref))
    chains = [(d, b, c) for d in range(2) for b in range(batch) for c in range(tiles)]

    def operands(d, b, c):
        q_ref, k_ref, v_ref, _ = dirs[d]
        sl = slice(c * LANE, (c + 1) * LANE)
        return q_ref[b, :, sl], k_ref[b, :, sl], v_ref[b, :, sl]

    probs = []
    for d, b, c in chains:
        q, k, _ = operands(d, b, c)
        zero = jnp.zeros_like(q)
        q2 = jnp.concatenate([jnp.where(lo, q, zero), jnp.where(lo, zero, q)], axis=0)
        probs.append((_dot_nt(q2, k) * dm_ref[d * tiles + c]).astype(BF16))
    for (d, b, c), p in zip(chains, probs):
        q, _, v = operands(d, b, c)
        t = d * tiles + c
        si = (d * batch + b) * tiles + c
        y_intra = jnp.where(lo, _dot(p[:AB], v), _dot(p[AB:], v))
        q_dec = (q.astype(F32) * qd_ref[t]).astype(BF16)
        dirs[d][3][b, :, c * LANE:(c + 1) * LANE] = (y_intra + _dot(q_dec, st_ref[si].astype(BF16))).astype(BF16)
    for d, b, c in chains:
        _, k, v = operands(d, b, c)
        t = d * tiles + c
        si = (d * batch + b) * tiles + c
        k_dec = (k.astype(F32) * kd_ref[t]).astype(BF16)
        st_ref[si] = st_ref[si] * cd_ref[t] + jnp.where(same_head, _dot_tn(k_dec, v), 0.0)


def _retention(rq, rk, rv, log_g):
    b, s, w = rq.shape
    nb = s // AB
    tiles = w // LANE

    def back(i):
        return jnp.where(i < CTX_BLOCKS, CTX_BLOCKS - 1 - i, nb - 1 + CTX_BLOCKS - i)

    fwd = pl.BlockSpec((b, AB, w), lambda i: (0, i, 0))
    bwd = pl.BlockSpec((b, AB, w), lambda i: (0, back(i), 0))
    return pl.pallas_call(
        functools.partial(_ret_kernel, batch=b),
        grid=(nb,),
        in_specs=[pl.BlockSpec(memory_space=pltpu.SMEM), fwd, fwd, fwd, bwd, bwd, bwd],
        out_specs=[fwd, bwd],
        out_shape=[jax.ShapeDtypeStruct((b, s, w), BF16)] * 2,
        scratch_shapes=[
            pltpu.VMEM((2 * b * tiles, LANE, LANE), F32),
            pltpu.VMEM((2 * tiles, 2 * AB, LANE), F32),
            pltpu.VMEM((2 * tiles, AB, LANE), F32),
            pltpu.VMEM((2 * tiles, AB, LANE), F32),
            pltpu.VMEM((2 * tiles, AB, LANE), F32),
        ],
        compiler_params=_params("arbitrary"),
    )(log_g.reshape(-1), rq, rk, rv, rq, rk, rv)


def _route_tile(logits, tri_ref, ls_ref, rw_ref, cnt_ref):
    ne = N_EXPERTS
    eidx = lax.broadcasted_iota(I32, (ne, TM), 0).astype(F32)
    tops, hots = [], []
    for _ in range(TOP_K):
        mx = jnp.max(logits, axis=0, keepdims=True)
        idx = jnp.min(jnp.where(logits == mx, eidx, float(ne)), axis=0, keepdims=True)
        hot = eidx == idx
        logits = jnp.where(hot, NEG * 2.0, logits)
        tops.append(mx)
        hots.append(hot)
    ex = [jnp.exp(mx - tops[0]) for mx in tops]
    tot = ex[0]
    for e in ex[1:]:
        tot = tot + e
    sel = jnp.zeros((ne, TM), F32)
    for hot in hots:
        sel = jnp.where(hot, 1.0, sel)
    incl = _dot(sel.astype(BF16), tri_ref[...])
    cnt = incl[:, TM - 1:TM]
    run = jnp.floor((cnt + (SUBLANE - 1.0)) * (1.0 / SUBLANE)) * SUBLANE
    ends = jnp.broadcast_to(run, (ne, LANE))
    erow = lax.broadcasted_iota(I32, (ne, LANE), 0)
    shift = 1
    while shift < ne:
        ends = ends + jnp.where(erow >= shift, pltpu.roll(ends, shift, 0), 0.0)
        shift *= 2
    slot = (ends[:, :1] - run) + incl - 1.0
    ls_rows = [jnp.sum(jnp.where(hot, slot, 0.0), axis=0, keepdims=True) for hot in hots]
    rw_rows = [e / tot for e in ex]
    lst = jnp.concatenate(ls_rows + [jnp.full((SUBLANE - TOP_K, TM), -1.0, F32)], axis=0)
    ls_ref[...] = jnp.transpose(
        jnp.concatenate([lst, jnp.full((LANE - SUBLANE, TM), -1.0, F32)], axis=0)).astype(I32)
    rw_ref[...] = jnp.transpose(jnp.concatenate(rw_rows + [jnp.zeros((LANE - TOP_K, TM), F32)], axis=0))
    cnt_ref[...] = jnp.broadcast_to(cnt, cnt_ref.shape).astype(I32)
    return lst


def _merge_kernel(*refs, seq, nt, ntt, n_ctx):
    _merge_body(refs[n_ctx], refs[:n_ctx], *refs[n_ctx + 1:], seq=seq, nt=nt, ntt=ntt)


def _merge_body(x_ref, c_refs, attn_ref, up_ref, uc_ref, un_ref, yf_ref, yb_ref, rg_ref, g_ref, mod_ref,
                  wbd_ref, ps_ref, woa_ref, wop_ref, wor_ref, wout_ref, n2_ref, wr_ref,
                  br_ref, avg_ref, tri_ref,
                  xo_ref, xs_ref, ls_ref, rw_ref, cnt_ref, lg_ref, h2_ref, *, seq, nt, ntt):
    i = pl.program_id(0)
    j = jnp.minimum(i, ntt - 1) % nt

    @pl.when(i == 0)
    def _():
        lg_ref[...] = jnp.zeros_like(lg_ref)
        h2_ref[...] = jnp.zeros_like(h2_ref)

    lst = _route_tile(lg_ref[...], tri_ref, ls_ref, rw_ref, cnt_ref)

    ext = TM + 2 * POOL_HALO
    seq_len = jnp.where(j == 0, TM, seq)
    start = jnp.where(j == 0, 0, (j - 1) * TM)
    u = uc_ref[0]
    ue = jnp.concatenate([up_ref[0], u, un_ref[0]], axis=0)
    erow = lax.broadcasted_iota(I32, (ext, 1), 0) + (start - POOL_HALO)
    ue = jnp.where((erow >= 0) & (erow < seq_len), ue, 0.0)
    t_pos = lax.broadcasted_iota(I32, (TM, 1), 0) + start
    glane = lax.broadcasted_iota(I32, (TM, POOL_WIDTH), 1) // (POOL_WIDTH // len(POOL_WINDOWS))
    run = ue
    width = 1
    diff = jnp.zeros((TM, POOL_WIDTH), F32)
    for gi, w in enumerate(POOL_WINDOWS):
        while width < w:
            run = run + pltpu.roll(run, ext - width, 0)
            width *= 2
        win = pltpu.roll(run, w // 2, 0)[POOL_HALO:POOL_HALO + TM]
        cnt = jnp.minimum(t_pos - w // 2 + w, seq_len) - jnp.maximum(t_pos - w // 2, 0)
        diff = jnp.where(glane == gi, win / cnt.astype(F32) - u, diff)
    pool = _dot(diff.astype(BF16), wbd_ref[...]) * ps_ref[...]

    def head_norm(y):
        dlt = y.astype(F32) - _dot(y, avg_ref[...])
        var = _dot_hilo(dlt * dlt, avg_ref[...])
        return dlt * lax.rsqrt(var + GN_EPS)

    rg = rg_ref[0].astype(F32)
    ret = head_norm(yf_ref[0]) * _silu(rg[:, :RET_W]) + head_norm(yb_ref[0]) * _silu(rg[:, RET_W:])

    slot = lax.broadcasted_iota(I32, (SORT_ROWS, TM), 0).astype(F32)
    p = jnp.zeros((SORT_ROWS, TM), F32)
    for k in range(TOP_K):
        p = jnp.where(slot == lst[k:k + 1, :], 1.0, p)
    xs_ref[...] = _dot(p.astype(BF16), h2_ref[...])

    d = x_ref.shape[-1]
    gate = lambda t: _sigmoid(g_ref[0, :, t * d:(t + 1) * d])
    m = (gate(0) * _dot(attn_ref[0], woa_ref[...]).astype(BF16)
         + gate(1) * _dot(pool.astype(BF16), wop_ref[...]).astype(BF16)
         + gate(2) * _dot(ret.astype(BF16), wor_ref[...]).astype(BF16))
    xn = _merged_rows((x_ref,), c_refs, j == 0) + mod_ref[0, 2:3, :] * _dot(m, wout_ref[...])
    xo_ref[0] = xn

    ms = jnp.mean(xn * xn, axis=-1, keepdims=True)
    h2 = xn * lax.rsqrt(ms + NORM_EPS) * n2_ref[...] * (1.0 + mod_ref[0, 4:5, :]) + mod_ref[0, 3:4, :]
    h2_ref[...] = h2.astype(BF16)
    hh, hl = _split(h2)
    ne = N_EXPERTS
    full = _dot_nt(wr_ref[...], hh)
    logits = full[:ne] + full[ne:] + _dot_nt(wr_ref[:ne], hl) + br_ref[...]
    lg_ref[...] = logits


def _merge(x, ctx, attn, u, yf, yb, rg, gates, mod_l, seq, wts):
    b, s, _ = attn.shape
    d = x.shape[-1]
    nt = s // TM
    ntt = b * nt
    hb = TM // POOL_HALO
    nh = s // POOL_HALO
    cur = lambda i: jnp.minimum(i, ntt - 1)
    lag = lambda i: jnp.maximum(i - 1, 0)
    tok = lambda width: pl.BlockSpec((1, TM, width), lambda i: (cur(i) // nt, cur(i) % nt, 0))
    full = lambda a: pl.BlockSpec(a.shape, lambda i: (0,) * a.ndim)
    lanes = pl.BlockSpec((TM, LANE), lambda i: (lag(i), 0))
    if ctx is None:
        stream = [tok(d)]
    else:
        stream = [pl.BlockSpec((1, TM, d), lambda i: (cur(i) // nt, 0, 0)),
                  pl.BlockSpec((1, TM, d), lambda i: (cur(i) // nt, jnp.maximum(cur(i) % nt - 1, 0), 0))]
    in_specs = stream + [
        tok(ATT_Q),
        pl.BlockSpec((1, POOL_HALO, POOL_WIDTH),
                     lambda i: (cur(i) // nt, jnp.maximum((cur(i) % nt) * hb - 1, 0), 0)),
        tok(POOL_WIDTH),
        pl.BlockSpec((1, POOL_HALO, POOL_WIDTH),
                     lambda i: (cur(i) // nt, jnp.minimum((cur(i) % nt + 1) * hb, nh - 1), 0)),
        tok(RET_W), tok(RET_W), tok(2 * RET_W), tok(3 * d),
        pl.BlockSpec((1, 6, d), lambda i: (jnp.where(cur(i) % nt == 0, b, cur(i) // nt), 0, 0)),
    ] + [full(a) for a in wts]
    return pl.pallas_call(
        functools.partial(_merge_kernel, seq=seq, nt=nt, ntt=ntt, n_ctx=len(stream) - 1),
        grid=(ntt + 1,),
        in_specs=in_specs,
        out_specs=[tok(d), pl.BlockSpec((SORT_ROWS, d), lambda i: (lag(i), 0)), lanes, lanes,
                   pl.BlockSpec((N_EXPERTS, LANE), lambda i: (lag(i), 0))],
        out_shape=[jax.ShapeDtypeStruct((b, s, d), F32), jax.ShapeDtypeStruct((ntt * SORT_ROWS, d), F32),
                   jax.ShapeDtypeStruct((b * s, LANE), I32),
                   jax.ShapeDtypeStruct((b * s, LANE), F32), jax.ShapeDtypeStruct((ntt * N_EXPERTS, LANE), I32)],
        scratch_shapes=[pltpu.VMEM((N_EXPERTS, TM), F32), pltpu.VMEM((TM, d), BF16)],
        compiler_params=_params("arbitrary"),
    )(*([] if ctx is None else [ctx]), x, attn, u, u, u, yf, yb, rg, gates, mod_l, *wts)


def _rows(tile_index):
    return pl.ds(pl.multiple_of(tile_index * SUBLANE, SUBLANE), SUBLANE)


def _tile_loop(count_ref, tile, fn):
    def body(j, carry):
        fn(j)
        return carry

    lax.fori_loop(0, MIN_TILES, body, 0, unroll=8)
    lax.fori_loop(MIN_TILES, count_ref[tile], body, 0)


def _tile_copies(table_ref, count_ref, tile, copy):
    _tile_loop(count_ref, tile, lambda j: copy(j, table_ref[tile * SORT_TILES + j]))


def _expert_kernel(be_ref, nu_ref, src_ref, xs_hbm, w1_ref, b1_ref, w2_ref, b2_ref, y_ref, w1b, w2b, xbuf, sem):
    bi = pl.program_id(0)
    used = bi < nu_ref[0]
    fresh = (bi == 0) | (be_ref[bi] != be_ref[jnp.maximum(bi - 1, 0)])
    cur = bi % 2
    block_tiles = EB // SUBLANE

    def each_tile(fn):
        def body(t, carry):
            fn(t)
            return carry
        lax.fori_loop(0, block_tiles, body, 0, unroll=8)

    def fetch(blk, half):
        each_tile(lambda t: pltpu.make_async_copy(
            xs_hbm.at[_rows(src_ref[blk * block_tiles + t])], xbuf.at[half, _rows(t)], sem.at[half]).start())

    @pl.when(bi == 0)
    def _():
        fetch(0, 0)

    @pl.when(bi + 1 < nu_ref[0])
    def _():
        fetch(bi + 1, 1 - cur)

    @pl.when(used & fresh)
    def _():
        w1b[...] = w1_ref[0, 0].astype(BF16)
        w2b[...] = w2_ref[0, 0].astype(BF16)

    @pl.when(jnp.logical_not(used))
    def _():
        y_ref[...] = jnp.zeros_like(y_ref)

    @pl.when(used)
    def _():
        each_tile(lambda t: pltpu.make_async_copy(
            xs_hbm.at[_rows(t)], xbuf.at[cur, _rows(t)], sem.at[cur]).wait())
        xb = xbuf[cur].astype(BF16)
        y = None
        for c in range(0, D_FF, FF_CHUNK):
            glu = _dot(xb, w1b[:, c:c + FF_CHUNK]) + b1_ref[0, 0, :, c:c + FF_CHUNK]
            lin = _dot(xb, w1b[:, D_FF + c:D_FF + c + FF_CHUNK]) + b1_ref[0, 0, :, D_FF + c:D_FF + c + FF_CHUNK]
            glu = jnp.minimum(glu, SWIGLU_LIMIT)
            lin = jnp.clip(lin, -SWIGLU_LIMIT, SWIGLU_LIMIT)
            act = glu * _sigmoid(SWIGLU_ALPHA * glu) * (lin + 1.0)
            part = _dot(act.astype(BF16), w2b[c:c + FF_CHUNK, :])
            y = part if y is None else y + part
        y_ref[...] = y + b2_ref[0, 0]


def _experts(xs, block_e, n_used, src, w1, b1, w2, b2, layer, n_blocks):
    d = xs.shape[1]
    depth, ne, _, f2 = w1.shape
    grid_spec = pltpu.PrefetchScalarGridSpec(
        num_scalar_prefetch=3,
        grid=(n_blocks,),
        in_specs=[
            pl.BlockSpec(memory_space=pl.ANY),
            pl.BlockSpec((1, 1, d, f2), lambda bi, be, *_: (layer, be[bi], 0, 0)),
            pl.BlockSpec((1, 1, 1, f2), lambda bi, be, *_: (layer, be[bi], 0, 0)),
            pl.BlockSpec((1, 1, f2 // 2, d), lambda bi, be, *_: (layer, be[bi], 0, 0)),
            pl.BlockSpec((1, 1, 1, d), lambda bi, be, *_: (layer, be[bi], 0, 0)),
        ],
        out_specs=pl.BlockSpec((EB, d), lambda bi, *_: (bi, 0)),
        scratch_shapes=[pltpu.VMEM((d, f2), BF16), pltpu.VMEM((f2 // 2, d), BF16),
                        pltpu.VMEM((2, EB, d), F32), pltpu.SemaphoreType.DMA((2,))],
    )
    return pl.pallas_call(
        _expert_kernel,
        grid_spec=grid_spec,
        out_shape=jax.ShapeDtypeStruct((n_blocks * EB, d), F32),
        compiler_params=_params("arbitrary"),
    )(block_e, n_used, src, xs, w1, b1.reshape(depth, ne, 1, f2), w2, b2.reshape(depth, ne, 1, d))


def _combine_kernel(dst_ref, cnt_ref, x_ref, ls_ref, rw_ref, mod_ref, fg_ref, ys_hbm, xo_ref, buf, sem, *, final):
    i = pl.program_id(0)
    last = pl.num_programs(0) - 1
    cur = i % 2

    def fetch(tile, half):
        _tile_copies(dst_ref, cnt_ref, tile, lambda j, t: pltpu.make_async_copy(
            ys_hbm.at[_rows(t)], buf.at[half, _rows(j)], sem.at[half]).start())

    @pl.when(i == 0)
    def _():
        buf[...] = jnp.zeros_like(buf)
        fetch(0, 0)

    @pl.when(i < last)
    def _():
        fetch(i + 1, 1 - cur)

    _tile_loop(cnt_ref, i, lambda j: pltpu.make_async_copy(
        ys_hbm.at[_rows(j)], buf.at[cur, _rows(j)], sem.at[cur]).wait())
    yb = buf[cur].astype(BF16)
    slot = lax.broadcasted_iota(I32, (TM, SORT_ROWS), 1)
    ls = ls_ref[...]
    rw = rw_ref[...]
    g = jnp.zeros((TM, SORT_ROWS), F32)
    for k in range(TOP_K):
        g = jnp.where(slot == ls[:, k:k + 1], rw[:, k:k + 1], g)
    xn = x_ref[0] + mod_ref[0, 5:6, :] * _dot(g.astype(BF16), yb)
    if final:
        ms = jnp.mean(xn * xn, axis=-1, keepdims=True)
        xn = xn * lax.rsqrt(ms + NORM_EPS) * fg_ref[...]
    xo_ref[0] = xn


def _combine(xn, ls, rw, mod_l, final_gain, ys, tabs, final):
    b, s, d = xn.shape
    nt = s // TM
    tok = pl.BlockSpec((1, TM, d), lambda i, *_: (i // nt, i % nt, 0))
    lanes = pl.BlockSpec((TM, LANE), lambda i, *_: (i, 0))
    if final:
        out_spec = pl.BlockSpec((1, TM, d), lambda i, *_: (i // nt, jnp.maximum(i % nt - 1, 0), 0))
        out_shape = jax.ShapeDtypeStruct((b, s - TM, d), F32)
    else:
        out_spec, out_shape = tok, jax.ShapeDtypeStruct((b, s, d), F32)
    grid_spec = pltpu.PrefetchScalarGridSpec(
        num_scalar_prefetch=2,
        grid=(b * nt,),
        in_specs=[tok, lanes, lanes,
                  pl.BlockSpec((1, 6, d), lambda i, *_: (jnp.where(i % nt == 0, b, i // nt), 0, 0)),
                  pl.BlockSpec((1, d), lambda i, *_: (0, 0)),
                  pl.BlockSpec(memory_space=pl.ANY)],
        out_specs=out_spec,
        scratch_shapes=[pltpu.VMEM((2, SORT_ROWS, d), F32), pltpu.SemaphoreType.DMA((2,))],
    )
    return pl.pallas_call(
        functools.partial(_combine_kernel, final=final),
        grid_spec=grid_spec,
        out_shape=out_shape,
        compiler_params=_params("arbitrary"),
    )(tabs["dst"], tabs["count"], xn, ls, rw, mod_l, final_gain.reshape(1, d), ys)


def _inproj_weight(w_in_l):
    d = w_in_l.shape[0]
    sizes = (ATT_Q, N_KV_HEADS * HEAD_DIM, N_KV_HEADS * HEAD_DIM, POOL_WIDTH,
             RET_W, RET_W, RET_W, RET_W, RET_W, d, d, d)
    parts, off = [], 0
    for sz in sizes:
        parts.append(w_in_l[:, off:off + sz])
        off += sz
    q, k, v, u, rq, rk, rv, rgf, rgb, ga, gp, gr = parts
    twice = lambda w: jnp.concatenate(
        [w[:, h * HEAD_DIM:(h + 1) * HEAD_DIM] for h in range(N_KV_HEADS) for _ in range(2)], axis=1)
    cols = [q * HEAD_DIM ** -0.5, twice(k), twice(v), rq, rk * RET_DIM ** -0.5, rv, rgf, rgb, u, ga, gp, gr]
    return jnp.concatenate(cols, axis=1).astype(BF16)


def _block_diag(blocks):
    n, r, c = blocks.shape
    out = jnp.zeros((n * r, n * c), blocks.dtype)
    for g in range(n):
        out = out.at[g * r:(g + 1) * r, g * c:(g + 1) * c].set(blocks[g])
    return out


def _routing_tables(cnt, n_blocks):
    ntt = cnt.shape[0] // N_EXPERTS
    counts = cnt.reshape(ntt, N_EXPERTS, LANE)[:, :, 0]
    run = (counts + SUBLANE - 1) // SUBLANE
    total = jnp.sum(run, axis=0)
    eb = EB // SUBLANE
    padded = (total + eb - 1) // eb * eb
    pad_end = jnp.cumsum(padded)
    pad_start = pad_end - padded
    off = pad_start[None, :] + jnp.cumsum(run, axis=0) - run
    run_end = jnp.cumsum(run, axis=1)
    j = jnp.arange(SORT_TILES)
    owner = jnp.sum(run_end[:, None, :] <= j[None, :, None], axis=2)
    mine = owner[:, :, None] == jnp.arange(N_EXPERTS)[None, None, :]
    in_region = j[None, :] + jnp.sum(jnp.where(mine, (off - (run_end - run))[:, None, :], 0), axis=2)
    dst = jnp.where(owner < N_EXPERTS, in_region, 0)
    n_used = pad_end[-1] // eb
    blk = jnp.minimum(jnp.arange(n_blocks), n_used - 1) * eb
    block_e = jnp.minimum(jnp.sum(pad_end[None, :] <= blk[:, None], axis=1), N_EXPERTS - 1)
    mine_b = block_e[:, None] == jnp.arange(N_EXPERTS)[None, :]
    of_block = lambda table: jnp.sum(jnp.where(mine_b[:, None, :], table[None, :, :], 0), axis=2)
    first = of_block(jnp.cumsum(run, axis=0) - run)
    length = of_block(run)
    local = of_block(run_end - run)
    start_b = jnp.sum(jnp.where(mine_b, pad_start[None, :], 0), axis=1)
    pos = (jnp.arange(n_blocks) * eb - start_b)[:, None] + jnp.arange(eb)[None, :]
    inside = (pos[:, :, None] >= first[:, None, :]) & (pos[:, :, None] < (first + length)[:, None, :])
    where = (jnp.arange(ntt) * SORT_TILES)[None, None, :] + (local - first)[:, None, :] + pos[:, :, None]
    zero_tile = SORT_TILES - 1
    src = jnp.where(jnp.any(inside, axis=2), jnp.sum(jnp.where(inside, where, 0), axis=2), zero_tile)
    tabs = dict(dst=dst.reshape(-1).astype(I32), count=run_end[:, -1].astype(I32),
                src=src.reshape(-1).astype(I32))
    return tabs, block_e.astype(I32), n_used.reshape(1).astype(I32)


def kernel(x, c, ctx, c_ctx, w_mod, b_mod, norm1, norm2, w_in, attn_sinks, pool_w, pool_scale, ret_decay,
           w_o_attn, w_o_pool, w_o_ret, w_out, w_router, b_router, w_expert_in, b_expert_in, w_expert_out,
           b_expert_out, final_norm):
    b, seq, d = x.shape
    depth = w_mod.shape[0]
    assert ctx.shape[1] == TM and seq % TM == 0 and seq % GRID_W == 0 and b + 1 <= SUBLANE
    s = seq + TM
    xm = x
    mod = _modulation(c, c_ctx, w_mod, b_mod)
    tables = _rope_tables(seq)
    ntt = b * s // TM
    n_blocks = -(-(b * s * TOP_K + ntt * N_EXPERTS * (SUBLANE - 1)) // EB) + N_EXPERTS
    avg = _block_diag(jnp.full((RET_HEADS, RET_DIM, RET_DIM), 1.0 / RET_DIM, F32)).astype(BF16)
    tri = (jnp.arange(TM)[:, None] <= jnp.arange(TM)[None, :]).astype(BF16)
    for l in range(depth):
        first_ctx = ctx if l == 0 else None
        q, k2, v2, rq, rk, rv, rg, u, gates = _inproj(xm, first_ctx, mod[l], norm1[l], _inproj_weight(w_in[l]), tables)
        attn = _attention(q, k2, v2, attn_sinks[l])
        yf, yb = _retention(rq, rk, rv, jax.nn.log_sigmoid(ret_decay[l].astype(F32)))
        wr_hilo = jnp.concatenate(_split(w_router[l].T), axis=0)
        br = jnp.broadcast_to(b_router[l][:, None], (N_EXPERTS, TM))
        wts = (_block_diag(pool_w[l]).astype(BF16), pool_scale[l].reshape(1, -1),
               w_o_attn[l].astype(BF16), w_o_pool[l].astype(BF16), w_o_ret[l].astype(BF16),
               w_out[l].astype(BF16), norm2[l].reshape(1, d), wr_hilo, br, avg, tri)
        xn, xs, ls, rw, cnt = _merge(xm, first_ctx, attn, u, yf, yb, rg, gates, mod[l], seq, wts)
        tabs, block_e, n_used = _routing_tables(cnt, n_blocks)
        ys = _experts(xs, block_e, n_used, tabs["src"], w_expert_in, b_expert_in, w_expert_out, b_expert_out, l,
                      n_blocks)
        xm = _combine(xn, ls, rw, mod[l], final_norm, ys, tabs, final=l == depth - 1)
    return xm
```

```python
import functools

import jax
import jax.numpy as jnp
from jax import lax
from jax.experimental import pallas as pl
from jax.experimental.pallas import tpu as pltpu

F32 = jnp.float32
BF16 = jnp.bfloat16
I32 = jnp.int32

GRID_W = 64
HEAD_DIM = 64
N_Q_HEADS = 8
N_KV_HEADS = 2
WINDOW = 128
ROPE_THETA = 10000.0
POOL_WIDTH = 256
POOL_WINDOWS = (2, 4, 8, 16)
RET_HEADS = 4
RET_DIM = 64
N_EXPERTS = 32
TOP_K = 4
D_FF = 1024
SWIGLU_LIMIT = 7.0
SWIGLU_ALPHA = 1.702
NORM_EPS = 1e-6
GN_EPS = 1e-5

LANE = 128
SUBLANE = 8
MXU_N = 256
TM = 256
AB = 128
IN_TILE = 640
CTX_BLOCKS = TM // AB
Q_BLOCKS = 2
EB = 512
EB_STEP = 128
FF_CHUNK = 512
POOL_HALO = SUBLANE
SORT_ROWS = TM * TOP_K + N_EXPERTS * SUBLANE
SORT_TILES = SORT_ROWS // SUBLANE
MIN_TILES = TM * TOP_K // SUBLANE
NEG = -1e30
VMEM_LIMIT = 56 * 1024 * 1024

ATT_Q = N_Q_HEADS * HEAD_DIM
KV2 = 2 * N_KV_HEADS * HEAD_DIM
RET_W = RET_HEADS * RET_DIM


def _dot(a, b):
    return jnp.dot(a, b, preferred_element_type=F32)


def _dot_nt(a, b):
    return lax.dot_general(a, b, (((1,), (1,)), ((), ())), preferred_element_type=F32)


def _dot_tn(a, b):
    return lax.dot_general(a, b, (((0,), (0,)), ((), ())), preferred_element_type=F32)


def _split(a):
    hi = a.astype(BF16)
    lo = (a - hi.astype(F32)).astype(BF16)
    return hi, lo


def _dot_hilo(a, m):
    hi, lo = _split(a)
    return _dot(hi, m) + _dot(lo, m)


def _sigmoid(x):
    return 0.5 * jnp.tanh(0.5 * x) + 0.5


def _silu(x):
    return x * _sigmoid(x)


def _params(*sem):
    return pltpu.CompilerParams(dimension_semantics=sem, vmem_limit_bytes=VMEM_LIMIT)


def _mod_kernel(c_ref, w_ref, b_ref, o_ref):
    s = _silu(c_ref[...])
    sh, sl = _split(s)
    wh, wl = _split(w_ref[0])
    o_ref[0] = _dot(sh, wh) + _dot(sh, wl) + _dot(sl, wh) + b_ref[0]


def _modulation(c, c_ctx, w_mod, b_mod):
    depth, d, six_d = w_mod.shape
    b = c.shape[0]
    cc = jnp.zeros((SUBLANE, d), F32).at[:b].set(c).at[b].set(c_ctx)
    out = pl.pallas_call(
        _mod_kernel,
        grid=(depth, six_d // d),
        in_specs=[
            pl.BlockSpec((SUBLANE, d), lambda l, n: (0, 0)),
            pl.BlockSpec((1, d, d), lambda l, n: (l, 0, n)),
            pl.BlockSpec((1, 1, d), lambda l, n: (l, 0, n)),
        ],
        out_specs=pl.BlockSpec((1, SUBLANE, d), lambda l, n: (l, 0, n)),
        out_shape=jax.ShapeDtypeStruct((depth, SUBLANE, six_d), F32),
        compiler_params=_params("arbitrary", "arbitrary"),
    )(cc, w_mod, b_mod.reshape(depth, 1, six_d))
    return out[:, : b + 1].reshape(depth, b + 1, 6, d)


def _rope(x, cos, sin, half, first):
    partner = jnp.where(first, pltpu.roll(x, LANE - half, 1), pltpu.roll(x, half, 1))
    return x * cos + partner * sin


def _merged_rows(x_refs, c_refs, first):
    parts = []
    for n, ref in enumerate(x_refs):
        part = ref[0]
        if n < len(c_refs):
            part = jnp.where(first, c_refs[n][0], part)
        parts.append(part)
    return parts[0] if len(parts) == 1 else jnp.concatenate(parts, axis=0)


def _inproj_kernel(*refs, n_sub, n_ctx):
    x_refs, c_refs, refs = refs[:n_sub], refs[n_sub:n_sub + n_ctx], refs[n_sub + n_ctx:]
    mod_ref, modc_ref, n1_ref, w_ref, ac_ref, as_ref, rc_ref, rs_ref = refs[:8]
    q_ref, k_ref, v_ref, rq_ref, rk_ref, rv_ref, rg_ref, u_ref, g_ref = refs[8:]
    x = _merged_rows(x_refs, c_refs, pl.program_id(1) == 0)
    rows = x.shape[0]
    ms = jnp.mean(x * x, axis=-1, keepdims=True)
    y = x * lax.rsqrt(ms + NORM_EPS) * n1_ref[...]
    is_ctx = (lax.broadcasted_iota(I32, (rows, 1), 0) < TM) & (pl.program_id(1) == 0)
    scale = jnp.where(is_ctx, modc_ref[0, 1:2, :], mod_ref[0, 1:2, :])
    shift = jnp.where(is_ctx, modc_ref[0, 0:1, :], mod_ref[0, 0:1, :])
    hb = (y * (1.0 + scale) + shift).astype(BF16)
    lane = lax.broadcasted_iota(I32, (rows, LANE), 1)
    a_first = (lane % (HEAD_DIM // 2)) < (HEAD_DIM // 4)
    r_first = (lane % RET_DIM) < (RET_DIM // 2)
    ac, asn, rc, rsn = ac_ref[...], as_ref[...], rc_ref[...], rs_ref[...]

    def proj(off, width):
        return _dot(hb, w_ref[:, off:off + width])

    def rotated(ref, width, cos, sin, half, first, off):
        for t in range(0, width, MXU_N):
            pr = proj(off + t, MXU_N)
            for g in range(0, MXU_N, LANE):
                ref[0, :, t + g:t + g + LANE] = _rope(pr[:, g:g + LANE], cos, sin, half, first).astype(BF16)

    off = 0
    rotated(q_ref, ATT_Q, ac, asn, HEAD_DIM // 4, a_first, off)
    off += ATT_Q
    rotated(k_ref, KV2, ac, asn, HEAD_DIM // 4, a_first, off)
    off += KV2
    v_ref[0] = proj(off, KV2).astype(BF16)
    off += KV2
    for ref in (rq_ref, rk_ref):
        rotated(ref, RET_W, rc, rsn, RET_DIM // 2, r_first, off)
        off += RET_W
    rv_ref[0] = proj(off, RET_W).astype(BF16)
    off += RET_W
    rg_ref[0] = proj(off, 2 * RET_W).astype(BF16)
    off += 2 * RET_W
    u_ref[0] = proj(off, POOL_WIDTH)
    off += POOL_WIDTH
    d = x.shape[-1]
    for t in range(3):
        g_ref[0, :, t * d:(t + 1) * d] = proj(off, d).astype(BF16)
        off += d


def _inproj(x, ctx, mod_l, norm1_l, w1, tables):
    b, _, d = x.shape
    s = x.shape[1] + (0 if ctx is None else ctx.shape[1])
    ti = IN_TILE if s % IN_TILE == 0 else TM
    n_sub = ti // AB
    n_ctx = 0 if ctx is None else CTX_BLOCKS
    pieces = [pl.BlockSpec((1, AB, d), lambda bi, j, n=n: (bi, jnp.maximum(n_sub * j + n - n_ctx, 0), 0))
              for n in range(n_sub)]
    ctx_pieces = [pl.BlockSpec((1, AB, d), lambda bi, j, n=n: (bi, n, 0)) for n in range(n_ctx)]
    wcols = w1.shape[1]
    tok = lambda width: pl.BlockSpec((1, ti, width), lambda bi, j: (bi, j, 0))
    tab = pl.BlockSpec((ti, LANE), lambda bi, j: (j, 0))
    widths = (ATT_Q, KV2, KV2, RET_W, RET_W, RET_W, 2 * RET_W, POOL_WIDTH, 3 * d)
    dtypes = (BF16,) * 7 + (F32, BF16)
    return pl.pallas_call(
        functools.partial(_inproj_kernel, n_sub=n_sub, n_ctx=n_ctx),
        grid=(b, s // ti),
        in_specs=pieces + ctx_pieces + [
            pl.BlockSpec((1, 6, d), lambda bi, j: (bi, 0, 0)),
            pl.BlockSpec((1, 6, d), lambda bi, j: (b, 0, 0)),
            pl.BlockSpec((1, d), lambda bi, j: (0, 0)),
            pl.BlockSpec((d, wcols), lambda bi, j: (0, 0), pipeline_mode=pl.Buffered(1)),
            tab, tab, tab, tab,
        ],
        out_specs=[tok(w) for w in widths],
        out_shape=[jax.ShapeDtypeStruct((b, s, w), dt) for w, dt in zip(widths, dtypes)],
        compiler_params=_params("arbitrary", "arbitrary"),
    )(*([x] * n_sub), *([ctx] * n_ctx), mod_l, mod_l, norm1_l.reshape(1, d), w1, *tables)


def _rope_tables(seq):
    rows = seq // GRID_W
    rpos = jnp.arange(rows, dtype=F32)[:, None]
    cpos = jnp.arange(GRID_W, dtype=F32)[:, None]
    lane = jnp.arange(LANE)
    grid = lambda per_row, per_col: (per_row[:, None, :] + per_col[None, :, :]).reshape(seq, LANE)
    axis_dim = HEAD_DIM // 2
    inv_a = ROPE_THETA ** (-jnp.arange(0, axis_dim, 2, dtype=F32) / axis_dim)
    hl = lane % HEAD_DIM
    inv_al = inv_a[(hl % axis_dim) % (axis_dim // 2)][None, :]
    by_row = (hl < axis_dim)[None, :]
    a_sign = jnp.where((hl % axis_dim) < axis_dim // 2, -1.0, 1.0)[None, :]
    a_cos = grid(jnp.where(by_row, jnp.cos(rpos * inv_al), 0.0), jnp.where(by_row, 0.0, jnp.cos(cpos * inv_al)))
    a_sin = grid(jnp.where(by_row, jnp.sin(rpos * inv_al), 0.0), jnp.where(by_row, 0.0, jnp.sin(cpos * inv_al)))
    inv_r = 1.0 / (ROPE_THETA ** jnp.linspace(0.0, 1.0, RET_DIM // 2, dtype=F32))
    rl = lane % RET_DIM
    inv_rl = inv_r[rl % (RET_DIM // 2)][None, :]
    r_sign = jnp.where(rl < RET_DIM // 2, -1.0, 1.0)[None, :]
    hi = (rpos * GRID_W) * inv_rl
    lo = cpos * inv_rl
    outer = lambda a, b: (a[:, None, :] * b[None, :, :]).reshape(seq, LANE)
    r_cos = outer(jnp.cos(hi), jnp.cos(lo)) - outer(jnp.sin(hi), jnp.sin(lo))
    r_sin = outer(jnp.sin(hi), jnp.cos(lo)) + outer(jnp.cos(hi), jnp.sin(lo))
    ones = jnp.ones((TM, LANE), F32)
    zeros = jnp.zeros((TM, LANE), F32)
    cat = lambda head, body: jnp.concatenate([head, body], axis=0)
    return (cat(ones, a_cos), cat(zeros, a_sin * a_sign), cat(ones, r_cos), cat(zeros, r_sin * r_sign))


def _attn_kernel(sink_ref, bias0_ref, bias1_ref, q_ref, k0_ref, k1_ref, k2_ref, k3_ref, kx_ref,
                 v0_ref, v1_ref, v2_ref, v3_ref, vx_ref, o_ref):
    g_heads = N_Q_HEADS // N_KV_HEADS
    lane = lax.broadcasted_iota(I32, (AB, LANE), 1)
    lo = lane < HEAD_DIM
    rows = lax.broadcasted_iota(I32, (g_heads * AB, 1), 0)
    k_refs = (k0_ref, k1_ref, k2_ref, k3_ref)
    v_refs = (v0_ref, v1_ref, v2_ref, v3_ref)
    biases = (bias0_ref[0], bias1_ref[0])
    chains = [(t, g) for t in range(Q_BLOCKS) for g in range(N_KV_HEADS)]
    scores, sinks, values = [], [], []
    for t, g in chains:
        ks = slice(g * LANE, (g + 1) * LANE)
        kd = jnp.concatenate([r[0, :, ks] for r in k_refs[t:t + 3]] + [kx_ref[0, :, ks]], axis=0)
        values.append(jnp.concatenate([r[0, :, ks] for r in v_refs[t:t + 3]] + [vx_ref[0, :, ks]], axis=0))
        qs = []
        for c in range(2 * g, 2 * g + 2):
            qc = q_ref[0, t * AB:(t + 1) * AB, c * LANE:(c + 1) * LANE]
            zero = jnp.zeros_like(qc)
            qs += [jnp.where(lo, qc, zero), jnp.where(lo, zero, qc)]
        scores.append(_dot_nt(jnp.concatenate(qs, axis=0), kd) + biases[t])
        sink = jnp.full((g_heads * AB, 1), sink_ref[g_heads * g + g_heads - 1], F32)
        for h in range(g_heads - 2, -1, -1):
            sink = jnp.where(rows < (h + 1) * AB, sink_ref[g_heads * g + h], sink)
        sinks.append(sink)
    for (t, g), s, sink, vd in zip(chains, scores, sinks, values):
        m = jnp.maximum(jnp.max(s, axis=-1, keepdims=True), sink)
        p = jnp.exp(s - m)
        den = jnp.sum(p, axis=-1, keepdims=True) + jnp.exp(sink - m)
        o = _dot(p.astype(BF16), vd) / den
        for h in range(2):
            c = 2 * g + h
            o_ref[0, t * AB:(t + 1) * AB, c * LANE:(c + 1) * LANE] = jnp.where(
                lo, o[2 * h * AB:(2 * h + 1) * AB], o[(2 * h + 1) * AB:(2 * h + 2) * AB]).astype(BF16)


def _attn_bias():
    g_heads = N_Q_HEADS // N_KV_HEADS
    r = jnp.arange(g_heads * AB)[:, None] % AB
    j = jnp.arange(3 * AB + TM)[None, :]
    band = jnp.abs(j - AB - r) <= WINDOW
    is_ctx = j >= 3 * AB
    variants = (band, band & (j >= AB), band & (j < 2 * AB), jnp.zeros_like(band))
    return jnp.stack([jnp.where(v | is_ctx, 0.0, NEG) for v in variants]).astype(F32)


def _attention(q, k2, v2, sinks):
    b, s, _ = q.shape
    nb = s // AB
    assert nb - CTX_BLOCKS >= 2 and nb % Q_BLOCKS == 0 and CTX_BLOCKS % Q_BLOCKS == 0
    bias = _attn_bias()

    def variant(blk):
        return jnp.where(blk < CTX_BLOCKS, 3, jnp.where(blk == CTX_BLOCKS, 1, jnp.where(blk == nb - 1, 2, 0)))

    def kv(offset):
        return pl.BlockSpec((1, AB, KV2), lambda bi, i: (bi, jnp.clip(Q_BLOCKS * i + offset, 0, nb - 1), 0))

    def bias_spec(t):
        return pl.BlockSpec((1,) + bias.shape[1:], lambda bi, i: (variant(Q_BLOCKS * i + t), 0, 0))

    cx = pl.BlockSpec((1, TM, KV2), lambda bi, i: (bi, 0, 0))
    kvs = [kv(o) for o in range(-1, Q_BLOCKS + 1)] + [cx]
    qo = pl.BlockSpec((1, Q_BLOCKS * AB, ATT_Q), lambda bi, i: (bi, i, 0))
    return pl.pallas_call(
        _attn_kernel,
        grid=(b, nb // Q_BLOCKS),
        in_specs=[pl.BlockSpec(memory_space=pltpu.SMEM), bias_spec(0), bias_spec(1), qo] + kvs + kvs,
        out_specs=qo,
        out_shape=jax.ShapeDtypeStruct((b, s, ATT_Q), BF16),
        compiler_params=_params("arbitrary", "arbitrary"),
    )(sinks, bias, bias, q, *([k2] * len(kvs)), *([v2] * len(kvs)))


def _ret_kernel(lg_ref, qf_ref, kf_ref, vf_ref, qb_ref, kb_ref, vb_ref, yf_ref, yb_ref,
                st_ref, dm_ref, qd_ref, kd_ref, cd_ref, *, batch):
    step = pl.program_id(0)
    lane = lax.broadcasted_iota(I32, (AB, LANE), 1)
    row = lax.broadcasted_iota(I32, (AB, LANE), 0)
    lo = lane < RET_DIM
    tiles = RET_W // LANE

    @pl.when(step == 0)
    def _():
        st_ref[...] = jnp.zeros_like(st_ref)
        ii = row.astype(F32)
        jj = lane.astype(F32)
        for d in range(2):
            for c in range(tiles):
                lg0 = lg_ref[d * RET_HEADS + 2 * c]
                lg1 = lg_ref[d * RET_HEADS + 2 * c + 1]
                lgl = jnp.where(lo, lg0, lg1)
                q_exp = ii + 1.0 if d == 0 else AB - ii
                k_exp = (AB - 1.0) - ii if d == 0 else ii
                qd_ref[d * tiles + c] = jnp.exp(q_exp * lgl)
                kd_ref[d * tiles + c] = jnp.exp(k_exp * lgl)
                cd_ref[d * tiles + c] = jnp.exp(AB * lgl)
                rel = ii - jj if d == 0 else jj - ii
                for hh, lgh in enumerate((lg0, lg1)):
                    dm_ref[d * tiles + c, hh * AB:(hh + 1) * AB, :] = jnp.where(
                        rel >= 0, jnp.exp(jnp.maximum(rel, 0.0) * lgh), 0.0)

    same_head = (row < RET_DIM) == lo
    dirs = ((qf_ref, kf_ref, vf_ref, yf_ref), (qb_ref, kb_ref, vb_ref, yb_ref))
    chains = [(d, b, c) for d in range(2) for b in range(batch) for c in range(tiles)]

    def operands(d, b, c):
        q_ref, k_ref, v_ref, _ = dirs[d]
        sl = slice(c * LANE, (c + 1) * LANE)
        return q_ref[b, :, sl], k_ref[b, :, sl], v_ref[b, :, sl]

    probs = []
    for d, b, c in chains:
        q, k, _ = operands(d, b, c)
        zero = jnp.zeros_like(q)
        q2 = jnp.concatenate([jnp.where(lo, q, zero), jnp.where(lo, zero, q)], axis=0)
        probs.append((_dot_nt(q2, k) * dm_ref[d * tiles + c]).astype(BF16))
    for (d, b, c), p in zip(chains, probs):
        q, _, v = operands(d, b, c)
        t = d * tiles + c
        si = (d * batch + b) * tiles + c
        y_intra = jnp.where(lo, _dot(p[:AB], v), _dot(p[AB:], v))
        q_dec = (q.astype(F32) * qd_ref[t]).astype(BF16)
        dirs[d][3][b, :, c * LANE:(c + 1) * LANE] = (y_intra + _dot(q_dec, st_ref[si].astype(BF16))).astype(BF16)
    for d, b, c in chains:
        _, k, v = operands(d, b, c)
        t = d * tiles + c
        si = (d * batch + b) * tiles + c
        k_dec = (k.astype(F32) * kd_ref[t]).astype(BF16)
        st_ref[si] = st_ref[si] * cd_ref[t] + jnp.where(same_head, _dot_tn(k_dec, v), 0.0)


def _retention(rq, rk, rv, log_g):
    b, s, w = rq.shape
    nb = s // AB
    tiles = w // LANE

    def back(i):
        return jnp.where(i < CTX_BLOCKS, CTX_BLOCKS - 1 - i, nb - 1 + CTX_BLOCKS - i)

    fwd = pl.BlockSpec((b, AB, w), lambda i: (0, i, 0))
    bwd = pl.BlockSpec((b, AB, w), lambda i: (0, back(i), 0))
    return pl.pallas_call(
        functools.partial(_ret_kernel, batch=b),
        grid=(nb,),
        in_specs=[pl.BlockSpec(memory_space=pltpu.SMEM), fwd, fwd, fwd, bwd, bwd, bwd],
        out_specs=[fwd, bwd],
        out_shape=[jax.ShapeDtypeStruct((b, s, w), BF16)] * 2,
        scratch_shapes=[
            pltpu.VMEM((2 * b * tiles, LANE, LANE), F32),
            pltpu.VMEM((2 * tiles, 2 * AB, LANE), F32),
            pltpu.VMEM((2 * tiles, AB, LANE), F32),
            pltpu.VMEM((2 * tiles, AB, LANE), F32),
            pltpu.VMEM((2 * tiles, AB, LANE), F32),
        ],
        compiler_params=_params("arbitrary"),
    )(log_g.reshape(-1), rq, rk, rv, rq, rk, rv)


def _route_tile(logits, tri_ref, ls_ref, rw_ref, cnt_ref):
    ne = N_EXPERTS
    eidx = lax.broadcasted_iota(I32, (ne, TM), 0).astype(F32)
    tops, hots = [], []
    for _ in range(TOP_K):
        mx = jnp.max(logits, axis=0, keepdims=True)
        idx = jnp.min(jnp.where(logits == mx, eidx, float(ne)), axis=0, keepdims=True)
        hot = eidx == idx
        logits = jnp.where(hot, NEG * 2.0, logits)
        tops.append(mx)
        hots.append(hot)
    ex = [jnp.exp(mx - tops[0]) for mx in tops]
    tot = ex[0]
    for e in ex[1:]:
        tot = tot + e
    sel = jnp.zeros((ne, TM), F32)
    for hot in hots:
        sel = jnp.where(hot, 1.0, sel)
    incl = _dot(sel.astype(BF16), tri_ref[...])
    cnt = incl[:, TM - 1:TM]
    run = jnp.floor((cnt + (SUBLANE - 1.0)) * (1.0 / SUBLANE)) * SUBLANE
    ends = jnp.broadcast_to(run, (ne, LANE))
    erow = lax.broadcasted_iota(I32, (ne, LANE), 0)
    shift = 1
    while shift < ne:
        ends = ends + jnp.where(erow >= shift, pltpu.roll(ends, shift, 0), 0.0)
        shift *= 2
    slot = (ends[:, :1] - run) + incl - 1.0
    ls_rows = [jnp.sum(jnp.where(hot, slot, 0.0), axis=0, keepdims=True) for hot in hots]
    rw_rows = [e / tot for e in ex]
    lst = jnp.concatenate(ls_rows + [jnp.full((SUBLANE - TOP_K, TM), -1.0, F32)], axis=0)
    ls_ref[...] = jnp.transpose(
        jnp.concatenate([lst, jnp.full((LANE - SUBLANE, TM), -1.0, F32)], axis=0)).astype(I32)
    rw_ref[...] = jnp.transpose(jnp.concatenate(rw_rows + [jnp.zeros((LANE - TOP_K, TM), F32)], axis=0))
    cnt_ref[...] = jnp.broadcast_to(cnt, cnt_ref.shape).astype(I32)
    return lst


def _merge_kernel(*refs, seq, nt, ntt, n_ctx):
    _merge_body(refs[n_ctx], refs[:n_ctx], *refs[n_ctx + 1:], seq=seq, nt=nt, ntt=ntt)


def _merge_body(x_ref, c_refs, attn_ref, up_ref, uc_ref, un_ref, yf_ref, yb_ref, rg_ref, g_ref, mod_ref,
                  wbd_ref, ps_ref, woa_ref, wop_ref, wor_ref, wout_ref, n2_ref, wr_ref,
                  br_ref, avg_ref, tri_ref,
                  xo_ref, xs_ref, ls_ref, rw_ref, cnt_ref, lg_ref, h2_ref, *, seq, nt, ntt):
    i = pl.program_id(0)
    j = jnp.minimum(i, ntt - 1) % nt

    @pl.when(i == 0)
    def _():
        lg_ref[...] = jnp.zeros_like(lg_ref)
        h2_ref[...] = jnp.zeros_like(h2_ref)

    lst = _route_tile(lg_ref[...], tri_ref, ls_ref, rw_ref, cnt_ref)

    ext = TM + 2 * POOL_HALO
    seq_len = jnp.where(j == 0, TM, seq)
    start = jnp.where(j == 0, 0, (j - 1) * TM)
    u = uc_ref[0]
    ue = jnp.concatenate([up_ref[0], u, un_ref[0]], axis=0)
    erow = lax.broadcasted_iota(I32, (ext, 1), 0) + (start - POOL_HALO)
    ue = jnp.where((erow >= 0) & (erow < seq_len), ue, 0.0)
    t_pos = lax.broadcasted_iota(I32, (TM, 1), 0) + start
    glane = lax.broadcasted_iota(I32, (TM, POOL_WIDTH), 1) // (POOL_WIDTH // len(POOL_WINDOWS))
    run = ue
    width = 1
    diff = jnp.zeros((TM, POOL_WIDTH), F32)
    for gi, w in enumerate(POOL_WINDOWS):
        while width < w:
            run = run + pltpu.roll(run, ext - width, 0)
            width *= 2
        win = pltpu.roll(run, w // 2, 0)[POOL_HALO:POOL_HALO + TM]
        cnt = jnp.minimum(t_pos - w // 2 + w, seq_len) - jnp.maximum(t_pos - w // 2, 0)
        diff = jnp.where(glane == gi, win / cnt.astype(F32) - u, diff)
    pool = _dot(diff.astype(BF16), wbd_ref[...]) * ps_ref[...]

    def head_norm(y):
        dlt = y.astype(F32) - _dot(y, avg_ref[...])
        var = _dot_hilo(dlt * dlt, avg_ref[...])
        return dlt * lax.rsqrt(var + GN_EPS)

    rg = rg_ref[0].astype(F32)
    ret = head_norm(yf_ref[0]) * _silu(rg[:, :RET_W]) + head_norm(yb_ref[0]) * _silu(rg[:, RET_W:])

    slot = lax.broadcasted_iota(I32, (SORT_ROWS, TM), 0).astype(F32)
    p = jnp.zeros((SORT_ROWS, TM), F32)
    for k in range(TOP_K):
        p = jnp.where(slot == lst[k:k + 1, :], 1.0, p)
    xs_ref[...] = _dot(p.astype(BF16), h2_ref[...])

    d = x_ref.shape[-1]
    gate = lambda t: _sigmoid(g_ref[0, :, t * d:(t + 1) * d])
    m = (gate(0) * _dot(attn_ref[0], woa_ref[...]).astype(BF16)
         + gate(1) * _dot(pool.astype(BF16), wop_ref[...]).astype(BF16)
         + gate(2) * _dot(ret.astype(BF16), wor_ref[...]).astype(BF16))
    xn = _merged_rows((x_ref,), c_refs, j == 0) + mod_ref[0, 2:3, :] * _dot(m, wout_ref[...])
    xo_ref[0] = xn

    ms = jnp.mean(xn * xn, axis=-1, keepdims=True)
    h2 = xn * lax.rsqrt(ms + NORM_EPS) * n2_ref[...] * (1.0 + mod_ref[0, 4:5, :]) + mod_ref[0, 3:4, :]
    h2_ref[...] = h2.astype(BF16)
    hh, hl = _split(h2)
    ne = N_EXPERTS
    full = _dot_nt(wr_ref[...], hh)
    logits = full[:ne] + full[ne:] + _dot_nt(wr_ref[:ne], hl) + br_ref[...]
    lg_ref[...] = logits


def _merge(x, ctx, attn, u, yf, yb, rg, gates, mod_l, seq, wts):
    b, s, _ = attn.shape
    d = x.shape[-1]
    nt = s // TM
    ntt = b * nt
    hb = TM // POOL_HALO
    nh = s // POOL_HALO
    cur = lambda i: jnp.minimum(i, ntt - 1)
    lag = lambda i: jnp.maximum(i - 1, 0)
    tok = lambda width: pl.BlockSpec((1, TM, width), lambda i: (cur(i) // nt, cur(i) % nt, 0))
    full = lambda a: pl.BlockSpec(a.shape, lambda i: (0,) * a.ndim)
    lanes = pl.BlockSpec((TM, LANE), lambda i: (lag(i), 0))
    if ctx is None:
        stream = [tok(d)]
    else:
        stream = [pl.BlockSpec((1, TM, d), lambda i: (cur(i) // nt, 0, 0)),
                  pl.BlockSpec((1, TM, d), lambda i: (cur(i) // nt, jnp.maximum(cur(i) % nt - 1, 0), 0))]
    in_specs = stream + [
        tok(ATT_Q),
        pl.BlockSpec((1, POOL_HALO, POOL_WIDTH),
                     lambda i: (cur(i) // nt, jnp.maximum((cur(i) % nt) * hb - 1, 0), 0)),
        tok(POOL_WIDTH),
        pl.BlockSpec((1, POOL_HALO, POOL_WIDTH),
                     lambda i: (cur(i) // nt, jnp.minimum((cur(i) % nt + 1) * hb, nh - 1), 0)),
        tok(RET_W), tok(RET_W), tok(2 * RET_W), tok(3 * d),
        pl.BlockSpec((1, 6, d), lambda i: (jnp.where(cur(i) % nt == 0, b, cur(i) // nt), 0, 0)),
    ] + [full(a) for a in wts]
    return pl.pallas_call(
        functools.partial(_merge_kernel, seq=seq, nt=nt, ntt=ntt, n_ctx=len(stream) - 1),
        grid=(ntt + 1,),
        in_specs=in_specs,
        out_specs=[tok(d), pl.BlockSpec((SORT_ROWS, d), lambda i: (lag(i), 0)), lanes, lanes,
                   pl.BlockSpec((N_EXPERTS, LANE), lambda i: (lag(i), 0))],
        out_shape=[jax.ShapeDtypeStruct((b, s, d), F32), jax.ShapeDtypeStruct((ntt * SORT_ROWS, d), F32),
                   jax.ShapeDtypeStruct((b * s, LANE), I32),
                   jax.ShapeDtypeStruct((b * s, LANE), F32), jax.ShapeDtypeStruct((ntt * N_EXPERTS, LANE), I32)],
        scratch_shapes=[pltpu.VMEM((N_EXPERTS, TM), F32), pltpu.VMEM((TM, d), BF16)],
        compiler_params=_params("arbitrary"),
    )(*([] if ctx is None else [ctx]), x, attn, u, u, u, yf, yb, rg, gates, mod_l, *wts)


def _rows(tile_index):
    return pl.ds(pl.multiple_of(tile_index * SUBLANE, SUBLANE), SUBLANE)


def _tile_loop(count_ref, tile, fn):
    def body(j, carry):
        fn(j)
        return carry

    lax.fori_loop(0, MIN_TILES, body, 0, unroll=8)
    lax.fori_loop(MIN_TILES, count_ref[tile], body, 0)


def _tile_copies(table_ref, count_ref, tile, copy):
    _tile_loop(count_ref, tile, lambda j: copy(j, table_ref[tile * SORT_TILES + j]))


def _expert_kernel(be_ref, nu_ref, src_ref, valid_ref, xs_hbm, w1_ref, b1_ref, w2_ref, b2_ref, y_ref, w1b, w2b, xbuf, sem):
    bi = pl.program_id(0)
    used = bi < nu_ref[0]
    fresh = (bi == 0) | (be_ref[bi] != be_ref[jnp.maximum(bi - 1, 0)])
    cur = bi % 2
    block_tiles = EB // SUBLANE

    def each_tile(fn):
        def body(t, carry):
            fn(t)
            return carry
        lax.fori_loop(0, block_tiles, body, 0, unroll=8)

    def fetch(blk, half):
        each_tile(lambda t: pltpu.make_async_copy(
            xs_hbm.at[_rows(src_ref[blk * block_tiles + t])], xbuf.at[half, _rows(t)], sem.at[half]).start())

    @pl.when(bi == 0)
    def _():
        fetch(0, 0)

    @pl.when(bi + 1 < nu_ref[0])
    def _():
        fetch(bi + 1, 1 - cur)

    @pl.when(used & fresh)
    def _():
        w1b[...] = w1_ref[0, 0].astype(BF16)
        w2b[...] = w2_ref[0, 0].astype(BF16)

    @pl.when(jnp.logical_not(used))
    def _():
        y_ref[...] = jnp.zeros_like(y_ref)

    @pl.when(used)
    def _():
        each_tile(lambda t: pltpu.make_async_copy(
            xs_hbm.at[_rows(t)], xbuf.at[cur, _rows(t)], sem.at[cur]).wait())

    def compute(rows):
        xb = xbuf[cur, :rows].astype(BF16)
        y = None
        for c in range(0, D_FF, FF_CHUNK):
            glu = _dot(xb, w1b[:, c:c + FF_CHUNK]) + b1_ref[0, 0, :, c:c + FF_CHUNK]
            lin = _dot(xb, w1b[:, D_FF + c:D_FF + c + FF_CHUNK]) + b1_ref[0, 0, :, D_FF + c:D_FF + c + FF_CHUNK]
            glu = jnp.minimum(glu, SWIGLU_LIMIT)
            lin = jnp.clip(lin, -SWIGLU_LIMIT, SWIGLU_LIMIT)
            act = glu * _sigmoid(SWIGLU_ALPHA * glu) * (lin + 1.0)
            part = _dot(act.astype(BF16), w2b[c:c + FF_CHUNK, :])
            y = part if y is None else y + part
        y_ref[:rows] = y + b2_ref[0, 0]
        if rows < EB:
            y_ref[rows:] = jnp.zeros((EB - rows, y_ref.shape[1]), F32)

    n_valid = valid_ref[bi]
    for rows in range(EB_STEP, EB + 1, EB_STEP):
        pl.when(used & (n_valid > rows - EB_STEP) & (n_valid <= rows))(functools.partial(compute, rows))


def _experts(xs, block_e, n_used, src, valid, w1, b1, w2, b2, layer, n_blocks):
    d = xs.shape[1]
    depth, ne, _, f2 = w1.shape
    grid_spec = pltpu.PrefetchScalarGridSpec(
        num_scalar_prefetch=4,
        grid=(n_blocks,),
        in_specs=[
            pl.BlockSpec(memory_space=pl.ANY),
            pl.BlockSpec((1, 1, d, f2), lambda bi, be, *_: (layer, be[bi], 0, 0)),
            pl.BlockSpec((1, 1, 1, f2), lambda bi, be, *_: (layer, be[bi], 0, 0)),
            pl.BlockSpec((1, 1, f2 // 2, d), lambda bi, be, *_: (layer, be[bi], 0, 0)),
            pl.BlockSpec((1, 1, 1, d), lambda bi, be, *_: (layer, be[bi], 0, 0)),
        ],
        out_specs=pl.BlockSpec((EB, d), lambda bi, *_: (bi, 0)),
        scratch_shapes=[pltpu.VMEM((d, f2), BF16), pltpu.VMEM((f2 // 2, d), BF16),
                        pltpu.VMEM((2, EB, d), F32), pltpu.SemaphoreType.DMA((2,))],
    )
    return pl.pallas_call(
        _expert_kernel,
        grid_spec=grid_spec,
        out_shape=jax.ShapeDtypeStruct((n_blocks * EB, d), F32),
        compiler_params=_params("arbitrary"),
    )(block_e, n_used, src, valid, xs, w1, b1.reshape(depth, ne, 1, f2), w2, b2.reshape(depth, ne, 1, d))


def _combine_kernel(dst_ref, cnt_ref, x_ref, ls_ref, rw_ref, mod_ref, fg_ref, ys_hbm, xo_ref, buf, sem, *, final):
    i = pl.program_id(0)
    last = pl.num_programs(0) - 1
    cur = i % 2

    def fetch(tile, half):
        _tile_copies(dst_ref, cnt_ref, tile, lambda j, t: pltpu.make_async_copy(
            ys_hbm.at[_rows(t)], buf.at[half, _rows(j)], sem.at[half]).start())

    @pl.when(i == 0)
    def _():
        buf[...] = jnp.zeros_like(buf)
        fetch(0, 0)

    @pl.when(i < last)
    def _():
        fetch(i + 1, 1 - cur)

    _tile_loop(cnt_ref, i, lambda j: pltpu.make_async_copy(
        ys_hbm.at[_rows(j)], buf.at[cur, _rows(j)], sem.at[cur]).wait())
    yb = buf[cur].astype(BF16)
    slot = lax.broadcasted_iota(I32, (TM, SORT_ROWS), 1)
    ls = ls_ref[...]
    rw = rw_ref[...]
    g = jnp.zeros((TM, SORT_ROWS), F32)
    for k in range(TOP_K):
        g = jnp.where(slot == ls[:, k:k + 1], rw[:, k:k + 1], g)
    xn = x_ref[0] + mod_ref[0, 5:6, :] * _dot(g.astype(BF16), yb)
    if final:
        ms = jnp.mean(xn * xn, axis=-1, keepdims=True)
        xn = xn * lax.rsqrt(ms + NORM_EPS) * fg_ref[...]
    xo_ref[0] = xn


def _combine(xn, ls, rw, mod_l, final_gain, ys, tabs, final):
    b, s, d = xn.shape
    nt = s // TM
    tok = pl.BlockSpec((1, TM, d), lambda i, *_: (i // nt, i % nt, 0))
    lanes = pl.BlockSpec((TM, LANE), lambda i, *_: (i, 0))
    if final:
        out_spec = pl.BlockSpec((1, TM, d), lambda i, *_: (i // nt, jnp.maximum(i % nt - 1, 0), 0))
        out_shape = jax.ShapeDtypeStruct((b, s - TM, d), F32)
    else:
        out_spec, out_shape = tok, jax.ShapeDtypeStruct((b, s, d), F32)
    grid_spec = pltpu.PrefetchScalarGridSpec(
        num_scalar_prefetch=2,
        grid=(b * nt,),
        in_specs=[tok, lanes, lanes,
                  pl.BlockSpec((1, 6, d), lambda i, *_: (jnp.where(i % nt == 0, b, i // nt), 0, 0)),
                  pl.BlockSpec((1, d), lambda i, *_: (0, 0)),
                  pl.BlockSpec(memory_space=pl.ANY)],
        out_specs=out_spec,
        scratch_shapes=[pltpu.VMEM((2, SORT_ROWS, d), F32), pltpu.SemaphoreType.DMA((2,))],
    )
    return pl.pallas_call(
        functools.partial(_combine_kernel, final=final),
        grid_spec=grid_spec,
        out_shape=out_shape,
        compiler_params=_params("arbitrary"),
    )(tabs["dst"], tabs["count"], xn, ls, rw, mod_l, final_gain.reshape(1, d), ys)


def _inproj_weight(w_in_l):
    d = w_in_l.shape[0]
    sizes = (ATT_Q, N_KV_HEADS * HEAD_DIM, N_KV_HEADS * HEAD_DIM, POOL_WIDTH,
             RET_W, RET_W, RET_W, RET_W, RET_W, d, d, d)
    parts, off = [], 0
    for sz in sizes:
        parts.append(w_in_l[:, off:off + sz])
        off += sz
    q, k, v, u, rq, rk, rv, rgf, rgb, ga, gp, gr = parts
    twice = lambda w: jnp.concatenate(
        [w[:, h * HEAD_DIM:(h + 1) * HEAD_DIM] for h in range(N_KV_HEADS) for _ in range(2)], axis=1)
    cols = [q * HEAD_DIM ** -0.5, twice(k), twice(v), rq, rk * RET_DIM ** -0.5, rv, rgf, rgb, u, ga, gp, gr]
    return jnp.concatenate(cols, axis=1).astype(BF16)


def _block_diag(blocks):
    n, r, c = blocks.shape
    out = jnp.zeros((n * r, n * c), blocks.dtype)
    for g in range(n):
        out = out.at[g * r:(g + 1) * r, g * c:(g + 1) * c].set(blocks[g])
    return out


def _routing_tables(cnt, n_blocks):
    ntt = cnt.shape[0] // N_EXPERTS
    counts = cnt.reshape(ntt, N_EXPERTS, LANE)[:, :, 0]
    run = (counts + SUBLANE - 1) // SUBLANE
    total = jnp.sum(run, axis=0)
    eb = EB // SUBLANE
    padded = (total + eb - 1) // eb * eb
    pad_end = jnp.cumsum(padded)
    pad_start = pad_end - padded
    off = pad_start[None, :] + jnp.cumsum(run, axis=0) - run
    run_end = jnp.cumsum(run, axis=1)
    j = jnp.arange(SORT_TILES)
    owner = jnp.sum(run_end[:, None, :] <= j[None, :, None], axis=2)
    mine = owner[:, :, None] == jnp.arange(N_EXPERTS)[None, None, :]
    in_region = j[None, :] + jnp.sum(jnp.where(mine, (off - (run_end - run))[:, None, :], 0), axis=2)
    dst = jnp.where(owner < N_EXPERTS, in_region, 0)
    n_used = pad_end[-1] // eb
    blk = jnp.minimum(jnp.arange(n_blocks), n_used - 1) * eb
    block_e = jnp.minimum(jnp.sum(pad_end[None, :] <= blk[:, None], axis=1), N_EXPERTS - 1)
    mine_b = block_e[:, None] == jnp.arange(N_EXPERTS)[None, :]
    of_block = lambda table: jnp.sum(jnp.where(mine_b[:, None, :], table[None, :, :], 0), axis=2)
    first = of_block(jnp.cumsum(run, axis=0) - run)
    length = of_block(run)
    local = of_block(run_end - run)
    start_b = jnp.sum(jnp.where(mine_b, pad_start[None, :], 0), axis=1)
    total_b = jnp.sum(jnp.where(mine_b, total[None, :], 0), axis=1)
    valid = jnp.clip(total_b - (jnp.arange(n_blocks) * eb - start_b), 0, eb) * SUBLANE
    pos = (jnp.arange(n_blocks) * eb - start_b)[:, None] + jnp.arange(eb)[None, :]
    inside = (pos[:, :, None] >= first[:, None, :]) & (pos[:, :, None] < (first + length)[:, None, :])
    where = (jnp.arange(ntt) * SORT_TILES)[None, None, :] + (local - first)[:, None, :] + pos[:, :, None]
    zero_tile = SORT_TILES - 1
    src = jnp.where(jnp.any(inside, axis=2), jnp.sum(jnp.where(inside, where, 0), axis=2), zero_tile)
    tabs = dict(dst=dst.reshape(-1).astype(I32), count=run_end[:, -1].astype(I32),
                src=src.reshape(-1).astype(I32), valid=valid.astype(I32))
    return tabs, block_e.astype(I32), n_used.reshape(1).astype(I32)


def kernel(x, c, ctx, c_ctx, w_mod, b_mod, norm1, norm2, w_in, attn_sinks, pool_w, pool_scale, ret_decay,
           w_o_attn, w_o_pool, w_o_ret, w_out, w_router, b_router, w_expert_in, b_expert_in, w_expert_out,
           b_expert_out, final_norm):
    b, seq, d = x.shape
    depth = w_mod.shape[0]
    assert ctx.shape[1] == TM and seq % TM == 0 and seq % GRID_W == 0 and b + 1 <= SUBLANE
    s = seq + TM
    xm = x
    mod = _modulation(c, c_ctx, w_mod, b_mod)
    tables = _rope_tables(seq)
    ntt = b * s // TM
    n_blocks = -(-(b * s * TOP_K + ntt * N_EXPERTS * (SUBLANE - 1)) // EB) + N_EXPERTS
    avg = _block_diag(jnp.full((RET_HEADS, RET_DIM, RET_DIM), 1.0 / RET_DIM, F32)).astype(BF16)
    tri = (jnp.arange(TM)[:, None] <= jnp.arange(TM)[None, :]).astype(BF16)
    for l in range(depth):
        first_ctx = ctx if l == 0 else None
        q, k2, v2, rq, rk, rv, rg, u, gates = _inproj(xm, first_ctx, mod[l], norm1[l], _inproj_weight(w_in[l]), tables)
        attn = _attention(q, k2, v2, attn_sinks[l])
        yf, yb = _retention(rq, rk, rv, jax.nn.log_sigmoid(ret_decay[l].astype(F32)))
        wr_hilo = jnp.concatenate(_split(w_router[l].T), axis=0)
        br = jnp.broadcast_to(b_router[l][:, None], (N_EXPERTS, TM))
        wts = (_block_diag(pool_w[l]).astype(BF16), pool_scale[l].reshape(1, -1),
               w_o_attn[l].astype(BF16), w_o_pool[l].astype(BF16), w_o_ret[l].astype(BF16),
               w_out[l].astype(BF16), norm2[l].reshape(1, d), wr_hilo, br, avg, tri)
        xn, xs, ls, rw, cnt = _merge(xm, first_ctx, attn, u, yf, yb, rg, gates, mod[l], seq, wts)
        tabs, block_e, n_used = _routing_tables(cnt, n_blocks)
        ys = _experts(xs, block_e, n_used, tabs["src"], tabs["valid"], w_expert_in, b_expert_in, w_expert_out,
                      b_expert_out, l, n_blocks)
        xm = _combine(xn, ls, rw, mod[l], final_norm, ys, tabs, final=l == depth - 1)
    return xm
```

```python
import functools

import jax
import jax.numpy as jnp
from jax import lax
from jax.experimental import pallas as pl
from jax.experimental.pallas import tpu as pltpu

F32 = jnp.float32
BF16 = jnp.bfloat16
I32 = jnp.int32

GRID_W = 64
HEAD_DIM = 64
N_Q_HEADS = 8
N_KV_HEADS = 2
WINDOW = 128
ROPE_THETA = 10000.0
POOL_WIDTH = 256
POOL_WINDOWS = (2, 4, 8, 16)
RET_HEADS = 4
RET_DIM = 64
N_EXPERTS = 32
TOP_K = 4
D_FF = 1024
SWIGLU_LIMIT = 7.0
SWIGLU_ALPHA = 1.702
NORM_EPS = 1e-6
GN_EPS = 1e-5

LANE = 128
SUBLANE = 8
MXU_N = 256
TM = 256
AB = 128
IN_TILE = 640
CTX_BLOCKS = TM // AB
Q_BLOCKS = 2
EB = 512
FF_CHUNK = 512
POOL_HALO = SUBLANE
SORT_ROWS = TM * TOP_K + N_EXPERTS * SUBLANE
SORT_TILES = SORT_ROWS // SUBLANE
MIN_TILES = TM * TOP_K // SUBLANE
DMA_QUEUES = 2
NEG = -1e30
VMEM_LIMIT = 56 * 1024 * 1024

ATT_Q = N_Q_HEADS * HEAD_DIM
KV2 = 2 * N_KV_HEADS * HEAD_DIM
RET_W = RET_HEADS * RET_DIM


def _dot(a, b):
    return jnp.dot(a, b, preferred_element_type=F32)


def _dot_nt(a, b):
    return lax.dot_general(a, b, (((1,), (1,)), ((), ())), preferred_element_type=F32)


def _dot_tn(a, b):
    return lax.dot_general(a, b, (((0,), (0,)), ((), ())), preferred_element_type=F32)


def _split(a):
    hi = a.astype(BF16)
    lo = (a - hi.astype(F32)).astype(BF16)
    return hi, lo


def _dot_hilo(a, m):
    hi, lo = _split(a)
    return _dot(hi, m) + _dot(lo, m)


def _sigmoid(x):
    return 0.5 * jnp.tanh(0.5 * x) + 0.5


def _silu(x):
    return x * _sigmoid(x)


def _params(*sem):
    return pltpu.CompilerParams(dimension_semantics=sem, vmem_limit_bytes=VMEM_LIMIT)


def _mod_kernel(c_ref, w_ref, b_ref, o_ref):
    s = _silu(c_ref[...])
    sh, sl = _split(s)
    wh, wl = _split(w_ref[0])
    o_ref[0] = _dot(sh, wh) + _dot(sh, wl) + _dot(sl, wh) + b_ref[0]


def _modulation(c, c_ctx, w_mod, b_mod):
    depth, d, six_d = w_mod.shape
    b = c.shape[0]
    cc = jnp.zeros((SUBLANE, d), F32).at[:b].set(c).at[b].set(c_ctx)
    out = pl.pallas_call(
        _mod_kernel,
        grid=(depth, six_d // d),
        in_specs=[
            pl.BlockSpec((SUBLANE, d), lambda l, n: (0, 0)),
            pl.BlockSpec((1, d, d), lambda l, n: (l, 0, n)),
            pl.BlockSpec((1, 1, d), lambda l, n: (l, 0, n)),
        ],
        out_specs=pl.BlockSpec((1, SUBLANE, d), lambda l, n: (l, 0, n)),
        out_shape=jax.ShapeDtypeStruct((depth, SUBLANE, six_d), F32),
        compiler_params=_params("arbitrary", "arbitrary"),
    )(cc, w_mod, b_mod.reshape(depth, 1, six_d))
    return out[:, : b + 1].reshape(depth, b + 1, 6, d)


def _rope(x, cos, sin, half, first):
    partner = jnp.where(first, pltpu.roll(x, LANE - half, 1), pltpu.roll(x, half, 1))
    return x * cos + partner * sin


def _merged_rows(x_refs, c_refs, first):
    parts = []
    for n, ref in enumerate(x_refs):
        part = ref[0]
        if n < len(c_refs):
            part = jnp.where(first, c_refs[n][0], part)
        parts.append(part)
    return parts[0] if len(parts) == 1 else jnp.concatenate(parts, axis=0)


def _inproj_kernel(*refs, n_sub, n_ctx):
    x_refs, c_refs, refs = refs[:n_sub], refs[n_sub:n_sub + n_ctx], refs[n_sub + n_ctx:]
    mod_ref, modc_ref, n1_ref, w_ref, ac_ref, as_ref, rc_ref, rs_ref = refs[:8]
    q_ref, k_ref, v_ref, rq_ref, rk_ref, rv_ref, rg_ref, u_ref, g_ref = refs[8:]
    x = _merged_rows(x_refs, c_refs, pl.program_id(1) == 0)
    rows = x.shape[0]
    ms = jnp.mean(x * x, axis=-1, keepdims=True)
    y = x * lax.rsqrt(ms + NORM_EPS) * n1_ref[...]
    is_ctx = (lax.broadcasted_iota(I32, (rows, 1), 0) < TM) & (pl.program_id(1) == 0)
    scale = jnp.where(is_ctx, modc_ref[0, 1:2, :], mod_ref[0, 1:2, :])
    shift = jnp.where(is_ctx, modc_ref[0, 0:1, :], mod_ref[0, 0:1, :])
    hb = (y * (1.0 + scale) + shift).astype(BF16)
    lane = lax.broadcasted_iota(I32, (rows, LANE), 1)
    a_first = (lane % (HEAD_DIM // 2)) < (HEAD_DIM // 4)
    r_first = (lane % RET_DIM) < (RET_DIM // 2)
    ac, asn, rc, rsn = ac_ref[...], as_ref[...], rc_ref[...], rs_ref[...]

    def proj(off, width):
        return _dot(hb, w_ref[:, off:off + width])

    def rotated(ref, width, cos, sin, half, first, off):
        for t in range(0, width, MXU_N):
            pr = proj(off + t, MXU_N)
            for g in range(0, MXU_N, LANE):
                ref[0, :, t + g:t + g + LANE] = _rope(pr[:, g:g + LANE], cos, sin, half, first).astype(BF16)

    off = 0
    rotated(q_ref, ATT_Q, ac, asn, HEAD_DIM // 4, a_first, off)
    off += ATT_Q
    rotated(k_ref, KV2, ac, asn, HEAD_DIM // 4, a_first, off)
    off += KV2
    v_ref[0] = proj(off, KV2).astype(BF16)
    off += KV2
    for ref in (rq_ref, rk_ref):
        rotated(ref, RET_W, rc, rsn, RET_DIM // 2, r_first, off)
        off += RET_W
    rv_ref[0] = proj(off, RET_W).astype(BF16)
    off += RET_W
    rg_ref[0] = proj(off, 2 * RET_W).astype(BF16)
    off += 2 * RET_W
    u_ref[0] = proj(off, POOL_WIDTH)
    off += POOL_WIDTH
    d = x.shape[-1]
    for t in range(3):
        g_ref[0, :, t * d:(t + 1) * d] = proj(off, d).astype(BF16)
        off += d


def _inproj(x, ctx, mod_l, norm1_l, w1, tables):
    b, _, d = x.shape
    s = x.shape[1] + (0 if ctx is None else ctx.shape[1])
    ti = IN_TILE if s % IN_TILE == 0 else TM
    n_sub = ti // AB
    n_ctx = 0 if ctx is None else CTX_BLOCKS
    pieces = [pl.BlockSpec((1, AB, d), lambda bi, j, n=n: (bi, jnp.maximum(n_sub * j + n - n_ctx, 0), 0))
              for n in range(n_sub)]
    ctx_pieces = [pl.BlockSpec((1, AB, d), lambda bi, j, n=n: (bi, n, 0)) for n in range(n_ctx)]
    wcols = w1.shape[1]
    tok = lambda width: pl.BlockSpec((1, ti, width), lambda bi, j: (bi, j, 0))
    tab = pl.BlockSpec((ti, LANE), lambda bi, j: (j, 0))
    widths = (ATT_Q, KV2, KV2, RET_W, RET_W, RET_W, 2 * RET_W, POOL_WIDTH, 3 * d)
    dtypes = (BF16,) * 7 + (F32, BF16)
    return pl.pallas_call(
        functools.partial(_inproj_kernel, n_sub=n_sub, n_ctx=n_ctx),
        grid=(b, s // ti),
        in_specs=pieces + ctx_pieces + [
            pl.BlockSpec((1, 6, d), lambda bi, j: (bi, 0, 0)),
            pl.BlockSpec((1, 6, d), lambda bi, j: (b, 0, 0)),
            pl.BlockSpec((1, d), lambda bi, j: (0, 0)),
            pl.BlockSpec((d, wcols), lambda bi, j: (0, 0), pipeline_mode=pl.Buffered(1)),
            tab, tab, tab, tab,
        ],
        out_specs=[tok(w) for w in widths],
        out_shape=[jax.ShapeDtypeStruct((b, s, w), dt) for w, dt in zip(widths, dtypes)],
        compiler_params=_params("arbitrary", "arbitrary"),
    )(*([x] * n_sub), *([ctx] * n_ctx), mod_l, mod_l, norm1_l.reshape(1, d), w1, *tables)


def _rope_tables(seq):
    rows = seq // GRID_W
    rpos = jnp.arange(rows, dtype=F32)[:, None]
    cpos = jnp.arange(GRID_W, dtype=F32)[:, None]
    lane = jnp.arange(LANE)
    grid = lambda per_row, per_col: (per_row[:, None, :] + per_col[None, :, :]).reshape(seq, LANE)
    axis_dim = HEAD_DIM // 2
    inv_a = ROPE_THETA ** (-jnp.arange(0, axis_dim, 2, dtype=F32) / axis_dim)
    hl = lane % HEAD_DIM
    inv_al = inv_a[(hl % axis_dim) % (axis_dim // 2)][None, :]
    by_row = (hl < axis_dim)[None, :]
    a_sign = jnp.where((hl % axis_dim) < axis_dim // 2, -1.0, 1.0)[None, :]
    a_cos = grid(jnp.where(by_row, jnp.cos(rpos * inv_al), 0.0), jnp.where(by_row, 0.0, jnp.cos(cpos * inv_al)))
    a_sin = grid(jnp.where(by_row, jnp.sin(rpos * inv_al), 0.0), jnp.where(by_row, 0.0, jnp.sin(cpos * inv_al)))
    inv_r = 1.0 / (ROPE_THETA ** jnp.linspace(0.0, 1.0, RET_DIM // 2, dtype=F32))
    rl = lane % RET_DIM
    inv_rl = inv_r[rl % (RET_DIM // 2)][None, :]
    r_sign = jnp.where(rl < RET_DIM // 2, -1.0, 1.0)[None, :]
    hi = (rpos * GRID_W) * inv_rl
    lo = cpos * inv_rl
    outer = lambda a, b: (a[:, None, :] * b[None, :, :]).reshape(seq, LANE)
    r_cos = outer(jnp.cos(hi), jnp.cos(lo)) - outer(jnp.sin(hi), jnp.sin(lo))
    r_sin = outer(jnp.sin(hi), jnp.cos(lo)) + outer(jnp.cos(hi), jnp.sin(lo))
    ones = jnp.ones((TM, LANE), F32)
    zeros = jnp.zeros((TM, LANE), F32)
    cat = lambda head, body: jnp.concatenate([head, body], axis=0)
    return (cat(ones, a_cos), cat(zeros, a_sin * a_sign), cat(ones, r_cos), cat(zeros, r_sin * r_sign))


def _attn_kernel(sink_ref, bias0_ref, bias1_ref, q_ref, k0_ref, k1_ref, k2_ref, k3_ref, kx_ref,
                 v0_ref, v1_ref, v2_ref, v3_ref, vx_ref, o_ref):
    g_heads = N_Q_HEADS // N_KV_HEADS
    lane = lax.broadcasted_iota(I32, (AB, LANE), 1)
    lo = lane < HEAD_DIM
    rows = lax.broadcasted_iota(I32, (g_heads * AB, 1), 0)
    k_refs = (k0_ref, k1_ref, k2_ref, k3_ref)
    v_refs = (v0_ref, v1_ref, v2_ref, v3_ref)
    biases = (bias0_ref[0], bias1_ref[0])
    chains = [(t, g) for t in range(Q_BLOCKS) for g in range(N_KV_HEADS)]
    scores, sinks, values = [], [], []
    for t, g in chains:
        ks = slice(g * LANE, (g + 1) * LANE)
        kd = jnp.concatenate([r[0, :, ks] for r in k_refs[t:t + 3]] + [kx_ref[0, :, ks]], axis=0)
        values.append(jnp.concatenate([r[0, :, ks] for r in v_refs[t:t + 3]] + [vx_ref[0, :, ks]], axis=0))
        qs = []
        for c in range(2 * g, 2 * g + 2):
            qc = q_ref[0, t * AB:(t + 1) * AB, c * LANE:(c + 1) * LANE]
            zero = jnp.zeros_like(qc)
            qs += [jnp.where(lo, qc, zero), jnp.where(lo, zero, qc)]
        scores.append(_dot_nt(jnp.concatenate(qs, axis=0), kd) + biases[t])
        sink = jnp.full((g_heads * AB, 1), sink_ref[g_heads * g + g_heads - 1], F32)
        for h in range(g_heads - 2, -1, -1):
            sink = jnp.where(rows < (h + 1) * AB, sink_ref[g_heads * g + h], sink)
        sinks.append(sink)
    for (t, g), s, sink, vd in zip(chains, scores, sinks, values):
        m = jnp.maximum(jnp.max(s, axis=-1, keepdims=True), sink)
        p = jnp.exp(s - m)
        den = jnp.sum(p, axis=-1, keepdims=True) + jnp.exp(sink - m)
        o = _dot(p.astype(BF16), vd) / den
        for h in range(2):
            c = 2 * g + h
            o_ref[0, t * AB:(t + 1) * AB, c * LANE:(c + 1) * LANE] = jnp.where(
                lo, o[2 * h * AB:(2 * h + 1) * AB], o[(2 * h + 1) * AB:(2 * h + 2) * AB]).astype(BF16)


def _attn_bias():
    g_heads = N_Q_HEADS // N_KV_HEADS
    r = jnp.arange(g_heads * AB)[:, None] % AB
    j = jnp.arange(3 * AB + TM)[None, :]
    band = jnp.abs(j - AB - r) <= WINDOW
    is_ctx = j >= 3 * AB
    variants = (band, band & (j >= AB), band & (j < 2 * AB), jnp.zeros_like(band))
    return jnp.stack([jnp.where(v | is_ctx, 0.0, NEG) for v in variants]).astype(F32)


def _attention(q, k2, v2, sinks):
    b, s, _ = q.shape
    nb = s // AB
    assert nb - CTX_BLOCKS >= 2 and nb % Q_BLOCKS == 0 and CTX_BLOCKS % Q_BLOCKS == 0
    bias = _attn_bias()

    def variant(blk):
        return jnp.where(blk < CTX_BLOCKS, 3, jnp.where(blk == CTX_BLOCKS, 1, jnp.where(blk == nb - 1, 2, 0)))

    def kv(offset):
        return pl.BlockSpec((1, AB, KV2), lambda bi, i: (bi, jnp.clip(Q_BLOCKS * i + offset, 0, nb - 1), 0))

    def bias_spec(t):
        return pl.BlockSpec((1,) + bias.shape[1:], lambda bi, i: (variant(Q_BLOCKS * i + t), 0, 0))

    cx = pl.BlockSpec((1, TM, KV2), lambda bi, i: (bi, 0, 0))
    kvs = [kv(o) for o in range(-1, Q_BLOCKS + 1)] + [cx]
    qo = pl.BlockSpec((1, Q_BLOCKS * AB, ATT_Q), lambda bi, i: (bi, i, 0))
    return pl.pallas_call(
        _attn_kernel,
        grid=(b, nb // Q_BLOCKS),
        in_specs=[pl.BlockSpec(memory_space=pltpu.SMEM), bias_spec(0), bias_spec(1), qo] + kvs + kvs,
        out_specs=qo,
        out_shape=jax.ShapeDtypeStruct((b, s, ATT_Q), BF16),
        compiler_params=_params("arbitrary", "arbitrary"),
    )(sinks, bias, bias, q, *([k2] * len(kvs)), *([v2] * len(kvs)))


def _ret_kernel(lg_ref, qf_ref, kf_ref, vf_ref, qb_ref, kb_ref, vb_ref, yf_ref, yb_ref,
                st_ref, dm_ref, qd_ref, kd_ref, cd_ref, *, batch):
    step = pl.program_id(0)
    lane = lax.broadcasted_iota(I32, (AB, LANE), 1)
    row = lax.broadcasted_iota(I32, (AB, LANE), 0)
    lo = lane < RET_DIM
    tiles = RET_W // LANE

    @pl.when(step == 0)
    def _():
        st_ref[...] = jnp.zeros_like(st_ref)
        ii = row.astype(F32)
        jj = lane.astype(F32)
        for d in range(2):
            for c in range(tiles):
                lg0 = lg_ref[d * RET_HEADS + 2 * c]
                lg1 = lg_ref[d * RET_HEADS + 2 * c + 1]
                lgl = jnp.where(lo, lg0, lg1)
                q_exp = ii + 1.0 if d == 0 else AB - ii
                k_exp = (AB - 1.0) - ii if d == 0 else ii
                qd_ref[d * tiles + c] = jnp.exp(q_exp * lgl)
                kd_ref[d * tiles + c] = jnp.exp(k_exp * lgl)
                cd_ref[d * tiles + c] = jnp.exp(AB * lgl)
                rel = ii - jj if d == 0 else jj - ii
                for hh, lgh in enumerate((lg0, lg1)):
                    dm_ref[d * tiles + c, hh * AB:(hh + 1) * AB, :] = jnp.where(
                        rel >= 0, jnp.exp(jnp.maximum(rel, 0.0) * lgh), 0.0)

    same_head = (row < RET_DIM) == lo
    dirs = ((qf_ref, kf_ref, vf_ref, yf_ref), (qb_ref, kb_ref, vb_ref, yb_ref))
    chains = [(d, b, c) for d in range(2) for b in range(batch) for c in range(tiles)]

    def operands(d, b, c):
        q_ref, k_ref, v_ref, _ = dirs[d]
        sl = slice(c * LANE, (c + 1) * LANE)
        return q_ref[b, :, sl], k_ref[b, :, sl], v_ref[b, :, sl]

    probs = []
    for d, b, c in chains:
        q, k, _ = operands(d, b, c)
        zero = jnp.zeros_like(q)
        q2 = jnp.concatenate([jnp.where(lo, q, zero), jnp.where(lo, zero, q)], axis=0)
        probs.append((_dot_nt(q2, k) * dm_ref[d * tiles + c]).astype(BF16))
    for (d, b, c), p in zip(chains, probs):
        q, _, v = operands(d, b, c)
        t = d * tiles + c
        si = (d * batch + b) * tiles + c
        y_intra = jnp.where(lo, _dot(p[:AB], v), _dot(p[AB:], v))
        q_dec = (q.astype(F32) * qd_ref[t]).astype(BF16)
        dirs[d][3][b, :, c * LANE:(c + 1) * LANE] = (y_intra + _dot(q_dec, st_ref[si].astype(BF16))).astype(BF16)
    for d, b, c in chains:
        _, k, v = operands(d, b, c)
        t = d * tiles + c
        si = (d * batch + b) * tiles + c
        k_dec = (k.astype(F32) * kd_ref[t]).astype(BF16)
        st_ref[si] = st_ref[si] * cd_ref[t] + jnp.where(same_head, _dot_tn(k_dec, v), 0.0)


def _retention(rq, rk, rv, log_g):
    b, s, w = rq.shape
    nb = s // AB
    tiles = w // LANE

    def back(i):
        return jnp.where(i < CTX_BLOCKS, CTX_BLOCKS - 1 - i, nb - 1 + CTX_BLOCKS - i)

    fwd = pl.BlockSpec((b, AB, w), lambda i: (0, i, 0))
    bwd = pl.BlockSpec((b, AB, w), lambda i: (0, back(i), 0))
    return pl.pallas_call(
        functools.partial(_ret_kernel, batch=b),
        grid=(nb,),
        in_specs=[pl.BlockSpec(memory_space=pltpu.SMEM), fwd, fwd, fwd, bwd, bwd, bwd],
        out_specs=[fwd, bwd],
        out_shape=[jax.ShapeDtypeStruct((b, s, w), BF16)] * 2,
        scratch_shapes=[
            pltpu.VMEM((2 * b * tiles, LANE, LANE), F32),
            pltpu.VMEM((2 * tiles, 2 * AB, LANE), F32),
            pltpu.VMEM((2 * tiles, AB, LANE), F32),
            pltpu.VMEM((2 * tiles, AB, LANE), F32),
            pltpu.VMEM((2 * tiles, AB, LANE), F32),
        ],
        compiler_params=_params("arbitrary"),
    )(log_g.reshape(-1), rq, rk, rv, rq, rk, rv)


def _route_tile(logits, tri_ref, ls_ref, rw_ref, cnt_ref):
    ne = N_EXPERTS
    eidx = lax.broadcasted_iota(I32, (ne, TM), 0).astype(F32)
    tops, hots = [], []
    for _ in range(TOP_K):
        mx = jnp.max(logits, axis=0, keepdims=True)
        idx = jnp.min(jnp.where(logits == mx, eidx, float(ne)), axis=0, keepdims=True)
        hot = eidx == idx
        logits = jnp.where(hot, NEG * 2.0, logits)
        tops.append(mx)
        hots.append(hot)
    ex = [jnp.exp(mx - tops[0]) for mx in tops]
    tot = ex[0]
    for e in ex[1:]:
        tot = tot + e
    sel = jnp.zeros((ne, TM), F32)
    for hot in hots:
        sel = jnp.where(hot, 1.0, sel)
    incl = _dot(sel.astype(BF16), tri_ref[...])
    cnt = incl[:, TM - 1:TM]
    run = jnp.floor((cnt + (SUBLANE - 1.0)) * (1.0 / SUBLANE)) * SUBLANE
    ends = jnp.broadcast_to(run, (ne, LANE))
    erow = lax.broadcasted_iota(I32, (ne, LANE), 0)
    shift = 1
    while shift < ne:
        ends = ends + jnp.where(erow >= shift, pltpu.roll(ends, shift, 0), 0.0)
        shift *= 2
    slot = (ends[:, :1] - run) + incl - 1.0
    ls_rows = [jnp.sum(jnp.where(hot, slot, 0.0), axis=0, keepdims=True) for hot in hots]
    rw_rows = [e / tot for e in ex]
    lst = jnp.concatenate(ls_rows + [jnp.full((SUBLANE - TOP_K, TM), -1.0, F32)], axis=0)
    ls_ref[...] = jnp.transpose(
        jnp.concatenate([lst, jnp.full((LANE - SUBLANE, TM), -1.0, F32)], axis=0)).astype(I32)
    rw_ref[...] = jnp.transpose(jnp.concatenate(rw_rows + [jnp.zeros((LANE - TOP_K, TM), F32)], axis=0))
    cnt_ref[...] = jnp.broadcast_to(cnt, cnt_ref.shape).astype(I32)
    return lst


def _merge_kernel(*refs, seq, nt, ntt, n_ctx):
    _merge_body(refs[n_ctx], refs[:n_ctx], *refs[n_ctx + 1:], seq=seq, nt=nt, ntt=ntt)


def _merge_body(x_ref, c_refs, attn_ref, up_ref, uc_ref, un_ref, yf_ref, yb_ref, rg_ref, g_ref, mod_ref,
                  wbd_ref, ps_ref, woa_ref, wop_ref, wor_ref, wout_ref, n2_ref, wr_ref,
                  br_ref, avg_ref, tri_ref,
                  xo_ref, xs_ref, ls_ref, rw_ref, cnt_ref, lg_ref, h2_ref, *, seq, nt, ntt):
    i = pl.program_id(0)
    j = jnp.minimum(i, ntt - 1) % nt

    @pl.when(i == 0)
    def _():
        lg_ref[...] = jnp.zeros_like(lg_ref)
        h2_ref[...] = jnp.zeros_like(h2_ref)

    lst = _route_tile(lg_ref[...], tri_ref, ls_ref, rw_ref, cnt_ref)

    ext = TM + 2 * POOL_HALO
    seq_len = jnp.where(j == 0, TM, seq)
    start = jnp.where(j == 0, 0, (j - 1) * TM)
    u = uc_ref[0]
    ue = jnp.concatenate([up_ref[0], u, un_ref[0]], axis=0)
    erow = lax.broadcasted_iota(I32, (ext, 1), 0) + (start - POOL_HALO)
    ue = jnp.where((erow >= 0) & (erow < seq_len), ue, 0.0)
    t_pos = lax.broadcasted_iota(I32, (TM, 1), 0) + start
    glane = lax.broadcasted_iota(I32, (TM, POOL_WIDTH), 1) // (POOL_WIDTH // len(POOL_WINDOWS))
    run = ue
    width = 1
    diff = jnp.zeros((TM, POOL_WIDTH), F32)
    for gi, w in enumerate(POOL_WINDOWS):
        while width < w:
            run = run + pltpu.roll(run, ext - width, 0)
            width *= 2
        win = pltpu.roll(run, w // 2, 0)[POOL_HALO:POOL_HALO + TM]
        cnt = jnp.minimum(t_pos - w // 2 + w, seq_len) - jnp.maximum(t_pos - w // 2, 0)
        diff = jnp.where(glane == gi, win / cnt.astype(F32) - u, diff)
    pool = _dot(diff.astype(BF16), wbd_ref[...]) * ps_ref[...]

    def head_norm(y):
        dlt = y.astype(F32) - _dot(y, avg_ref[...])
        var = _dot_hilo(dlt * dlt, avg_ref[...])
        return dlt * lax.rsqrt(var + GN_EPS)

    rg = rg_ref[0].astype(F32)
    ret = head_norm(yf_ref[0]) * _silu(rg[:, :RET_W]) + head_norm(yb_ref[0]) * _silu(rg[:, RET_W:])

    slot = lax.broadcasted_iota(I32, (SORT_ROWS, TM), 0).astype(F32)
    p = jnp.zeros((SORT_ROWS, TM), F32)
    for k in range(TOP_K):
        p = jnp.where(slot == lst[k:k + 1, :], 1.0, p)
    xs_ref[...] = _dot(p.astype(BF16), h2_ref[...])

    d = x_ref.shape[-1]
    gate = lambda t: _sigmoid(g_ref[0, :, t * d:(t + 1) * d])
    m = (gate(0) * _dot(attn_ref[0], woa_ref[...]).astype(BF16)
         + gate(1) * _dot(pool.astype(BF16), wop_ref[...]).astype(BF16)
         + gate(2) * _dot(ret.astype(BF16), wor_ref[...]).astype(BF16))
    xn = _merged_rows((x_ref,), c_refs, j == 0) + mod_ref[0, 2:3, :] * _dot(m, wout_ref[...])
    xo_ref[0] = xn

    ms = jnp.mean(xn * xn, axis=-1, keepdims=True)
    h2 = xn * lax.rsqrt(ms + NORM_EPS) * n2_ref[...] * (1.0 + mod_ref[0, 4:5, :]) + mod_ref[0, 3:4, :]
    h2_ref[...] = h2.astype(BF16)
    hh, hl = _split(h2)
    ne = N_EXPERTS
    full = _dot_nt(wr_ref[...], hh)
    logits = full[:ne] + full[ne:] + _dot_nt(wr_ref[:ne], hl) + br_ref[...]
    lg_ref[...] = logits


def _merge(x, ctx, attn, u, yf, yb, rg, gates, mod_l, seq, wts):
    b, s, _ = attn.shape
    d = x.shape[-1]
    nt = s // TM
    ntt = b * nt
    hb = TM // POOL_HALO
    nh = s // POOL_HALO
    cur = lambda i: jnp.minimum(i, ntt - 1)
    lag = lambda i: jnp.maximum(i - 1, 0)
    tok = lambda width: pl.BlockSpec((1, TM, width), lambda i: (cur(i) // nt, cur(i) % nt, 0))
    full = lambda a: pl.BlockSpec(a.shape, lambda i: (0,) * a.ndim)
    lanes = pl.BlockSpec((TM, LANE), lambda i: (lag(i), 0))
    if ctx is None:
        stream = [tok(d)]
    else:
        stream = [pl.BlockSpec((1, TM, d), lambda i: (cur(i) // nt, 0, 0)),
                  pl.BlockSpec((1, TM, d), lambda i: (cur(i) // nt, jnp.maximum(cur(i) % nt - 1, 0), 0))]
    in_specs = stream + [
        tok(ATT_Q),
        pl.BlockSpec((1, POOL_HALO, POOL_WIDTH),
                     lambda i: (cur(i) // nt, jnp.maximum((cur(i) % nt) * hb - 1, 0), 0)),
        tok(POOL_WIDTH),
        pl.BlockSpec((1, POOL_HALO, POOL_WIDTH),
                     lambda i: (cur(i) // nt, jnp.minimum((cur(i) % nt + 1) * hb, nh - 1), 0)),
        tok(RET_W), tok(RET_W), tok(2 * RET_W), tok(3 * d),
        pl.BlockSpec((1, 6, d), lambda i: (jnp.where(cur(i) % nt == 0, b, cur(i) // nt), 0, 0)),
    ] + [full(a) for a in wts]
    return pl.pallas_call(
        functools.partial(_merge_kernel, seq=seq, nt=nt, ntt=ntt, n_ctx=len(stream) - 1),
        grid=(ntt + 1,),
        in_specs=in_specs,
        out_specs=[tok(d), pl.BlockSpec((SORT_ROWS, d), lambda i: (lag(i), 0)), lanes, lanes,
                   pl.BlockSpec((N_EXPERTS, LANE), lambda i: (lag(i), 0))],
        out_shape=[jax.ShapeDtypeStruct((b, s, d), F32), jax.ShapeDtypeStruct((ntt * SORT_ROWS, d), F32),
                   jax.ShapeDtypeStruct((b * s, LANE), I32),
                   jax.ShapeDtypeStruct((b * s, LANE), F32), jax.ShapeDtypeStruct((ntt * N_EXPERTS, LANE), I32)],
        scratch_shapes=[pltpu.VMEM((N_EXPERTS, TM), F32), pltpu.VMEM((TM, d), BF16)],
        compiler_params=_params("arbitrary"),
    )(*([] if ctx is None else [ctx]), x, attn, u, u, u, yf, yb, rg, gates, mod_l, *wts)


def _rows(tile_index):
    return pl.ds(pl.multiple_of(tile_index * SUBLANE, SUBLANE), SUBLANE)


def _tile_loop(count_ref, tile, fn):
    def pair(jj, carry):
        for queue in range(DMA_QUEUES):
            fn(DMA_QUEUES * jj + queue, queue)
        return carry

    def single(j, carry):
        fn(j, 0)
        return carry

    lax.fori_loop(0, MIN_TILES // DMA_QUEUES, pair, 0, unroll=4)
    lax.fori_loop(MIN_TILES, count_ref[tile], single, 0)


def _tile_copies(table_ref, count_ref, tile, copy):
    _tile_loop(count_ref, tile, lambda j, queue: copy(j, table_ref[tile * SORT_TILES + j], queue))


def _expert_kernel(be_ref, nu_ref, src_ref, xs_hbm, w1_ref, b1_ref, w2_ref, b2_ref, y_ref, w1b, w2b, xbuf, sem):
    bi = pl.program_id(0)
    used = bi < nu_ref[0]
    fresh = (bi == 0) | (be_ref[bi] != be_ref[jnp.maximum(bi - 1, 0)])
    cur = bi % 2
    block_tiles = EB // SUBLANE

    def each_tile(fn):
        def pair(tt, carry):
            for queue in range(DMA_QUEUES):
                fn(DMA_QUEUES * tt + queue, queue)
            return carry
        lax.fori_loop(0, block_tiles // DMA_QUEUES, pair, 0, unroll=4)

    def fetch(blk, half):
        each_tile(lambda t, queue: pltpu.make_async_copy(
            xs_hbm.at[_rows(src_ref[blk * block_tiles + t])], xbuf.at[half, _rows(t)], sem.at[half]
        ).start(priority=queue))

    @pl.when(bi == 0)
    def _():
        fetch(0, 0)

    @pl.when(bi + 1 < nu_ref[0])
    def _():
        fetch(bi + 1, 1 - cur)

    @pl.when(used & fresh)
    def _():
        w1b[...] = w1_ref[0, 0].astype(BF16)
        w2b[...] = w2_ref[0, 0].astype(BF16)

    @pl.when(jnp.logical_not(used))
    def _():
        y_ref[...] = jnp.zeros_like(y_ref)

    @pl.when(used)
    def _():
        each_tile(lambda t, queue: pltpu.make_async_copy(
            xs_hbm.at[_rows(t)], xbuf.at[cur, _rows(t)], sem.at[cur]).wait())
        xb = xbuf[cur].astype(BF16)
        y = None
        for c in range(0, D_FF, FF_CHUNK):
            glu = _dot(xb, w1b[:, c:c + FF_CHUNK]) + b1_ref[0, 0, :, c:c + FF_CHUNK]
            lin = _dot(xb, w1b[:, D_FF + c:D_FF + c + FF_CHUNK]) + b1_ref[0, 0, :, D_FF + c:D_FF + c + FF_CHUNK]
            glu = jnp.minimum(glu, SWIGLU_LIMIT)
            lin = jnp.clip(lin, -SWIGLU_LIMIT, SWIGLU_LIMIT)
            act = glu * _sigmoid(SWIGLU_ALPHA * glu) * (lin + 1.0)
            part = _dot(act.astype(BF16), w2b[c:c + FF_CHUNK, :])
            y = part if y is None else y + part
        y_ref[...] = y + b2_ref[0, 0]


def _experts(xs, block_e, n_used, src, w1, b1, w2, b2, layer, n_blocks):
    d = xs.shape[1]
    depth, ne, _, f2 = w1.shape
    grid_spec = pltpu.PrefetchScalarGridSpec(
        num_scalar_prefetch=3,
        grid=(n_blocks,),
        in_specs=[
            pl.BlockSpec(memory_space=pl.ANY),
            pl.BlockSpec((1, 1, d, f2), lambda bi, be, *_: (layer, be[bi], 0, 0)),
            pl.BlockSpec((1, 1, 1, f2), lambda bi, be, *_: (layer, be[bi], 0, 0)),
            pl.BlockSpec((1, 1, f2 // 2, d), lambda bi, be, *_: (layer, be[bi], 0, 0)),
            pl.BlockSpec((1, 1, 1, d), lambda bi, be, *_: (layer, be[bi], 0, 0)),
        ],
        out_specs=pl.BlockSpec((EB, d), lambda bi, *_: (bi, 0)),
        scratch_shapes=[pltpu.VMEM((d, f2), BF16), pltpu.VMEM((f2 // 2, d), BF16),
                        pltpu.VMEM((2, EB, d), F32), pltpu.SemaphoreType.DMA((2,))],
    )
    return pl.pallas_call(
        _expert_kernel,
        grid_spec=grid_spec,
        out_shape=jax.ShapeDtypeStruct((n_blocks * EB, d), F32),
        compiler_params=_params("arbitrary"),
    )(block_e, n_used, src, xs, w1, b1.reshape(depth, ne, 1, f2), w2, b2.reshape(depth, ne, 1, d))


def _combine_kernel(dst_ref, cnt_ref, x_ref, ls_ref, rw_ref, mod_ref, fg_ref, ys_hbm, xo_ref, buf, sem, *, final):
    i = pl.program_id(0)
    last = pl.num_programs(0) - 1
    cur = i % 2

    def fetch(tile, half):
        _tile_copies(dst_ref, cnt_ref, tile, lambda j, t, queue: pltpu.make_async_copy(
            ys_hbm.at[_rows(t)], buf.at[half, _rows(j)], sem.at[half]).start(priority=queue))

    @pl.when(i == 0)
    def _():
        buf[...] = jnp.zeros_like(buf)
        fetch(0, 0)

    @pl.when(i < last)
    def _():
        fetch(i + 1, 1 - cur)

    _tile_loop(cnt_ref, i, lambda j, queue: pltpu.make_async_copy(
        ys_hbm.at[_rows(j)], buf.at[cur, _rows(j)], sem.at[cur]).wait())
    yb = buf[cur].astype(BF16)
    slot = lax.broadcasted_iota(I32, (TM, SORT_ROWS), 1)
    ls = ls_ref[...]
    rw = rw_ref[...]
    g = jnp.zeros((TM, SORT_ROWS), F32)
    for k in range(TOP_K):
        g = jnp.where(slot == ls[:, k:k + 1], rw[:, k:k + 1], g)
    xn = x_ref[0] + mod_ref[0, 5:6, :] * _dot(g.astype(BF16), yb)
    if final:
        ms = jnp.mean(xn * xn, axis=-1, keepdims=True)
        xn = xn * lax.rsqrt(ms + NORM_EPS) * fg_ref[...]
    xo_ref[0] = xn


def _combine(xn, ls, rw, mod_l, final_gain, ys, tabs, final):
    b, s, d = xn.shape
    nt = s // TM
    tok = pl.BlockSpec((1, TM, d), lambda i, *_: (i // nt, i % nt, 0))
    lanes = pl.BlockSpec((TM, LANE), lambda i, *_: (i, 0))
    if final:
        out_spec = pl.BlockSpec((1, TM, d), lambda i, *_: (i // nt, jnp.maximum(i % nt - 1, 0), 0))
        out_shape = jax.ShapeDtypeStruct((b, s - TM, d), F32)
    else:
        out_spec, out_shape = tok, jax.ShapeDtypeStruct((b, s, d), F32)
    grid_spec = pltpu.PrefetchScalarGridSpec(
        num_scalar_prefetch=2,
        grid=(b * nt,),
        in_specs=[tok, lanes, lanes,
                  pl.BlockSpec((1, 6, d), lambda i, *_: (jnp.where(i % nt == 0, b, i // nt), 0, 0)),
                  pl.BlockSpec((1, d), lambda i, *_: (0, 0)),
                  pl.BlockSpec(memory_space=pl.ANY)],
        out_specs=out_spec,
        scratch_shapes=[pltpu.VMEM((2, SORT_ROWS, d), F32), pltpu.SemaphoreType.DMA((2,))],
    )
    return pl.pallas_call(
        functools.partial(_combine_kernel, final=final),
        grid_spec=grid_spec,
        out_shape=out_shape,
        compiler_params=_params("arbitrary"),
    )(tabs["dst"], tabs["count"], xn, ls, rw, mod_l, final_gain.reshape(1, d), ys)


def _inproj_weight(w_in_l):
    d = w_in_l.shape[0]
    sizes = (ATT_Q, N_KV_HEADS * HEAD_DIM, N_KV_HEADS * HEAD_DIM, POOL_WIDTH,
             RET_W, RET_W, RET_W, RET_W, RET_W, d, d, d)
    parts, off = [], 0
    for sz in sizes:
        parts.append(w_in_l[:, off:off + sz])
        off += sz
    q, k, v, u, rq, rk, rv, rgf, rgb, ga, gp, gr = parts
    twice = lambda w: jnp.concatenate(
        [w[:, h * HEAD_DIM:(h + 1) * HEAD_DIM] for h in range(N_KV_HEADS) for _ in range(2)], axis=1)
    cols = [q * HEAD_DIM ** -0.5, twice(k), twice(v), rq, rk * RET_DIM ** -0.5, rv, rgf, rgb, u, ga, gp, gr]
    return jnp.concatenate(cols, axis=1).astype(BF16)


def _block_diag(blocks):
    n, r, c = blocks.shape
    out = jnp.zeros((n * r, n * c), blocks.dtype)
    for g in range(n):
        out = out.at[g * r:(g + 1) * r, g * c:(g + 1) * c].set(blocks[g])
    return out


def _routing_tables(cnt, n_blocks):
    ntt = cnt.shape[0] // N_EXPERTS
    counts = cnt.reshape(ntt, N_EXPERTS, LANE)[:, :, 0]
    run = (counts + SUBLANE - 1) // SUBLANE
    total = jnp.sum(run, axis=0)
    eb = EB // SUBLANE
    padded = (total + eb - 1) // eb * eb
    pad_end = jnp.cumsum(padded)
    pad_start = pad_end - padded
    off = pad_start[None, :] + jnp.cumsum(run, axis=0) - run
    run_end = jnp.cumsum(run, axis=1)
    j = jnp.arange(SORT_TILES)
    owner = jnp.sum(run_end[:, None, :] <= j[None, :, None], axis=2)
    mine = owner[:, :, None] == jnp.arange(N_EXPERTS)[None, None, :]
    in_region = j[None, :] + jnp.sum(jnp.where(mine, (off - (run_end - run))[:, None, :], 0), axis=2)
    dst = jnp.where(owner < N_EXPERTS, in_region, 0)
    n_used = pad_end[-1] // eb
    blk = jnp.minimum(jnp.arange(n_blocks), n_used - 1) * eb
    block_e = jnp.minimum(jnp.sum(pad_end[None, :] <= blk[:, None], axis=1), N_EXPERTS - 1)
    mine_b = block_e[:, None] == jnp.arange(N_EXPERTS)[None, :]
    of_block = lambda table: jnp.sum(jnp.where(mine_b[:, None, :], table[None, :, :], 0), axis=2)
    first = of_block(jnp.cumsum(run, axis=0) - run)
    length = of_block(run)
    local = of_block(run_end - run)
    start_b = jnp.sum(jnp.where(mine_b, pad_start[None, :], 0), axis=1)
    pos = (jnp.arange(n_blocks) * eb - start_b)[:, None] + jnp.arange(eb)[None, :]
    inside = (pos[:, :, None] >= first[:, None, :]) & (pos[:, :, None] < (first + length)[:, None, :])
    where = (jnp.arange(ntt) * SORT_TILES)[None, None, :] + (local - first)[:, None, :] + pos[:, :, None]
    zero_tile = SORT_TILES - 1
    src = jnp.where(jnp.any(inside, axis=2), jnp.sum(jnp.where(inside, where, 0), axis=2), zero_tile)
    tabs = dict(dst=dst.reshape(-1).astype(I32), count=run_end[:, -1].astype(I32),
                src=src.reshape(-1).astype(I32))
    return tabs, block_e.astype(I32), n_used.reshape(1).astype(I32)


def kernel(x, c, ctx, c_ctx, w_mod, b_mod, norm1, norm2, w_in, attn_sinks, pool_w, pool_scale, ret_decay,
           w_o_attn, w_o_pool, w_o_ret, w_out, w_router, b_router, w_expert_in, b_expert_in, w_expert_out,
           b_expert_out, final_norm):
    b, seq, d = x.shape
    depth = w_mod.shape[0]
    assert ctx.shape[1] == TM and seq % TM == 0 and seq % GRID_W == 0 and b + 1 <= SUBLANE
    s = seq + TM
    xm = x
    mod = _modulation(c, c_ctx, w_mod, b_mod)
    tables = _rope_tables(seq)
    ntt = b * s // TM
    n_blocks = -(-(b * s * TOP_K + ntt * N_EXPERTS * (SUBLANE - 1)) // EB) + N_EXPERTS
    avg = _block_diag(jnp.full((RET_HEADS, RET_DIM, RET_DIM), 1.0 / RET_DIM, F32)).astype(BF16)
    tri = (jnp.arange(TM)[:, None] <= jnp.arange(TM)[None, :]).astype(BF16)
    for l in range(depth):
        first_ctx = ctx if l == 0 else None
        q, k2, v2, rq, rk, rv, rg, u, gates = _inproj(xm, first_ctx, mod[l], norm1[l], _inproj_weight(w_in[l]), tables)
        attn = _attention(q, k2, v2, attn_sinks[l])
        yf, yb = _retention(rq, rk, rv, jax.nn.log_sigmoid(ret_decay[l].astype(F32)))
        wr_hilo = jnp.concatenate(_split(w_router[l].T), axis=0)
        br = jnp.broadcast_to(b_router[l][:, None], (N_EXPERTS, TM))
        wts = (_block_diag(pool_w[l]).astype(BF16), pool_scale[l].reshape(1, -1),
               w_o_attn[l].astype(BF16), w_o_pool[l].astype(BF16), w_o_ret[l].astype(BF16),
               w_out[l].astype(BF16), norm2[l].reshape(1, d), wr_hilo, br, avg, tri)
        xn, xs, ls, rw, cnt = _merge(xm, first_ctx, attn, u, yf, yb, rg, gates, mod[l], seq, wts)
        tabs, block_e, n_used = _routing_tables(cnt, n_blocks)
        ys = _experts(xs, block_e, n_used, tabs["src"], w_expert_in, b_expert_in, w_expert_out, b_expert_out, l,
                      n_blocks)
        xm = _combine(xn, ls, rw, mod[l], final_norm, ys, tabs, final=l == depth - 1)
    return xm
```

```python
import functools

import jax
import jax.numpy as jnp
from jax import lax
from jax.experimental import pallas as pl
from jax.experimental.pallas import tpu as pltpu

F32 = jnp.float32
BF16 = jnp.bfloat16
I32 = jnp.int32

GRID_W = 64
HEAD_DIM = 64
N_Q_HEADS = 8
N_KV_HEADS = 2
WINDOW = 128
ROPE_THETA = 10000.0
POOL_WIDTH = 256
POOL_WINDOWS = (2, 4, 8, 16)
RET_HEADS = 4
RET_DIM = 64
N_EXPERTS = 32
TOP_K = 4
D_FF = 1024
SWIGLU_LIMIT = 7.0
SWIGLU_ALPHA = 1.702
NORM_EPS = 1e-6
GN_EPS = 1e-5

LANE = 128
SUBLANE = 8
MXU_N = 256
TM = 256
AB = 128
IN_TILE = 640
CTX_BLOCKS = TM // AB
Q_BLOCKS = 2
EB = 512
FF_CHUNK = 512
POOL_HALO = SUBLANE
SORT_ROWS = TM * TOP_K + N_EXPERTS * SUBLANE
SORT_TILES = SORT_ROWS // SUBLANE
MIN_TILES = TM * TOP_K // SUBLANE
COMBINE_SLOTS = 3
NEG = -1e30
VMEM_LIMIT = 56 * 1024 * 1024

ATT_Q = N_Q_HEADS * HEAD_DIM
KV2 = 2 * N_KV_HEADS * HEAD_DIM
RET_W = RET_HEADS * RET_DIM


def _dot(a, b):
    return jnp.dot(a, b, preferred_element_type=F32)


def _dot_nt(a, b):
    return lax.dot_general(a, b, (((1,), (1,)), ((), ())), preferred_element_type=F32)


def _dot_tn(a, b):
    return lax.dot_general(a, b, (((0,), (0,)), ((), ())), preferred_element_type=F32)


def _split(a):
    hi = a.astype(BF16)
    lo = (a - hi.astype(F32)).astype(BF16)
    return hi, lo


def _dot_hilo(a, m):
    hi, lo = _split(a)
    return _dot(hi, m) + _dot(lo, m)


def _sigmoid(x):
    return 0.5 * jnp.tanh(0.5 * x) + 0.5


def _silu(x):
    return x * _sigmoid(x)


def _params(*sem):
    return pltpu.CompilerParams(dimension_semantics=sem, vmem_limit_bytes=VMEM_LIMIT)


def _mod_kernel(c_ref, w_ref, b_ref, o_ref):
    s = _silu(c_ref[...])
    sh, sl = _split(s)
    wh, wl = _split(w_ref[0])
    o_ref[0] = _dot(sh, wh) + _dot(sh, wl) + _dot(sl, wh) + b_ref[0]


def _modulation(c, c_ctx, w_mod, b_mod):
    depth, d, six_d = w_mod.shape
    b = c.shape[0]
    cc = jnp.zeros((SUBLANE, d), F32).at[:b].set(c).at[b].set(c_ctx)
    out = pl.pallas_call(
        _mod_kernel,
        grid=(depth, six_d // d),
        in_specs=[
            pl.BlockSpec((SUBLANE, d), lambda l, n: (0, 0)),
            pl.BlockSpec((1, d, d), lambda l, n: (l, 0, n)),
            pl.BlockSpec((1, 1, d), lambda l, n: (l, 0, n)),
        ],
        out_specs=pl.BlockSpec((1, SUBLANE, d), lambda l, n: (l, 0, n)),
        out_shape=jax.ShapeDtypeStruct((depth, SUBLANE, six_d), F32),
        compiler_params=_params("arbitrary", "arbitrary"),
    )(cc, w_mod, b_mod.reshape(depth, 1, six_d))
    return out[:, : b + 1].reshape(depth, b + 1, 6, d)


def _rope(x, cos, sin, half, first):
    partner = jnp.where(first, pltpu.roll(x, LANE - half, 1), pltpu.roll(x, half, 1))
    return x * cos + partner * sin


def _merged_rows(x_refs, c_refs, first):
    parts = []
    for n, ref in enumerate(x_refs):
        part = ref[0]
        if n < len(c_refs):
            part = jnp.where(first, c_refs[n][0], part)
        parts.append(part)
    return parts[0] if len(parts) == 1 else jnp.concatenate(parts, axis=0)


def _inproj_kernel(*refs, n_sub, n_ctx):
    x_refs, c_refs, refs = refs[:n_sub], refs[n_sub:n_sub + n_ctx], refs[n_sub + n_ctx:]
    mod_ref, modc_ref, n1_ref, w_ref, ac_ref, as_ref, rc_ref, rs_ref = refs[:8]
    q_ref, k_ref, v_ref, rq_ref, rk_ref, rv_ref, rg_ref, u_ref, g_ref = refs[8:]
    x = _merged_rows(x_refs, c_refs, pl.program_id(1) == 0)
    rows = x.shape[0]
    ms = jnp.mean(x * x, axis=-1, keepdims=True)
    y = x * lax.rsqrt(ms + NORM_EPS) * n1_ref[...]
    is_ctx = (lax.broadcasted_iota(I32, (rows, 1), 0) < TM) & (pl.program_id(1) == 0)
    scale = jnp.where(is_ctx, modc_ref[0, 1:2, :], mod_ref[0, 1:2, :])
    shift = jnp.where(is_ctx, modc_ref[0, 0:1, :], mod_ref[0, 0:1, :])
    hb = (y * (1.0 + scale) + shift).astype(BF16)
    lane = lax.broadcasted_iota(I32, (rows, LANE), 1)
    a_first = (lane % (HEAD_DIM // 2)) < (HEAD_DIM // 4)
    r_first = (lane % RET_DIM) < (RET_DIM // 2)
    ac, asn, rc, rsn = ac_ref[...], as_ref[...], rc_ref[...], rs_ref[...]

    def proj(off, width):
        return _dot(hb, w_ref[:, off:off + width])

    def rotated(ref, width, cos, sin, half, first, off):
        for t in range(0, width, MXU_N):
            pr = proj(off + t, MXU_N)
            for g in range(0, MXU_N, LANE):
                ref[0, :, t + g:t + g + LANE] = _rope(pr[:, g:g + LANE], cos, sin, half, first).astype(BF16)

    off = 0
    rotated(q_ref, ATT_Q, ac, asn, HEAD_DIM // 4, a_first, off)
    off += ATT_Q
    rotated(k_ref, KV2, ac, asn, HEAD_DIM // 4, a_first, off)
    off += KV2
    v_ref[0] = proj(off, KV2).astype(BF16)
    off += KV2
    for ref in (rq_ref, rk_ref):
        rotated(ref, RET_W, rc, rsn, RET_DIM // 2, r_first, off)
        off += RET_W
    rv_ref[0] = proj(off, RET_W).astype(BF16)
    off += RET_W
    rg_ref[0] = proj(off, 2 * RET_W).astype(BF16)
    off += 2 * RET_W
    u_ref[0] = proj(off, POOL_WIDTH)
    off += POOL_WIDTH
    d = x.shape[-1]
    for t in range(3):
        g_ref[0, :, t * d:(t + 1) * d] = proj(off, d).astype(BF16)
        off += d


def _inproj(x, ctx, mod_l, norm1_l, w1, tables):
    b, _, d = x.shape
    s = x.shape[1] + (0 if ctx is None else ctx.shape[1])
    ti = IN_TILE if s % IN_TILE == 0 else TM
    n_sub = ti // AB
    n_ctx = 0 if ctx is None else CTX_BLOCKS
    pieces = [pl.BlockSpec((1, AB, d), lambda bi, j, n=n: (bi, jnp.maximum(n_sub * j + n - n_ctx, 0), 0))
              for n in range(n_sub)]
    ctx_pieces = [pl.BlockSpec((1, AB, d), lambda bi, j, n=n: (bi, n, 0)) for n in range(n_ctx)]
    wcols = w1.shape[1]
    tok = lambda width: pl.BlockSpec((1, ti, width), lambda bi, j: (bi, j, 0))
    tab = pl.BlockSpec((ti, LANE), lambda bi, j: (j, 0))
    widths = (ATT_Q, KV2, KV2, RET_W, RET_W, RET_W, 2 * RET_W, POOL_WIDTH, 3 * d)
    dtypes = (BF16,) * 7 + (F32, BF16)
    return pl.pallas_call(
        functools.partial(_inproj_kernel, n_sub=n_sub, n_ctx=n_ctx),
        grid=(b, s // ti),
        in_specs=pieces + ctx_pieces + [
            pl.BlockSpec((1, 6, d), lambda bi, j: (bi, 0, 0)),
            pl.BlockSpec((1, 6, d), lambda bi, j: (b, 0, 0)),
            pl.BlockSpec((1, d), lambda bi, j: (0, 0)),
            pl.BlockSpec((d, wcols), lambda bi, j: (0, 0), pipeline_mode=pl.Buffered(1)),
            tab, tab, tab, tab,
        ],
        out_specs=[tok(w) for w in widths],
        out_shape=[jax.ShapeDtypeStruct((b, s, w), dt) for w, dt in zip(widths, dtypes)],
        compiler_params=_params("arbitrary", "arbitrary"),
    )(*([x] * n_sub), *([ctx] * n_ctx), mod_l, mod_l, norm1_l.reshape(1, d), w1, *tables)


def _rope_tables(seq):
    rows = seq // GRID_W
    rpos = jnp.arange(rows, dtype=F32)[:, None]
    cpos = jnp.arange(GRID_W, dtype=F32)[:, None]
    lane = jnp.arange(LANE)
    grid = lambda per_row, per_col: (per_row[:, None, :] + per_col[None, :, :]).reshape(seq, LANE)
    axis_dim = HEAD_DIM // 2
    inv_a = ROPE_THETA ** (-jnp.arange(0, axis_dim, 2, dtype=F32) / axis_dim)
    hl = lane % HEAD_DIM
    inv_al = inv_a[(hl % axis_dim) % (axis_dim // 2)][None, :]
    by_row = (hl < axis_dim)[None, :]
    a_sign = jnp.where((hl % axis_dim) < axis_dim // 2, -1.0, 1.0)[None, :]
    a_cos = grid(jnp.where(by_row, jnp.cos(rpos * inv_al), 0.0), jnp.where(by_row, 0.0, jnp.cos(cpos * inv_al)))
    a_sin = grid(jnp.where(by_row, jnp.sin(rpos * inv_al), 0.0), jnp.where(by_row, 0.0, jnp.sin(cpos * inv_al)))
    inv_r = 1.0 / (ROPE_THETA ** jnp.linspace(0.0, 1.0, RET_DIM // 2, dtype=F32))
    rl = lane % RET_DIM
    inv_rl = inv_r[rl % (RET_DIM // 2)][None, :]
    r_sign = jnp.where(rl < RET_DIM // 2, -1.0, 1.0)[None, :]
    hi = (rpos * GRID_W) * inv_rl
    lo = cpos * inv_rl
    outer = lambda a, b: (a[:, None, :] * b[None, :, :]).reshape(seq, LANE)
    r_cos = outer(jnp.cos(hi), jnp.cos(lo)) - outer(jnp.sin(hi), jnp.sin(lo))
    r_sin = outer(jnp.sin(hi), jnp.cos(lo)) + outer(jnp.cos(hi), jnp.sin(lo))
    ones = jnp.ones((TM, LANE), F32)
    zeros = jnp.zeros((TM, LANE), F32)
    cat = lambda head, body: jnp.concatenate([head, body], axis=0)
    return (cat(ones, a_cos), cat(zeros, a_sin * a_sign), cat(ones, r_cos), cat(zeros, r_sin * r_sign))


def _attn_kernel(sink_ref, bias0_ref, bias1_ref, q_ref, k0_ref, k1_ref, k2_ref, k3_ref, kx_ref,
                 v0_ref, v1_ref, v2_ref, v3_ref, vx_ref, o_ref):
    g_heads = N_Q_HEADS // N_KV_HEADS
    lane = lax.broadcasted_iota(I32, (AB, LANE), 1)
    lo = lane < HEAD_DIM
    rows = lax.broadcasted_iota(I32, (g_heads * AB, 1), 0)
    k_refs = (k0_ref, k1_ref, k2_ref, k3_ref)
    v_refs = (v0_ref, v1_ref, v2_ref, v3_ref)
    biases = (bias0_ref[0], bias1_ref[0])
    chains = [(t, g) for t in range(Q_BLOCKS) for g in range(N_KV_HEADS)]
    scores, sinks, values = [], [], []
    for t, g in chains:
        ks = slice(g * LANE, (g + 1) * LANE)
        kd = jnp.concatenate([r[0, :, ks] for r in k_refs[t:t + 3]] + [kx_ref[0, :, ks]], axis=0)
        values.append(jnp.concatenate([r[0, :, ks] for r in v_refs[t:t + 3]] + [vx_ref[0, :, ks]], axis=0))
        qs = []
        for c in range(2 * g, 2 * g + 2):
            qc = q_ref[0, t * AB:(t + 1) * AB, c * LANE:(c + 1) * LANE]
            zero = jnp.zeros_like(qc)
            qs += [jnp.where(lo, qc, zero), jnp.where(lo, zero, qc)]
        scores.append(_dot_nt(jnp.concatenate(qs, axis=0), kd) + biases[t])
        sink = jnp.full((g_heads * AB, 1), sink_ref[g_heads * g + g_heads - 1], F32)
        for h in range(g_heads - 2, -1, -1):
            sink = jnp.where(rows < (h + 1) * AB, sink_ref[g_heads * g + h], sink)
        sinks.append(sink)
    for (t, g), s, sink, vd in zip(chains, scores, sinks, values):
        m = jnp.maximum(jnp.max(s, axis=-1, keepdims=True), sink)
        p = jnp.exp(s - m)
        den = jnp.sum(p, axis=-1, keepdims=True) + jnp.exp(sink - m)
        o = _dot(p.astype(BF16), vd) / den
        for h in range(2):
            c = 2 * g + h
            o_ref[0, t * AB:(t + 1) * AB, c * LANE:(c + 1) * LANE] = jnp.where(
                lo, o[2 * h * AB:(2 * h + 1) * AB], o[(2 * h + 1) * AB:(2 * h + 2) * AB]).astype(BF16)


def _attn_bias():
    g_heads = N_Q_HEADS // N_KV_HEADS
    r = jnp.arange(g_heads * AB)[:, None] % AB
    j = jnp.arange(3 * AB + TM)[None, :]
    band = jnp.abs(j - AB - r) <= WINDOW
    is_ctx = j >= 3 * AB
    variants = (band, band & (j >= AB), band & (j < 2 * AB), jnp.zeros_like(band))
    return jnp.stack([jnp.where(v | is_ctx, 0.0, NEG) for v in variants]).astype(F32)


def _attention(q, k2, v2, sinks):
    b, s, _ = q.shape
    nb = s // AB
    assert nb - CTX_BLOCKS >= 2 and nb % Q_BLOCKS == 0 and CTX_BLOCKS % Q_BLOCKS == 0
    bias = _attn_bias()

    def variant(blk):
        return jnp.where(blk < CTX_BLOCKS, 3, jnp.where(blk == CTX_BLOCKS, 1, jnp.where(blk == nb - 1, 2, 0)))

    def kv(offset):
        return pl.BlockSpec((1, AB, KV2), lambda bi, i: (bi, jnp.clip(Q_BLOCKS * i + offset, 0, nb - 1), 0))

    def bias_spec(t):
        return pl.BlockSpec((1,) + bias.shape[1:], lambda bi, i: (variant(Q_BLOCKS * i + t), 0, 0))

    cx = pl.BlockSpec((1, TM, KV2), lambda bi, i: (bi, 0, 0))
    kvs = [kv(o) for o in range(-1, Q_BLOCKS + 1)] + [cx]
    qo = pl.BlockSpec((1, Q_BLOCKS * AB, ATT_Q), lambda bi, i: (bi, i, 0))
    return pl.pallas_call(
        _attn_kernel,
        grid=(b, nb // Q_BLOCKS),
        in_specs=[pl.BlockSpec(memory_space=pltpu.SMEM), bias_spec(0), bias_spec(1), qo] + kvs + kvs,
        out_specs=qo,
        out_shape=jax.ShapeDtypeStruct((b, s, ATT_Q), BF16),
        compiler_params=_params("arbitrary", "arbitrary"),
    )(sinks, bias, bias, q, *([k2] * len(kvs)), *([v2] * len(kvs)))


def _ret_kernel(lg_ref, qf_ref, kf_ref, vf_ref, qb_ref, kb_ref, vb_ref, yf_ref, yb_ref,
                st_ref, dm_ref, qd_ref, kd_ref, cd_ref, *, batch):
    step = pl.program_id(0)
    lane = lax.broadcasted_iota(I32, (AB, LANE), 1)
    row = lax.broadcasted_iota(I32, (AB, LANE), 0)
    lo = lane < RET_DIM
    tiles = RET_W // LANE

    @pl.when(step == 0)
    def _():
        st_ref[...] = jnp.zeros_like(st_ref)
        ii = row.astype(F32)
        jj = lane.astype(F32)
        for d in range(2):
            for c in range(tiles):
                lg0 = lg_ref[d * RET_HEADS + 2 * c]
                lg1 = lg_ref[d * RET_HEADS + 2 * c + 1]
                lgl = jnp.where(lo, lg0, lg1)
                q_exp = ii + 1.0 if d == 0 else AB - ii
                k_exp = (AB - 1.0) - ii if d == 0 else ii
                qd_ref[d * tiles + c] = jnp.exp(q_exp * lgl)
                kd_ref[d * tiles + c] = jnp.exp(k_exp * lgl)
                cd_ref[d * tiles + c] = jnp.exp(AB * lgl)
                rel = ii - jj if d == 0 else jj - ii
                for hh, lgh in enumerate((lg0, lg1)):
                    dm_ref[d * tiles + c, hh * AB:(hh + 1) * AB, :] = jnp.where(
                        rel >= 0, jnp.exp(jnp.maximum(rel, 0.0) * lgh), 0.0)

    same_head = (row < RET_DIM) == lo
    dirs = ((qf_ref, kf_ref, vf_ref, yf_ref), (qb_ref, kb_ref, vb_ref, yb_ref))
    chains = [(d, b, c) for d in range(2) for b in range(batch) for c in range(tiles)]

    def operands(d, b, c):
        q_ref, k_ref, v_ref, _ = dirs[d]
        sl = slice(c * LANE, (c + 1) * LANE)
        return q_ref[b, :, sl], k_ref[b, :, sl], v_ref[b, :, sl]

    probs = []
    for d, b, c in chains:
        q, k, _ = operands(d, b, c)
        zero = jnp.zeros_like(q)
        q2 = jnp.concatenate([jnp.where(lo, q, zero), jnp.where(lo, zero, q)], axis=0)
        probs.append((_dot_nt(q2, k) * dm_ref[d * tiles + c]).astype(BF16))
    for (d, b, c), p in zip(chains, probs):
        q, _, v = operands(d, b, c)
        t = d * tiles + c
        si = (d * batch + b) * tiles + c
        y_intra = jnp.where(lo, _dot(p[:AB], v), _dot(p[AB:], v))
        q_dec = (q.astype(F32) * qd_ref[t]).astype(BF16)
        dirs[d][3][b, :, c * LANE:(c + 1) * LANE] = (y_intra + _dot(q_dec, st_ref[si].astype(BF16))).astype(BF16)
    for d, b, c in chains:
        _, k, v = operands(d, b, c)
        t = d * tiles + c
        si = (d * batch + b) * tiles + c
        k_dec = (k.astype(F32) * kd_ref[t]).astype(BF16)
        st_ref[si] = st_ref[si] * cd_ref[t] + jnp.where(same_head, _dot_tn(k_dec, v), 0.0)


def _retention(rq, rk, rv, log_g):
    b, s, w = rq.shape
    nb = s // AB
    tiles = w // LANE

    def back(i):
        return jnp.where(i < CTX_BLOCKS, CTX_BLOCKS - 1 - i, nb - 1 + CTX_BLOCKS - i)

    fwd = pl.BlockSpec((b, AB, w), lambda i: (0, i, 0))
    bwd = pl.BlockSpec((b, AB, w), lambda i: (0, back(i), 0))
    return pl.pallas_call(
        functools.partial(_ret_kernel, batch=b),
        grid=(nb,),
        in_specs=[pl.BlockSpec(memory_space=pltpu.SMEM), fwd, fwd, fwd, bwd, bwd, bwd],
        out_specs=[fwd, bwd],
        out_shape=[jax.ShapeDtypeStruct((b, s, w), BF16)] * 2,
        scratch_shapes=[
            pltpu.VMEM((2 * b * tiles, LANE, LANE), F32),
            pltpu.VMEM((2 * tiles, 2 * AB, LANE), F32),
            pltpu.VMEM((2 * tiles, AB, LANE), F32),
            pltpu.VMEM((2 * tiles, AB, LANE), F32),
            pltpu.VMEM((2 * tiles, AB, LANE), F32),
        ],
        compiler_params=_params("arbitrary"),
    )(log_g.reshape(-1), rq, rk, rv, rq, rk, rv)


def _route_tile(logits, tri_ref, ls_ref, rw_ref, cnt_ref):
    ne = N_EXPERTS
    eidx = lax.broadcasted_iota(I32, (ne, TM), 0).astype(F32)
    tops, hots = [], []
    for _ in range(TOP_K):
        mx = jnp.max(logits, axis=0, keepdims=True)
        idx = jnp.min(jnp.where(logits == mx, eidx, float(ne)), axis=0, keepdims=True)
        hot = eidx == idx
        logits = jnp.where(hot, NEG * 2.0, logits)
        tops.append(mx)
        hots.append(hot)
    ex = [jnp.exp(mx - tops[0]) for mx in tops]
    tot = ex[0]
    for e in ex[1:]:
        tot = tot + e
    sel = jnp.zeros((ne, TM), F32)
    for hot in hots:
        sel = jnp.where(hot, 1.0, sel)
    incl = _dot(sel.astype(BF16), tri_ref[...])
    cnt = incl[:, TM - 1:TM]
    run = jnp.floor((cnt + (SUBLANE - 1.0)) * (1.0 / SUBLANE)) * SUBLANE
    ends = jnp.broadcast_to(run, (ne, LANE))
    erow = lax.broadcasted_iota(I32, (ne, LANE), 0)
    shift = 1
    while shift < ne:
        ends = ends + jnp.where(erow >= shift, pltpu.roll(ends, shift, 0), 0.0)
        shift *= 2
    slot = (ends[:, :1] - run) + incl - 1.0
    ls_rows = [jnp.sum(jnp.where(hot, slot, 0.0), axis=0, keepdims=True) for hot in hots]
    rw_rows = [e / tot for e in ex]
    lst = jnp.concatenate(ls_rows + [jnp.full((SUBLANE - TOP_K, TM), -1.0, F32)], axis=0)
    ls_ref[...] = jnp.transpose(
        jnp.concatenate([lst, jnp.full((LANE - SUBLANE, TM), -1.0, F32)], axis=0)).astype(I32)
    rw_ref[...] = jnp.transpose(jnp.concatenate(rw_rows + [jnp.zeros((LANE - TOP_K, TM), F32)], axis=0))
    cnt_ref[...] = jnp.broadcast_to(cnt, cnt_ref.shape).astype(I32)
    return lst


def _merge_kernel(*refs, seq, nt, ntt, n_ctx):
    _merge_body(refs[n_ctx], refs[:n_ctx], *refs[n_ctx + 1:], seq=seq, nt=nt, ntt=ntt)


def _merge_body(x_ref, c_refs, attn_ref, up_ref, uc_ref, un_ref, yf_ref, yb_ref, rg_ref, g_ref, mod_ref,
                  wbd_ref, ps_ref, woa_ref, wop_ref, wor_ref, wout_ref, n2_ref, wr_ref,
                  br_ref, avg_ref, tri_ref,
                  xo_ref, xs_ref, ls_ref, rw_ref, cnt_ref, lg_ref, h2_ref, *, seq, nt, ntt):
    i = pl.program_id(0)
    j = jnp.minimum(i, ntt - 1) % nt

    @pl.when(i == 0)
    def _():
        lg_ref[...] = jnp.zeros_like(lg_ref)
        h2_ref[...] = jnp.zeros_like(h2_ref)

    lst = _route_tile(lg_ref[...], tri_ref, ls_ref, rw_ref, cnt_ref)

    ext = TM + 2 * POOL_HALO
    seq_len = jnp.where(j == 0, TM, seq)
    start = jnp.where(j == 0, 0, (j - 1) * TM)
    u = uc_ref[0]
    ue = jnp.concatenate([up_ref[0], u, un_ref[0]], axis=0)
    erow = lax.broadcasted_iota(I32, (ext, 1), 0) + (start - POOL_HALO)
    ue = jnp.where((erow >= 0) & (erow < seq_len), ue, 0.0)
    t_pos = lax.broadcasted_iota(I32, (TM, 1), 0) + start
    glane = lax.broadcasted_iota(I32, (TM, POOL_WIDTH), 1) // (POOL_WIDTH // len(POOL_WINDOWS))
    run = ue
    width = 1
    diff = jnp.zeros((TM, POOL_WIDTH), F32)
    for gi, w in enumerate(POOL_WINDOWS):
        while width < w:
            run = run + pltpu.roll(run, ext - width, 0)
            width *= 2
        win = pltpu.roll(run, w // 2, 0)[POOL_HALO:POOL_HALO + TM]
        cnt = jnp.minimum(t_pos - w // 2 + w, seq_len) - jnp.maximum(t_pos - w // 2, 0)
        diff = jnp.where(glane == gi, win / cnt.astype(F32) - u, diff)
    pool = _dot(diff.astype(BF16), wbd_ref[...]) * ps_ref[...]

    def head_norm(y):
        dlt = y.astype(F32) - _dot(y, avg_ref[...])
        var = _dot_hilo(dlt * dlt, avg_ref[...])
        return dlt * lax.rsqrt(var + GN_EPS)

    rg = rg_ref[0].astype(F32)
    ret = head_norm(yf_ref[0]) * _silu(rg[:, :RET_W]) + head_norm(yb_ref[0]) * _silu(rg[:, RET_W:])

    slot = lax.broadcasted_iota(I32, (SORT_ROWS, TM), 0).astype(F32)
    p = jnp.zeros((SORT_ROWS, TM), F32)
    for k in range(TOP_K):
        p = jnp.where(slot == lst[k:k + 1, :], 1.0, p)
    xs_ref[...] = _dot(p.astype(BF16), h2_ref[...])

    d = x_ref.shape[-1]
    gate = lambda t: _sigmoid(g_ref[0, :, t * d:(t + 1) * d])
    m = (gate(0) * _dot(attn_ref[0], woa_ref[...]).astype(BF16)
         + gate(1) * _dot(pool.astype(BF16), wop_ref[...]).astype(BF16)
         + gate(2) * _dot(ret.astype(BF16), wor_ref[...]).astype(BF16))
    xn = _merged_rows((x_ref,), c_refs, j == 0) + mod_ref[0, 2:3, :] * _dot(m, wout_ref[...])
    xo_ref[0] = xn

    ms = jnp.mean(xn * xn, axis=-1, keepdims=True)
    h2 = xn * lax.rsqrt(ms + NORM_EPS) * n2_ref[...] * (1.0 + mod_ref[0, 4:5, :]) + mod_ref[0, 3:4, :]
    h2_ref[...] = h2.astype(BF16)
    hh, hl = _split(h2)
    ne = N_EXPERTS
    full = _dot_nt(wr_ref[...], hh)
    logits = full[:ne] + full[ne:] + _dot_nt(wr_ref[:ne], hl) + br_ref[...]
    lg_ref[...] = logits


def _merge(x, ctx, attn, u, yf, yb, rg, gates, mod_l, seq, wts):
    b, s, _ = attn.shape
    d = x.shape[-1]
    nt = s // TM
    ntt = b * nt
    hb = TM // POOL_HALO
    nh = s // POOL_HALO
    cur = lambda i: jnp.minimum(i, ntt - 1)
    lag = lambda i: jnp.maximum(i - 1, 0)
    tok = lambda width: pl.BlockSpec((1, TM, width), lambda i: (cur(i) // nt, cur(i) % nt, 0))
    full = lambda a: pl.BlockSpec(a.shape, lambda i: (0,) * a.ndim)
    lanes = pl.BlockSpec((TM, LANE), lambda i: (lag(i), 0))
    if ctx is None:
        stream = [tok(d)]
    else:
        stream = [pl.BlockSpec((1, TM, d), lambda i: (cur(i) // nt, 0, 0)),
                  pl.BlockSpec((1, TM, d), lambda i: (cur(i) // nt, jnp.maximum(cur(i) % nt - 1, 0), 0))]
    in_specs = stream + [
        tok(ATT_Q),
        pl.BlockSpec((1, POOL_HALO, POOL_WIDTH),
                     lambda i: (cur(i) // nt, jnp.maximum((cur(i) % nt) * hb - 1, 0), 0)),
        tok(POOL_WIDTH),
        pl.BlockSpec((1, POOL_HALO, POOL_WIDTH),
                     lambda i: (cur(i) // nt, jnp.minimum((cur(i) % nt + 1) * hb, nh - 1), 0)),
        tok(RET_W), tok(RET_W), tok(2 * RET_W), tok(3 * d),
        pl.BlockSpec((1, 6, d), lambda i: (jnp.where(cur(i) % nt == 0, b, cur(i) // nt), 0, 0)),
    ] + [full(a) for a in wts]
    return pl.pallas_call(
        functools.partial(_merge_kernel, seq=seq, nt=nt, ntt=ntt, n_ctx=len(stream) - 1),
        grid=(ntt + 1,),
        in_specs=in_specs,
        out_specs=[tok(d), pl.BlockSpec((SORT_ROWS, d), lambda i: (lag(i), 0)), lanes, lanes,
                   pl.BlockSpec((N_EXPERTS, LANE), lambda i: (lag(i), 0))],
        out_shape=[jax.ShapeDtypeStruct((b, s, d), F32), jax.ShapeDtypeStruct((ntt * SORT_ROWS, d), F32),
                   jax.ShapeDtypeStruct((b * s, LANE), I32),
                   jax.ShapeDtypeStruct((b * s, LANE), F32), jax.ShapeDtypeStruct((ntt * N_EXPERTS, LANE), I32)],
        scratch_shapes=[pltpu.VMEM((N_EXPERTS, TM), F32), pltpu.VMEM((TM, d), BF16)],
        compiler_params=_params("arbitrary"),
    )(*([] if ctx is None else [ctx]), x, attn, u, u, u, yf, yb, rg, gates, mod_l, *wts)


def _rows(tile_index):
    return pl.ds(pl.multiple_of(tile_index * SUBLANE, SUBLANE), SUBLANE)


def _tile_loop(count_ref, tile, fn):
    def body(j, carry):
        fn(j)
        return carry

    lax.fori_loop(0, MIN_TILES, body, 0, unroll=8)
    lax.fori_loop(MIN_TILES, count_ref[tile], body, 0)


def _tile_copies(table_ref, count_ref, tile, copy):
    _tile_loop(count_ref, tile, lambda j: copy(j, table_ref[tile * SORT_TILES + j]))


def _expert_kernel(be_ref, nu_ref, src_ref, xs_hbm, w1_ref, b1_ref, w2_ref, b2_ref, y_ref, w1b, w2b, xbuf, sem):
    bi = pl.program_id(0)
    used = bi < nu_ref[0]
    fresh = (bi == 0) | (be_ref[bi] != be_ref[jnp.maximum(bi - 1, 0)])
    cur = bi % 2
    block_tiles = EB // SUBLANE

    def each_tile(fn):
        def body(t, carry):
            fn(t)
            return carry
        lax.fori_loop(0, block_tiles, body, 0, unroll=8)

    def fetch(blk, half):
        each_tile(lambda t: pltpu.make_async_copy(
            xs_hbm.at[_rows(src_ref[blk * block_tiles + t])], xbuf.at[half, _rows(t)], sem.at[half]).start())

    @pl.when(bi == 0)
    def _():
        fetch(0, 0)

    @pl.when(bi + 1 < nu_ref[0])
    def _():
        fetch(bi + 1, 1 - cur)

    @pl.when(used & fresh)
    def _():
        w1b[...] = w1_ref[0, 0].astype(BF16)
        w2b[...] = w2_ref[0, 0].astype(BF16)

    @pl.when(jnp.logical_not(used))
    def _():
        y_ref[...] = jnp.zeros_like(y_ref)

    @pl.when(used)
    def _():
        each_tile(lambda t: pltpu.make_async_copy(
            xs_hbm.at[_rows(t)], xbuf.at[cur, _rows(t)], sem.at[cur]).wait())
        xb = xbuf[cur].astype(BF16)
        y = None
        for c in range(0, D_FF, FF_CHUNK):
            glu = _dot(xb, w1b[:, c:c + FF_CHUNK]) + b1_ref[0, 0, :, c:c + FF_CHUNK]
            lin = _dot(xb, w1b[:, D_FF + c:D_FF + c + FF_CHUNK]) + b1_ref[0, 0, :, D_FF + c:D_FF + c + FF_CHUNK]
            glu = jnp.minimum(glu, SWIGLU_LIMIT)
            lin = jnp.clip(lin, -SWIGLU_LIMIT, SWIGLU_LIMIT)
            act = glu * _sigmoid(SWIGLU_ALPHA * glu) * (lin + 1.0)
            part = _dot(act.astype(BF16), w2b[c:c + FF_CHUNK, :])
            y = part if y is None else y + part
        y_ref[...] = y + b2_ref[0, 0]


def _experts(xs, block_e, n_used, src, w1, b1, w2, b2, layer, n_blocks):
    d = xs.shape[1]
    depth, ne, _, f2 = w1.shape
    grid_spec = pltpu.PrefetchScalarGridSpec(
        num_scalar_prefetch=3,
        grid=(n_blocks,),
        in_specs=[
            pl.BlockSpec(memory_space=pl.ANY),
            pl.BlockSpec((1, 1, d, f2), lambda bi, be, *_: (layer, be[bi], 0, 0)),
            pl.BlockSpec((1, 1, 1, f2), lambda bi, be, *_: (layer, be[bi], 0, 0)),
            pl.BlockSpec((1, 1, f2 // 2, d), lambda bi, be, *_: (layer, be[bi], 0, 0)),
            pl.BlockSpec((1, 1, 1, d), lambda bi, be, *_: (layer, be[bi], 0, 0)),
        ],
        out_specs=pl.BlockSpec((EB, d), lambda bi, *_: (bi, 0)),
        scratch_shapes=[pltpu.VMEM((d, f2), BF16), pltpu.VMEM((f2 // 2, d), BF16),
                        pltpu.VMEM((2, EB, d), F32), pltpu.SemaphoreType.DMA((2,))],
    )
    return pl.pallas_call(
        _expert_kernel,
        grid_spec=grid_spec,
        out_shape=jax.ShapeDtypeStruct((n_blocks * EB, d), F32),
        compiler_params=_params("arbitrary"),
    )(block_e, n_used, src, xs, w1, b1.reshape(depth, ne, 1, f2), w2, b2.reshape(depth, ne, 1, d))


def _combine_kernel(dst_ref, cnt_ref, x_ref, ls_ref, rw_ref, mod_ref, fg_ref, ys_hbm, xo_ref, buf, sem, *, final):
    i = pl.program_id(0)
    last = pl.num_programs(0) - 1
    cur = i % COMBINE_SLOTS

    def fetch(tile, half):
        _tile_copies(dst_ref, cnt_ref, tile, lambda j, t: pltpu.make_async_copy(
            ys_hbm.at[_rows(t)], buf.at[half, _rows(j)], sem.at[half]).start())

    @pl.when(i == 0)
    def _():
        buf[...] = jnp.zeros_like(buf)
        for ahead in range(COMBINE_SLOTS - 1):
            pl.when(ahead <= last)(functools.partial(fetch, ahead, ahead))

    @pl.when(i + COMBINE_SLOTS - 1 <= last)
    def _():
        fetch(i + COMBINE_SLOTS - 1, (i + COMBINE_SLOTS - 1) % COMBINE_SLOTS)

    _tile_loop(cnt_ref, i, lambda j: pltpu.make_async_copy(
        ys_hbm.at[_rows(j)], buf.at[cur, _rows(j)], sem.at[cur]).wait())
    yb = buf[cur].astype(BF16)
    slot = lax.broadcasted_iota(I32, (TM, SORT_ROWS), 1)
    ls = ls_ref[...]
    rw = rw_ref[...]
    g = jnp.zeros((TM, SORT_ROWS), F32)
    for k in range(TOP_K):
        g = jnp.where(slot == ls[:, k:k + 1], rw[:, k:k + 1], g)
    xn = x_ref[0] + mod_ref[0, 5:6, :] * _dot(g.astype(BF16), yb)
    if final:
        ms = jnp.mean(xn * xn, axis=-1, keepdims=True)
        xn = xn * lax.rsqrt(ms + NORM_EPS) * fg_ref[...]
    xo_ref[0] = xn


def _combine(xn, ls, rw, mod_l, final_gain, ys, tabs, final):
    b, s, d = xn.shape
    nt = s // TM
    tok = pl.BlockSpec((1, TM, d), lambda i, *_: (i // nt, i % nt, 0))
    lanes = pl.BlockSpec((TM, LANE), lambda i, *_: (i, 0))
    if final:
        out_spec = pl.BlockSpec((1, TM, d), lambda i, *_: (i // nt, jnp.maximum(i % nt - 1, 0), 0))
        out_shape = jax.ShapeDtypeStruct((b, s - TM, d), F32)
    else:
        out_spec, out_shape = tok, jax.ShapeDtypeStruct((b, s, d), F32)
    grid_spec = pltpu.PrefetchScalarGridSpec(
        num_scalar_prefetch=2,
        grid=(b * nt,),
        in_specs=[tok, lanes, lanes,
                  pl.BlockSpec((1, 6, d), lambda i, *_: (jnp.where(i % nt == 0, b, i // nt), 0, 0)),
                  pl.BlockSpec((1, d), lambda i, *_: (0, 0)),
                  pl.BlockSpec(memory_space=pl.ANY)],
        out_specs=out_spec,
        scratch_shapes=[pltpu.VMEM((COMBINE_SLOTS, SORT_ROWS, d), F32), pltpu.SemaphoreType.DMA((COMBINE_SLOTS,))],
    )
    return pl.pallas_call(
        functools.partial(_combine_kernel, final=final),
        grid_spec=grid_spec,
        out_shape=out_shape,
        compiler_params=_params("arbitrary"),
    )(tabs["dst"], tabs["count"], xn, ls, rw, mod_l, final_gain.reshape(1, d), ys)


def _inproj_weight(w_in_l):
    d = w_in_l.shape[0]
    sizes = (ATT_Q, N_KV_HEADS * HEAD_DIM, N_KV_HEADS * HEAD_DIM, POOL_WIDTH,
             RET_W, RET_W, RET_W, RET_W, RET_W, d, d, d)
    parts, off = [], 0
    for sz in sizes:
        parts.append(w_in_l[:, off:off + sz])
        off += sz
    q, k, v, u, rq, rk, rv, rgf, rgb, ga, gp, gr = parts
    twice = lambda w: jnp.concatenate(
        [w[:, h * HEAD_DIM:(h + 1) * HEAD_DIM] for h in range(N_KV_HEADS) for _ in range(2)], axis=1)
    cols = [q * HEAD_DIM ** -0.5, twice(k), twice(v), rq, rk * RET_DIM ** -0.5, rv, rgf, rgb, u, ga, gp, gr]
    return jnp.concatenate(cols, axis=1).astype(BF16)


def _block_diag(blocks):
    n, r, c = blocks.shape
    out = jnp.zeros((n * r, n * c), blocks.dtype)
    for g in range(n):
        out = out.at[g * r:(g + 1) * r, g * c:(g + 1) * c].set(blocks[g])
    return out


def _routing_tables(cnt, n_blocks):
    ntt = cnt.shape[0] // N_EXPERTS
    counts = cnt.reshape(ntt, N_EXPERTS, LANE)[:, :, 0]
    run = (counts + SUBLANE - 1) // SUBLANE
    total = jnp.sum(run, axis=0)
    eb = EB // SUBLANE
    padded = (total + eb - 1) // eb * eb
    pad_end = jnp.cumsum(padded)
    pad_start = pad_end - padded
    off = pad_start[None, :] + jnp.cumsum(run, axis=0) - run
    run_end = jnp.cumsum(run, axis=1)
    j = jnp.arange(SORT_TILES)
    owner = jnp.sum(run_end[:, None, :] <= j[None, :, None], axis=2)
    mine = owner[:, :, None] == jnp.arange(N_EXPERTS)[None, None, :]
    in_region = j[None, :] + jnp.sum(jnp.where(mine, (off - (run_end - run))[:, None, :], 0), axis=2)
    dst = jnp.where(owner < N_EXPERTS, in_region, 0)
    n_used = pad_end[-1] // eb
    blk = jnp.minimum(jnp.arange(n_blocks), n_used - 1) * eb
    block_e = jnp.minimum(jnp.sum(pad_end[None, :] <= blk[:, None], axis=1), N_EXPERTS - 1)
    mine_b = block_e[:, None] == jnp.arange(N_EXPERTS)[None, :]
    of_block = lambda table: jnp.sum(jnp.where(mine_b[:, None, :], table[None, :, :], 0), axis=2)
    first = of_block(jnp.cumsum(run, axis=0) - run)
    length = of_block(run)
    local = of_block(run_end - run)
    start_b = jnp.sum(jnp.where(mine_b, pad_start[None, :], 0), axis=1)
    pos = (jnp.arange(n_blocks) * eb - start_b)[:, None] + jnp.arange(eb)[None, :]
    inside = (pos[:, :, None] >= first[:, None, :]) & (pos[:, :, None] < (first + length)[:, None, :])
    where = (jnp.arange(ntt) * SORT_TILES)[None, None, :] + (local - first)[:, None, :] + pos[:, :, None]
    zero_tile = SORT_TILES - 1
    src = jnp.where(jnp.any(inside, axis=2), jnp.sum(jnp.where(inside, where, 0), axis=2), zero_tile)
    tabs = dict(dst=dst.reshape(-1).astype(I32), count=run_end[:, -1].astype(I32),
                src=src.reshape(-1).astype(I32))
    return tabs, block_e.astype(I32), n_used.reshape(1).astype(I32)


def kernel(x, c, ctx, c_ctx, w_mod, b_mod, norm1, norm2, w_in, attn_sinks, pool_w, pool_scale, ret_decay,
           w_o_attn, w_o_pool, w_o_ret, w_out, w_router, b_router, w_expert_in, b_expert_in, w_expert_out,
           b_expert_out, final_norm):
    b, seq, d = x.shape
    depth = w_mod.shape[0]
    assert ctx.shape[1] == TM and seq % TM == 0 and seq % GRID_W == 0 and b + 1 <= SUBLANE
    s = seq + TM
    xm = x
    mod = _modulation(c, c_ctx, w_mod, b_mod)
    tables = _rope_tables(seq)
    ntt = b * s // TM
    n_blocks = -(-(b * s * TOP_K + ntt * N_EXPERTS * (SUBLANE - 1)) // EB) + N_EXPERTS
    avg = _block_diag(jnp.full((RET_HEADS, RET_DIM, RET_DIM), 1.0 / RET_DIM, F32)).astype(BF16)
    tri = (jnp.arange(TM)[:, None] <= jnp.arange(TM)[None, :]).astype(BF16)
    for l in range(depth):
        first_ctx = ctx if l == 0 else None
        q, k2, v2, rq, rk, rv, rg, u, gates = _inproj(xm, first_ctx, mod[l], norm1[l], _inproj_weight(w_in[l]), tables)
        attn = _attention(q, k2, v2, attn_sinks[l])
        yf, yb = _retention(rq, rk, rv, jax.nn.log_sigmoid(ret_decay[l].astype(F32)))
        wr_hilo = jnp.concatenate(_split(w_router[l].T), axis=0)
        br = jnp.broadcast_to(b_router[l][:, None], (N_EXPERTS, TM))
        wts = (_block_diag(pool_w[l]).astype(BF16), pool_scale[l].reshape(1, -1),
               w_o_attn[l].astype(BF16), w_o_pool[l].astype(BF16), w_o_ret[l].astype(BF16),
               w_out[l].astype(BF16), norm2[l].reshape(1, d), wr_hilo, br, avg, tri)
        xn, xs, ls, rw, cnt = _merge(xm, first_ctx, attn, u, yf, yb, rg, gates, mod[l], seq, wts)
        tabs, block_e, n_used = _routing_tables(cnt, n_blocks)
        ys = _experts(xs, block_e, n_used, tabs["src"], w_expert_in, b_expert_in, w_expert_out, b_expert_out, l,
                      n_blocks)
        xm = _combine(xn, ls, rw, mod[l], final_norm, ys, tabs, final=l == depth - 1)
    return xm
```
